```python
import jax, jax.numpy as jnp
from jax import lax
import numpy as np

D_MODEL = 2048
BATCH = 8
SEQ = 8192
DEPTH = 1

CHUNK = 64
D_MIX = D_MODEL
CONV_WIDTH = D_MIX // 2
CONV_HEADS = 8
CONV_HEAD_DIM = CONV_WIDTH // CONV_HEADS
CONV_K = 3
POOL_WIDTH = D_MIX - CONV_WIDTH
POOL_WINDOWS = (2, 4, 8, 16)
N_POOL_GROUPS = len(POOL_WINDOWS)
POOL_GROUP_DIM = POOL_WIDTH // N_POOL_GROUPS
IN_PROJ_WIDTH = 3 * CONV_WIDTH + POOL_WIDTH
D_FF = ((8 * D_MODEL // 3 + 255) // 256) * 256
EPS = 1e-6

kernel_name = "hybrid_shortconv_multiscale_pool_block"


def rms_norm(x, g):
    xf = x.astype(jnp.float32)
    y = xf * lax.rsqrt(jnp.mean(xf * xf, axis=-1, keepdims=True) + EPS)
    return (y * g.astype(jnp.float32)).astype(x.dtype)


def rms_norm_plain(x):
    xf = x.astype(jnp.float32)
    y = xf * lax.rsqrt(jnp.mean(xf * xf, axis=-1, keepdims=True) + EPS)
    return y.astype(x.dtype)


def short_conv_causal(u, w):
    c = u.shape[-1]
    rhs = w[:, None, :].astype(u.dtype)
    return lax.conv_general_dilated(
        u, rhs, window_strides=(1,), padding=[(CONV_K - 1, 0)],
        dimension_numbers=("NWC", "WIO", "NWC"), feature_group_count=c)


def multiscale_pool_causal(v):
    bn, s, _ = v.shape
    vg = v.reshape(bn, s, N_POOL_GROUPS, POOL_GROUP_DIM).astype(jnp.float32)
    cs = jnp.cumsum(vg, axis=1)
    pos = jnp.arange(1, s + 1, dtype=jnp.float32)
    outs = []
    for gi, w in enumerate(POOL_WINDOWS):
        c = cs[:, :, gi]
        prev = jnp.pad(c, ((0, 0), (w, 0), (0, 0)))[:, :s]
        cnt = jnp.minimum(pos, float(w))[None, :, None]
        outs.append((c - prev) / cnt - vg[:, :, gi])
    return jnp.stack(outs, axis=2)


def _fwd_setup_inputs(seed: int = 0) -> dict:
    key = jax.random.key(seed)
    ks = jax.random.split(key, 16)
    L = DEPTH

    def nrm(k, shape, fan_in):
        return jax.random.normal(k, shape, jnp.float32) * (fan_in ** -0.5)

    def gain(k, shape):
        return 1.0 + 0.05 * jax.random.normal(k, shape, jnp.float32)

    return {
        "x": jax.random.normal(ks[0], (BATCH, SEQ, D_MODEL), jnp.float32),
        "ln_mix_pre": gain(ks[1], (L, D_MODEL)),
        "w_in": nrm(ks[2], (L, D_MODEL, IN_PROJ_WIDTH), D_MODEL),
        "conv_w": nrm(ks[3], (L, CONV_K, CONV_WIDTH), CONV_K),
        "pool_w": nrm(ks[4], (L, N_POOL_GROUPS, POOL_GROUP_DIM, POOL_GROUP_DIM), POOL_GROUP_DIM),
        "pool_scale": gain(ks[5], (L, POOL_WIDTH)),
        "w_out": nrm(ks[6], (L, D_MIX, D_MODEL), D_MIX),
        "ln_mix_post": gain(ks[7], (L, D_MODEL)),
        "ln_ffn_pre": gain(ks[8], (L, D_MODEL)),
        "w_gate": nrm(ks[9], (L, D_MODEL, D_FF), D_MODEL),
        "w_up": nrm(ks[10], (L, D_MODEL, D_FF), D_MODEL),
        "w_down": nrm(ks[11], (L, D_FF, D_MODEL), D_FF),
        "ln_ffn_post": gain(ks[12], (L, D_MODEL)),
    }


def _fwd_reference(x, ln_mix_pre, w_in, conv_w, pool_w, pool_scale, w_out, ln_mix_post,
              ln_ffn_pre, w_gate, w_up, w_down, ln_ffn_post):
    bn, s, _ = x.shape
    for l in range(DEPTH):
        h = rms_norm(x, ln_mix_pre[l])
        proj = jnp.einsum("bsd,de->bse", h, w_in[l])
        gate_b, gate_c, u, v = jnp.split(
            proj, [CONV_WIDTH, 2 * CONV_WIDTH, 3 * CONV_WIDTH], axis=-1)

        y_conv = gate_b * short_conv_causal(gate_c * u, conv_w[l])
        y_conv = rms_norm_plain(y_conv.reshape(bn, s, CONV_HEADS, CONV_HEAD_DIM))
        y_conv = y_conv.reshape(bn, s, CONV_WIDTH)

        pooled = multiscale_pool_causal(v).astype(v.dtype)
        y_pool = jnp.einsum("bsgc,gcd->bsgd", pooled, pool_w[l])
        y_pool = rms_norm_plain(y_pool).reshape(bn, s, POOL_WIDTH) * pool_scale[l]

        mixed = jnp.concatenate([y_conv, y_pool], axis=-1)
        mix_out = jnp.einsum("bse,ed->bsd", mixed, w_out[l])
        x = x + rms_norm(mix_out, ln_mix_post[l])

        hf = rms_norm(x, ln_ffn_pre[l])
        g = jnp.einsum("bsd,df->bsf", hf, w_gate[l])
        up = jnp.einsum("bsd,df->bsf", hf, w_up[l])
        ff = jnp.einsum("bsf,fd->bsd", jax.nn.silu(g) * up, w_down[l])
        x = x + rms_norm(ff, ln_ffn_post[l])
    return x


import jax as _jax
import jax.numpy as _jnp

TWIN_FORMAT = 'train_step'
FWD_PARAMS = ['x', 'ln_mix_pre', 'w_in', 'conv_w', 'pool_w', 'pool_scale', 'w_out', 'ln_mix_post', 'ln_ffn_pre', 'w_gate', 'w_up', 'w_down', 'ln_ffn_post']
TWIN_WEIGHTS = ['ln_mix_pre', 'w_in', 'conv_w', 'pool_w', 'pool_scale', 'w_out', 'ln_mix_post', 'ln_ffn_pre', 'w_gate', 'w_up', 'w_down', 'ln_ffn_post']
TWIN_DIFF_INPUT = 'x'
TWIN_INPUTS = ['x', 'ln_mix_pre', 'w_in', 'conv_w', 'pool_w', 'pool_scale', 'w_out', 'ln_mix_post', 'ln_ffn_pre', 'w_gate', 'w_up', 'w_down', 'ln_ffn_post', 'loss_target', 'm_ln_mix_pre', 'm_w_in', 'm_conv_w', 'm_pool_w', 'm_pool_scale', 'm_w_out', 'm_ln_mix_post', 'm_ln_ffn_pre', 'm_w_gate', 'm_w_up', 'm_w_down', 'm_ln_ffn_post', 'v_ln_mix_pre', 'v_w_in', 'v_conv_w', 'v_pool_w', 'v_pool_scale', 'v_w_out', 'v_ln_mix_post', 'v_ln_ffn_pre', 'v_w_gate', 'v_w_up', 'v_w_down', 'v_ln_ffn_post']
TWIN_OUTPUTS = ['loss', 'grad_x', 'grad_ln_mix_pre', 'grad_w_in', 'grad_conv_w', 'grad_pool_w', 'grad_pool_scale', 'grad_w_out', 'grad_ln_mix_post', 'grad_ln_ffn_pre', 'grad_w_gate', 'grad_w_up', 'grad_w_down', 'grad_ln_ffn_post', 'delta_ln_mix_pre', 'delta_w_in', 'delta_conv_w', 'delta_pool_w', 'delta_pool_scale', 'delta_w_out', 'delta_ln_mix_post', 'delta_ln_ffn_pre', 'delta_w_gate', 'delta_w_up', 'delta_w_down', 'delta_ln_ffn_post', 'new_m_ln_mix_pre', 'new_m_w_in', 'new_m_conv_w', 'new_m_pool_w', 'new_m_pool_scale', 'new_m_w_out', 'new_m_ln_mix_post', 'new_m_ln_ffn_pre', 'new_m_w_gate', 'new_m_w_up', 'new_m_w_down', 'new_m_ln_ffn_post', 'new_v_ln_mix_pre', 'new_v_w_in', 'new_v_conv_w', 'new_v_pool_w', 'new_v_pool_scale', 'new_v_w_out', 'new_v_ln_mix_post', 'new_v_ln_ffn_pre', 'new_v_w_gate', 'new_v_w_up', 'new_v_w_down', 'new_v_ln_ffn_post']
TWIN_LEAF_KINDS = {'loss': 'loss', 'grad_x': 'grad_x', 'grad_ln_mix_pre': 'grad_w', 'grad_w_in': 'grad_w', 'grad_conv_w': 'grad_w', 'grad_pool_w': 'grad_w', 'grad_pool_scale': 'grad_w', 'grad_w_out': 'grad_w', 'grad_ln_mix_post': 'grad_w', 'grad_ln_ffn_pre': 'grad_w', 'grad_w_gate': 'grad_w', 'grad_w_up': 'grad_w', 'grad_w_down': 'grad_w', 'grad_ln_ffn_post': 'grad_w', 'delta_ln_mix_pre': 'delta_w', 'delta_w_in': 'delta_w', 'delta_conv_w': 'delta_w', 'delta_pool_w': 'delta_w', 'delta_pool_scale': 'delta_w', 'delta_w_out': 'delta_w', 'delta_ln_mix_post': 'delta_w', 'delta_ln_ffn_pre': 'delta_w', 'delta_w_gate': 'delta_w', 'delta_w_up': 'delta_w', 'delta_w_down': 'delta_w', 'delta_ln_ffn_post': 'delta_w', 'new_m_ln_mix_pre': 'new_m', 'new_m_w_in': 'new_m', 'new_m_conv_w': 'new_m', 'new_m_pool_w': 'new_m', 'new_m_pool_scale': 'new_m', 'new_m_w_out': 'new_m', 'new_m_ln_mix_post': 'new_m', 'new_m_ln_ffn_pre': 'new_m', 'new_m_w_gate': 'new_m', 'new_m_w_up': 'new_m', 'new_m_w_down': 'new_m', 'new_m_ln_ffn_post': 'new_m', 'new_v_ln_mix_pre': 'new_v', 'new_v_w_in': 'new_v', 'new_v_conv_w': 'new_v', 'new_v_pool_w': 'new_v', 'new_v_pool_scale': 'new_v', 'new_v_w_out': 'new_v', 'new_v_ln_mix_post': 'new_v', 'new_v_ln_ffn_pre': 'new_v', 'new_v_w_gate': 'new_v', 'new_v_w_up': 'new_v', 'new_v_w_down': 'new_v', 'new_v_ln_ffn_post': 'new_v'}


def _forward(args):
    return _fwd_reference(*[args[k] for k in FWD_PARAMS])


def _output_shape():
    def fwd():
        inp = _fwd_setup_inputs(0)
        return _fwd_reference(*[inp[k] for k in FWD_PARAMS])
    out = _jax.eval_shape(fwd)
    return out.shape, out.dtype

N_MICROBATCH = 1
ADAM_LR = 0.001
ADAM_B1 = 0.9
ADAM_B2 = 0.999
ADAM_EPS = 1e-08
ADAM_WD = 0.01
ADAM_STEP = 10
PER_EXAMPLE_BATCH_AXIS = {'x': 0, 'loss_target': 0}
SHARED_INPUTS = []
_WEIGHT_DTYPES = {'ln_mix_pre': _jnp.float32, 'w_in': _jnp.float32, 'conv_w': _jnp.float32, 'pool_w': _jnp.float32, 'pool_scale': _jnp.float32, 'w_out': _jnp.float32, 'ln_mix_post': _jnp.float32, 'ln_ffn_pre': _jnp.float32, 'w_gate': _jnp.float32, 'w_up': _jnp.float32, 'w_down': _jnp.float32, 'ln_ffn_post': _jnp.float32}
MOMENT_SCALE = {'ln_mix_pre': 4.770642e-01, 'w_in': 3.643053e-01, 'conv_w': 3.242558e-01, 'pool_w': 5.294018e-01, 'pool_scale': 5.355566e-01, 'w_out': 4.545924e-01, 'ln_mix_post': 3.204557e+01, 'ln_ffn_pre': 3.848587e-01, 'w_gate': 1.359639e-01, 'w_up': 2.052851e-01, 'w_down': 3.410801e-01, 'ln_ffn_post': 3.203962e+01}


def _to_microbatches(a, axis):
    t = _jnp.moveaxis(a, axis, 0)
    t = t.reshape((N_MICROBATCH, t.shape[0] // N_MICROBATCH) + t.shape[1:])
    return _jnp.moveaxis(t, 1, axis + 1)


def setup_inputs(seed: int = 0) -> dict:
    inp = _fwd_setup_inputs(seed)
    key = _jax.random.fold_in(_jax.random.key(seed), 7919)
    shape, _ = _output_shape()
    out = dict(inp)
    out["loss_target"] = _jax.random.normal(_jax.random.fold_in(key, 0), shape, _jnp.float32)
    for i, name in enumerate(TWIN_WEIGHTS):
        w = inp[name].astype(_jnp.float32)
        if MOMENT_SCALE is None:
            s = _jnp.sqrt(_jnp.mean(_jnp.square(w)) + 1e-30)
        else:
            s = MOMENT_SCALE[name]
        km, kv = _jax.random.split(_jax.random.fold_in(key, i + 1))
        out[name] = w
        out["m_" + name] = s * _jax.random.normal(km, w.shape, _jnp.float32)
        out["v_" + name] = (s * s) * _jax.random.uniform(kv, w.shape, _jnp.float32, 0.5, 1.5)
    if N_MICROBATCH > 1:
        for name, axis in PER_EXAMPLE_BATCH_AXIS.items():
            out[name] = _to_microbatches(out[name], axis)
    return {'x': out['x'], 'ln_mix_pre': out['ln_mix_pre'], 'w_in': out['w_in'], 'conv_w': out['conv_w'], 'pool_w': out['pool_w'], 'pool_scale': out['pool_scale'], 'w_out': out['w_out'], 'ln_mix_post': out['ln_mix_post'], 'ln_ffn_pre': out['ln_ffn_pre'], 'w_gate': out['w_gate'], 'w_up': out['w_up'], 'w_down': out['w_down'], 'ln_ffn_post': out['ln_ffn_post'], 'loss_target': out['loss_target'], 'm_ln_mix_pre': out['m_ln_mix_pre'], 'm_w_in': out['m_w_in'], 'm_conv_w': out['m_conv_w'], 'm_pool_w': out['m_pool_w'], 'm_pool_scale': out['m_pool_scale'], 'm_w_out': out['m_w_out'], 'm_ln_mix_post': out['m_ln_mix_post'], 'm_ln_ffn_pre': out['m_ln_ffn_pre'], 'm_w_gate': out['m_w_gate'], 'm_w_up': out['m_w_up'], 'm_w_down': out['m_w_down'], 'm_ln_ffn_post': out['m_ln_ffn_post'], 'v_ln_mix_pre': out['v_ln_mix_pre'], 'v_w_in': out['v_w_in'], 'v_conv_w': out['v_conv_w'], 'v_pool_w': out['v_pool_w'], 'v_pool_scale': out['v_pool_scale'], 'v_w_out': out['v_w_out'], 'v_ln_mix_post': out['v_ln_mix_post'], 'v_ln_ffn_pre': out['v_ln_ffn_pre'], 'v_w_gate': out['v_w_gate'], 'v_w_up': out['v_w_up'], 'v_w_down': out['v_w_down'], 'v_ln_ffn_post': out['v_ln_ffn_post']}


def _loss(weights, diff, rest, loss_target):
    with _jax.named_scope("forward"):
        args = {**rest, TWIN_DIFF_INPUT: diff, **{k: w.astype(_WEIGHT_DTYPES[k]) for k, w in weights.items()}}
        y = _forward(args)
    with _jax.named_scope("loss_head"):
        err = _jnp.square(y.astype(_jnp.float32) - loss_target)
        return 0.5 * _jnp.sum(_jnp.mean(err, axis=-1)) if err.ndim else 0.5 * err


def _adamw(w, g, m, v):
    m = ADAM_B1 * m + (1.0 - ADAM_B1) * g
    v = ADAM_B2 * v + (1.0 - ADAM_B2) * _jnp.square(g)
    m_hat = m / (1.0 - ADAM_B1 ** ADAM_STEP)
    v_hat = v / (1.0 - ADAM_B2 ** ADAM_STEP)
    delta = -ADAM_LR * (m_hat / (_jnp.sqrt(v_hat) + ADAM_EPS) + ADAM_WD * w)
    return delta, m, v


def reference(x, ln_mix_pre, w_in, conv_w, pool_w, pool_scale, w_out, ln_mix_post, ln_ffn_pre, w_gate, w_up, w_down, ln_ffn_post, loss_target, m_ln_mix_pre, m_w_in, m_conv_w, m_pool_w, m_pool_scale, m_w_out, m_ln_mix_post, m_ln_ffn_pre, m_w_gate, m_w_up, m_w_down, m_ln_ffn_post, v_ln_mix_pre, v_w_in, v_conv_w, v_pool_w, v_pool_scale, v_w_out, v_ln_mix_post, v_ln_ffn_pre, v_w_gate, v_w_up, v_w_down, v_ln_ffn_post):
    given = dict(x=x, ln_mix_pre=ln_mix_pre, w_in=w_in, conv_w=conv_w, pool_w=pool_w, pool_scale=pool_scale, w_out=w_out, ln_mix_post=ln_mix_post, ln_ffn_pre=ln_ffn_pre, w_gate=w_gate, w_up=w_up, w_down=w_down, ln_ffn_post=ln_ffn_post, loss_target=loss_target, m_ln_mix_pre=m_ln_mix_pre, m_w_in=m_w_in, m_conv_w=m_conv_w, m_pool_w=m_pool_w, m_pool_scale=m_pool_scale, m_w_out=m_w_out, m_ln_mix_post=m_ln_mix_post, m_ln_ffn_pre=m_ln_ffn_pre, m_w_gate=m_w_gate, m_w_up=m_w_up, m_w_down=m_w_down, m_ln_ffn_post=m_ln_ffn_post, v_ln_mix_pre=v_ln_mix_pre, v_w_in=v_w_in, v_conv_w=v_conv_w, v_pool_w=v_pool_w, v_pool_scale=v_pool_scale, v_w_out=v_w_out, v_ln_mix_post=v_ln_mix_post, v_ln_ffn_pre=v_ln_ffn_pre, v_w_gate=v_w_gate, v_w_up=v_w_up, v_w_down=v_w_down, v_ln_ffn_post=v_ln_ffn_post)
    weights = {n: given[n] for n in TWIN_WEIGHTS}
    shared = {n: given[n] for n in SHARED_INPUTS}
    per_example = {n: given[n] for n in ['x']}
    grad_fn = _jax.value_and_grad(_loss, argnums=(0, 1))

    def one_microbatch(ex, loss_target):
        ex = dict(ex)
        diff = ex.pop(TWIN_DIFF_INPUT)
        return grad_fn(weights, diff, {**shared, **ex}, loss_target)

    if N_MICROBATCH == 1:
        loss, (grad_w, grad_x) = one_microbatch(per_example, given["loss_target"])
    else:
        def body(carry, xs):
            loss_sum, grad_sum = carry
            l_k, (gw_k, gx_k) = one_microbatch(xs[0], xs[1])
            with _jax.named_scope("update"):
                return (loss_sum + l_k, _jax.tree.map(_jnp.add, grad_sum, gw_k)), gx_k

        init = (_jnp.zeros((), _jnp.float32), _jax.tree.map(_jnp.zeros_like, weights))
        (loss, grad_w), grad_x = _jax.lax.scan(body, init, (per_example, given["loss_target"]))
    with _jax.named_scope("update"):
        delta_w, new_m, new_v = {}, {}, {}
        for n in TWIN_WEIGHTS:
            delta_w[n], new_m[n], new_v[n] = _adamw(weights[n], grad_w[n], given["m_" + n], given["v_" + n])
    return (loss, grad_x, *[grad_w[n] for n in TWIN_WEIGHTS], *[delta_w[n] for n in TWIN_WEIGHTS],
            *[new_m[n] for n in TWIN_WEIGHTS], *[new_v[n] for n in TWIN_WEIGHTS])
```

```python
import functools

import jax
import jax.numpy as jnp
from jax import lax
from jax.experimental import pallas as pl
from jax.experimental.pallas import tpu as pltpu

EPS = 1e-6
CONV_HEAD_DIM = 128
CONV_K = 3
POOL_WINDOWS = (2, 4, 8, 16)
HALO = 16
N_CHIPS = 4
N_DEV = 8

ADAM_LR = 0.001
ADAM_B1 = 0.9
ADAM_B2 = 0.999
ADAM_EPS = 1e-08
ADAM_WD = 0.01
ADAM_STEP = 10

MM_DTYPE = jnp.bfloat16
COMM_DTYPE = jnp.bfloat16
VMEM_LIMIT = 56 * 1024 * 1024
MESH = pl.DeviceIdType.MESH
ANY = pl.BlockSpec(memory_space=pl.ANY)


def _tile(n, pref):
    t = min(pref, n)
    while n % t:
        t //= 2
    return t


def _params(sem):
    return pltpu.CompilerParams(dimension_semantics=sem, vmem_limit_bytes=VMEM_LIMIT)


def _rstd(x):
    return lax.rsqrt(jnp.mean(x * x, axis=-1, keepdims=True) + EPS)


def _norm_bwd(dn, n, rstd):
    return rstd * (dn - n * jnp.mean(dn * n, axis=-1, keepdims=True))


_DOT_DIMS = {
    "nn": (((1,), (0,)), ((), ())),
    "nt": (((1,), (1,)), ((), ())),
    "tn": (((0,), (0,)), ((), ())),
}


def _dot(a, b, mode):
    return lax.dot_general(a.astype(MM_DTYPE), b.astype(MM_DTYPE), _DOT_DIMS[mode],
                           preferred_element_type=jnp.float32)


def _matmul(name, *, grid, mode, pairs, pair_specs, acc_shapes, extras=(), extra_specs=(),
            out_shapes, out_specs, epilogue):
    nk = grid[2]
    flat_in, flat_specs, counts = [], [], []
    for ps, ss in zip(pairs, pair_specs):
        counts.append(len(ps))
        for (a, b), (sa, sb) in zip(ps, ss):
            flat_in += [a, b]
            flat_specs += [sa, sb]
    n_pair_refs = len(flat_in)
    n_extra = len(extras)
    n_out = len(out_shapes)

    def body(*refs):
        pair_refs = refs[:n_pair_refs]
        extra_refs = refs[n_pair_refs:n_pair_refs + n_extra]
        out_refs = refs[n_pair_refs + n_extra:n_pair_refs + n_extra + n_out]
        acc_refs = refs[n_pair_refs + n_extra + n_out:]
        k = pl.program_id(2)

        def partial_sums():
            res, p = [], 0
            for cnt in counts:
                tot = None
                for _ in range(cnt):
                    d = _dot(pair_refs[p][...], pair_refs[p + 1][...], mode)
                    tot = d if tot is None else tot + d
                    p += 2
                res.append(tot)
            return res

        def finish(accs):
            outs = epilogue(accs, extra_refs)
            for o_ref, o in zip(out_refs, outs):
                o_ref[...] = o.astype(o_ref.dtype)

        if nk == 1:
            finish(partial_sums())
        else:
            @pl.when(k == 0)
            def _():
                for acc_ref in acc_refs:
                    acc_ref[...] = jnp.zeros_like(acc_ref)

            for acc_ref, s in zip(acc_refs, partial_sums()):
                acc_ref[...] += s

            @pl.when(k == nk - 1)
            def _():
                finish([a[...] for a in acc_refs])

    scratch = [] if nk == 1 else [pltpu.VMEM(s, jnp.float32) for s in acc_shapes]
    return pl.pallas_call(
        body, name=name, grid=grid,
        in_specs=flat_specs + list(extra_specs), out_specs=list(out_specs), out_shape=list(out_shapes),
        scratch_shapes=scratch,
        compiler_params=_params(("arbitrary", "arbitrary", "arbitrary")),
    )(*flat_in, *extras)


def _identity_epilogue(accs, extra_refs):
    return tuple(accs)


def _row_spec(tr, n):
    return pl.BlockSpec((tr, n), lambda i: (i, 0))


def _const_spec(shape):
    return pl.BlockSpec(shape, lambda i: tuple(0 for _ in shape))


def _accumulate(ref, val, i):
    @pl.when(i == 0)
    def _():
        ref[...] = val

    @pl.when(i > 0)
    def _():
        ref[...] += val


def _pre_norm(x, gain):
    t, d = x.shape
    tr = _tile(t, 512)

    def body(x_ref, g_ref, h_ref):
        xv = x_ref[...]
        h_ref[...] = (xv * _rstd(xv) * g_ref[...]).astype(h_ref.dtype)

    return pl.pallas_call(
        body, name="pre_norm", grid=(t // tr,),
        in_specs=[_row_spec(tr, d), _const_spec((1, d))], out_specs=_row_spec(tr, d),
        out_shape=jax.ShapeDtypeStruct((t, d), MM_DTYPE), compiler_params=_params(("arbitrary",)),
    )(x, gain)


def _post_mix(x, mix_out, g_post, g_ffn_pre):
    t, d = x.shape
    tr = _tile(t, 512)

    def body(x_ref, mo_ref, g2_ref, g3_ref, x1_ref, hf_ref):
        mo = mo_ref[...]
        x1 = x_ref[...] + mo * _rstd(mo) * g2_ref[...]
        x1_ref[...] = x1
        hf_ref[...] = (x1 * _rstd(x1) * g3_ref[...]).astype(hf_ref.dtype)

    return pl.pallas_call(
        body, name="post_mix", grid=(t // tr,),
        in_specs=[_row_spec(tr, d), _row_spec(tr, d), _const_spec((1, d)), _const_spec((1, d))],
        out_specs=[_row_spec(tr, d), _row_spec(tr, d)],
        out_shape=[jax.ShapeDtypeStruct((t, d), jnp.float32), jax.ShapeDtypeStruct((t, d), MM_DTYPE)],
        compiler_params=_params(("arbitrary",)),
    )(x, mix_out, g_post, g_ffn_pre)


def _loss_head(ff, x1, target, g_post):
    t, d = ff.shape
    tr = _tile(t, 512)

    def body(ff_ref, x1_ref, tg_ref, g_ref, dout_ref, dff_ref, loss_ref, dg_ref):
        i = pl.program_id(0)
        ff_v = ff_ref[...]
        rstd = _rstd(ff_v)
        n = ff_v * rstd
        g = g_ref[...]
        err = x1_ref[...] + n * g - tg_ref[...]
        tile_loss = 0.5 * jnp.sum(jnp.mean(err * err, axis=-1, keepdims=True), axis=0, keepdims=True)
        dout = err / d
        dout_ref[...] = dout
        dff_ref[...] = _norm_bwd(dout * g, n, rstd).astype(dff_ref.dtype)
        _accumulate(loss_ref, jnp.broadcast_to(tile_loss, loss_ref.shape), i)
        _accumulate(dg_ref, jnp.sum(dout * n, axis=0, keepdims=True), i)

    return pl.pallas_call(
        body, name="loss_head", grid=(t // tr,),
        in_specs=[_row_spec(tr, d), _row_spec(tr, d), _row_spec(tr, d), _const_spec((1, d))],
        out_specs=[_row_spec(tr, d), _row_spec(tr, d), _const_spec((8, 128)), _const_spec((1, d))],
        out_shape=[jax.ShapeDtypeStruct((t, d), jnp.float32), jax.ShapeDtypeStruct((t, d), MM_DTYPE),
                   jax.ShapeDtypeStruct((8, 128), jnp.float32), jax.ShapeDtypeStruct((1, d), jnp.float32)],
        compiler_params=_params(("arbitrary",)),
    )(ff, x1, target, g_post)


def _ffn_pre_bwd(dhf, x1, dout, mix_out, g_ffn_pre, g_mix_post):
    t, d = dhf.shape
    tr = _tile(t, 256)

    def body(dhf_ref, x1_ref, dout_ref, mo_ref, g3_ref, g2_ref, dx1_ref, dmo_ref, dg3_ref, dg2_ref):
        i = pl.program_id(0)
        dhf_v = dhf_ref[...]
        x1 = x1_ref[...]
        rstd3 = _rstd(x1)
        n3 = x1 * rstd3
        dx1 = dout_ref[...] + _norm_bwd(dhf_v * g3_ref[...], n3, rstd3)
        dx1_ref[...] = dx1
        mo = mo_ref[...]
        rstd2 = _rstd(mo)
        n2 = mo * rstd2
        dmo_ref[...] = _norm_bwd(dx1 * g2_ref[...], n2, rstd2).astype(dmo_ref.dtype)
        _accumulate(dg3_ref, jnp.sum(dhf_v * n3, axis=0, keepdims=True), i)
        _accumulate(dg2_ref, jnp.sum(dx1 * n2, axis=0, keepdims=True), i)

    return pl.pallas_call(
        body, name="ffn_pre_bwd", grid=(t // tr,),
        in_specs=[_row_spec(tr, d)] * 4 + [_const_spec((1, d))] * 2,
        out_specs=[_row_spec(tr, d), _row_spec(tr, d), _const_spec((1, d)), _const_spec((1, d))],
        out_shape=[jax.ShapeDtypeStruct((t, d), jnp.float32), jax.ShapeDtypeStruct((t, d), MM_DTYPE),
                   jax.ShapeDtypeStruct((1, d), jnp.float32), jax.ShapeDtypeStruct((1, d), jnp.float32)],
        compiler_params=_params(("arbitrary",)),
    )(dhf, x1, dout, mix_out, g_ffn_pre, g_mix_post)


def _mix_pre_bwd(dh, x, dx1, g_mix_pre):
    t, d = dh.shape
    tr = _tile(t, 512)

    def body(dh_ref, x_ref, dx1_ref, g_ref, dx_ref, dg_ref):
        i = pl.program_id(0)
        dh_v = dh_ref[...]
        xv = x_ref[...]
        rstd = _rstd(xv)
        n = xv * rstd
        dx_ref[...] = dx1_ref[...] + _norm_bwd(dh_v * g_ref[...], n, rstd)
        _accumulate(dg_ref, jnp.sum(dh_v * n, axis=0, keepdims=True), i)

    return pl.pallas_call(
        body, name="mix_pre_bwd", grid=(t // tr,),
        in_specs=[_row_spec(tr, d)] * 3 + [_const_spec((1, d))],
        out_specs=[_row_spec(tr, d), _const_spec((1, d))],
        out_shape=[jax.ShapeDtypeStruct((t, d), jnp.float32), jax.ShapeDtypeStruct((1, d), jnp.float32)],
        compiler_params=_params(("arbitrary",)),
    )(dh, x, dx1, g_mix_pre)


def _pool_matrix(pool_ref, g):
    return jnp.concatenate([pool_ref[c, g] for c in range(N_CHIPS)], axis=0)


def _inv_count(row0, n, w):
    pos = (row0 + lax.broadcasted_iota(jnp.int32, (n, 1), 0) + 1).astype(jnp.float32)
    return 1.0 / jnp.minimum(pos, float(w))


def _conv_piece(cu_buf, start, n, b_piece, convw_ref):
    conv = None
    for k in range(CONV_K):
        term = convw_ref[k:k + 1, :] * cu_buf[pl.ds(HALO + start + k - (CONV_K - 1), n), :]
        conv = term if conv is None else conv + term
    return conv, b_piece * conv


def _head_stats(a, width):
    return [_rstd(a[:, h * width:(h + 1) * width]) for h in range(a.shape[1] // width)]


def _pooled_piece(v_buf, start, n, v_piece, row0, dg):
    outs = []
    for gi, w in enumerate(POOL_WINDOWS):
        cols = slice(gi * dg, (gi + 1) * dg)
        win = None
        for k in range(w):
            term = v_buf[pl.ds(HALO + start - k, n), cols]
            win = term if win is None else win + term
        outs.append(win * _inv_count(row0 + start, n, w) - v_piece[:, cols])
    return outs


def _halo_specs(t, tr, width, col):
    per = tr // HALO
    last = t // HALO - 1
    prev = pl.BlockSpec((HALO, width), lambda i: (jnp.maximum(i * per - 1, 0), col))
    nxt = pl.BlockSpec((HALO, width), lambda i: (jnp.minimum((i + 1) * per, last), col))
    return prev, nxt


def _mixers_fwd(proj, conv_w, pool_g, pool_scale):
    t, e = proj.shape
    cw = e // 4
    dg = pool_g.shape[-1]
    tr = _tile(t, 256)

    def main(col):
        return pl.BlockSpec((tr, cw), lambda i: (i, col))

    def body(b_ref, c_ref, u_ref, v_ref, cp_ref, up_ref, vp_ref, convw_ref, pool_ref, scale_ref,
             out_ref, cu_buf, v_buf):
        i = pl.program_id(0)
        keep = (i > 0).astype(jnp.float32)
        cu_buf[pl.ds(0, HALO), :] = cp_ref[...] * up_ref[...] * keep
        cu_buf[pl.ds(HALO, tr), :] = c_ref[...] * u_ref[...]
        v_buf[pl.ds(0, HALO), :] = vp_ref[...] * keep
        v_buf[pl.ds(HALO, tr), :] = v_ref[...]
        _, a = _conv_piece(cu_buf, 0, tr, b_ref[...], convw_ref)
        for h, rstd in enumerate(_head_stats(a, CONV_HEAD_DIM)):
            cols = slice(h * CONV_HEAD_DIM, (h + 1) * CONV_HEAD_DIM)
            out_ref[:, cols] = (a[:, cols] * rstd).astype(out_ref.dtype)
        pooled = _pooled_piece(v_buf, 0, tr, v_ref[...], i * tr, dg)
        for gi, p in enumerate(pooled):
            z = _dot(p, _pool_matrix(pool_ref, gi), "nn")
            cols = slice(gi * dg, (gi + 1) * dg)
            out_ref[:, cw + gi * dg:cw + (gi + 1) * dg] = (z * _rstd(z) * scale_ref[:, cols]).astype(out_ref.dtype)

    prev_c, _ = _halo_specs(t, tr, cw, 1)
    prev_u, _ = _halo_specs(t, tr, cw, 2)
    prev_v, _ = _halo_specs(t, tr, cw, 3)
    return pl.pallas_call(
        body, name="mixers_fwd", grid=(t // tr,),
        in_specs=[main(0), main(1), main(2), main(3), prev_c, prev_u, prev_v,
                  _const_spec(conv_w.shape), _const_spec(pool_g.shape), _const_spec(pool_scale.shape)],
        out_specs=_row_spec(tr, 2 * cw),
        out_shape=jax.ShapeDtypeStruct((t, 2 * cw), MM_DTYPE),
        scratch_shapes=[pltpu.VMEM((tr + HALO, cw), jnp.float32), pltpu.VMEM((tr + HALO, cw), jnp.float32)],
        compiler_params=_params(("arbitrary",)),
    )(proj, proj, proj, proj, proj, proj, proj, conv_w, pool_g, pool_scale)


def _mixers_bwd(proj, dmixed, conv_w, pool_g, pool_scale):
    t, e = proj.shape
    cw = e // 4
    dg = pool_g.shape[-1]
    n_groups = len(POOL_WINDOWS)
    tr = _tile(t, 256)
    n_tiles = t // tr
    ext = tr + 2 * HALO

    def main(col):
        return pl.BlockSpec((tr, cw), lambda i: (i, col))

    def body(b_ref, c_ref, u_ref, v_ref, dyc_ref, dyp_ref,
             cp_ref, up_ref, vp_ref,
             bn_ref, cn_ref, un_ref, vn_ref, dycn_ref, dypn_ref,
             convw_ref, pool_ref, scale_ref,
             dproj_ref, dconvw_ref, dpool_ref, dscale_ref,
             cu_buf, v_buf, dconv_buf, dpn_buf, dpooled_buf):
        i = pl.program_id(0)
        keep_prev = (i > 0).astype(jnp.float32)
        keep_next = (i < n_tiles - 1).astype(jnp.float32)
        cu_buf[pl.ds(0, HALO), :] = cp_ref[...] * up_ref[...] * keep_prev
        cu_buf[pl.ds(HALO, tr), :] = c_ref[...] * u_ref[...]
        cu_buf[pl.ds(HALO + tr, HALO), :] = cn_ref[...] * un_ref[...]
        v_buf[pl.ds(0, HALO), :] = vp_ref[...] * keep_prev
        v_buf[pl.ds(HALO, tr), :] = v_ref[...]
        v_buf[pl.ds(HALO + tr, HALO), :] = vn_ref[...]

        def conv_piece(start, n, b_piece, dyc_piece, keep, is_main):
            conv, a = _conv_piece(cu_buf, start, n, b_piece, convw_ref)
            for h, rstd in enumerate(_head_stats(a, CONV_HEAD_DIM)):
                cols = slice(h * CONV_HEAD_DIM, (h + 1) * CONV_HEAD_DIM)
                da = _norm_bwd(dyc_piece[:, cols], a[:, cols] * rstd, rstd)
                dconv_buf[pl.ds(start, n), cols] = da * b_piece[:, cols] * keep
                if is_main:
                    dproj_ref[:, cols] = (da * conv[:, cols]).astype(dproj_ref.dtype)

        conv_piece(0, tr, b_ref[...], dyc_ref[...], 1.0, True)
        conv_piece(tr, HALO, bn_ref[...], dycn_ref[...], keep_next, False)

        dconv_main = dconv_buf[pl.ds(0, tr), :]
        dcu = None
        dw_rows = []
        for k in range(CONV_K):
            shift = CONV_K - 1 - k
            term = convw_ref[k:k + 1, :] * dconv_buf[pl.ds(shift, tr), :]
            dcu = term if dcu is None else dcu + term
            dw_rows.append(jnp.sum(dconv_main * cu_buf[pl.ds(HALO - shift, tr), :], axis=0, keepdims=True))
        dproj_ref[:, cw:2 * cw] = (dcu * u_ref[...]).astype(dproj_ref.dtype)
        dproj_ref[:, 2 * cw:3 * cw] = (dcu * c_ref[...]).astype(dproj_ref.dtype)
        _accumulate(dconvw_ref, jnp.concatenate(dw_rows, axis=0), i)

        def pool_piece(start, n, v_piece, dyp_piece, keep, is_main):
            pooled = _pooled_piece(v_buf, start, n, v_piece, i * tr, dg)
            dscale, dmats = [], []
            for gi, w in enumerate(POOL_WINDOWS):
                cols = slice(gi * dg, (gi + 1) * dg)
                mat = _pool_matrix(pool_ref, gi)
                z = _dot(pooled[gi], mat, "nn")
                rstd = _rstd(z)
                nz = z * rstd
                dyp_g = dyp_piece[:, cols]
                dz = _norm_bwd(dyp_g * scale_ref[:, cols], nz, rstd)
                dpooled = _dot(dz, mat, "nt") * keep
                dpn_buf[pl.ds(start, n), cols] = dpooled * _inv_count(i * tr + start, n, w)
                if is_main:
                    dpooled_buf[:, cols] = dpooled
                    dscale.append(jnp.sum(dyp_g * nz, axis=0, keepdims=True))
                    dmats.append(_dot(pooled[gi], dz, "tn"))
            return dscale, dmats

        dscale, dmats = pool_piece(0, tr, v_ref[...], dyp_ref[...], 1.0, True)
        pool_piece(tr, HALO, vn_ref[...], dypn_ref[...], keep_next, False)
        for gi, w in enumerate(POOL_WINDOWS):
            cols = slice(gi * dg, (gi + 1) * dg)
            back = None
            for k in range(w):
                term = dpn_buf[pl.ds(k, tr), cols]
                back = term if back is None else back + term
            dproj_ref[:, 3 * cw + gi * dg:3 * cw + (gi + 1) * dg] = (back - dpooled_buf[:, cols]).astype(dproj_ref.dtype)
        _accumulate(dscale_ref, jnp.concatenate(dscale, axis=1), i)
        rows = dg // N_CHIPS
        for gi in range(n_groups):
            for c in range(N_CHIPS):
                _accumulate(dpool_ref.at[c, gi], dmats[gi][c * rows:(c + 1) * rows, :], i)

    prev_c, next_c = _halo_specs(t, tr, cw, 1)
    prev_u, next_u = _halo_specs(t, tr, cw, 2)
    prev_v, next_v = _halo_specs(t, tr, cw, 3)
    _, next_b = _halo_specs(t, tr, cw, 0)
    _, next_dyc = _halo_specs(t, tr, cw, 0)
    _, next_dyp = _halo_specs(t, tr, cw, 1)
    return pl.pallas_call(
        body, name="mixers_bwd", grid=(n_tiles,),
        in_specs=[main(0), main(1), main(2), main(3), main(0), main(1),
                  prev_c, prev_u, prev_v,
                  next_b, next_c, next_u, next_v, next_dyc, next_dyp,
                  _const_spec(conv_w.shape), _const_spec(pool_g.shape), _const_spec(pool_scale.shape)],
        out_specs=[_row_spec(tr, e), _const_spec(conv_w.shape), _const_spec(pool_g.shape),
                   _const_spec(pool_scale.shape)],
        out_shape=[jax.ShapeDtypeStruct((t, e), MM_DTYPE), jax.ShapeDtypeStruct(conv_w.shape, jnp.float32),
                   jax.ShapeDtypeStruct(pool_g.shape, jnp.float32),
                   jax.ShapeDtypeStruct(pool_scale.shape, jnp.float32)],
        scratch_shapes=[pltpu.VMEM((ext, cw), jnp.float32), pltpu.VMEM((ext, cw), jnp.float32),
                        pltpu.VMEM((tr + HALO, cw), jnp.float32), pltpu.VMEM((tr + HALO, cw), jnp.float32),
                        pltpu.VMEM((tr, cw), jnp.float32)],
        compiler_params=_params(("arbitrary",)),
    )(proj, proj, proj, proj, dmixed, dmixed,
      proj, proj, proj,
      proj, proj, proj, proj, dmixed, dmixed,
      conv_w, pool_g, pool_scale)


def _cast_rows(name, w, dtype):
    r, c = w.shape
    tr = _tile(r, 512)

    def body(w_ref, o_ref):
        o_ref[...] = w_ref[...].astype(o_ref.dtype)

    return pl.pallas_call(
        body, name=name, grid=(r // tr,), in_specs=[_row_spec(tr, c)], out_specs=_row_spec(tr, c),
        out_shape=jax.ShapeDtypeStruct((r, c), dtype), compiler_params=_params(("arbitrary",)),
    )(w)


def _sum_slots(name, slots):
    n, r, c = slots.shape
    tr = _tile(r, 256)

    def body(s_ref, o_ref):
        tot = s_ref[0].astype(jnp.float32)
        for s in range(1, n):
            tot = tot + s_ref[s].astype(jnp.float32)
        o_ref[...] = tot

    return pl.pallas_call(
        body, name=name, grid=(r // tr,),
        in_specs=[pl.BlockSpec((n, tr, c), lambda i: (0, i, 0))], out_specs=_row_spec(tr, c),
        out_shape=jax.ShapeDtypeStruct((r, c), jnp.float32), compiler_params=_params(("arbitrary",)),
    )(slots)


def _adamw_math(w, g, m, v):
    m = ADAM_B1 * m + (1.0 - ADAM_B1) * g
    v = ADAM_B2 * v + (1.0 - ADAM_B2) * (g * g)
    m_hat = m / (1.0 - ADAM_B1 ** ADAM_STEP)
    v_hat = v / (1.0 - ADAM_B2 ** ADAM_STEP)
    delta = -ADAM_LR * (m_hat / (jnp.sqrt(v_hat) + ADAM_EPS) + ADAM_WD * w)
    return delta, m, v


def _adamw(name, w, m, v, grad_parts):
    r, c = w.shape
    tr = _tile(r, 256)
    n_parts = len(grad_parts)

    def body(*refs):
        w_ref, m_ref, v_ref = refs[:3]
        part_refs = refs[3:3 + n_parts]
        g_ref, d_ref, nm_ref, nv_ref = refs[3 + n_parts:]
        g = part_refs[0][...]
        for p in part_refs[1:]:
            g = g + p[...]
        delta, nm, nv = _adamw_math(w_ref[...], g, m_ref[...], v_ref[...])
        g_ref[...] = g
        d_ref[...] = delta
        nm_ref[...] = nm
        nv_ref[...] = nv

    spec = _row_spec(tr, c)
    out = jax.ShapeDtypeStruct((r, c), jnp.float32)
    return pl.pallas_call(
        body, name=name, grid=(r // tr,), in_specs=[spec] * (3 + n_parts), out_specs=[spec] * 4,
        out_shape=[out] * 4, compiler_params=_params(("arbitrary",)),
    )(w, m, v, *grad_parts)


def _chip_peers():
    x, y, c = lax.axis_index("x"), lax.axis_index("y"), lax.axis_index("c")
    return x, y, c, [(1 - x, y), (x, 1 - y), (1 - x, 1 - y)]


def _gather_chips(name, shards):
    n = len(shards)

    def body(*refs):
        in_refs, out_refs = refs[:n], refs[n:2 * n]
        send_sems, recv_sems, local_sems = refs[2 * n:]
        x, y, c, peers = _chip_peers()
        me = 2 * x + y
        local, sends, recvs = [], [], []
        for a in range(n):
            cp = pltpu.make_async_copy(in_refs[a], out_refs[a].at[me], local_sems.at[a])
            cp.start()
            local.append(cp)
            for p, (px, py) in enumerate(peers):
                k = 3 * a + p
                cp = pltpu.make_async_remote_copy(
                    src_ref=in_refs[a], dst_ref=out_refs[a].at[me], send_sem=send_sems.at[k],
                    recv_sem=recv_sems.at[k], device_id=(px, py, c), device_id_type=MESH)
                cp.start()
                sends.append(cp)
                recvs.append(pltpu.make_async_remote_copy(
                    src_ref=in_refs[a], dst_ref=out_refs[a].at[2 * px + py], send_sem=send_sems.at[k],
                    recv_sem=recv_sems.at[k], device_id=(px, py, c), device_id_type=MESH))
        for cp in recvs:
            cp.wait_recv()
        for cp in sends:
            cp.wait_send()
        for cp in local:
            cp.wait()

    return pl.pallas_call(
        body, name=name, in_specs=[ANY] * n, out_specs=[ANY] * n,
        out_shape=[jax.ShapeDtypeStruct((N_CHIPS,) + s.shape, s.dtype) for s in shards],
        scratch_shapes=[pltpu.SemaphoreType.DMA((3 * n,)), pltpu.SemaphoreType.DMA((3 * n,)),
                        pltpu.SemaphoreType.DMA((n,))],
    )(*shards)


def _scatter_chips(name, grads):
    n = len(grads)

    def body(*refs):
        in_refs, out_refs = refs[:n], refs[n:2 * n]
        send_sems, recv_sems, local_sems = refs[2 * n:]
        x, y, c, peers = _chip_peers()
        me = 2 * x + y
        local, sends, recvs = [], [], []
        for a in range(n):
            cp = pltpu.make_async_copy(in_refs[a].at[me], out_refs[a].at[me], local_sems.at[a])
            cp.start()
            local.append(cp)
            for p, (px, py) in enumerate(peers):
                k = 3 * a + p
                cp = pltpu.make_async_remote_copy(
                    src_ref=in_refs[a].at[2 * px + py], dst_ref=out_refs[a].at[me], send_sem=send_sems.at[k],
                    recv_sem=recv_sems.at[k], device_id=(px, py, c), device_id_type=MESH)
                cp.start()
                sends.append(cp)
                recvs.append(pltpu.make_async_remote_copy(
                    src_ref=in_refs[a].at[me], dst_ref=out_refs[a].at[2 * px + py], send_sem=send_sems.at[k],
                    recv_sem=recv_sems.at[k], device_id=(px, py, c), device_id_type=MESH))
        for cp in recvs:
            cp.wait_recv()
        for cp in sends:
            cp.wait_send()
        for cp in local:
            cp.wait()

    return pl.pallas_call(
        body, name=name, in_specs=[ANY] * n, out_specs=[ANY] * n,
        out_shape=[jax.ShapeDtypeStruct(g.shape, g.dtype) for g in grads],
        scratch_shapes=[pltpu.SemaphoreType.DMA((3 * n,)), pltpu.SemaphoreType.DMA((3 * n,)),
                        pltpu.SemaphoreType.DMA((n,))],
    )(*grads)


def _swap_sibling(name, parts):
    n = len(parts)

    def body(*refs):
        in_refs, out_refs = refs[:n], refs[n:2 * n]
        send_sems, recv_sems = refs[2 * n:]
        x, y, c = lax.axis_index("x"), lax.axis_index("y"), lax.axis_index("c")
        copies = []
        for a in range(n):
            cp = pltpu.make_async_remote_copy(
                src_ref=in_refs[a], dst_ref=out_refs[a], send_sem=send_sems.at[a], recv_sem=recv_sems.at[a],
                device_id=(x, y, 1 - c), device_id_type=MESH)
            cp.start()
            copies.append(cp)
        for cp in copies:
            cp.wait_recv()
        for cp in copies:
            cp.wait_send()

    return pl.pallas_call(
        body, name=name, in_specs=[ANY] * n, out_specs=[ANY] * n,
        out_shape=[jax.ShapeDtypeStruct(p.shape, p.dtype) for p in parts],
        scratch_shapes=[pltpu.SemaphoreType.DMA((n,)), pltpu.SemaphoreType.DMA((n,))],
    )(*parts)


def _gather_devices(name, block):
    def body(in_ref, out_ref, send_sems, recv_sems, local_sem):
        x, y, c = lax.axis_index("x"), lax.axis_index("y"), lax.axis_index("c")
        me = 4 * x + 2 * y + c
        local = pltpu.make_async_copy(in_ref, out_ref.at[me], local_sem)
        local.start()
        sends, recvs = [], []
        k = 0
        for fx in range(2):
            for fy in range(2):
                for fc in range(2):
                    if fx == fy == fc == 0:
                        continue
                    px = x if fx == 0 else 1 - x
                    py = y if fy == 0 else 1 - y
                    pc = c if fc == 0 else 1 - c
                    cp = pltpu.make_async_remote_copy(
                        src_ref=in_ref, dst_ref=out_ref.at[me], send_sem=send_sems.at[k], recv_sem=recv_sems.at[k],
                        device_id=(px, py, pc), device_id_type=MESH)
                    cp.start()
                    sends.append(cp)
                    recvs.append(pltpu.make_async_remote_copy(
                        src_ref=in_ref, dst_ref=out_ref.at[4 * px + 2 * py + pc], send_sem=send_sems.at[k],
                        recv_sem=recv_sems.at[k], device_id=(px, py, pc), device_id_type=MESH))
                    k += 1
        for cp in recvs:
            cp.wait_recv()
        for cp in sends:
            cp.wait_send()
        local.wait()

    return pl.pallas_call(
        body, name=name, in_specs=[ANY], out_specs=ANY,
        out_shape=jax.ShapeDtypeStruct((N_DEV,) + block.shape, block.dtype),
        scratch_shapes=[pltpu.SemaphoreType.DMA((N_DEV - 1,)), pltpu.SemaphoreType.DMA((N_DEV - 1,)),
                        pltpu.SemaphoreType.DMA],
    )(block)


def _pack_rows(pieces, width):
    flat = jnp.concatenate([p.reshape(-1) for p in pieces])
    rows = -(-flat.shape[0] // width)
    rows = -(-rows // 8) * 8
    flat = jnp.pad(flat, (0, rows * width - flat.shape[0]))
    return flat.reshape(rows, width)


def _unpack_rows(packed, shapes):
    flat = packed.reshape(-1)
    out, off = [], 0
    for s in shapes:
        size = 1
        for d in s:
            size *= d
        out.append(flat[off:off + size].reshape(s))
        off += size
    return out


def kernel(x, ln_mix_pre, w_in, conv_w, pool_w, pool_scale, w_out, ln_mix_post, ln_ffn_pre, w_gate, w_up, w_down, ln_ffn_post, loss_target, m_ln_mix_pre, m_w_in, m_conv_w, m_pool_w, m_pool_scale, m_w_out, m_ln_mix_post, m_ln_ffn_pre, m_w_gate, m_w_up, m_w_down, m_ln_ffn_post, v_ln_mix_pre, v_w_in, v_conv_w, v_pool_w, v_pool_scale, v_w_out, v_ln_mix_post, v_ln_ffn_pre, v_w_gate, v_w_up, v_w_down, v_ln_ffn_post):
    t, d = x.shape[1], x.shape[2]
    e4 = w_in.shape[2]
    e = N_CHIPS * e4
    f4 = w_gate.shape[2]
    f = N_CHIPS * f4
    n_groups, dg4, dg = pool_w.shape[1], pool_w.shape[2], pool_w.shape[3]
    cw4 = conv_w.shape[2]
    chip = 2 * lax.axis_index("x") + lax.axis_index("y")
    xs, tgt = x[0], loss_target[0]

    big = {"w_in": w_in[0], "w_out": w_out[0], "w_gate": w_gate[0], "w_up": w_up[0], "w_down": w_down[0],
           "pool_w": pool_w[0].reshape(n_groups * dg4, dg)}
    names = list(big)
    shards = [_cast_rows("cast_" + k, big[k], MM_DTYPE) for k in names]
    gathered = dict(zip(names, _gather_chips("gather_weights", shards)))
    win_g, wg_g, wu_g = gathered["w_in"], gathered["w_gate"], gathered["w_up"]
    wout_full = gathered["w_out"].reshape(d, d)
    wdown_full = gathered["w_down"].reshape(f, d)
    pool_g = gathered["pool_w"].reshape(N_CHIPS, n_groups, dg4, dg)
    conv_all = _gather_devices("gather_conv_w", _pack_rows([conv_w[0]], 128))
    conv_full = jnp.concatenate(
        [conv_all[2 * j].reshape(-1)[:CONV_K * cw4].reshape(CONV_K, cw4) for j in range(N_CHIPS)], axis=1)

    h = _pre_norm(xs, ln_mix_pre)
    tm = _tile(t, 1024)
    proj = _matmul(
        "in_proj", grid=(t // tm, N_CHIPS, 1), mode="nn",
        pairs=[[(h, win_g)]],
        pair_specs=[[(pl.BlockSpec((tm, d), lambda i, j, k: (i, 0)),
                      pl.BlockSpec((None, d, e4), lambda i, j, k: (j, 0, 0)))]],
        acc_shapes=[(tm, e4)], out_shapes=[jax.ShapeDtypeStruct((t, e), jnp.float32)],
        out_specs=[pl.BlockSpec((tm, e4), lambda i, j, k: (i, j))], epilogue=_identity_epilogue)[0]
    mixed = _mixers_fwd(proj, conv_full, pool_g, pool_scale)
    tn = _tile(d, 1024)
    mix_out = _matmul(
        "out_proj", grid=(t // tm, d // tn, 1), mode="nn",
        pairs=[[(mixed, wout_full)]],
        pair_specs=[[(pl.BlockSpec((tm, d), lambda i, j, k: (i, 0)),
                      pl.BlockSpec((d, tn), lambda i, j, k: (0, j)))]],
        acc_shapes=[(tm, tn)], out_shapes=[jax.ShapeDtypeStruct((t, d), jnp.float32)],
        out_specs=[pl.BlockSpec((tm, tn), lambda i, j, k: (i, j))], epilogue=_identity_epilogue)[0]
    x1, hf = _post_mix(xs, mix_out, ln_mix_post, ln_ffn_pre)

    def gate_up_epilogue(accs, extra_refs):
        g, up = accs
        return g, up, g * jax.nn.sigmoid(g) * up

    tm_ff = _tile(t, 512)
    ff_tile = jax.ShapeDtypeStruct((t, f), MM_DTYPE)
    g_act, up_act, act = _matmul(
        "gate_up", grid=(N_CHIPS, t // tm_ff, 1), mode="nn",
        pairs=[[(hf, wg_g)], [(hf, wu_g)]],
        pair_specs=[[(pl.BlockSpec((tm_ff, d), lambda j, i, k: (i, 0)),
                      pl.BlockSpec((None, d, f4), lambda j, i, k: (j, 0, 0)))]] * 2,
        acc_shapes=[(tm_ff, f4)] * 2, out_shapes=[ff_tile] * 3,
        out_specs=[pl.BlockSpec((tm_ff, f4), lambda j, i, k: (i, j))] * 3, epilogue=gate_up_epilogue)
    ff = _matmul(
        "down_proj", grid=(t // tm, d // tn, N_CHIPS), mode="nn",
        pairs=[[(act, wdown_full)]],
        pair_specs=[[(pl.BlockSpec((tm, f4), lambda i, j, k: (i, k)),
                      pl.BlockSpec((f4, tn), lambda i, j, k: (k, j)))]],
        acc_shapes=[(tm, tn)], out_shapes=[jax.ShapeDtypeStruct((t, d), jnp.float32)],
        out_specs=[pl.BlockSpec((tm, tn), lambda i, j, k: (i, j))], epilogue=_identity_epilogue)[0]
    dout, dff, loss_tile, dg_ffn_post = _loss_head(ff, x1, tgt, ln_ffn_post)

    def dact_epilogue(accs, extra_refs):
        dact = accs[0]
        g = extra_refs[0][...].astype(jnp.float32)
        up = extra_refs[1][...].astype(jnp.float32)
        sig = jax.nn.sigmoid(g)
        silu = g * sig
        return dact * up * (sig + silu * (1.0 - sig)), dact * silu

    ff_spec_ji = pl.BlockSpec((tm_ff, f4), lambda j, i, k: (i, j))
    dg_act, dup_act = _matmul(
        "dact", grid=(N_CHIPS, t // tm_ff, 1), mode="nt",
        pairs=[[(dff, wdown_full)]],
        pair_specs=[[(pl.BlockSpec((tm_ff, d), lambda j, i, k: (i, 0)),
                      pl.BlockSpec((f4, d), lambda j, i, k: (j, 0)))]],
        acc_shapes=[(tm_ff, f4)], extras=[g_act, up_act], extra_specs=[ff_spec_ji, ff_spec_ji],
        out_shapes=[ff_tile] * 2, out_specs=[ff_spec_ji] * 2, epilogue=dact_epilogue)
    tk = _tile(t, 1024)
    dw_down = _matmul(
        "dw_down", grid=(N_CHIPS, d // tn, t // tk), mode="tn",
        pairs=[[(act, dff)]],
        pair_specs=[[(pl.BlockSpec((tk, f4), lambda i, j, k: (k, i)),
                      pl.BlockSpec((tk, tn), lambda i, j, k: (k, j)))]],
        acc_shapes=[(f4, tn)], out_shapes=[jax.ShapeDtypeStruct((N_CHIPS, f4, d), COMM_DTYPE)],
        out_specs=[pl.BlockSpec((None, f4, tn), lambda i, j, k: (i, 0, j))], epilogue=_identity_epilogue)[0]
    dhf = _matmul(
        "dhf", grid=(t // tm, d // tn, N_CHIPS), mode="nt",
        pairs=[[(dg_act, wg_g), (dup_act, wu_g)]],
        pair_specs=[[(pl.BlockSpec((tm, f4), lambda i, j, k: (i, k)),
                      pl.BlockSpec((None, tn, f4), lambda i, j, k: (k, j, 0)))] * 2],
        acc_shapes=[(tm, tn)], out_shapes=[jax.ShapeDtypeStruct((t, d), jnp.float32)],
        out_specs=[pl.BlockSpec((tm, tn), lambda i, j, k: (i, j))], epilogue=_identity_epilogue)[0]
    tmo = _tile(d, 1024)
    slot_ff = jax.ShapeDtypeStruct((N_CHIPS, d, f4), COMM_DTYPE)
    dw_gate, dw_up = _matmul(
        "dw_gate_up", grid=(d // tmo, N_CHIPS, t // tk), mode="tn",
        pairs=[[(hf, dg_act)], [(hf, dup_act)]],
        pair_specs=[[(pl.BlockSpec((tk, tmo), lambda i, j, k: (k, i)),
                      pl.BlockSpec((tk, f4), lambda i, j, k: (k, j)))]] * 2,
        acc_shapes=[(tmo, f4)] * 2, out_shapes=[slot_ff] * 2,
        out_specs=[pl.BlockSpec((None, tmo, f4), lambda i, j, k: (j, i, 0))] * 2, epilogue=_identity_epilogue)
    dx1, dmo, dg_ffn_pre, dg_mix_post = _ffn_pre_bwd(dhf, x1, dout, mix_out, ln_ffn_pre, ln_mix_post)

    dmixed = _matmul(
        "dmixed", grid=(t // tm, d // tn, 1), mode="nt",
        pairs=[[(dmo, wout_full)]],
        pair_specs=[[(pl.BlockSpec((tm, d), lambda i, j, k: (i, 0)),
                      pl.BlockSpec((tn, d), lambda i, j, k: (j, 0)))]],
        acc_shapes=[(tm, tn)], out_shapes=[jax.ShapeDtypeStruct((t, d), jnp.float32)],
        out_specs=[pl.BlockSpec((tm, tn), lambda i, j, k: (i, j))], epilogue=_identity_epilogue)[0]
    dw_out = _matmul(
        "dw_out", grid=(d // tmo, d // tn, t // tk), mode="tn",
        pairs=[[(mixed, dmo)]],
        pair_specs=[[(pl.BlockSpec((tk, tmo), lambda i, j, k: (k, i)),
                      pl.BlockSpec((tk, tn), lambda i, j, k: (k, j)))]],
        acc_shapes=[(tmo, tn)], out_shapes=[jax.ShapeDtypeStruct((d, d), COMM_DTYPE)],
        out_specs=[pl.BlockSpec((tmo, tn), lambda i, j, k: (i, j))], epilogue=_identity_epilogue)[0]
    dproj, dconv_full, dpool_g, dpool_scale = _mixers_bwd(proj, dmixed, conv_full, pool_g, pool_scale)
    dw_in = _matmul(
        "dw_in", grid=(d // tmo, N_CHIPS, t // tk), mode="tn",
        pairs=[[(h, dproj)]],
        pair_specs=[[(pl.BlockSpec((tk, tmo), lambda i, j, k: (k, i)),
                      pl.BlockSpec((tk, e4), lambda i, j, k: (k, j)))]],
        acc_shapes=[(tmo, e4)], out_shapes=[jax.ShapeDtypeStruct((N_CHIPS, d, e4), COMM_DTYPE)],
        out_specs=[pl.BlockSpec((None, tmo, e4), lambda i, j, k: (j, i, 0))], epilogue=_identity_epilogue)[0]
    dh = _matmul(
        "dh", grid=(t // tm, d // tn, N_CHIPS), mode="nt",
        pairs=[[(dproj, win_g)]],
        pair_specs=[[(pl.BlockSpec((tm, e4), lambda i, j, k: (i, k)),
                      pl.BlockSpec((None, tn, e4), lambda i, j, k: (k, j, 0)))]],
        acc_shapes=[(tm, tn)], out_shapes=[jax.ShapeDtypeStruct((t, d), jnp.float32)],
        out_specs=[pl.BlockSpec((tm, tn), lambda i, j, k: (i, j))], epilogue=_identity_epilogue)[0]
    grad_x, dg_mix_pre = _mix_pre_bwd(dh, xs, dx1, ln_mix_pre)

    dpool_slots = _cast_rows("cast_dpool", dpool_g.reshape(N_CHIPS * n_groups * dg4, dg), COMM_DTYPE)
    slot_arrays = {
        "w_in": dw_in, "w_out": dw_out.reshape(N_CHIPS, d // N_CHIPS, d), "w_gate": dw_gate, "w_up": dw_up,
        "w_down": dw_down, "pool_w": dpool_slots.reshape(N_CHIPS, n_groups * dg4, dg)}
    received = _scatter_chips("scatter_grads", [slot_arrays[k] for k in names])
    partial = [_sum_slots("sum_" + k, r) for k, r in zip(names, received)]
    other = _swap_sibling("swap_grads", partial)
    moments = {"w_in": (m_w_in, v_w_in), "w_out": (m_w_out, v_w_out), "w_gate": (m_w_gate, v_w_gate),
               "w_up": (m_w_up, v_w_up), "w_down": (m_w_down, v_w_down), "pool_w": (m_pool_w, v_pool_w)}
    result = {}
    for k, mine, theirs in zip(names, partial, other):
        shape = moments[k][0].shape
        two_d = big[k].shape
        outs = _adamw("adamw_" + k, big[k], moments[k][0].reshape(two_d), moments[k][1].reshape(two_d),
                      [mine, theirs])
        result[k] = [o.reshape(shape) for o in outs]

    small_shapes = [(1, d)] * 4 + [pool_scale.shape, (CONV_K, N_CHIPS * cw4)]
    packed = _pack_rows([dg_mix_pre, dg_mix_post, dg_ffn_pre, dg_ffn_post, dpool_scale, dconv_full], 1024)
    summed = _sum_slots("sum_small", _gather_devices("gather_small", packed))
    g_mix_pre, g_mix_post, g_ffn_pre, g_ffn_post, g_pool_scale, g_conv_full = _unpack_rows(summed, small_shapes)
    g_conv = lax.dynamic_slice(g_conv_full, (0, chip * cw4), (CONV_K, cw4))[None]
    small = [("ln_mix_pre", ln_mix_pre, m_ln_mix_pre, v_ln_mix_pre, g_mix_pre),
             ("conv_w", conv_w, m_conv_w, v_conv_w, g_conv),
             ("pool_scale", pool_scale, m_pool_scale, v_pool_scale, g_pool_scale),
             ("ln_mix_post", ln_mix_post, m_ln_mix_post, v_ln_mix_post, g_mix_post),
             ("ln_ffn_pre", ln_ffn_pre, m_ln_ffn_pre, v_ln_ffn_pre, g_ffn_pre),
             ("ln_ffn_post", ln_ffn_post, m_ln_ffn_post, v_ln_ffn_post, g_ffn_post)]
    shapes_small = [s[1].shape for s in small]
    packs = [_pack_rows([s[q] for s in small], 128) for q in (1, 2, 3, 4)]
    outs = _adamw("adamw_small", packs[0], packs[1], packs[2], [packs[3]])
    unpacked = [_unpack_rows(o, shapes_small) for o in outs]
    for idx, s in enumerate(small):
        result[s[0]] = [u[idx] for u in unpacked]

    loss = lax.psum(loss_tile[0, 0], ("x", "y", "c"))
    order = ["ln_mix_pre", "w_in", "conv_w", "pool_w", "pool_scale", "w_out", "ln_mix_post", "ln_ffn_pre",
             "w_gate", "w_up", "w_down", "ln_ffn_post"]
    return (loss, grad_x[None], *[result[k][0] for k in order], *[result[k][1] for k in order],
            *[result[k][2] for k in order], *[result[k][3] for k in order])
```

```python
import functools

import jax
import jax.numpy as jnp
from jax import lax
from jax.experimental import pallas as pl
from jax.experimental.pallas import tpu as pltpu

EPS = 1e-6
CONV_HEAD_DIM = 128
CONV_K = 3
POOL_WINDOWS = (2, 4, 8, 16)
HALO = 16
N_CHIPS = 4
N_DEV = 8

ADAM_LR = 0.001
ADAM_B1 = 0.9
ADAM_B2 = 0.999
ADAM_EPS = 1e-08
ADAM_WD = 0.01
ADAM_STEP = 10

MM_DTYPE = jnp.bfloat16
COMM_DTYPE = jnp.bfloat16
VMEM_LIMIT = 56 * 1024 * 1024
MESH = pl.DeviceIdType.MESH
ANY = pl.BlockSpec(memory_space=pl.ANY)


def _tile(n, pref):
    t = min(pref, n)
    while n % t:
        t //= 2
    return t


def _params(sem):
    return pltpu.CompilerParams(dimension_semantics=sem, vmem_limit_bytes=VMEM_LIMIT)


def _rstd(x):
    return lax.rsqrt(jnp.mean(x * x, axis=-1, keepdims=True) + EPS)


def _norm_bwd(dn, n, rstd):
    return rstd * (dn - n * jnp.mean(dn * n, axis=-1, keepdims=True))


_DOT_DIMS = {
    "nn": (((1,), (0,)), ((), ())),
    "nt": (((1,), (1,)), ((), ())),
    "tn": (((0,), (0,)), ((), ())),
}


def _dot(a, b, mode):
    return lax.dot_general(a.astype(MM_DTYPE), b.astype(MM_DTYPE), _DOT_DIMS[mode],
                           preferred_element_type=jnp.float32)


def _matmul(name, *, grid, mode, pairs, pair_specs, acc_shapes, extras=(), extra_specs=(),
            out_shapes, out_specs, epilogue, deps=()):
    nk = grid[2]
    flat_in, flat_specs, counts = [], [], []
    for ps, ss in zip(pairs, pair_specs):
        counts.append(len(ps))
        for (a, b), (sa, sb) in zip(ps, ss):
            flat_in += [a, b]
            flat_specs += [sa, sb]
    n_pair_refs = len(flat_in)
    n_extra = len(extras)
    n_out = len(out_shapes)
    n_in = n_pair_refs + n_extra + len(deps)

    def body(*refs):
        pair_refs = refs[:n_pair_refs]
        extra_refs = refs[n_pair_refs:n_pair_refs + n_extra]
        out_refs = refs[n_in:n_in + n_out]
        acc_refs = refs[n_in + n_out:]
        k = pl.program_id(2)

        def partial_sums():
            res, p = [], 0
            for cnt in counts:
                tot = None
                for _ in range(cnt):
                    d = _dot(pair_refs[p][...], pair_refs[p + 1][...], mode)
                    tot = d if tot is None else tot + d
                    p += 2
                res.append(tot)
            return res

        def finish(accs):
            outs = epilogue(accs, extra_refs)
            for o_ref, o in zip(out_refs, outs):
                o_ref[...] = o.astype(o_ref.dtype)

        if nk == 1:
            finish(partial_sums())
        else:
            @pl.when(k == 0)
            def _():
                for acc_ref in acc_refs:
                    acc_ref[...] = jnp.zeros_like(acc_ref)

            for acc_ref, s in zip(acc_refs, partial_sums()):
                acc_ref[...] += s

            @pl.when(k == nk - 1)
            def _():
                finish([a[...] for a in acc_refs])

    scratch = [] if nk == 1 else [pltpu.VMEM(s, jnp.float32) for s in acc_shapes]
    return pl.pallas_call(
        body, name=name, grid=grid,
        in_specs=flat_specs + list(extra_specs) + [ANY] * len(deps), out_specs=list(out_specs),
        out_shape=list(out_shapes), scratch_shapes=scratch,
        compiler_params=_params(("arbitrary", "arbitrary", "arbitrary")),
    )(*flat_in, *extras, *deps)


def _identity_epilogue(accs, extra_refs):
    return tuple(accs)


def _row_spec(tr, n):
    return pl.BlockSpec((tr, n), lambda i: (i, 0))


def _const_spec(shape):
    return pl.BlockSpec(shape, lambda i: tuple(0 for _ in shape))


def _accumulate(ref, val, i):
    @pl.when(i == 0)
    def _():
        ref[...] = val

    @pl.when(i > 0)
    def _():
        ref[...] += val


def _pre_norm(x, gain, deps=()):
    t, d = x.shape
    tr = _tile(t, 512)

    def body(x_ref, g_ref, *rest):
        h_ref = rest[-1]
        xv = x_ref[...]
        h_ref[...] = (xv * _rstd(xv) * g_ref[...]).astype(h_ref.dtype)

    return pl.pallas_call(
        body, name="pre_norm", grid=(t // tr,),
        in_specs=[_row_spec(tr, d), _const_spec((1, d))] + [ANY] * len(deps), out_specs=_row_spec(tr, d),
        out_shape=jax.ShapeDtypeStruct((t, d), MM_DTYPE), compiler_params=_params(("arbitrary",)),
    )(x, gain, *deps)


def _post_mix(x, mix_out, g_post, g_ffn_pre):
    t, d = x.shape
    tr = _tile(t, 512)

    def body(x_ref, mo_ref, g2_ref, g3_ref, x1_ref, hf_ref):
        mo = mo_ref[...]
        x1 = x_ref[...] + mo * _rstd(mo) * g2_ref[...]
        x1_ref[...] = x1
        hf_ref[...] = (x1 * _rstd(x1) * g3_ref[...]).astype(hf_ref.dtype)

    return pl.pallas_call(
        body, name="post_mix", grid=(t // tr,),
        in_specs=[_row_spec(tr, d), _row_spec(tr, d), _const_spec((1, d)), _const_spec((1, d))],
        out_specs=[_row_spec(tr, d), _row_spec(tr, d)],
        out_shape=[jax.ShapeDtypeStruct((t, d), jnp.float32), jax.ShapeDtypeStruct((t, d), MM_DTYPE)],
        compiler_params=_params(("arbitrary",)),
    )(x, mix_out, g_post, g_ffn_pre)


def _loss_head(ff, x1, target, g_post):
    t, d = ff.shape
    tr = _tile(t, 512)

    def body(ff_ref, x1_ref, tg_ref, g_ref, dout_ref, dff_ref, loss_ref, dg_ref):
        i = pl.program_id(0)
        ff_v = ff_ref[...]
        rstd = _rstd(ff_v)
        n = ff_v * rstd
        g = g_ref[...]
        err = x1_ref[...] + n * g - tg_ref[...]
        tile_loss = 0.5 * jnp.sum(jnp.mean(err * err, axis=-1, keepdims=True), axis=0, keepdims=True)
        dout = err / d
        dout_ref[...] = dout
        dff_ref[...] = _norm_bwd(dout * g, n, rstd).astype(dff_ref.dtype)
        _accumulate(loss_ref, jnp.broadcast_to(tile_loss, loss_ref.shape), i)
        _accumulate(dg_ref, jnp.sum(dout * n, axis=0, keepdims=True), i)

    return pl.pallas_call(
        body, name="loss_head", grid=(t // tr,),
        in_specs=[_row_spec(tr, d), _row_spec(tr, d), _row_spec(tr, d), _const_spec((1, d))],
        out_specs=[_row_spec(tr, d), _row_spec(tr, d), _const_spec((8, 128)), _const_spec((1, d))],
        out_shape=[jax.ShapeDtypeStruct((t, d), jnp.float32), jax.ShapeDtypeStruct((t, d), MM_DTYPE),
                   jax.ShapeDtypeStruct((8, 128), jnp.float32), jax.ShapeDtypeStruct((1, d), jnp.float32)],
        compiler_params=_params(("arbitrary",)),
    )(ff, x1, target, g_post)


def _ffn_pre_bwd(dhf, x1, dout, mix_out, g_ffn_pre, g_mix_post, deps=()):
    t, d = dhf.shape
    tr = _tile(t, 256)

    def body(dhf_ref, x1_ref, dout_ref, mo_ref, g3_ref, g2_ref, *rest):
        dx1_ref, dmo_ref, dg3_ref, dg2_ref = rest[len(deps):]
        i = pl.program_id(0)
        dhf_v = dhf_ref[...]
        x1 = x1_ref[...]
        rstd3 = _rstd(x1)
        n3 = x1 * rstd3
        dx1 = dout_ref[...] + _norm_bwd(dhf_v * g3_ref[...], n3, rstd3)
        dx1_ref[...] = dx1
        mo = mo_ref[...]
        rstd2 = _rstd(mo)
        n2 = mo * rstd2
        dmo_ref[...] = _norm_bwd(dx1 * g2_ref[...], n2, rstd2).astype(dmo_ref.dtype)
        _accumulate(dg3_ref, jnp.sum(dhf_v * n3, axis=0, keepdims=True), i)
        _accumulate(dg2_ref, jnp.sum(dx1 * n2, axis=0, keepdims=True), i)

    return pl.pallas_call(
        body, name="ffn_pre_bwd", grid=(t // tr,),
        in_specs=[_row_spec(tr, d)] * 4 + [_const_spec((1, d))] * 2 + [ANY] * len(deps),
        out_specs=[_row_spec(tr, d), _row_spec(tr, d), _const_spec((1, d)), _const_spec((1, d))],
        out_shape=[jax.ShapeDtypeStruct((t, d), jnp.float32), jax.ShapeDtypeStruct((t, d), MM_DTYPE),
                   jax.ShapeDtypeStruct((1, d), jnp.float32), jax.ShapeDtypeStruct((1, d), jnp.float32)],
        compiler_params=_params(("arbitrary",)),
    )(dhf, x1, dout, mix_out, g_ffn_pre, g_mix_post, *deps)


def _mix_pre_bwd(dh, x, dx1, g_mix_pre):
    t, d = dh.shape
    tr = _tile(t, 512)

    def body(dh_ref, x_ref, dx1_ref, g_ref, dx_ref, dg_ref):
        i = pl.program_id(0)
        dh_v = dh_ref[...]
        xv = x_ref[...]
        rstd = _rstd(xv)
        n = xv * rstd
        dx_ref[...] = dx1_ref[...] + _norm_bwd(dh_v * g_ref[...], n, rstd)
        _accumulate(dg_ref, jnp.sum(dh_v * n, axis=0, keepdims=True), i)

    return pl.pallas_call(
        body, name="mix_pre_bwd", grid=(t // tr,),
        in_specs=[_row_spec(tr, d)] * 3 + [_const_spec((1, d))],
        out_specs=[_row_spec(tr, d), _const_spec((1, d))],
        out_shape=[jax.ShapeDtypeStruct((t, d), jnp.float32), jax.ShapeDtypeStruct((1, d), jnp.float32)],
        compiler_params=_params(("arbitrary",)),
    )(dh, x, dx1, g_mix_pre)


def _pool_matrix(pool_ref, g):
    return jnp.concatenate([pool_ref[c, g] for c in range(N_CHIPS)], axis=0)


def _inv_count(row0, n, w):
    pos = (row0 + lax.broadcasted_iota(jnp.int32, (n, 1), 0) + 1).astype(jnp.float32)
    return 1.0 / jnp.minimum(pos, float(w))


def _conv_piece(cu_buf, start, n, b_piece, convw_ref):
    conv = None
    for k in range(CONV_K):
        term = convw_ref[k:k + 1, :] * cu_buf[pl.ds(HALO + start + k - (CONV_K - 1), n), :]
        conv = term if conv is None else conv + term
    return conv, b_piece * conv


def _head_stats(a, width):
    return [_rstd(a[:, h * width:(h + 1) * width]) for h in range(a.shape[1] // width)]


def _pooled_piece(v_buf, start, n, v_piece, row0, dg):
    outs = []
    for gi, w in enumerate(POOL_WINDOWS):
        cols = slice(gi * dg, (gi + 1) * dg)
        win = None
        for k in range(w):
            term = v_buf[pl.ds(HALO + start - k, n), cols]
            win = term if win is None else win + term
        outs.append(win * _inv_count(row0 + start, n, w) - v_piece[:, cols])
    return outs


def _halo_specs(t, tr, width, col):
    per = tr // HALO
    last = t // HALO - 1
    prev = pl.BlockSpec((HALO, width), lambda i: (jnp.maximum(i * per - 1, 0), col))
    nxt = pl.BlockSpec((HALO, width), lambda i: (jnp.minimum((i + 1) * per, last), col))
    return prev, nxt


def _mixers_fwd(proj, conv_w, pool_g, pool_scale):
    t, e = proj.shape
    cw = e // 4
    dg = pool_g.shape[-1]
    tr = _tile(t, 256)

    def main(col):
        return pl.BlockSpec((tr, cw), lambda i: (i, col))

    def body(b_ref, c_ref, u_ref, v_ref, cp_ref, up_ref, vp_ref, convw_ref, pool_ref, scale_ref,
             out_ref, cu_buf, v_buf):
        i = pl.program_id(0)
        keep = (i > 0).astype(jnp.float32)
        cu_buf[pl.ds(0, HALO), :] = cp_ref[...] * up_ref[...] * keep
        cu_buf[pl.ds(HALO, tr), :] = c_ref[...] * u_ref[...]
        v_buf[pl.ds(0, HALO), :] = vp_ref[...] * keep
        v_buf[pl.ds(HALO, tr), :] = v_ref[...]
        _, a = _conv_piece(cu_buf, 0, tr, b_ref[...], convw_ref)
        for h, rstd in enumerate(_head_stats(a, CONV_HEAD_DIM)):
            cols = slice(h * CONV_HEAD_DIM, (h + 1) * CONV_HEAD_DIM)
            out_ref[:, cols] = (a[:, cols] * rstd).astype(out_ref.dtype)
        pooled = _pooled_piece(v_buf, 0, tr, v_ref[...], i * tr, dg)
        for gi, p in enumerate(pooled):
            z = _dot(p, _pool_matrix(pool_ref, gi), "nn")
            cols = slice(gi * dg, (gi + 1) * dg)
            out_ref[:, cw + gi * dg:cw + (gi + 1) * dg] = (z * _rstd(z) * scale_ref[:, cols]).astype(out_ref.dtype)

    prev_c, _ = _halo_specs(t, tr, cw, 1)
    prev_u, _ = _halo_specs(t, tr, cw, 2)
    prev_v, _ = _halo_specs(t, tr, cw, 3)
    return pl.pallas_call(
        body, name="mixers_fwd", grid=(t // tr,),
        in_specs=[main(0), main(1), main(2), main(3), prev_c, prev_u, prev_v,
                  _const_spec(conv_w.shape), _const_spec(pool_g.shape), _const_spec(pool_scale.shape)],
        out_specs=_row_spec(tr, 2 * cw),
        out_shape=jax.ShapeDtypeStruct((t, 2 * cw), MM_DTYPE),
        scratch_shapes=[pltpu.VMEM((tr + HALO, cw), jnp.float32), pltpu.VMEM((tr + HALO, cw), jnp.float32)],
        compiler_params=_params(("arbitrary",)),
    )(proj, proj, proj, proj, proj, proj, proj, conv_w, pool_g, pool_scale)


def _mixers_bwd(proj, dmixed, conv_w, pool_g, pool_scale, deps=()):
    t, e = proj.shape
    cw = e // 4
    dg = pool_g.shape[-1]
    n_groups = len(POOL_WINDOWS)
    tr = _tile(t, 256)
    n_tiles = t // tr
    ext = tr + 2 * HALO

    def main(col):
        return pl.BlockSpec((tr, cw), lambda i: (i, col))

    def body(b_ref, c_ref, u_ref, v_ref, dyc_ref, dyp_ref,
             cp_ref, up_ref, vp_ref,
             bn_ref, cn_ref, un_ref, vn_ref, dycn_ref, dypn_ref,
             convw_ref, pool_ref, scale_ref, *rest):
        (dproj_ref, dconvw_ref, dpool_ref, dscale_ref,
         cu_buf, v_buf, dconv_buf, dpn_buf, dpooled_buf) = rest[len(deps):]
        i = pl.program_id(0)
        keep_prev = (i > 0).astype(jnp.float32)
        keep_next = (i < n_tiles - 1).astype(jnp.float32)
        cu_buf[pl.ds(0, HALO), :] = cp_ref[...] * up_ref[...] * keep_prev
        cu_buf[pl.ds(HALO, tr), :] = c_ref[...] * u_ref[...]
        cu_buf[pl.ds(HALO + tr, HALO), :] = cn_ref[...] * un_ref[...]
        v_buf[pl.ds(0, HALO), :] = vp_ref[...] * keep_prev
        v_buf[pl.ds(HALO, tr), :] = v_ref[...]
        v_buf[pl.ds(HALO + tr, HALO), :] = vn_ref[...]

        def conv_piece(start, n, b_piece, dyc_piece, keep, is_main):
            conv, a = _conv_piece(cu_buf, start, n, b_piece, convw_ref)
            for h, rstd in enumerate(_head_stats(a, CONV_HEAD_DIM)):
                cols = slice(h * CONV_HEAD_DIM, (h + 1) * CONV_HEAD_DIM)
                da = _norm_bwd(dyc_piece[:, cols], a[:, cols] * rstd, rstd)
                dconv_buf[pl.ds(start, n), cols] = da * b_piece[:, cols] * keep
                if is_main:
                    dproj_ref[:, cols] = (da * conv[:, cols]).astype(dproj_ref.dtype)

        conv_piece(0, tr, b_ref[...], dyc_ref[...], 1.0, True)
        conv_piece(tr, HALO, bn_ref[...], dycn_ref[...], keep_next, False)

        dconv_main = dconv_buf[pl.ds(0, tr), :]
        dcu = None
        dw_rows = []
        for k in range(CONV_K):
            shift = CONV_K - 1 - k
            term = convw_ref[k:k + 1, :] * dconv_buf[pl.ds(shift, tr), :]
            dcu = term if dcu is None else dcu + term
            dw_rows.append(jnp.sum(dconv_main * cu_buf[pl.ds(HALO - shift, tr), :], axis=0, keepdims=True))
        dproj_ref[:, cw:2 * cw] = (dcu * u_ref[...]).astype(dproj_ref.dtype)
        dproj_ref[:, 2 * cw:3 * cw] = (dcu * c_ref[...]).astype(dproj_ref.dtype)
        _accumulate(dconvw_ref, jnp.concatenate(dw_rows, axis=0), i)

        def pool_piece(start, n, v_piece, dyp_piece, keep, is_main):
            pooled = _pooled_piece(v_buf, start, n, v_piece, i * tr, dg)
            dscale, dmats = [], []
            for gi, w in enumerate(POOL_WINDOWS):
                cols = slice(gi * dg, (gi + 1) * dg)
                mat = _pool_matrix(pool_ref, gi)
                z = _dot(pooled[gi], mat, "nn")
                rstd = _rstd(z)
                nz = z * rstd
                dyp_g = dyp_piece[:, cols]
                dz = _norm_bwd(dyp_g * scale_ref[:, cols], nz, rstd)
                dpooled = _dot(dz, mat, "nt") * keep
                dpn_buf[pl.ds(start, n), cols] = dpooled * _inv_count(i * tr + start, n, w)
                if is_main:
                    dpooled_buf[:, cols] = dpooled
                    dscale.append(jnp.sum(dyp_g * nz, axis=0, keepdims=True))
                    dmats.append(_dot(pooled[gi], dz, "tn"))
            return dscale, dmats

        dscale, dmats = pool_piece(0, tr, v_ref[...], dyp_ref[...], 1.0, True)
        pool_piece(tr, HALO, vn_ref[...], dypn_ref[...], keep_next, False)
        for gi, w in enumerate(POOL_WINDOWS):
            cols = slice(gi * dg, (gi + 1) * dg)
            back = None
            for k in range(w):
                term = dpn_buf[pl.ds(k, tr), cols]
                back = term if back is None else back + term
            dproj_ref[:, 3 * cw + gi * dg:3 * cw + (gi + 1) * dg] = (back - dpooled_buf[:, cols]).astype(dproj_ref.dtype)
        _accumulate(dscale_ref, jnp.concatenate(dscale, axis=1), i)
        rows = dg // N_CHIPS
        for gi in range(n_groups):
            for c in range(N_CHIPS):
                _accumulate(dpool_ref.at[c, gi], dmats[gi][c * rows:(c + 1) * rows, :], i)

    prev_c, next_c = _halo_specs(t, tr, cw, 1)
    prev_u, next_u = _halo_specs(t, tr, cw, 2)
    prev_v, next_v = _halo_specs(t, tr, cw, 3)
    _, next_b = _halo_specs(t, tr, cw, 0)
    _, next_dyc = _halo_specs(t, tr, cw, 0)
    _, next_dyp = _halo_specs(t, tr, cw, 1)
    return pl.pallas_call(
        body, name="mixers_bwd", grid=(n_tiles,),
        in_specs=[main(0), main(1), main(2), main(3), main(0), main(1),
                  prev_c, prev_u, prev_v,
                  next_b, next_c, next_u, next_v, next_dyc, next_dyp,
                  _const_spec(conv_w.shape), _const_spec(pool_g.shape), _const_spec(pool_scale.shape)]
        + [ANY] * len(deps),
        out_specs=[_row_spec(tr, e), _const_spec(conv_w.shape), _const_spec(pool_g.shape),
                   _const_spec(pool_scale.shape)],
        out_shape=[jax.ShapeDtypeStruct((t, e), MM_DTYPE), jax.ShapeDtypeStruct(conv_w.shape, jnp.float32),
                   jax.ShapeDtypeStruct(pool_g.shape, jnp.float32),
                   jax.ShapeDtypeStruct(pool_scale.shape, jnp.float32)],
        scratch_shapes=[pltpu.VMEM((ext, cw), jnp.float32), pltpu.VMEM((ext, cw), jnp.float32),
                        pltpu.VMEM((tr + HALO, cw), jnp.float32), pltpu.VMEM((tr + HALO, cw), jnp.float32),
                        pltpu.VMEM((tr, cw), jnp.float32)],
        compiler_params=_params(("arbitrary",)),
    )(proj, proj, proj, proj, dmixed, dmixed,
      proj, proj, proj,
      proj, proj, proj, proj, dmixed, dmixed,
      conv_w, pool_g, pool_scale, *deps)


def _cast_rows(name, w, dtype):
    r, c = w.shape
    tr = _tile(r, 512)

    def body(w_ref, o_ref):
        o_ref[...] = w_ref[...].astype(o_ref.dtype)

    return pl.pallas_call(
        body, name=name, grid=(r // tr,), in_specs=[_row_spec(tr, c)], out_specs=_row_spec(tr, c),
        out_shape=jax.ShapeDtypeStruct((r, c), dtype), compiler_params=_params(("arbitrary",)),
    )(w)


def _cast_to_slot(name, w, chip):
    r, c = w.shape
    tr = _tile(r, 512)

    def body(chip_ref, w_ref, o_ref):
        o_ref[...] = w_ref[...].astype(o_ref.dtype)

    return pl.pallas_call(
        body, name=name,
        grid_spec=pltpu.PrefetchScalarGridSpec(
            num_scalar_prefetch=1, grid=(r // tr,),
            in_specs=[pl.BlockSpec((tr, c), lambda i, chip_ref: (i, 0))],
            out_specs=pl.BlockSpec((None, tr, c), lambda i, chip_ref: (chip_ref[0], i, 0))),
        out_shape=jax.ShapeDtypeStruct((N_CHIPS, r, c), MM_DTYPE), compiler_params=_params(("arbitrary",)),
    )(chip, w)


def _sum_own_and_received(name, own, land, chip):
    _, r, c = own.shape
    tr = _tile(r, 256)

    def body(chip_ref, own_ref, a_ref, b_ref, c_ref, o_ref):
        tot = own_ref[...].astype(jnp.float32) + a_ref[...].astype(jnp.float32)
        tot = tot + b_ref[...].astype(jnp.float32)
        o_ref[...] = tot + c_ref[...].astype(jnp.float32)

    def slot(k):
        return pl.BlockSpec((None, tr, c), lambda i, chip_ref: ((chip_ref[0] + k) % N_CHIPS, i, 0))

    return pl.pallas_call(
        body, name=name,
        grid_spec=pltpu.PrefetchScalarGridSpec(
            num_scalar_prefetch=1, grid=(r // tr,), in_specs=[slot(0), slot(1), slot(2), slot(3)],
            out_specs=pl.BlockSpec((tr, c), lambda i, chip_ref: (i, 0))),
        out_shape=jax.ShapeDtypeStruct((r, c), jnp.float32), compiler_params=_params(("arbitrary",)),
    )(chip, own, land, land, land)


def _sum_slots(name, slots):
    n, r, c = slots.shape
    tr = _tile(r, 256)

    def body(s_ref, o_ref):
        tot = s_ref[0].astype(jnp.float32)
        for s in range(1, n):
            tot = tot + s_ref[s].astype(jnp.float32)
        o_ref[...] = tot

    return pl.pallas_call(
        body, name=name, grid=(r // tr,),
        in_specs=[pl.BlockSpec((n, tr, c), lambda i: (0, i, 0))], out_specs=_row_spec(tr, c),
        out_shape=jax.ShapeDtypeStruct((r, c), jnp.float32), compiler_params=_params(("arbitrary",)),
    )(slots)


def _adamw_math(w, g, m, v):
    m = ADAM_B1 * m + (1.0 - ADAM_B1) * g
    v = ADAM_B2 * v + (1.0 - ADAM_B2) * (g * g)
    m_hat = m / (1.0 - ADAM_B1 ** ADAM_STEP)
    v_hat = v / (1.0 - ADAM_B2 ** ADAM_STEP)
    delta = -ADAM_LR * (m_hat / (jnp.sqrt(v_hat) + ADAM_EPS) + ADAM_WD * w)
    return delta, m, v


def _adamw(name, w, m, v, grad_parts):
    r, c = w.shape
    tr = _tile(r, 256)
    n_parts = len(grad_parts)

    def body(*refs):
        w_ref, m_ref, v_ref = refs[:3]
        part_refs = refs[3:3 + n_parts]
        g_ref, d_ref, nm_ref, nv_ref = refs[3 + n_parts:]
        g = part_refs[0][...]
        for p in part_refs[1:]:
            g = g + p[...]
        delta, nm, nv = _adamw_math(w_ref[...], g, m_ref[...], v_ref[...])
        g_ref[...] = g
        d_ref[...] = delta
        nm_ref[...] = nm
        nv_ref[...] = nv

    spec = _row_spec(tr, c)
    out = jax.ShapeDtypeStruct((r, c), jnp.float32)
    return pl.pallas_call(
        body, name=name, grid=(r // tr,), in_specs=[spec] * (3 + n_parts), out_specs=[spec] * 4,
        out_shape=[out] * 4, compiler_params=_params(("arbitrary",)),
    )(w, m, v, *grad_parts)


def _chip_peers():
    x, y, c = lax.axis_index("x"), lax.axis_index("y"), lax.axis_index("c")
    return x, y, c, [(1 - x, y), (x, 1 - y), (1 - x, 1 - y)]


HBM = pl.BlockSpec(memory_space=pltpu.HBM)
SEM = pl.BlockSpec(memory_space=pltpu.SEMAPHORE)
TOKEN = jax.ShapeDtypeStruct((8, 128), jnp.float32)
N_PEER_CHIPS = N_CHIPS - 1


def _in_flight():
    return pltpu.CompilerParams(has_side_effects=pltpu.SideEffectType.DATAFLOW_SIDE_EFFECTING)


def _gather_start(name, full, deps=()):
    def body(full_ref, *rest):
        send_sems, recv_sems, _, token_ref = rest[len(deps):]
        x, y, c, peers = _chip_peers()
        me = 2 * x + y
        for p, (px, py) in enumerate(peers):
            pltpu.make_async_remote_copy(
                src_ref=full_ref.at[me], dst_ref=full_ref.at[me], send_sem=send_sems.at[p], recv_sem=recv_sems.at[p],
                device_id=(px, py, c), device_id_type=MESH).start()
        token_ref[...] = jnp.zeros_like(token_ref)

    return pl.pallas_call(
        body, name=name,
        out_shape=(pltpu.SemaphoreType.DMA((N_PEER_CHIPS,)), pltpu.SemaphoreType.DMA((N_PEER_CHIPS,)),
                   pltpu.HBM(full.shape, full.dtype), TOKEN),
        in_specs=[HBM] + [ANY] * len(deps), out_specs=(SEM, SEM, HBM, pl.BlockSpec(memory_space=pltpu.VMEM)),
        input_output_aliases={0: 2}, compiler_params=_in_flight(),
    )(pltpu.with_memory_space_constraint(full, pltpu.HBM), *deps)


def _gather_wait(name, send_sems, recv_sems, full, after):
    def body(full_ref, send_sems, recv_sems, after_ref, out_ref):
        x, y, c, peers = _chip_peers()
        me = 2 * x + y
        for p, (px, py) in enumerate(peers):
            cp = pltpu.make_async_remote_copy(
                src_ref=full_ref.at[me], dst_ref=full_ref.at[2 * px + py], send_sem=send_sems.at[p],
                recv_sem=recv_sems.at[p], device_id=(px, py, c), device_id_type=MESH)
            cp.wait_send()
            cp.wait_recv()

    return pl.pallas_call(
        body, name=name, out_shape=pltpu.HBM(full.shape, full.dtype),
        in_specs=(HBM, SEM, SEM, ANY), out_specs=HBM, input_output_aliases={0: 0}, compiler_params=_in_flight(),
    )(full, send_sems, recv_sems, after)


def _scatter_start(name, grads, deps=()):
    def body(g_ref, land_ref, *rest):
        send_sems, recv_sems, _, _, token_ref = rest[len(deps):]
        x, y, c, peers = _chip_peers()
        me = 2 * x + y
        for p, (px, py) in enumerate(peers):
            pltpu.make_async_remote_copy(
                src_ref=g_ref.at[2 * px + py], dst_ref=land_ref.at[me], send_sem=send_sems.at[p],
                recv_sem=recv_sems.at[p], device_id=(px, py, c), device_id_type=MESH).start()
        token_ref[...] = jnp.zeros_like(token_ref)

    land = lax.empty(grads.shape, grads.dtype)
    return pl.pallas_call(
        body, name=name,
        out_shape=(pltpu.SemaphoreType.DMA((N_PEER_CHIPS,)), pltpu.SemaphoreType.DMA((N_PEER_CHIPS,)),
                   pltpu.HBM(grads.shape, grads.dtype), pltpu.HBM(grads.shape, grads.dtype), TOKEN),
        in_specs=[HBM, HBM] + [ANY] * len(deps),
        out_specs=(SEM, SEM, HBM, HBM, pl.BlockSpec(memory_space=pltpu.VMEM)),
        input_output_aliases={0: 2, 1: 3}, compiler_params=_in_flight(),
    )(pltpu.with_memory_space_constraint(grads, pltpu.HBM), pltpu.with_memory_space_constraint(land, pltpu.HBM), *deps)


def _scatter_wait(name, send_sems, recv_sems, grads, land, after):
    def body(g_ref, land_ref, send_sems, recv_sems, after_ref, g_out, land_out):
        x, y, c, peers = _chip_peers()
        for p, (px, py) in enumerate(peers):
            cp = pltpu.make_async_remote_copy(
                src_ref=g_ref.at[2 * px + py], dst_ref=land_ref.at[2 * px + py], send_sem=send_sems.at[p],
                recv_sem=recv_sems.at[p], device_id=(px, py, c), device_id_type=MESH)
            cp.wait_send()
            cp.wait_recv()

    return pl.pallas_call(
        body, name=name,
        out_shape=(pltpu.HBM(grads.shape, grads.dtype), pltpu.HBM(land.shape, land.dtype)),
        in_specs=(HBM, HBM, SEM, SEM, ANY), out_specs=(HBM, HBM), input_output_aliases={0: 0, 1: 1},
        compiler_params=_in_flight(),
    )(grads, land, send_sems, recv_sems, after)


def _swap_sibling(name, parts):
    n = len(parts)

    def body(*refs):
        in_refs, out_refs = refs[:n], refs[n:2 * n]
        send_sems, recv_sems = refs[2 * n:]
        x, y, c = lax.axis_index("x"), lax.axis_index("y"), lax.axis_index("c")
        copies = []
        for a in range(n):
            cp = pltpu.make_async_remote_copy(
                src_ref=in_refs[a], dst_ref=out_refs[a], send_sem=send_sems.at[a], recv_sem=recv_sems.at[a],
                device_id=(x, y, 1 - c), device_id_type=MESH)
            cp.start()
            copies.append(cp)
        for cp in copies:
            cp.wait_recv()
        for cp in copies:
            cp.wait_send()

    return pl.pallas_call(
        body, name=name, in_specs=[ANY] * n, out_specs=[ANY] * n,
        out_shape=[jax.ShapeDtypeStruct(p.shape, p.dtype) for p in parts],
        scratch_shapes=[pltpu.SemaphoreType.DMA((n,)), pltpu.SemaphoreType.DMA((n,))],
    )(*parts)


def _gather_devices(name, block):
    def body(in_ref, out_ref, send_sems, recv_sems, local_sem):
        x, y, c = lax.axis_index("x"), lax.axis_index("y"), lax.axis_index("c")
        me = 4 * x + 2 * y + c
        local = pltpu.make_async_copy(in_ref, out_ref.at[me], local_sem)
        local.start()
        sends, recvs = [], []
        k = 0
        for fx in range(2):
            for fy in range(2):
                for fc in range(2):
                    if fx == fy == fc == 0:
                        continue
                    px = x if fx == 0 else 1 - x
                    py = y if fy == 0 else 1 - y
                    pc = c if fc == 0 else 1 - c
                    cp = pltpu.make_async_remote_copy(
                        src_ref=in_ref, dst_ref=out_ref.at[me], send_sem=send_sems.at[k], recv_sem=recv_sems.at[k],
                        device_id=(px, py, pc), device_id_type=MESH)
                    cp.start()
                    sends.append(cp)
                    recvs.append(pltpu.make_async_remote_copy(
                        src_ref=in_ref, dst_ref=out_ref.at[4 * px + 2 * py + pc], send_sem=send_sems.at[k],
                        recv_sem=recv_sems.at[k], device_id=(px, py, pc), device_id_type=MESH))
                    k += 1
        for cp in recvs:
            cp.wait_recv()
        for cp in sends:
            cp.wait_send()
        local.wait()

    return pl.pallas_call(
        body, name=name, in_specs=[ANY], out_specs=ANY,
        out_shape=jax.ShapeDtypeStruct((N_DEV,) + block.shape, block.dtype),
        scratch_shapes=[pltpu.SemaphoreType.DMA((N_DEV - 1,)), pltpu.SemaphoreType.DMA((N_DEV - 1,)),
                        pltpu.SemaphoreType.DMA],
    )(block)


def _pack_rows(pieces, width):
    flat = jnp.concatenate([p.reshape(-1) for p in pieces])
    rows = -(-flat.shape[0] // width)
    rows = -(-rows // 8) * 8
    flat = jnp.pad(flat, (0, rows * width - flat.shape[0]))
    return flat.reshape(rows, width)


def _unpack_rows(packed, shapes):
    flat = packed.reshape(-1)
    out, off = [], 0
    for s in shapes:
        size = 1
        for d in s:
            size *= d
        out.append(flat[off:off + size].reshape(s))
        off += size
    return out


def kernel(x, ln_mix_pre, w_in, conv_w, pool_w, pool_scale, w_out, ln_mix_post, ln_ffn_pre, w_gate, w_up, w_down, ln_ffn_post, loss_target, m_ln_mix_pre, m_w_in, m_conv_w, m_pool_w, m_pool_scale, m_w_out, m_ln_mix_post, m_ln_ffn_pre, m_w_gate, m_w_up, m_w_down, m_ln_ffn_post, v_ln_mix_pre, v_w_in, v_conv_w, v_pool_w, v_pool_scale, v_w_out, v_ln_mix_post, v_ln_ffn_pre, v_w_gate, v_w_up, v_w_down, v_ln_ffn_post):
    t, d = x.shape[1], x.shape[2]
    e4 = w_in.shape[2]
    e = N_CHIPS * e4
    f4 = w_gate.shape[2]
    f = N_CHIPS * f4
    n_groups, dg4, dg = pool_w.shape[1], pool_w.shape[2], pool_w.shape[3]
    cw4 = conv_w.shape[2]
    chip = 2 * lax.axis_index("x") + lax.axis_index("y")
    xs, tgt = x[0], loss_target[0]

    big = {"w_in": w_in[0], "w_out": w_out[0], "w_gate": w_gate[0], "w_up": w_up[0], "w_down": w_down[0],
           "pool_w": pool_w[0].reshape(n_groups * dg4, dg)}
    names = ["w_in", "pool_w", "w_out", "w_gate", "w_up", "w_down"]
    chip_arr = chip.astype(jnp.int32).reshape(1)
    in_flight, deps = {}, ()
    for k in names:
        in_flight[k] = _gather_start("gather_start_" + k, _cast_to_slot("cast_" + k, big[k], chip_arr), deps)
        deps = (in_flight[k][3],)

    def landed(k, after):
        send_sems, recv_sems, full, _ = in_flight[k]
        return _gather_wait("gather_wait_" + k, send_sems, recv_sems, full, after)

    conv_all = _gather_devices("gather_conv_w", _pack_rows([conv_w[0]], 128))
    conv_full = jnp.concatenate(
        [conv_all[2 * j].reshape(-1)[:CONV_K * cw4].reshape(CONV_K, cw4) for j in range(N_CHIPS)], axis=1)

    h = _pre_norm(xs, ln_mix_pre, deps)
    win_g = landed("w_in", h)
    tm = _tile(t, 1024)
    proj = _matmul(
        "in_proj", grid=(t // tm, N_CHIPS, 1), mode="nn",
        pairs=[[(h, win_g)]],
        pair_specs=[[(pl.BlockSpec((tm, d), lambda i, j, k: (i, 0)),
                      pl.BlockSpec((None, d, e4), lambda i, j, k: (j, 0, 0)))]],
        acc_shapes=[(tm, e4)], out_shapes=[jax.ShapeDtypeStruct((t, e), jnp.float32)],
        out_specs=[pl.BlockSpec((tm, e4), lambda i, j, k: (i, j))], epilogue=_identity_epilogue)[0]
    pool_g = landed("pool_w", proj).reshape(N_CHIPS, n_groups, dg4, dg)
    mixed = _mixers_fwd(proj, conv_full, pool_g, pool_scale)
    wout_full = landed("w_out", mixed).reshape(d, d)
    tn = _tile(d, 1024)
    mix_out = _matmul(
        "out_proj", grid=(t // tm, d // tn, 1), mode="nn",
        pairs=[[(mixed, wout_full)]],
        pair_specs=[[(pl.BlockSpec((tm, d), lambda i, j, k: (i, 0)),
                      pl.BlockSpec((d, tn), lambda i, j, k: (0, j)))]],
        acc_shapes=[(tm, tn)], out_shapes=[jax.ShapeDtypeStruct((t, d), jnp.float32)],
        out_specs=[pl.BlockSpec((tm, tn), lambda i, j, k: (i, j))], epilogue=_identity_epilogue)[0]
    x1, hf = _post_mix(xs, mix_out, ln_mix_post, ln_ffn_pre)
    wg_g = landed("w_gate", hf)
    wu_g = landed("w_up", wg_g)

    def gate_up_epilogue(accs, extra_refs):
        g, up = accs
        return g, up, g * jax.nn.sigmoid(g) * up

    tm_ff = _tile(t, 512)
    ff_tile = jax.ShapeDtypeStruct((t, f), MM_DTYPE)
    g_act, up_act, act = _matmul(
        "gate_up", grid=(N_CHIPS, t // tm_ff, 1), mode="nn",
        pairs=[[(hf, wg_g)], [(hf, wu_g)]],
        pair_specs=[[(pl.BlockSpec((tm_ff, d), lambda j, i, k: (i, 0)),
                      pl.BlockSpec((None, d, f4), lambda j, i, k: (j, 0, 0)))]] * 2,
        acc_shapes=[(tm_ff, f4)] * 2, out_shapes=[ff_tile] * 3,
        out_specs=[pl.BlockSpec((tm_ff, f4), lambda j, i, k: (i, j))] * 3, epilogue=gate_up_epilogue)
    wdown_full = landed("w_down", act).reshape(f, d)
    ff = _matmul(
        "down_proj", grid=(t // tm, d // tn, N_CHIPS), mode="nn",
        pairs=[[(act, wdown_full)]],
        pair_specs=[[(pl.BlockSpec((tm, f4), lambda i, j, k: (i, k)),
                      pl.BlockSpec((f4, tn), lambda i, j, k: (k, j)))]],
        acc_shapes=[(tm, tn)], out_shapes=[jax.ShapeDtypeStruct((t, d), jnp.float32)],
        out_specs=[pl.BlockSpec((tm, tn), lambda i, j, k: (i, j))], epilogue=_identity_epilogue)[0]
    dout, dff, loss_tile, dg_ffn_post = _loss_head(ff, x1, tgt, ln_ffn_post)

    def dact_epilogue(accs, extra_refs):
        dact = accs[0]
        g = extra_refs[0][...].astype(jnp.float32)
        up = extra_refs[1][...].astype(jnp.float32)
        sig = jax.nn.sigmoid(g)
        silu = g * sig
        return dact * up * (sig + silu * (1.0 - sig)), dact * silu

    ff_spec_ji = pl.BlockSpec((tm_ff, f4), lambda j, i, k: (i, j))
    dg_act, dup_act = _matmul(
        "dact", grid=(N_CHIPS, t // tm_ff, 1), mode="nt",
        pairs=[[(dff, wdown_full)]],
        pair_specs=[[(pl.BlockSpec((tm_ff, d), lambda j, i, k: (i, 0)),
                      pl.BlockSpec((f4, d), lambda j, i, k: (j, 0)))]],
        acc_shapes=[(tm_ff, f4)], extras=[g_act, up_act], extra_specs=[ff_spec_ji, ff_spec_ji],
        out_shapes=[ff_tile] * 2, out_specs=[ff_spec_ji] * 2, epilogue=dact_epilogue)
    tk = _tile(t, 1024)
    dw_down = _matmul(
        "dw_down", grid=(N_CHIPS, d // tn, t // tk), mode="tn",
        pairs=[[(act, dff)]],
        pair_specs=[[(pl.BlockSpec((tk, f4), lambda i, j, k: (k, i)),
                      pl.BlockSpec((tk, tn), lambda i, j, k: (k, j)))]],
        acc_shapes=[(f4, tn)], out_shapes=[jax.ShapeDtypeStruct((N_CHIPS, f4, d), COMM_DTYPE)],
        out_specs=[pl.BlockSpec((None, f4, tn), lambda i, j, k: (i, 0, j))], epilogue=_identity_epilogue)[0]
    leaving = {"w_down": _scatter_start("scatter_start_w_down", dw_down)}
    dhf = _matmul(
        "dhf", grid=(t // tm, d // tn, N_CHIPS), mode="nt",
        pairs=[[(dg_act, wg_g), (dup_act, wu_g)]],
        pair_specs=[[(pl.BlockSpec((tm, f4), lambda i, j, k: (i, k)),
                      pl.BlockSpec((None, tn, f4), lambda i, j, k: (k, j, 0)))] * 2],
        acc_shapes=[(tm, tn)], out_shapes=[jax.ShapeDtypeStruct((t, d), jnp.float32)],
        out_specs=[pl.BlockSpec((tm, tn), lambda i, j, k: (i, j))], epilogue=_identity_epilogue,
        deps=leaving["w_down"][4:])[0]
    tmo = _tile(d, 1024)
    slot_ff = jax.ShapeDtypeStruct((N_CHIPS, d, f4), COMM_DTYPE)
    dw_gate, dw_up = _matmul(
        "dw_gate_up", grid=(d // tmo, N_CHIPS, t // tk), mode="tn",
        pairs=[[(hf, dg_act)], [(hf, dup_act)]],
        pair_specs=[[(pl.BlockSpec((tk, tmo), lambda i, j, k: (k, i)),
                      pl.BlockSpec((tk, f4), lambda i, j, k: (k, j)))]] * 2,
        acc_shapes=[(tmo, f4)] * 2, out_shapes=[slot_ff] * 2,
        out_specs=[pl.BlockSpec((None, tmo, f4), lambda i, j, k: (j, i, 0))] * 2, epilogue=_identity_epilogue)
    leaving["w_gate"] = _scatter_start("scatter_start_w_gate", dw_gate)
    leaving["w_up"] = _scatter_start("scatter_start_w_up", dw_up, leaving["w_gate"][4:])
    dx1, dmo, dg_ffn_pre, dg_mix_post = _ffn_pre_bwd(dhf, x1, dout, mix_out, ln_ffn_pre, ln_mix_post,
                                                     leaving["w_up"][4:])

    dmixed = _matmul(
        "dmixed", grid=(t // tm, d // tn, 1), mode="nt",
        pairs=[[(dmo, wout_full)]],
        pair_specs=[[(pl.BlockSpec((tm, d), lambda i, j, k: (i, 0)),
                      pl.BlockSpec((tn, d), lambda i, j, k: (j, 0)))]],
        acc_shapes=[(tm, tn)], out_shapes=[jax.ShapeDtypeStruct((t, d), jnp.float32)],
        out_specs=[pl.BlockSpec((tm, tn), lambda i, j, k: (i, j))], epilogue=_identity_epilogue)[0]
    dw_out = _matmul(
        "dw_out", grid=(d // tmo, d // tn, t // tk), mode="tn",
        pairs=[[(mixed, dmo)]],
        pair_specs=[[(pl.BlockSpec((tk, tmo), lambda i, j, k: (k, i)),
                      pl.BlockSpec((tk, tn), lambda i, j, k: (k, j)))]],
        acc_shapes=[(tmo, tn)], out_shapes=[jax.ShapeDtypeStruct((d, d), COMM_DTYPE)],
        out_specs=[pl.BlockSpec((tmo, tn), lambda i, j, k: (i, j))], epilogue=_identity_epilogue)[0]
    leaving["w_out"] = _scatter_start("scatter_start_w_out", dw_out.reshape(N_CHIPS, d // N_CHIPS, d))
    dproj, dconv_full, dpool_g, dpool_scale = _mixers_bwd(proj, dmixed, conv_full, pool_g, pool_scale,
                                                          leaving["w_out"][4:])
    dpool_slots = _cast_rows("cast_dpool", dpool_g.reshape(N_CHIPS * n_groups * dg4, dg), COMM_DTYPE)
    leaving["pool_w"] = _scatter_start("scatter_start_pool_w", dpool_slots.reshape(N_CHIPS, n_groups * dg4, dg))
    dw_in = _matmul(
        "dw_in", grid=(d // tmo, N_CHIPS, t // tk), mode="tn",
        pairs=[[(h, dproj)]],
        pair_specs=[[(pl.BlockSpec((tk, tmo), lambda i, j, k: (k, i)),
                      pl.BlockSpec((tk, e4), lambda i, j, k: (k, j)))]],
        acc_shapes=[(tmo, e4)], out_shapes=[jax.ShapeDtypeStruct((N_CHIPS, d, e4), COMM_DTYPE)],
        out_specs=[pl.BlockSpec((None, tmo, e4), lambda i, j, k: (j, i, 0))], epilogue=_identity_epilogue,
        deps=leaving["pool_w"][4:])[0]
    leaving["w_in"] = _scatter_start("scatter_start_w_in", dw_in)
    dh = _matmul(
        "dh", grid=(t // tm, d // tn, N_CHIPS), mode="nt",
        pairs=[[(dproj, win_g)]],
        pair_specs=[[(pl.BlockSpec((tm, e4), lambda i, j, k: (i, k)),
                      pl.BlockSpec((None, tn, e4), lambda i, j, k: (k, j, 0)))]],
        acc_shapes=[(tm, tn)], out_shapes=[jax.ShapeDtypeStruct((t, d), jnp.float32)],
        out_specs=[pl.BlockSpec((tm, tn), lambda i, j, k: (i, j))], epilogue=_identity_epilogue,
        deps=leaving["w_in"][4:])[0]
    grad_x, dg_mix_pre = _mix_pre_bwd(dh, xs, dx1, ln_mix_pre)

    names = ["w_down", "w_gate", "w_up", "w_out", "pool_w", "w_in"]
    partial, after = [], grad_x
    for k in names:
        send_sems, recv_sems, own, land, _ = leaving[k]
        own, land = _scatter_wait("scatter_wait_" + k, send_sems, recv_sems, own, land, after)
        partial.append(_sum_own_and_received("sum_" + k, own, land, chip_arr))
        after = land
    other = _swap_sibling("swap_grads", partial)
    moments = {"w_in": (m_w_in, v_w_in), "w_out": (m_w_out, v_w_out), "w_gate": (m_w_gate, v_w_gate),
               "w_up": (m_w_up, v_w_up), "w_down": (m_w_down, v_w_down), "pool_w": (m_pool_w, v_pool_w)}
    result = {}
    for k, mine, theirs in zip(names, partial, other):
        shape = moments[k][0].shape
        two_d = big[k].shape
        outs = _adamw("adamw_" + k, big[k], moments[k][0].reshape(two_d), moments[k][1].reshape(two_d),
                      [mine, theirs])
        result[k] = [o.reshape(shape) for o in outs]

    small_shapes = [(1, d)] * 4 + [pool_scale.shape, (CONV_K, N_CHIPS * cw4)]
    packed = _pack_rows([dg_mix_pre, dg_mix_post, dg_ffn_pre, dg_ffn_post, dpool_scale, dconv_full], 1024)
    summed = _sum_slots("sum_small", _gather_devices("gather_small", packed))
    g_mix_pre, g_mix_post, g_ffn_pre, g_ffn_post, g_pool_scale, g_conv_full = _unpack_rows(summed, small_shapes)
    g_conv = lax.dynamic_slice(g_conv_full, (0, chip * cw4), (CONV_K, cw4))[None]
    small = [("ln_mix_pre", ln_mix_pre, m_ln_mix_pre, v_ln_mix_pre, g_mix_pre),
             ("conv_w", conv_w, m_conv_w, v_conv_w, g_conv),
             ("pool_scale", pool_scale, m_pool_scale, v_pool_scale, g_pool_scale),
             ("ln_mix_post", ln_mix_post, m_ln_mix_post, v_ln_mix_post, g_mix_post),
             ("ln_ffn_pre", ln_ffn_pre, m_ln_ffn_pre, v_ln_ffn_pre, g_ffn_pre),
             ("ln_ffn_post", ln_ffn_post, m_ln_ffn_post, v_ln_ffn_post, g_ffn_post)]
    shapes_small = [s[1].shape for s in small]
    packs = [_pack_rows([s[q] for s in small], 128) for q in (1, 2, 3, 4)]
    outs = _adamw("adamw_small", packs[0], packs[1], packs[2], [packs[3]])
    unpacked = [_unpack_rows(o, shapes_small) for o in outs]
    for idx, s in enumerate(small):
        result[s[0]] = [u[idx] for u in unpacked]

    loss = lax.psum(loss_tile[0, 0], ("x", "y", "c"))
    order = ["ln_mix_pre", "w_in", "conv_w", "pool_w", "pool_scale", "w_out", "ln_mix_post", "ln_ffn_pre",
             "w_gate", "w_up", "w_down", "ln_ffn_post"]
    return (loss, grad_x[None], *[result[k][0] for k in order], *[result[k][1] for k in order],
            *[result[k][2] for k in order], *[result[k][3] for k in order])
```

```python
import functools

import jax
import jax.numpy as jnp
from jax import lax
from jax.experimental import pallas as pl
from jax.experimental.pallas import tpu as pltpu

EPS = 1e-6
CONV_HEAD_DIM = 128
CONV_K = 3
POOL_WINDOWS = (2, 4, 8, 16)
HALO = 16
N_CHIPS = 4
N_DEV = 8

ADAM_LR = 0.001
ADAM_B1 = 0.9
ADAM_B2 = 0.999
ADAM_EPS = 1e-08
ADAM_WD = 0.01
ADAM_STEP = 10

MM_DTYPE = jnp.bfloat16
COMM_DTYPE = jnp.bfloat16
VMEM_LIMIT = 56 * 1024 * 1024
MESH = pl.DeviceIdType.MESH
ANY = pl.BlockSpec(memory_space=pl.ANY)


def _tile(n, pref):
    t = min(pref, n)
    while n % t:
        t //= 2
    return t


def _params(sem):
    return pltpu.CompilerParams(dimension_semantics=sem, vmem_limit_bytes=VMEM_LIMIT)


def _rstd(x):
    return lax.rsqrt(jnp.mean(x * x, axis=-1, keepdims=True) + EPS)


def _norm_bwd(dn, n, rstd):
    return rstd * (dn - n * jnp.mean(dn * n, axis=-1, keepdims=True))


_DOT_DIMS = {
    "nn": (((1,), (0,)), ((), ())),
    "nt": (((1,), (1,)), ((), ())),
    "tn": (((0,), (0,)), ((), ())),
}


def _dot(a, b, mode):
    return lax.dot_general(a.astype(MM_DTYPE), b.astype(MM_DTYPE), _DOT_DIMS[mode],
                           preferred_element_type=jnp.float32)


def _matmul(name, *, grid, mode, pairs, pair_specs, acc_shapes, extras=(), extra_specs=(),
            out_shapes, out_specs, epilogue, deps=()):
    nk = grid[2]
    flat_in, flat_specs, counts = [], [], []
    for ps, ss in zip(pairs, pair_specs):
        counts.append(len(ps))
        for (a, b), (sa, sb) in zip(ps, ss):
            flat_in += [a, b]
            flat_specs += [sa, sb]
    n_pair_refs = len(flat_in)
    n_extra = len(extras)
    n_out = len(out_shapes)
    n_in = n_pair_refs + n_extra + len(deps)

    def body(*refs):
        pair_refs = refs[:n_pair_refs]
        extra_refs = refs[n_pair_refs:n_pair_refs + n_extra]
        out_refs = refs[n_in:n_in + n_out]
        acc_refs = refs[n_in + n_out:]
        k = pl.program_id(2)

        def partial_sums():
            res, p = [], 0
            for cnt in counts:
                tot = None
                for _ in range(cnt):
                    d = _dot(pair_refs[p][...], pair_refs[p + 1][...], mode)
                    tot = d if tot is None else tot + d
                    p += 2
                res.append(tot)
            return res

        def finish(accs):
            outs = epilogue(accs, extra_refs)
            for o_ref, o in zip(out_refs, outs):
                o_ref[...] = o.astype(o_ref.dtype)

        if nk == 1:
            finish(partial_sums())
        else:
            @pl.when(k == 0)
            def _():
                for acc_ref in acc_refs:
                    acc_ref[...] = jnp.zeros_like(acc_ref)

            for acc_ref, s in zip(acc_refs, partial_sums()):
                acc_ref[...] += s

            @pl.when(k == nk - 1)
            def _():
                finish([a[...] for a in acc_refs])

    scratch = [] if nk == 1 else [pltpu.VMEM(s, jnp.float32) for s in acc_shapes]
    return pl.pallas_call(
        body, name=name, grid=grid,
        in_specs=flat_specs + list(extra_specs) + [ANY] * len(deps), out_specs=list(out_specs),
        out_shape=list(out_shapes), scratch_shapes=scratch,
        compiler_params=_params(("arbitrary", "arbitrary", "arbitrary")),
    )(*flat_in, *extras, *deps)


def _identity_epilogue(accs, extra_refs):
    return tuple(accs)


def _row_spec(tr, n):
    return pl.BlockSpec((tr, n), lambda i: (i, 0))


def _const_spec(shape):
    return pl.BlockSpec(shape, lambda i: tuple(0 for _ in shape))


def _accumulate(ref, val, i):
    @pl.when(i == 0)
    def _():
        ref[...] = val

    @pl.when(i > 0)
    def _():
        ref[...] += val


def _pre_norm(x, gain, deps=()):
    t, d = x.shape
    tr = _tile(t, 512)

    def body(x_ref, g_ref, *rest):
        h_ref = rest[-1]
        xv = x_ref[...]
        h_ref[...] = (xv * _rstd(xv) * g_ref[...]).astype(h_ref.dtype)

    return pl.pallas_call(
        body, name="pre_norm", grid=(t // tr,),
        in_specs=[_row_spec(tr, d), _const_spec((1, d))] + [ANY] * len(deps), out_specs=_row_spec(tr, d),
        out_shape=jax.ShapeDtypeStruct((t, d), MM_DTYPE), compiler_params=_params(("arbitrary",)),
    )(x, gain, *deps)


def _post_mix(x, mix_out, g_post, g_ffn_pre):
    t, d = x.shape
    tr = _tile(t, 512)

    def body(x_ref, mo_ref, g2_ref, g3_ref, x1_ref, hf_ref):
        mo = mo_ref[...]
        x1 = x_ref[...] + mo * _rstd(mo) * g2_ref[...]
        x1_ref[...] = x1
        hf_ref[...] = (x1 * _rstd(x1) * g3_ref[...]).astype(hf_ref.dtype)

    return pl.pallas_call(
        body, name="post_mix", grid=(t // tr,),
        in_specs=[_row_spec(tr, d), _row_spec(tr, d), _const_spec((1, d)), _const_spec((1, d))],
        out_specs=[_row_spec(tr, d), _row_spec(tr, d)],
        out_shape=[jax.ShapeDtypeStruct((t, d), jnp.float32), jax.ShapeDtypeStruct((t, d), MM_DTYPE)],
        compiler_params=_params(("arbitrary",)),
    )(x, mix_out, g_post, g_ffn_pre)


def _loss_head(ff, x1, target, g_post):
    t, d = ff.shape
    tr = _tile(t, 512)

    def body(ff_ref, x1_ref, tg_ref, g_ref, dout_ref, dff_ref, loss_ref, dg_ref):
        i = pl.program_id(0)
        ff_v = ff_ref[...]
        rstd = _rstd(ff_v)
        n = ff_v * rstd
        g = g_ref[...]
        err = x1_ref[...] + n * g - tg_ref[...]
        tile_loss = 0.5 * jnp.sum(jnp.mean(err * err, axis=-1, keepdims=True), axis=0, keepdims=True)
        dout = err / d
        dout_ref[...] = dout
        dff_ref[...] = _norm_bwd(dout * g, n, rstd).astype(dff_ref.dtype)
        _accumulate(loss_ref, jnp.broadcast_to(tile_loss, loss_ref.shape), i)
        _accumulate(dg_ref, jnp.sum(dout * n, axis=0, keepdims=True), i)

    return pl.pallas_call(
        body, name="loss_head", grid=(t // tr,),
        in_specs=[_row_spec(tr, d), _row_spec(tr, d), _row_spec(tr, d), _const_spec((1, d))],
        out_specs=[_row_spec(tr, d), _row_spec(tr, d), _const_spec((8, 128)), _const_spec((1, d))],
        out_shape=[jax.ShapeDtypeStruct((t, d), jnp.float32), jax.ShapeDtypeStruct((t, d), MM_DTYPE),
                   jax.ShapeDtypeStruct((8, 128), jnp.float32), jax.ShapeDtypeStruct((1, d), jnp.float32)],
        compiler_params=_params(("arbitrary",)),
    )(ff, x1, target, g_post)


def _ffn_pre_bwd(dhf, x1, dout, mix_out, g_ffn_pre, g_mix_post, deps=()):
    t, d = dhf.shape
    tr = _tile(t, 256)

    def body(dhf_ref, x1_ref, dout_ref, mo_ref, g3_ref, g2_ref, *rest):
        dx1_ref, dmo_ref, dg3_ref, dg2_ref = rest[len(deps):]
        i = pl.program_id(0)
        dhf_v = dhf_ref[...]
        x1 = x1_ref[...]
        rstd3 = _rstd(x1)
        n3 = x1 * rstd3
        dx1 = dout_ref[...] + _norm_bwd(dhf_v * g3_ref[...], n3, rstd3)
        dx1_ref[...] = dx1
        mo = mo_ref[...]
        rstd2 = _rstd(mo)
        n2 = mo * rstd2
        dmo_ref[...] = _norm_bwd(dx1 * g2_ref[...], n2, rstd2).astype(dmo_ref.dtype)
        _accumulate(dg3_ref, jnp.sum(dhf_v * n3, axis=0, keepdims=True), i)
        _accumulate(dg2_ref, jnp.sum(dx1 * n2, axis=0, keepdims=True), i)

    return pl.pallas_call(
        body, name="ffn_pre_bwd", grid=(t // tr,),
        in_specs=[_row_spec(tr, d)] * 4 + [_const_spec((1, d))] * 2 + [ANY] * len(deps),
        out_specs=[_row_spec(tr, d), _row_spec(tr, d), _const_spec((1, d)), _const_spec((1, d))],
        out_shape=[jax.ShapeDtypeStruct((t, d), jnp.float32), jax.ShapeDtypeStruct((t, d), MM_DTYPE),
                   jax.ShapeDtypeStruct((1, d), jnp.float32), jax.ShapeDtypeStruct((1, d), jnp.float32)],
        compiler_params=_params(("arbitrary",)),
    )(dhf, x1, dout, mix_out, g_ffn_pre, g_mix_post, *deps)


def _mix_pre_bwd(dh, x, dx1, g_mix_pre):
    t, d = dh.shape
    tr = _tile(t, 512)

    def body(dh_ref, x_ref, dx1_ref, g_ref, dx_ref, dg_ref):
        i = pl.program_id(0)
        dh_v = dh_ref[...]
        xv = x_ref[...]
        rstd = _rstd(xv)
        n = xv * rstd
        dx_ref[...] = dx1_ref[...] + _norm_bwd(dh_v * g_ref[...], n, rstd)
        _accumulate(dg_ref, jnp.sum(dh_v * n, axis=0, keepdims=True), i)

    return pl.pallas_call(
        body, name="mix_pre_bwd", grid=(t // tr,),
        in_specs=[_row_spec(tr, d)] * 3 + [_const_spec((1, d))],
        out_specs=[_row_spec(tr, d), _const_spec((1, d))],
        out_shape=[jax.ShapeDtypeStruct((t, d), jnp.float32), jax.ShapeDtypeStruct((1, d), jnp.float32)],
        compiler_params=_params(("arbitrary",)),
    )(dh, x, dx1, g_mix_pre)


def _pool_matrix(pool_ref, g):
    return jnp.concatenate([pool_ref[c, g] for c in range(N_CHIPS)], axis=0)


def _inv_count(row0, n, w):
    pos = (row0 + lax.broadcasted_iota(jnp.int32, (n, 1), 0) + 1).astype(jnp.float32)
    return 1.0 / jnp.minimum(pos, float(w))


def _conv_piece(cu_buf, start, n, b_piece, convw_ref):
    conv = None
    for k in range(CONV_K):
        term = convw_ref[k:k + 1, :] * cu_buf[pl.ds(HALO + start + k - (CONV_K - 1), n), :]
        conv = term if conv is None else conv + term
    return conv, b_piece * conv


def _head_stats(a, width):
    return [_rstd(a[:, h * width:(h + 1) * width]) for h in range(a.shape[1] // width)]


def _pooled_piece(v_buf, start, n, v_piece, row0, dg):
    outs = []
    for gi, w in enumerate(POOL_WINDOWS):
        cols = slice(gi * dg, (gi + 1) * dg)
        win = None
        for k in range(w):
            term = v_buf[pl.ds(HALO + start - k, n), cols]
            win = term if win is None else win + term
        outs.append(win * _inv_count(row0 + start, n, w) - v_piece[:, cols])
    return outs


def _halo_specs(t, tr, width, col):
    per = tr // HALO
    last = t // HALO - 1
    prev = pl.BlockSpec((HALO, width), lambda i: (jnp.maximum(i * per - 1, 0), col))
    nxt = pl.BlockSpec((HALO, width), lambda i: (jnp.minimum((i + 1) * per, last), col))
    return prev, nxt


def _mixers_fwd(proj, conv_w, pool_g, pool_scale):
    t, e = proj.shape
    cw = e // 4
    dg = pool_g.shape[-1]
    tr = _tile(t, 256)

    def main(col):
        return pl.BlockSpec((tr, cw), lambda i: (i, col))

    def body(b_ref, c_ref, u_ref, v_ref, cp_ref, up_ref, vp_ref, convw_ref, pool_ref, scale_ref,
             out_ref, cu_buf, v_buf):
        i = pl.program_id(0)
        keep = (i > 0).astype(jnp.float32)
        cu_buf[pl.ds(0, HALO), :] = cp_ref[...] * up_ref[...] * keep
        cu_buf[pl.ds(HALO, tr), :] = c_ref[...] * u_ref[...]
        v_buf[pl.ds(0, HALO), :] = vp_ref[...] * keep
        v_buf[pl.ds(HALO, tr), :] = v_ref[...]
        _, a = _conv_piece(cu_buf, 0, tr, b_ref[...], convw_ref)
        for h, rstd in enumerate(_head_stats(a, CONV_HEAD_DIM)):
            cols = slice(h * CONV_HEAD_DIM, (h + 1) * CONV_HEAD_DIM)
            out_ref[:, cols] = (a[:, cols] * rstd).astype(out_ref.dtype)
        pooled = _pooled_piece(v_buf, 0, tr, v_ref[...], i * tr, dg)
        for gi, p in enumerate(pooled):
            z = _dot(p, _pool_matrix(pool_ref, gi), "nn")
            cols = slice(gi * dg, (gi + 1) * dg)
            out_ref[:, cw + gi * dg:cw + (gi + 1) * dg] = (z * _rstd(z) * scale_ref[:, cols]).astype(out_ref.dtype)

    prev_c, _ = _halo_specs(t, tr, cw, 1)
    prev_u, _ = _halo_specs(t, tr, cw, 2)
    prev_v, _ = _halo_specs(t, tr, cw, 3)
    return pl.pallas_call(
        body, name="mixers_fwd", grid=(t // tr,),
        in_specs=[main(0), main(1), main(2), main(3), prev_c, prev_u, prev_v,
                  _const_spec(conv_w.shape), _const_spec(pool_g.shape), _const_spec(pool_scale.shape)],
        out_specs=_row_spec(tr, 2 * cw),
        out_shape=jax.ShapeDtypeStruct((t, 2 * cw), MM_DTYPE),
        scratch_shapes=[pltpu.VMEM((tr + HALO, cw), jnp.float32), pltpu.VMEM((tr + HALO, cw), jnp.float32)],
        compiler_params=_params(("arbitrary",)),
    )(proj, proj, proj, proj, proj, proj, proj, conv_w, pool_g, pool_scale)


def _mixers_bwd(proj, dmixed, conv_w, pool_g, pool_scale, deps=()):
    t, e = proj.shape
    cw = e // 4
    dg = pool_g.shape[-1]
    n_groups = len(POOL_WINDOWS)
    tr = _tile(t, 256)
    n_tiles = t // tr
    ext = tr + 2 * HALO

    def main(col):
        return pl.BlockSpec((tr, cw), lambda i: (i, col))

    def body(b_ref, c_ref, u_ref, v_ref, dyc_ref, dyp_ref,
             cp_ref, up_ref, vp_ref,
             bn_ref, cn_ref, un_ref, vn_ref, dycn_ref, dypn_ref,
             convw_ref, pool_ref, scale_ref, *rest):
        (dproj_ref, dconvw_ref, dpool_ref, dscale_ref,
         cu_buf, v_buf, dconv_buf, dpn_buf, dpooled_buf) = rest[len(deps):]
        i = pl.program_id(0)
        keep_prev = (i > 0).astype(jnp.float32)
        keep_next = (i < n_tiles - 1).astype(jnp.float32)
        cu_buf[pl.ds(0, HALO), :] = cp_ref[...] * up_ref[...] * keep_prev
        cu_buf[pl.ds(HALO, tr), :] = c_ref[...] * u_ref[...]
        cu_buf[pl.ds(HALO + tr, HALO), :] = cn_ref[...] * un_ref[...]
        v_buf[pl.ds(0, HALO), :] = vp_ref[...] * keep_prev
        v_buf[pl.ds(HALO, tr), :] = v_ref[...]
        v_buf[pl.ds(HALO + tr, HALO), :] = vn_ref[...]

        def conv_piece(start, n, b_piece, dyc_piece, keep, is_main):
            conv, a = _conv_piece(cu_buf, start, n, b_piece, convw_ref)
            for h, rstd in enumerate(_head_stats(a, CONV_HEAD_DIM)):
                cols = slice(h * CONV_HEAD_DIM, (h + 1) * CONV_HEAD_DIM)
                da = _norm_bwd(dyc_piece[:, cols], a[:, cols] * rstd, rstd)
                dconv_buf[pl.ds(start, n), cols] = da * b_piece[:, cols] * keep
                if is_main:
                    dproj_ref[:, cols] = (da * conv[:, cols]).astype(dproj_ref.dtype)

        conv_piece(0, tr, b_ref[...], dyc_ref[...], 1.0, True)
        conv_piece(tr, HALO, bn_ref[...], dycn_ref[...], keep_next, False)

        dconv_main = dconv_buf[pl.ds(0, tr), :]
        dcu = None
        dw_rows = []
        for k in range(CONV_K):
            shift = CONV_K - 1 - k
            term = convw_ref[k:k + 1, :] * dconv_buf[pl.ds(shift, tr), :]
            dcu = term if dcu is None else dcu + term
            dw_rows.append(jnp.sum(dconv_main * cu_buf[pl.ds(HALO - shift, tr), :], axis=0, keepdims=True))
        dproj_ref[:, cw:2 * cw] = (dcu * u_ref[...]).astype(dproj_ref.dtype)
        dproj_ref[:, 2 * cw:3 * cw] = (dcu * c_ref[...]).astype(dproj_ref.dtype)
        _accumulate(dconvw_ref, jnp.concatenate(dw_rows, axis=0), i)

        def pool_piece(start, n, v_piece, dyp_piece, keep, is_main):
            pooled = _pooled_piece(v_buf, start, n, v_piece, i * tr, dg)
            dscale, dmats = [], []
            for gi, w in enumerate(POOL_WINDOWS):
                cols = slice(gi * dg, (gi + 1) * dg)
                mat = _pool_matrix(pool_ref, gi)
                z = _dot(pooled[gi], mat, "nn")
                rstd = _rstd(z)
                nz = z * rstd
                dyp_g = dyp_piece[:, cols]
                dz = _norm_bwd(dyp_g * scale_ref[:, cols], nz, rstd)
                dpooled = _dot(dz, mat, "nt") * keep
                dpn_buf[pl.ds(start, n), cols] = dpooled * _inv_count(i * tr + start, n, w)
                if is_main:
                    dpooled_buf[:, cols] = dpooled
                    dscale.append(jnp.sum(dyp_g * nz, axis=0, keepdims=True))
                    dmats.append(_dot(pooled[gi], dz, "tn"))
            return dscale, dmats

        dscale, dmats = pool_piece(0, tr, v_ref[...], dyp_ref[...], 1.0, True)
        pool_piece(tr, HALO, vn_ref[...], dypn_ref[...], keep_next, False)
        for gi, w in enumerate(POOL_WINDOWS):
            cols = slice(gi * dg, (gi + 1) * dg)
            back = None
            for k in range(w):
                term = dpn_buf[pl.ds(k, tr), cols]
                back = term if back is None else back + term
            dproj_ref[:, 3 * cw + gi * dg:3 * cw + (gi + 1) * dg] = (back - dpooled_buf[:, cols]).astype(dproj_ref.dtype)
        _accumulate(dscale_ref, jnp.concatenate(dscale, axis=1), i)
        rows = dg // N_CHIPS
        for gi in range(n_groups):
            for c in range(N_CHIPS):
                _accumulate(dpool_ref.at[c, gi], dmats[gi][c * rows:(c + 1) * rows, :], i)

    prev_c, next_c = _halo_specs(t, tr, cw, 1)
    prev_u, next_u = _halo_specs(t, tr, cw, 2)
    prev_v, next_v = _halo_specs(t, tr, cw, 3)
    _, next_b = _halo_specs(t, tr, cw, 0)
    _, next_dyc = _halo_specs(t, tr, cw, 0)
    _, next_dyp = _halo_specs(t, tr, cw, 1)
    return pl.pallas_call(
        body, name="mixers_bwd", grid=(n_tiles,),
        in_specs=[main(0), main(1), main(2), main(3), main(0), main(1),
                  prev_c, prev_u, prev_v,
                  next_b, next_c, next_u, next_v, next_dyc, next_dyp,
                  _const_spec(conv_w.shape), _const_spec(pool_g.shape), _const_spec(pool_scale.shape)]
        + [ANY] * len(deps),
        out_specs=[_row_spec(tr, e), _const_spec(conv_w.shape), _const_spec(pool_g.shape),
                   _const_spec(pool_scale.shape)],
        out_shape=[jax.ShapeDtypeStruct((t, e), MM_DTYPE), jax.ShapeDtypeStruct(conv_w.shape, jnp.float32),
                   jax.ShapeDtypeStruct(pool_g.shape, jnp.float32),
                   jax.ShapeDtypeStruct(pool_scale.shape, jnp.float32)],
        scratch_shapes=[pltpu.VMEM((ext, cw), jnp.float32), pltpu.VMEM((ext, cw), jnp.float32),
                        pltpu.VMEM((tr + HALO, cw), jnp.float32), pltpu.VMEM((tr + HALO, cw), jnp.float32),
                        pltpu.VMEM((tr, cw), jnp.float32)],
        compiler_params=_params(("arbitrary",)),
    )(proj, proj, proj, proj, dmixed, dmixed,
      proj, proj, proj,
      proj, proj, proj, proj, dmixed, dmixed,
      conv_w, pool_g, pool_scale, *deps)


def _cast_rows(name, w, dtype):
    r, c = w.shape
    tr = _tile(r, 512)

    def body(w_ref, o_ref):
        o_ref[...] = w_ref[...].astype(o_ref.dtype)

    return pl.pallas_call(
        body, name=name, grid=(r // tr,), in_specs=[_row_spec(tr, c)], out_specs=_row_spec(tr, c),
        out_shape=jax.ShapeDtypeStruct((r, c), dtype), compiler_params=_params(("arbitrary",)),
    )(w)


def _cast_to_slot(name, w, chip):
    r, c = w.shape
    tr = _tile(r, 512)

    def body(chip_ref, w_ref, o_ref):
        o_ref[...] = w_ref[...].astype(o_ref.dtype)

    return pl.pallas_call(
        body, name=name,
        grid_spec=pltpu.PrefetchScalarGridSpec(
            num_scalar_prefetch=1, grid=(r // tr,),
            in_specs=[pl.BlockSpec((tr, c), lambda i, chip_ref: (i, 0))],
            out_specs=pl.BlockSpec((None, tr, c), lambda i, chip_ref: (chip_ref[0], i, 0))),
        out_shape=jax.ShapeDtypeStruct((N_CHIPS, r, c), MM_DTYPE), compiler_params=_params(("arbitrary",)),
    )(chip, w)


def _sum_own_and_received(name, own, land, chip):
    _, r, c = own.shape
    tr = _tile(r, 256)

    def body(chip_ref, own_ref, a_ref, b_ref, c_ref, o_ref):
        tot = own_ref[...].astype(jnp.float32) + a_ref[...].astype(jnp.float32)
        tot = tot + b_ref[...].astype(jnp.float32)
        o_ref[...] = tot + c_ref[...].astype(jnp.float32)

    def slot(k):
        return pl.BlockSpec((None, tr, c), lambda i, chip_ref: ((chip_ref[0] + k) % N_CHIPS, i, 0))

    return pl.pallas_call(
        body, name=name,
        grid_spec=pltpu.PrefetchScalarGridSpec(
            num_scalar_prefetch=1, grid=(r // tr,), in_specs=[slot(0), slot(1), slot(2), slot(3)],
            out_specs=pl.BlockSpec((tr, c), lambda i, chip_ref: (i, 0))),
        out_shape=jax.ShapeDtypeStruct((r, c), jnp.float32), compiler_params=_params(("arbitrary",)),
    )(chip, own, land, land, land)


def _sum_slots(name, slots):
    n, r, c = slots.shape
    tr = _tile(r, 256)

    def body(s_ref, o_ref):
        tot = s_ref[0].astype(jnp.float32)
        for s in range(1, n):
            tot = tot + s_ref[s].astype(jnp.float32)
        o_ref[...] = tot

    return pl.pallas_call(
        body, name=name, grid=(r // tr,),
        in_specs=[pl.BlockSpec((n, tr, c), lambda i: (0, i, 0))], out_specs=_row_spec(tr, c),
        out_shape=jax.ShapeDtypeStruct((r, c), jnp.float32), compiler_params=_params(("arbitrary",)),
    )(slots)


def _adamw_math(w, g, m, v):
    m = ADAM_B1 * m + (1.0 - ADAM_B1) * g
    v = ADAM_B2 * v + (1.0 - ADAM_B2) * (g * g)
    m_hat = m / (1.0 - ADAM_B1 ** ADAM_STEP)
    v_hat = v / (1.0 - ADAM_B2 ** ADAM_STEP)
    delta = -ADAM_LR * (m_hat / (jnp.sqrt(v_hat) + ADAM_EPS) + ADAM_WD * w)
    return delta, m, v


def _adamw(name, w, m, v, grad_parts):
    r, c = w.shape
    tr = _tile(r, 256)
    n_parts = len(grad_parts)

    def body(*refs):
        w_ref, m_ref, v_ref = refs[:3]
        part_refs = refs[3:3 + n_parts]
        g_ref, d_ref, nm_ref, nv_ref = refs[3 + n_parts:]
        g = part_refs[0][...]
        for p in part_refs[1:]:
            g = g + p[...]
        delta, nm, nv = _adamw_math(w_ref[...], g, m_ref[...], v_ref[...])
        g_ref[...] = g
        d_ref[...] = delta
        nm_ref[...] = nm
        nv_ref[...] = nv

    spec = _row_spec(tr, c)
    out = jax.ShapeDtypeStruct((r, c), jnp.float32)
    return pl.pallas_call(
        body, name=name, grid=(r // tr,), in_specs=[spec] * (3 + n_parts), out_specs=[spec] * 4,
        out_shape=[out] * 4, compiler_params=_params(("arbitrary",)),
    )(w, m, v, *grad_parts)


def _chip_peers():
    x, y, c = lax.axis_index("x"), lax.axis_index("y"), lax.axis_index("c")
    return x, y, c, [(1 - x, y), (x, 1 - y), (1 - x, 1 - y)]


HBM = pl.BlockSpec(memory_space=pltpu.HBM)
SEM = pl.BlockSpec(memory_space=pltpu.SEMAPHORE)
TOKEN = jax.ShapeDtypeStruct((8, 128), jnp.float32)
N_PEER_CHIPS = N_CHIPS - 1


def _in_flight():
    return pltpu.CompilerParams(has_side_effects=pltpu.SideEffectType.DATAFLOW_SIDE_EFFECTING)


def _half_slot(ref, slot, half):
    rows = ref.shape[1] // 2
    return ref.at[slot, pl.ds(half * rows, rows)]


def _gather_start(name, full, deps=()):
    def body(full_ref, *rest):
        send_sems, recv_sems, _, token_ref = rest[len(deps):]
        x, y, c, peers = _chip_peers()
        mine = _half_slot(full_ref, 2 * x + y, c)
        for p, (px, py) in enumerate(peers):
            pltpu.make_async_remote_copy(
                src_ref=mine, dst_ref=mine, send_sem=send_sems.at[p], recv_sem=recv_sems.at[p],
                device_id=(px, py, c), device_id_type=MESH).start()
        token_ref[...] = jnp.zeros_like(token_ref)

    return pl.pallas_call(
        body, name=name,
        out_shape=(pltpu.SemaphoreType.DMA((N_PEER_CHIPS,)), pltpu.SemaphoreType.DMA((N_PEER_CHIPS,)),
                   pltpu.HBM(full.shape, full.dtype), TOKEN),
        in_specs=[HBM] + [ANY] * len(deps), out_specs=(SEM, SEM, HBM, pl.BlockSpec(memory_space=pltpu.VMEM)),
        input_output_aliases={0: 2}, compiler_params=_in_flight(),
    )(pltpu.with_memory_space_constraint(full, pltpu.HBM), *deps)


def _gather_wait(name, send_sems, recv_sems, full, after):
    def body(full_ref, send_sems, recv_sems, after_ref, out_ref):
        x, y, c, peers = _chip_peers()
        for p, (px, py) in enumerate(peers):
            cp = pltpu.make_async_remote_copy(
                src_ref=_half_slot(full_ref, 2 * x + y, c), dst_ref=_half_slot(full_ref, 2 * px + py, c),
                send_sem=send_sems.at[p], recv_sem=recv_sems.at[p], device_id=(px, py, c), device_id_type=MESH)
            cp.wait_send()
            cp.wait_recv()

    return pl.pallas_call(
        body, name=name, out_shape=pltpu.HBM(full.shape, full.dtype),
        in_specs=(HBM, SEM, SEM, ANY), out_specs=HBM, input_output_aliases={0: 0}, compiler_params=_in_flight(),
    )(full, send_sems, recv_sems, after)


def _forward_sibling(name, fulls):
    n = len(fulls)

    def body(*refs):
        in_refs, out_refs = refs[:n], refs[n:2 * n]
        send_sems, recv_sems = refs[2 * n:]
        x, y, c, peers = _chip_peers()
        sends, recvs = [], []
        for a in range(n):
            for p, (px, py) in enumerate(peers):
                k = N_PEER_CHIPS * a + p
                slot = 2 * px + py
                cp = pltpu.make_async_remote_copy(
                    src_ref=_half_slot(in_refs[a], slot, c), dst_ref=_half_slot(out_refs[a], slot, c),
                    send_sem=send_sems.at[k], recv_sem=recv_sems.at[k], device_id=(x, y, 1 - c), device_id_type=MESH)
                cp.start()
                sends.append(cp)
                recvs.append(pltpu.make_async_remote_copy(
                    src_ref=_half_slot(in_refs[a], slot, c), dst_ref=_half_slot(out_refs[a], slot, 1 - c),
                    send_sem=send_sems.at[k], recv_sem=recv_sems.at[k], device_id=(x, y, 1 - c), device_id_type=MESH))
        for cp in recvs:
            cp.wait_recv()
        for cp in sends:
            cp.wait_send()

    return pl.pallas_call(
        body, name=name, in_specs=[ANY] * n, out_specs=[ANY] * n,
        out_shape=[jax.ShapeDtypeStruct(f.shape, f.dtype) for f in fulls],
        input_output_aliases={a: a for a in range(n)},
        scratch_shapes=[pltpu.SemaphoreType.DMA((N_PEER_CHIPS * n,)), pltpu.SemaphoreType.DMA((N_PEER_CHIPS * n,))],
    )(*fulls)


def _scatter_start(name, grads, deps=()):
    def body(g_ref, land_ref, *rest):
        send_sems, recv_sems, _, _, token_ref = rest[len(deps):]
        x, y, c, peers = _chip_peers()
        me = 2 * x + y
        for p, (px, py) in enumerate(peers):
            pltpu.make_async_remote_copy(
                src_ref=g_ref.at[2 * px + py], dst_ref=land_ref.at[me], send_sem=send_sems.at[p],
                recv_sem=recv_sems.at[p], device_id=(px, py, c), device_id_type=MESH).start()
        token_ref[...] = jnp.zeros_like(token_ref)

    land = lax.empty(grads.shape, grads.dtype)
    return pl.pallas_call(
        body, name=name,
        out_shape=(pltpu.SemaphoreType.DMA((N_PEER_CHIPS,)), pltpu.SemaphoreType.DMA((N_PEER_CHIPS,)),
                   pltpu.HBM(grads.shape, grads.dtype), pltpu.HBM(grads.shape, grads.dtype), TOKEN),
        in_specs=[HBM, HBM] + [ANY] * len(deps),
        out_specs=(SEM, SEM, HBM, HBM, pl.BlockSpec(memory_space=pltpu.VMEM)),
        input_output_aliases={0: 2, 1: 3}, compiler_params=_in_flight(),
    )(pltpu.with_memory_space_constraint(grads, pltpu.HBM), pltpu.with_memory_space_constraint(land, pltpu.HBM), *deps)


def _scatter_wait(name, send_sems, recv_sems, grads, land, after):
    def body(g_ref, land_ref, send_sems, recv_sems, after_ref, g_out, land_out):
        x, y, c, peers = _chip_peers()
        for p, (px, py) in enumerate(peers):
            cp = pltpu.make_async_remote_copy(
                src_ref=g_ref.at[2 * px + py], dst_ref=land_ref.at[2 * px + py], send_sem=send_sems.at[p],
                recv_sem=recv_sems.at[p], device_id=(px, py, c), device_id_type=MESH)
            cp.wait_send()
            cp.wait_recv()

    return pl.pallas_call(
        body, name=name,
        out_shape=(pltpu.HBM(grads.shape, grads.dtype), pltpu.HBM(land.shape, land.dtype)),
        in_specs=(HBM, HBM, SEM, SEM, ANY), out_specs=(HBM, HBM), input_output_aliases={0: 0, 1: 1},
        compiler_params=_in_flight(),
    )(grads, land, send_sems, recv_sems, after)


def _swap_sibling(name, parts):
    n = len(parts)

    def body(*refs):
        in_refs, out_refs = refs[:n], refs[n:2 * n]
        send_sems, recv_sems = refs[2 * n:]
        x, y, c = lax.axis_index("x"), lax.axis_index("y"), lax.axis_index("c")
        copies = []
        for a in range(n):
            cp = pltpu.make_async_remote_copy(
                src_ref=in_refs[a], dst_ref=out_refs[a], send_sem=send_sems.at[a], recv_sem=recv_sems.at[a],
                device_id=(x, y, 1 - c), device_id_type=MESH)
            cp.start()
            copies.append(cp)
        for cp in copies:
            cp.wait_recv()
        for cp in copies:
            cp.wait_send()

    return pl.pallas_call(
        body, name=name, in_specs=[ANY] * n, out_specs=[ANY] * n,
        out_shape=[jax.ShapeDtypeStruct(p.shape, p.dtype) for p in parts],
        scratch_shapes=[pltpu.SemaphoreType.DMA((n,)), pltpu.SemaphoreType.DMA((n,))],
    )(*parts)


def _gather_devices(name, block):
    def body(in_ref, out_ref, send_sems, recv_sems, local_sem):
        x, y, c = lax.axis_index("x"), lax.axis_index("y"), lax.axis_index("c")
        me = 4 * x + 2 * y + c
        local = pltpu.make_async_copy(in_ref, out_ref.at[me], local_sem)
        local.start()
        sends, recvs = [], []
        k = 0
        for fx in range(2):
            for fy in range(2):
                for fc in range(2):
                    if fx == fy == fc == 0:
                        continue
                    px = x if fx == 0 else 1 - x
                    py = y if fy == 0 else 1 - y
                    pc = c if fc == 0 else 1 - c
                    cp = pltpu.make_async_remote_copy(
                        src_ref=in_ref, dst_ref=out_ref.at[me], send_sem=send_sems.at[k], recv_sem=recv_sems.at[k],
                        device_id=(px, py, pc), device_id_type=MESH)
                    cp.start()
                    sends.append(cp)
                    recvs.append(pltpu.make_async_remote_copy(
                        src_ref=in_ref, dst_ref=out_ref.at[4 * px + 2 * py + pc], send_sem=send_sems.at[k],
                        recv_sem=recv_sems.at[k], device_id=(px, py, pc), device_id_type=MESH))
                    k += 1
        for cp in recvs:
            cp.wait_recv()
        for cp in sends:
            cp.wait_send()
        local.wait()

    return pl.pallas_call(
        body, name=name, in_specs=[ANY], out_specs=ANY,
        out_shape=jax.ShapeDtypeStruct((N_DEV,) + block.shape, block.dtype),
        scratch_shapes=[pltpu.SemaphoreType.DMA((N_DEV - 1,)), pltpu.SemaphoreType.DMA((N_DEV - 1,)),
                        pltpu.SemaphoreType.DMA],
    )(block)


def _pack_rows(pieces, width):
    flat = jnp.concatenate([p.reshape(-1) for p in pieces])
    rows = -(-flat.shape[0] // width)
    rows = -(-rows // 8) * 8
    flat = jnp.pad(flat, (0, rows * width - flat.shape[0]))
    return flat.reshape(rows, width)


def _unpack_rows(packed, shapes):
    flat = packed.reshape(-1)
    out, off = [], 0
    for s in shapes:
        size = 1
        for d in s:
            size *= d
        out.append(flat[off:off + size].reshape(s))
        off += size
    return out


def kernel(x, ln_mix_pre, w_in, conv_w, pool_w, pool_scale, w_out, ln_mix_post, ln_ffn_pre, w_gate, w_up, w_down, ln_ffn_post, loss_target, m_ln_mix_pre, m_w_in, m_conv_w, m_pool_w, m_pool_scale, m_w_out, m_ln_mix_post, m_ln_ffn_pre, m_w_gate, m_w_up, m_w_down, m_ln_ffn_post, v_ln_mix_pre, v_w_in, v_conv_w, v_pool_w, v_pool_scale, v_w_out, v_ln_mix_post, v_ln_ffn_pre, v_w_gate, v_w_up, v_w_down, v_ln_ffn_post):
    t, d = x.shape[1], x.shape[2]
    e4 = w_in.shape[2]
    e = N_CHIPS * e4
    f4 = w_gate.shape[2]
    f = N_CHIPS * f4
    n_groups, dg4, dg = pool_w.shape[1], pool_w.shape[2], pool_w.shape[3]
    cw4 = conv_w.shape[2]
    chip = 2 * lax.axis_index("x") + lax.axis_index("y")
    xs, tgt = x[0], loss_target[0]

    big = {"w_in": w_in[0], "w_out": w_out[0], "w_gate": w_gate[0], "w_up": w_up[0], "w_down": w_down[0],
           "pool_w": pool_w[0].reshape(n_groups * dg4, dg)}
    names = ["w_in", "pool_w", "w_out", "w_gate", "w_up", "w_down"]
    chip_arr = chip.astype(jnp.int32).reshape(1)
    conv_all = _gather_devices("gather_conv_w", _pack_rows([conv_w[0]], 128))
    conv_full = jnp.concatenate(
        [conv_all[2 * j].reshape(-1)[:CONV_K * cw4].reshape(CONV_K, cw4) for j in range(N_CHIPS)], axis=1)
    in_flight, deps = {}, (conv_all,)
    for k in names:
        in_flight[k] = _gather_start("gather_start_" + k, _cast_to_slot("cast_" + k, big[k], chip_arr), deps)
        deps = (in_flight[k][3],)

    def landed(ks, after):
        fulls = []
        for k in ks:
            send_sems, recv_sems, full, _ = in_flight[k]
            fulls.append(_gather_wait("gather_wait_" + k, send_sems, recv_sems, full, after))
            after = fulls[-1]
        return _forward_sibling("forward_" + ks[0], fulls)

    h = _pre_norm(xs, ln_mix_pre, deps)
    (win_g,) = landed(["w_in"], h)
    tm = _tile(t, 1024)
    proj = _matmul(
        "in_proj", grid=(t // tm, N_CHIPS, 1), mode="nn",
        pairs=[[(h, win_g)]],
        pair_specs=[[(pl.BlockSpec((tm, d), lambda i, j, k: (i, 0)),
                      pl.BlockSpec((None, d, e4), lambda i, j, k: (j, 0, 0)))]],
        acc_shapes=[(tm, e4)], out_shapes=[jax.ShapeDtypeStruct((t, e), jnp.float32)],
        out_specs=[pl.BlockSpec((tm, e4), lambda i, j, k: (i, j))], epilogue=_identity_epilogue)[0]
    pool_g, wout_full = landed(["pool_w", "w_out"], proj)
    pool_g, wout_full = pool_g.reshape(N_CHIPS, n_groups, dg4, dg), wout_full.reshape(d, d)
    mixed = _mixers_fwd(proj, conv_full, pool_g, pool_scale)
    tn = _tile(d, 1024)
    mix_out = _matmul(
        "out_proj", grid=(t // tm, d // tn, 1), mode="nn",
        pairs=[[(mixed, wout_full)]],
        pair_specs=[[(pl.BlockSpec((tm, d), lambda i, j, k: (i, 0)),
                      pl.BlockSpec((d, tn), lambda i, j, k: (0, j)))]],
        acc_shapes=[(tm, tn)], out_shapes=[jax.ShapeDtypeStruct((t, d), jnp.float32)],
        out_specs=[pl.BlockSpec((tm, tn), lambda i, j, k: (i, j))], epilogue=_identity_epilogue)[0]
    x1, hf = _post_mix(xs, mix_out, ln_mix_post, ln_ffn_pre)
    wg_g, wu_g = landed(["w_gate", "w_up"], hf)

    def gate_up_epilogue(accs, extra_refs):
        g, up = accs
        return g, up, g * jax.nn.sigmoid(g) * up

    tm_ff = _tile(t, 512)
    ff_tile = jax.ShapeDtypeStruct((t, f), MM_DTYPE)
    g_act, up_act, act = _matmul(
        "gate_up", grid=(N_CHIPS, t // tm_ff, 1), mode="nn",
        pairs=[[(hf, wg_g)], [(hf, wu_g)]],
        pair_specs=[[(pl.BlockSpec((tm_ff, d), lambda j, i, k: (i, 0)),
                      pl.BlockSpec((None, d, f4), lambda j, i, k: (j, 0, 0)))]] * 2,
        acc_shapes=[(tm_ff, f4)] * 2, out_shapes=[ff_tile] * 3,
        out_specs=[pl.BlockSpec((tm_ff, f4), lambda j, i, k: (i, j))] * 3, epilogue=gate_up_epilogue)
    wdown_full = landed(["w_down"], act)[0].reshape(f, d)
    ff = _matmul(
        "down_proj", grid=(t // tm, d // tn, N_CHIPS), mode="nn",
        pairs=[[(act, wdown_full)]],
        pair_specs=[[(pl.BlockSpec((tm, f4), lambda i, j, k: (i, k)),
                      pl.BlockSpec((f4, tn), lambda i, j, k: (k, j)))]],
        acc_shapes=[(tm, tn)], out_shapes=[jax.ShapeDtypeStruct((t, d), jnp.float32)],
        out_specs=[pl.BlockSpec((tm, tn), lambda i, j, k: (i, j))], epilogue=_identity_epilogue)[0]
    dout, dff, loss_tile, dg_ffn_post = _loss_head(ff, x1, tgt, ln_ffn_post)

    def dact_epilogue(accs, extra_refs):
        dact = accs[0]
        g = extra_refs[0][...].astype(jnp.float32)
        up = extra_refs[1][...].astype(jnp.float32)
        sig = jax.nn.sigmoid(g)
        silu = g * sig
        return dact * up * (sig + silu * (1.0 - sig)), dact * silu

    ff_spec_ji = pl.BlockSpec((tm_ff, f4), lambda j, i, k: (i, j))
    dg_act, dup_act = _matmul(
        "dact", grid=(N_CHIPS, t // tm_ff, 1), mode="nt",
        pairs=[[(dff, wdown_full)]],
        pair_specs=[[(pl.BlockSpec((tm_ff, d), lambda j, i, k: (i, 0)),
                      pl.BlockSpec((f4, d), lambda j, i, k: (j, 0)))]],
        acc_shapes=[(tm_ff, f4)], extras=[g_act, up_act], extra_specs=[ff_spec_ji, ff_spec_ji],
        out_shapes=[ff_tile] * 2, out_specs=[ff_spec_ji] * 2, epilogue=dact_epilogue)
    tk = _tile(t, 1024)
    dw_down = _matmul(
        "dw_down", grid=(N_CHIPS, d // tn, t // tk), mode="tn",
        pairs=[[(act, dff)]],
        pair_specs=[[(pl.BlockSpec((tk, f4), lambda i, j, k: (k, i)),
                      pl.BlockSpec((tk, tn), lambda i, j, k: (k, j)))]],
        acc_shapes=[(f4, tn)], out_shapes=[jax.ShapeDtypeStruct((N_CHIPS, f4, d), COMM_DTYPE)],
        out_specs=[pl.BlockSpec((None, f4, tn), lambda i, j, k: (i, 0, j))], epilogue=_identity_epilogue)[0]
    leaving = {"w_down": _scatter_start("scatter_start_w_down", dw_down)}
    dhf = _matmul(
        "dhf", grid=(t // tm, d // tn, N_CHIPS), mode="nt",
        pairs=[[(dg_act, wg_g), (dup_act, wu_g)]],
        pair_specs=[[(pl.BlockSpec((tm, f4), lambda i, j, k: (i, k)),
                      pl.BlockSpec((None, tn, f4), lambda i, j, k: (k, j, 0)))] * 2],
        acc_shapes=[(tm, tn)], out_shapes=[jax.ShapeDtypeStruct((t, d), jnp.float32)],
        out_specs=[pl.BlockSpec((tm, tn), lambda i, j, k: (i, j))], epilogue=_identity_epilogue,
        deps=leaving["w_down"][4:])[0]
    tmo = _tile(d, 1024)
    slot_ff = jax.ShapeDtypeStruct((N_CHIPS, d, f4), COMM_DTYPE)
    dw_gate, dw_up = _matmul(
        "dw_gate_up", grid=(d // tmo, N_CHIPS, t // tk), mode="tn",
        pairs=[[(hf, dg_act)], [(hf, dup_act)]],
        pair_specs=[[(pl.BlockSpec((tk, tmo), lambda i, j, k: (k, i)),
                      pl.BlockSpec((tk, f4), lambda i, j, k: (k, j)))]] * 2,
        acc_shapes=[(tmo, f4)] * 2, out_shapes=[slot_ff] * 2,
        out_specs=[pl.BlockSpec((None, tmo, f4), lambda i, j, k: (j, i, 0))] * 2, epilogue=_identity_epilogue)
    leaving["w_gate"] = _scatter_start("scatter_start_w_gate", dw_gate)
    leaving["w_up"] = _scatter_start("scatter_start_w_up", dw_up, leaving["w_gate"][4:])
    dx1, dmo, dg_ffn_pre, dg_mix_post = _ffn_pre_bwd(dhf, x1, dout, mix_out, ln_ffn_pre, ln_mix_post,
                                                     leaving["w_up"][4:])

    dmixed = _matmul(
        "dmixed", grid=(t // tm, d // tn, 1), mode="nt",
        pairs=[[(dmo, wout_full)]],
        pair_specs=[[(pl.BlockSpec((tm, d), lambda i, j, k: (i, 0)),
                      pl.BlockSpec((tn, d), lambda i, j, k: (j, 0)))]],
        acc_shapes=[(tm, tn)], out_shapes=[jax.ShapeDtypeStruct((t, d), jnp.float32)],
        out_specs=[pl.BlockSpec((tm, tn), lambda i, j, k: (i, j))], epilogue=_identity_epilogue)[0]
    dw_out = _matmul(
        "dw_out", grid=(d // tmo, d // tn, t // tk), mode="tn",
        pairs=[[(mixed, dmo)]],
        pair_specs=[[(pl.BlockSpec((tk, tmo), lambda i, j, k: (k, i)),
                      pl.BlockSpec((tk, tn), lambda i, j, k: (k, j)))]],
        acc_shapes=[(tmo, tn)], out_shapes=[jax.ShapeDtypeStruct((d, d), COMM_DTYPE)],
        out_specs=[pl.BlockSpec((tmo, tn), lambda i, j, k: (i, j))], epilogue=_identity_epilogue)[0]
    leaving["w_out"] = _scatter_start("scatter_start_w_out", dw_out.reshape(N_CHIPS, d // N_CHIPS, d))
    dproj, dconv_full, dpool_g, dpool_scale = _mixers_bwd(proj, dmixed, conv_full, pool_g, pool_scale,
                                                          leaving["w_out"][4:])
    dpool_slots = _cast_rows("cast_dpool", dpool_g.reshape(N_CHIPS * n_groups * dg4, dg), COMM_DTYPE)
    leaving["pool_w"] = _scatter_start("scatter_start_pool_w", dpool_slots.reshape(N_CHIPS, n_groups * dg4, dg))
    dw_in = _matmul(
        "dw_in", grid=(d // tmo, N_CHIPS, t // tk), mode="tn",
        pairs=[[(h, dproj)]],
        pair_specs=[[(pl.BlockSpec((tk, tmo), lambda i, j, k: (k, i)),
                      pl.BlockSpec((tk, e4), lambda i, j, k: (k, j)))]],
        acc_shapes=[(tmo, e4)], out_shapes=[jax.ShapeDtypeStruct((N_CHIPS, d, e4), COMM_DTYPE)],
        out_specs=[pl.BlockSpec((None, tmo, e4), lambda i, j, k: (j, i, 0))], epilogue=_identity_epilogue,
        deps=leaving["pool_w"][4:])[0]
    leaving["w_in"] = _scatter_start("scatter_start_w_in", dw_in)
    dh = _matmul(
        "dh", grid=(t // tm, d // tn, N_CHIPS), mode="nt",
        pairs=[[(dproj, win_g)]],
        pair_specs=[[(pl.BlockSpec((tm, e4), lambda i, j, k: (i, k)),
                      pl.BlockSpec((None, tn, e4), lambda i, j, k: (k, j, 0)))]],
        acc_shapes=[(tm, tn)], out_shapes=[jax.ShapeDtypeStruct((t, d), jnp.float32)],
        out_specs=[pl.BlockSpec((tm, tn), lambda i, j, k: (i, j))], epilogue=_identity_epilogue,
        deps=leaving["w_in"][4:])[0]
    grad_x, dg_mix_pre = _mix_pre_bwd(dh, xs, dx1, ln_mix_pre)

    names = ["w_down", "w_gate", "w_up", "w_out", "pool_w", "w_in"]
    partial, after = [], grad_x
    for k in names:
        send_sems, recv_sems, own, land, _ = leaving[k]
        own, land = _scatter_wait("scatter_wait_" + k, send_sems, recv_sems, own, land, after)
        partial.append(_sum_own_and_received("sum_" + k, own, land, chip_arr))
        after = land
    other = _swap_sibling("swap_grads", partial)
    moments = {"w_in": (m_w_in, v_w_in), "w_out": (m_w_out, v_w_out), "w_gate": (m_w_gate, v_w_gate),
               "w_up": (m_w_up, v_w_up), "w_down": (m_w_down, v_w_down), "pool_w": (m_pool_w, v_pool_w)}
    result = {}
    for k, mine, theirs in zip(names, partial, other):
        shape = moments[k][0].shape
        two_d = big[k].shape
        outs = _adamw("adamw_" + k, big[k], moments[k][0].reshape(two_d), moments[k][1].reshape(two_d),
                      [mine, theirs])
        result[k] = [o.reshape(shape) for o in outs]

    small_shapes = [(1, d)] * 4 + [pool_scale.shape, (CONV_K, N_CHIPS * cw4)]
    packed = _pack_rows([dg_mix_pre, dg_mix_post, dg_ffn_pre, dg_ffn_post, dpool_scale, dconv_full], 1024)
    summed = _sum_slots("sum_small", _gather_devices("gather_small", packed))
    g_mix_pre, g_mix_post, g_ffn_pre, g_ffn_post, g_pool_scale, g_conv_full = _unpack_rows(summed, small_shapes)
    g_conv = lax.dynamic_slice(g_conv_full, (0, chip * cw4), (CONV_K, cw4))[None]
    small = [("ln_mix_pre", ln_mix_pre, m_ln_mix_pre, v_ln_mix_pre, g_mix_pre),
             ("conv_w", conv_w, m_conv_w, v_conv_w, g_conv),
             ("pool_scale", pool_scale, m_pool_scale, v_pool_scale, g_pool_scale),
             ("ln_mix_post", ln_mix_post, m_ln_mix_post, v_ln_mix_post, g_mix_post),
             ("ln_ffn_pre", ln_ffn_pre, m_ln_ffn_pre, v_ln_ffn_pre, g_ffn_pre),
             ("ln_ffn_post", ln_ffn_post, m_ln_ffn_post, v_ln_ffn_post, g_ffn_post)]
    shapes_small = [s[1].shape for s in small]
    packs = [_pack_rows([s[q] for s in small], 128) for q in (1, 2, 3, 4)]
    outs = _adamw("adamw_small", packs[0], packs[1], packs[2], [packs[3]])
    unpacked = [_unpack_rows(o, shapes_small) for o in outs]
    for idx, s in enumerate(small):
        result[s[0]] = [u[idx] for u in unpacked]

    loss = lax.psum(loss_tile[0, 0], ("x", "y", "c"))
    order = ["ln_mix_pre", "w_in", "conv_w", "pool_w", "pool_scale", "w_out", "ln_mix_post", "ln_ffn_pre",
             "w_gate", "w_up", "w_down", "ln_ffn_post"]
    return (loss, grad_x[None], *[result[k][0] for k in order], *[result[k][1] for k in order],
            *[result[k][2] for k in order], *[result[k][3] for k in order])
```

```python
import functools

import jax
import jax.numpy as jnp
from jax import lax
from jax.experimental import pallas as pl
from jax.experimental.pallas import tpu as pltpu

EPS = 1e-6
CONV_HEAD_DIM = 128
CONV_K = 3
POOL_WINDOWS = (2, 4, 8, 16)
HALO = 16
EPILOGUE_ROWS = 64
N_CHIPS = 4
N_DEV = 8

ADAM_LR = 0.001
ADAM_B1 = 0.9
ADAM_B2 = 0.999
ADAM_EPS = 1e-08
ADAM_WD = 0.01
ADAM_STEP = 10

MM_DTYPE = jnp.bfloat16
COMM_DTYPE = jnp.bfloat16
VMEM_LIMIT = 62 * 1024 * 1024
MESH = pl.DeviceIdType.MESH
ANY = pl.BlockSpec(memory_space=pl.ANY)


def _tile(n, pref):
    t = min(pref, n)
    while n % t:
        t //= 2
    return t


def _params(sem):
    return pltpu.CompilerParams(dimension_semantics=sem, vmem_limit_bytes=VMEM_LIMIT)


def _rstd(x):
    return lax.rsqrt(jnp.mean(x * x, axis=-1, keepdims=True) + EPS)


def _norm_bwd(dn, n, rstd):
    return rstd * (dn - n * jnp.mean(dn * n, axis=-1, keepdims=True))


_DOT_DIMS = {
    "nn": (((1,), (0,)), ((), ())),
    "nt": (((1,), (1,)), ((), ())),
    "tn": (((0,), (0,)), ((), ())),
}


def _dot(a, b, mode):
    return lax.dot_general(a.astype(MM_DTYPE), b.astype(MM_DTYPE), _DOT_DIMS[mode],
                           preferred_element_type=jnp.float32)


def _matmul(name, *, grid, mode, pairs, pair_specs, acc_shapes, extras=(), extra_specs=(),
            out_shapes, out_specs, epilogue, deps=(), epi_rows=0, sum_outs=()):
    nk = grid[2]
    operands, operand_specs, where, counts = [], [], {}, []
    pair_index = []
    for ps, ss in zip(pairs, pair_specs):
        counts.append(len(ps))
        for arrays, specs in zip(ps, ss):
            for arr, spec in zip(arrays, specs):
                key = (id(arr), id(spec))
                if key not in where:
                    where[key] = len(operands)
                    operands.append(arr)
                    operand_specs.append(spec)
                pair_index.append(where[key])
    n_operands = len(operands)
    n_extra = len(extras)
    n_out = len(out_shapes)
    n_in = n_operands + n_extra + len(deps)

    def body(*refs):
        pair_refs = [refs[q] for q in pair_index]
        extra_refs = refs[n_operands:n_operands + n_extra]
        out_refs = refs[n_in:n_in + n_out]
        acc_refs = refs[n_in + n_out:]
        i, k = pl.program_id(0), pl.program_id(2)

        def partial_sums():
            res, p = [], 0
            for cnt in counts:
                tot = None
                for _ in range(cnt):
                    d = _dot(pair_refs[p][...], pair_refs[p + 1][...], mode)
                    tot = d if tot is None else tot + d
                    p += 2
                res.append(tot)
            return res

        def finish(accs):
            n_rows = accs[0].shape[0]
            step = epi_rows or n_rows
            for r0 in range(0, n_rows, step):
                rows = slice(r0, r0 + step)
                outs = epilogue([a[rows, :] for a in accs], [e.at[rows] if e.shape[0] == n_rows else e
                                                             for e in extra_refs])
                for q, (o_ref, o) in enumerate(zip(out_refs, outs)):
                    if q not in sum_outs:
                        o_ref[rows, :] = o.astype(o_ref.dtype)
                    elif r0 == 0:
                        _accumulate(o_ref, o, i)
                    else:
                        o_ref[...] += o

        if nk == 1:
            finish(partial_sums())
        else:
            @pl.when(k == 0)
            def _():
                for acc_ref in acc_refs:
                    acc_ref[...] = jnp.zeros_like(acc_ref)

            for acc_ref, s in zip(acc_refs, partial_sums()):
                acc_ref[...] += s

            @pl.when(k == nk - 1)
            def _():
                finish(acc_refs)

    scratch = [] if nk == 1 else [pltpu.VMEM(s, jnp.float32) for s in acc_shapes]
    return pl.pallas_call(
        body, name=name, grid=grid,
        in_specs=operand_specs + list(extra_specs) + [ANY] * len(deps), out_specs=list(out_specs),
        out_shape=list(out_shapes), scratch_shapes=scratch,
        compiler_params=_params(("arbitrary", "arbitrary", "arbitrary")),
    )(*operands, *extras, *deps)


def _identity_epilogue(accs, extra_refs):
    return tuple(accs)


def _row_spec(tr, n):
    return pl.BlockSpec((tr, n), lambda i: (i, 0))


def _const_spec(shape):
    return pl.BlockSpec(shape, lambda i: tuple(0 for _ in shape))


def _accumulate(ref, val, i):
    @pl.when(i == 0)
    def _():
        ref[...] = val

    @pl.when(i > 0)
    def _():
        ref[...] += val


def _pre_norm(x, gain, deps=()):
    t, d = x.shape
    tr = _tile(t, 512)

    def body(x_ref, g_ref, *rest):
        h_ref = rest[-1]
        xv = x_ref[...]
        h_ref[...] = (xv * _rstd(xv) * g_ref[...]).astype(h_ref.dtype)

    return pl.pallas_call(
        body, name="pre_norm", grid=(t // tr,),
        in_specs=[_row_spec(tr, d), _const_spec((1, d))] + [ANY] * len(deps), out_specs=_row_spec(tr, d),
        out_shape=jax.ShapeDtypeStruct((t, d), MM_DTYPE), compiler_params=_params(("arbitrary",)),
    )(x, gain, *deps)


def _pool_matrix(pool_ref, g):
    return jnp.concatenate([pool_ref[c, g] for c in range(N_CHIPS)], axis=0)


def _inv_count(row0, n, w):
    pos = (row0 + lax.broadcasted_iota(jnp.int32, (n, 1), 0) + 1).astype(jnp.float32)
    return 1.0 / jnp.minimum(pos, float(w))


def _conv_piece(cu_buf, start, n, b_piece, convw_ref):
    conv = None
    for k in range(CONV_K):
        term = convw_ref[k:k + 1, :] * cu_buf[pl.ds(HALO + start + k - (CONV_K - 1), n), :]
        conv = term if conv is None else conv + term
    return conv, b_piece * conv


def _head_stats(a, width):
    return [_rstd(a[:, h * width:(h + 1) * width]) for h in range(a.shape[1] // width)]


def _pooled_piece(v_buf, start, n, v_piece, row0, dg):
    outs = []
    for gi, w in enumerate(POOL_WINDOWS):
        cols = slice(gi * dg, (gi + 1) * dg)
        win = None
        for k in range(w):
            term = v_buf[pl.ds(HALO + start - k, n), cols]
            win = term if win is None else win + term
        outs.append(win * _inv_count(row0 + start, n, w) - v_piece[:, cols])
    return outs


def _halo_specs(t, tr, width, col):
    per = tr // HALO
    last = t // HALO - 1
    prev = pl.BlockSpec((HALO, width), lambda i: (jnp.maximum(i * per - 1, 0), col))
    nxt = pl.BlockSpec((HALO, width), lambda i: (jnp.minimum((i + 1) * per, last), col))
    return prev, nxt


def _mixers_fwd(proj, conv_w, pool_g, pool_scale):
    t, e = proj.shape
    cw = e // 4
    dg = pool_g.shape[-1]
    tr = _tile(t, 256)

    def main(col):
        return pl.BlockSpec((tr, cw), lambda i: (i, col))

    def body(b_ref, c_ref, u_ref, v_ref, cp_ref, up_ref, vp_ref, convw_ref, pool_ref, scale_ref,
             out_ref, cu_buf, v_buf):
        i = pl.program_id(0)
        keep = (i > 0).astype(jnp.float32)
        cu_buf[pl.ds(0, HALO), :] = cp_ref[...] * up_ref[...] * keep
        cu_buf[pl.ds(HALO, tr), :] = c_ref[...] * u_ref[...]
        v_buf[pl.ds(0, HALO), :] = vp_ref[...] * keep
        v_buf[pl.ds(HALO, tr), :] = v_ref[...]
        _, a = _conv_piece(cu_buf, 0, tr, b_ref[...], convw_ref)
        for h, rstd in enumerate(_head_stats(a, CONV_HEAD_DIM)):
            cols = slice(h * CONV_HEAD_DIM, (h + 1) * CONV_HEAD_DIM)
            out_ref[:, cols] = (a[:, cols] * rstd).astype(out_ref.dtype)
        pooled = _pooled_piece(v_buf, 0, tr, v_ref[...], i * tr, dg)
        for gi, p in enumerate(pooled):
            z = _dot(p, _pool_matrix(pool_ref, gi), "nn")
            cols = slice(gi * dg, (gi + 1) * dg)
            out_ref[:, cw + gi * dg:cw + (gi + 1) * dg] = (z * _rstd(z) * scale_ref[:, cols]).astype(out_ref.dtype)

    prev_c, _ = _halo_specs(t, tr, cw, 1)
    prev_u, _ = _halo_specs(t, tr, cw, 2)
    prev_v, _ = _halo_specs(t, tr, cw, 3)
    return pl.pallas_call(
        body, name="mixers_fwd", grid=(t // tr,),
        in_specs=[main(0), main(1), main(2), main(3), prev_c, prev_u, prev_v,
                  _const_spec(conv_w.shape), _const_spec(pool_g.shape), _const_spec(pool_scale.shape)],
        out_specs=_row_spec(tr, 2 * cw),
        out_shape=jax.ShapeDtypeStruct((t, 2 * cw), MM_DTYPE),
        scratch_shapes=[pltpu.VMEM((tr + HALO, cw), jnp.float32), pltpu.VMEM((tr + HALO, cw), jnp.float32)],
        compiler_params=_params(("arbitrary",)),
    )(proj, proj, proj, proj, proj, proj, proj, conv_w, pool_g, pool_scale)


def _mixers_bwd(proj, dmixed, conv_w, pool_g, pool_scale, deps=()):
    t, e = proj.shape
    cw = e // 4
    dg = pool_g.shape[-1]
    n_groups = len(POOL_WINDOWS)
    tr = _tile(t, 256)
    n_tiles = t // tr
    ext = tr + 2 * HALO

    def main(col):
        return pl.BlockSpec((tr, cw), lambda i: (i, col))

    def body(b_ref, c_ref, u_ref, v_ref, dyc_ref, dyp_ref,
             cp_ref, up_ref, vp_ref,
             bn_ref, cn_ref, un_ref, vn_ref, dycn_ref, dypn_ref,
             convw_ref, pool_ref, scale_ref, *rest):
        (dproj_ref, dconvw_ref, dpool_ref, dscale_ref,
         cu_buf, v_buf, dconv_buf, dpn_buf, dpooled_buf) = rest[len(deps):]
        i = pl.program_id(0)
        keep_prev = (i > 0).astype(jnp.float32)
        keep_next = (i < n_tiles - 1).astype(jnp.float32)
        cu_buf[pl.ds(0, HALO), :] = cp_ref[...] * up_ref[...] * keep_prev
        cu_buf[pl.ds(HALO, tr), :] = c_ref[...] * u_ref[...]
        cu_buf[pl.ds(HALO + tr, HALO), :] = cn_ref[...] * un_ref[...]
        v_buf[pl.ds(0, HALO), :] = vp_ref[...] * keep_prev
        v_buf[pl.ds(HALO, tr), :] = v_ref[...]
        v_buf[pl.ds(HALO + tr, HALO), :] = vn_ref[...]

        def conv_piece(start, n, b_piece, dyc_piece, keep, is_main):
            conv, a = _conv_piece(cu_buf, start, n, b_piece, convw_ref)
            for h, rstd in enumerate(_head_stats(a, CONV_HEAD_DIM)):
                cols = slice(h * CONV_HEAD_DIM, (h + 1) * CONV_HEAD_DIM)
                da = _norm_bwd(dyc_piece[:, cols], a[:, cols] * rstd, rstd)
                dconv_buf[pl.ds(start, n), cols] = da * b_piece[:, cols] * keep
                if is_main:
                    dproj_ref[:, cols] = (da * conv[:, cols]).astype(dproj_ref.dtype)

        conv_piece(0, tr, b_ref[...], dyc_ref[...], 1.0, True)
        conv_piece(tr, HALO, bn_ref[...], dycn_ref[...], keep_next, False)

        dconv_main = dconv_buf[pl.ds(0, tr), :]
        dcu = None
        dw_rows = []
        for k in range(CONV_K):
            shift = CONV_K - 1 - k
            term = convw_ref[k:k + 1, :] * dconv_buf[pl.ds(shift, tr), :]
            dcu = term if dcu is None else dcu + term
            dw_rows.append(jnp.sum(dconv_main * cu_buf[pl.ds(HALO - shift, tr), :], axis=0, keepdims=True))
        dproj_ref[:, cw:2 * cw] = (dcu * u_ref[...]).astype(dproj_ref.dtype)
        dproj_ref[:, 2 * cw:3 * cw] = (dcu * c_ref[...]).astype(dproj_ref.dtype)
        _accumulate(dconvw_ref, jnp.concatenate(dw_rows, axis=0), i)

        def pool_piece(start, n, v_piece, dyp_piece, keep, is_main):
            pooled = _pooled_piece(v_buf, start, n, v_piece, i * tr, dg)
            dscale, dmats = [], []
            for gi, w in enumerate(POOL_WINDOWS):
                cols = slice(gi * dg, (gi + 1) * dg)
                mat = _pool_matrix(pool_ref, gi)
                z = _dot(pooled[gi], mat, "nn")
                rstd = _rstd(z)
                nz = z * rstd
                dyp_g = dyp_piece[:, cols]
                dz = _norm_bwd(dyp_g * scale_ref[:, cols], nz, rstd)
                dpooled = _dot(dz, mat, "nt") * keep
                dpn_buf[pl.ds(start, n), cols] = dpooled * _inv_count(i * tr + start, n, w)
                if is_main:
                    dpooled_buf[:, cols] = dpooled
                    dscale.append(jnp.sum(dyp_g * nz, axis=0, keepdims=True))
                    dmats.append(_dot(pooled[gi], dz, "tn"))
            return dscale, dmats

        dscale, dmats = pool_piece(0, tr, v_ref[...], dyp_ref[...], 1.0, True)
        pool_piece(tr, HALO, vn_ref[...], dypn_ref[...], keep_next, False)
        for gi, w in enumerate(POOL_WINDOWS):
            cols = slice(gi * dg, (gi + 1) * dg)
            back = None
            for k in range(w):
                term = dpn_buf[pl.ds(k, tr), cols]
                back = term if back is None else back + term
            dproj_ref[:, 3 * cw + gi * dg:3 * cw + (gi + 1) * dg] = (back - dpooled_buf[:, cols]).astype(dproj_ref.dtype)
        _accumulate(dscale_ref, jnp.concatenate(dscale, axis=1), i)
        rows = dg // N_CHIPS
        for gi in range(n_groups):
            for c in range(N_CHIPS):
                _accumulate(dpool_ref.at[c, gi], dmats[gi][c * rows:(c + 1) * rows, :], i)

    prev_c, next_c = _halo_specs(t, tr, cw, 1)
    prev_u, next_u = _halo_specs(t, tr, cw, 2)
    prev_v, next_v = _halo_specs(t, tr, cw, 3)
    _, next_b = _halo_specs(t, tr, cw, 0)
    _, next_dyc = _halo_specs(t, tr, cw, 0)
    _, next_dyp = _halo_specs(t, tr, cw, 1)
    return pl.pallas_call(
        body, name="mixers_bwd", grid=(n_tiles,),
        in_specs=[main(0), main(1), main(2), main(3), main(0), main(1),
                  prev_c, prev_u, prev_v,
                  next_b, next_c, next_u, next_v, next_dyc, next_dyp,
                  _const_spec(conv_w.shape), _const_spec(pool_g.shape), _const_spec(pool_scale.shape)]
        + [ANY] * len(deps),
        out_specs=[_row_spec(tr, e), _const_spec(conv_w.shape), _const_spec(pool_g.shape),
                   _const_spec(pool_scale.shape)],
        out_shape=[jax.ShapeDtypeStruct((t, e), MM_DTYPE), jax.ShapeDtypeStruct(conv_w.shape, jnp.float32),
                   jax.ShapeDtypeStruct(pool_g.shape, jnp.float32),
                   jax.ShapeDtypeStruct(pool_scale.shape, jnp.float32)],
        scratch_shapes=[pltpu.VMEM((ext, cw), jnp.float32), pltpu.VMEM((ext, cw), jnp.float32),
                        pltpu.VMEM((tr + HALO, cw), jnp.float32), pltpu.VMEM((tr + HALO, cw), jnp.float32),
                        pltpu.VMEM((tr, cw), jnp.float32)],
        compiler_params=_params(("arbitrary",)),
    )(proj, proj, proj, proj, dmixed, dmixed,
      proj, proj, proj,
      proj, proj, proj, proj, dmixed, dmixed,
      conv_w, pool_g, pool_scale, *deps)


def _cast_rows(name, w, dtype):
    r, c = w.shape
    tr = _tile(r, 512)

    def body(w_ref, o_ref):
        o_ref[...] = w_ref[...].astype(o_ref.dtype)

    return pl.pallas_call(
        body, name=name, grid=(r // tr,), in_specs=[_row_spec(tr, c)], out_specs=_row_spec(tr, c),
        out_shape=jax.ShapeDtypeStruct((r, c), dtype), compiler_params=_params(("arbitrary",)),
    )(w)


def _cast_to_slot(name, w, chip, by_columns=False):
    r, c = w.shape
    tr = _tile(r, 512)
    if by_columns:
        out_spec = pl.BlockSpec((tr, c), lambda i, chip_ref: (i, chip_ref[0]))
        out_shape = jax.ShapeDtypeStruct((r, N_CHIPS * c), MM_DTYPE)
    else:
        out_spec = pl.BlockSpec((None, tr, c), lambda i, chip_ref: (chip_ref[0], i, 0))
        out_shape = jax.ShapeDtypeStruct((N_CHIPS, r, c), MM_DTYPE)

    def body(chip_ref, w_ref, o_ref):
        o_ref[...] = w_ref[...].astype(o_ref.dtype)

    return pl.pallas_call(
        body, name=name,
        grid_spec=pltpu.PrefetchScalarGridSpec(
            num_scalar_prefetch=1, grid=(r // tr,),
            in_specs=[pl.BlockSpec((tr, c), lambda i, chip_ref: (i, 0))], out_specs=out_spec),
        out_shape=out_shape, compiler_params=_params(("arbitrary",)),
    )(chip, w)


def _sum_own_and_received(name, own, land, chip):
    _, r, c = land.shape
    tr = _tile(r, 256)

    def body(chip_ref, own_ref, a_ref, b_ref, c_ref, o_ref):
        tot = own_ref[...].astype(jnp.float32) + a_ref[...].astype(jnp.float32)
        tot = tot + b_ref[...].astype(jnp.float32)
        o_ref[...] = tot + c_ref[...].astype(jnp.float32)

    def slot(k):
        return pl.BlockSpec((None, tr, c), lambda i, chip_ref: ((chip_ref[0] + k) % N_CHIPS, i, 0))

    own_spec = slot(0) if len(own.shape) == 3 else pl.BlockSpec((tr, c), lambda i, chip_ref: (i, chip_ref[0]))
    return pl.pallas_call(
        body, name=name,
        grid_spec=pltpu.PrefetchScalarGridSpec(
            num_scalar_prefetch=1, grid=(r // tr,), in_specs=[own_spec, slot(1), slot(2), slot(3)],
            out_specs=pl.BlockSpec((tr, c), lambda i, chip_ref: (i, 0))),
        out_shape=jax.ShapeDtypeStruct((r, c), jnp.float32), compiler_params=_params(("arbitrary",)),
    )(chip, own, land, land, land)


def _sum_slots(name, slots):
    n, r, c = slots.shape
    tr = _tile(r, 256)

    def body(s_ref, o_ref):
        tot = s_ref[0].astype(jnp.float32)
        for s in range(1, n):
            tot = tot + s_ref[s].astype(jnp.float32)
        o_ref[...] = tot

    return pl.pallas_call(
        body, name=name, grid=(r // tr,),
        in_specs=[pl.BlockSpec((n, tr, c), lambda i: (0, i, 0))], out_specs=_row_spec(tr, c),
        out_shape=jax.ShapeDtypeStruct((r, c), jnp.float32), compiler_params=_params(("arbitrary",)),
    )(slots)


def _adamw_math(w, g, m, v):
    m = ADAM_B1 * m + (1.0 - ADAM_B1) * g
    v = ADAM_B2 * v + (1.0 - ADAM_B2) * (g * g)
    m_hat = m / (1.0 - ADAM_B1 ** ADAM_STEP)
    v_hat = v / (1.0 - ADAM_B2 ** ADAM_STEP)
    delta = -ADAM_LR * (m_hat / (jnp.sqrt(v_hat) + ADAM_EPS) + ADAM_WD * w)
    return delta, m, v


def _adamw(name, w, m, v, grad_parts):
    r, c = w.shape
    tr = _tile(r, 256)
    n_parts = len(grad_parts)

    def body(*refs):
        w_ref, m_ref, v_ref = refs[:3]
        part_refs = refs[3:3 + n_parts]
        g_ref, d_ref, nm_ref, nv_ref = refs[3 + n_parts:]
        g = part_refs[0][...]
        for p in part_refs[1:]:
            g = g + p[...]
        delta, nm, nv = _adamw_math(w_ref[...], g, m_ref[...], v_ref[...])
        g_ref[...] = g
        d_ref[...] = delta
        nm_ref[...] = nm
        nv_ref[...] = nv

    spec = _row_spec(tr, c)
    out = jax.ShapeDtypeStruct((r, c), jnp.float32)
    return pl.pallas_call(
        body, name=name, grid=(r // tr,), in_specs=[spec] * (3 + n_parts), out_specs=[spec] * 4,
        out_shape=[out] * 4, compiler_params=_params(("arbitrary",)),
    )(w, m, v, *grad_parts)


def _chip_peers():
    x, y, c = lax.axis_index("x"), lax.axis_index("y"), lax.axis_index("c")
    return x, y, c, [(1 - x, y), (x, 1 - y), (1 - x, 1 - y)]


HBM = pl.BlockSpec(memory_space=pltpu.HBM)
SEM = pl.BlockSpec(memory_space=pltpu.SEMAPHORE)
TOKEN = jax.ShapeDtypeStruct((8, 128), jnp.float32)
N_PEER_CHIPS = N_CHIPS - 1


def _in_flight():
    return pltpu.CompilerParams(has_side_effects=pltpu.SideEffectType.DATAFLOW_SIDE_EFFECTING)


def _slot(ref, slot):
    if len(ref.shape) == 3:
        return ref.at[slot]
    width = ref.shape[1] // N_CHIPS
    return ref.at[:, pl.ds(pl.multiple_of(slot * width, 128), width)]


def _half_slot(ref, slot, half):
    rows = ref.shape[-2] // 2
    if len(ref.shape) == 3:
        return ref.at[slot, pl.ds(half * rows, rows)]
    width = ref.shape[1] // N_CHIPS
    return ref.at[pl.ds(half * rows, rows), pl.ds(pl.multiple_of(slot * width, 128), width)]


def _slot_shape(shape):
    return shape[1:] if len(shape) == 3 else (shape[0], shape[1] // N_CHIPS)


def _gather_start(name, full, deps=()):
    def body(full_ref, *rest):
        send_sems, recv_sems, _, token_ref = rest[len(deps):]
        x, y, c, peers = _chip_peers()
        mine = _half_slot(full_ref, 2 * x + y, c)
        for p, (px, py) in enumerate(peers):
            pltpu.make_async_remote_copy(
                src_ref=mine, dst_ref=mine, send_sem=send_sems.at[p], recv_sem=recv_sems.at[p],
                device_id=(px, py, c), device_id_type=MESH).start()
        token_ref[...] = jnp.zeros_like(token_ref)

    return pl.pallas_call(
        body, name=name,
        out_shape=(pltpu.SemaphoreType.DMA((N_PEER_CHIPS,)), pltpu.SemaphoreType.DMA((N_PEER_CHIPS,)),
                   pltpu.HBM(full.shape, full.dtype), TOKEN),
        in_specs=[HBM] + [ANY] * len(deps), out_specs=(SEM, SEM, HBM, pl.BlockSpec(memory_space=pltpu.VMEM)),
        input_output_aliases={0: 2}, compiler_params=_in_flight(),
    )(pltpu.with_memory_space_constraint(full, pltpu.HBM), *deps)


def _gather_wait(name, send_sems, recv_sems, full, after):
    def body(full_ref, send_sems, recv_sems, after_ref, out_ref):
        x, y, c, peers = _chip_peers()
        for p, (px, py) in enumerate(peers):
            cp = pltpu.make_async_remote_copy(
                src_ref=_half_slot(full_ref, 2 * x + y, c), dst_ref=_half_slot(full_ref, 2 * px + py, c),
                send_sem=send_sems.at[p], recv_sem=recv_sems.at[p], device_id=(px, py, c), device_id_type=MESH)
            cp.wait_send()
            cp.wait_recv()

    return pl.pallas_call(
        body, name=name, out_shape=pltpu.HBM(full.shape, full.dtype),
        in_specs=(HBM, SEM, SEM, ANY), out_specs=HBM, input_output_aliases={0: 0}, compiler_params=_in_flight(),
    )(full, send_sems, recv_sems, after)


def _forward_sibling(name, fulls):
    n = len(fulls)

    def body(*refs):
        in_refs, out_refs = refs[:n], refs[n:2 * n]
        send_sems, recv_sems = refs[2 * n:]
        x, y, c, peers = _chip_peers()
        sends, recvs = [], []
        for a in range(n):
            for p, (px, py) in enumerate(peers):
                k = N_PEER_CHIPS * a + p
                slot = 2 * px + py
                cp = pltpu.make_async_remote_copy(
                    src_ref=_half_slot(in_refs[a], slot, c), dst_ref=_half_slot(out_refs[a], slot, c),
                    send_sem=send_sems.at[k], recv_sem=recv_sems.at[k], device_id=(x, y, 1 - c), device_id_type=MESH)
                cp.start()
                sends.append(cp)
                recvs.append(pltpu.make_async_remote_copy(
                    src_ref=_half_slot(in_refs[a], slot, c), dst_ref=_half_slot(out_refs[a], slot, 1 - c),
                    send_sem=send_sems.at[k], recv_sem=recv_sems.at[k], device_id=(x, y, 1 - c), device_id_type=MESH))
        for cp in recvs:
            cp.wait_recv()
        for cp in sends:
            cp.wait_send()

    return pl.pallas_call(
        body, name=name, in_specs=[ANY] * n, out_specs=[ANY] * n,
        out_shape=[jax.ShapeDtypeStruct(f.shape, f.dtype) for f in fulls],
        input_output_aliases={a: a for a in range(n)},
        scratch_shapes=[pltpu.SemaphoreType.DMA((N_PEER_CHIPS * n,)), pltpu.SemaphoreType.DMA((N_PEER_CHIPS * n,))],
    )(*fulls)


def _scatter_start(name, grads, deps=()):
    def body(g_ref, land_ref, *rest):
        send_sems, recv_sems, _, _, token_ref = rest[len(deps):]
        x, y, c, peers = _chip_peers()
        me = 2 * x + y
        for p, (px, py) in enumerate(peers):
            pltpu.make_async_remote_copy(
                src_ref=_slot(g_ref, 2 * px + py), dst_ref=land_ref.at[me], send_sem=send_sems.at[p],
                recv_sem=recv_sems.at[p], device_id=(px, py, c), device_id_type=MESH).start()
        token_ref[...] = jnp.zeros_like(token_ref)

    land = lax.empty((N_CHIPS,) + _slot_shape(grads.shape), grads.dtype)
    return pl.pallas_call(
        body, name=name,
        out_shape=(pltpu.SemaphoreType.DMA((N_PEER_CHIPS,)), pltpu.SemaphoreType.DMA((N_PEER_CHIPS,)),
                   pltpu.HBM(grads.shape, grads.dtype), pltpu.HBM(land.shape, land.dtype), TOKEN),
        in_specs=[HBM, HBM] + [ANY] * len(deps),
        out_specs=(SEM, SEM, HBM, HBM, pl.BlockSpec(memory_space=pltpu.VMEM)),
        input_output_aliases={0: 2, 1: 3}, compiler_params=_in_flight(),
    )(pltpu.with_memory_space_constraint(grads, pltpu.HBM), pltpu.with_memory_space_constraint(land, pltpu.HBM), *deps)


def _scatter_wait(name, send_sems, recv_sems, grads, land, after):
    def body(g_ref, land_ref, send_sems, recv_sems, after_ref, g_out, land_out):
        x, y, c, peers = _chip_peers()
        for p, (px, py) in enumerate(peers):
            cp = pltpu.make_async_remote_copy(
                src_ref=_slot(g_ref, 2 * px + py), dst_ref=land_ref.at[2 * px + py], send_sem=send_sems.at[p],
                recv_sem=recv_sems.at[p], device_id=(px, py, c), device_id_type=MESH)
            cp.wait_send()
            cp.wait_recv()

    return pl.pallas_call(
        body, name=name,
        out_shape=(pltpu.HBM(grads.shape, grads.dtype), pltpu.HBM(land.shape, land.dtype)),
        in_specs=(HBM, HBM, SEM, SEM, ANY), out_specs=(HBM, HBM), input_output_aliases={0: 0, 1: 1},
        compiler_params=_in_flight(),
    )(grads, land, send_sems, recv_sems, after)


def _swap_sibling(name, parts):
    n = len(parts)

    def body(*refs):
        in_refs, out_refs = refs[:n], refs[n:2 * n]
        send_sems, recv_sems = refs[2 * n:]
        x, y, c = lax.axis_index("x"), lax.axis_index("y"), lax.axis_index("c")
        copies = []
        for a in range(n):
            cp = pltpu.make_async_remote_copy(
                src_ref=in_refs[a], dst_ref=out_refs[a], send_sem=send_sems.at[a], recv_sem=recv_sems.at[a],
                device_id=(x, y, 1 - c), device_id_type=MESH)
            cp.start()
            copies.append(cp)
        for cp in copies:
            cp.wait_recv()
        for cp in copies:
            cp.wait_send()

    return pl.pallas_call(
        body, name=name, in_specs=[ANY] * n, out_specs=[ANY] * n,
        out_shape=[jax.ShapeDtypeStruct(p.shape, p.dtype) for p in parts],
        scratch_shapes=[pltpu.SemaphoreType.DMA((n,)), pltpu.SemaphoreType.DMA((n,))],
    )(*parts)


def _gather_devices(name, block):
    def body(in_ref, out_ref, send_sems, recv_sems, local_sem):
        x, y, c = lax.axis_index("x"), lax.axis_index("y"), lax.axis_index("c")
        me = 4 * x + 2 * y + c
        local = pltpu.make_async_copy(in_ref, out_ref.at[me], local_sem)
        local.start()
        sends, recvs = [], []
        k = 0
        for fx in range(2):
            for fy in range(2):
                for fc in range(2):
                    if fx == fy == fc == 0:
                        continue
                    px = x if fx == 0 else 1 - x
                    py = y if fy == 0 else 1 - y
                    pc = c if fc == 0 else 1 - c
                    cp = pltpu.make_async_remote_copy(
                        src_ref=in_ref, dst_ref=out_ref.at[me], send_sem=send_sems.at[k], recv_sem=recv_sems.at[k],
                        device_id=(px, py, pc), device_id_type=MESH)
                    cp.start()
                    sends.append(cp)
                    recvs.append(pltpu.make_async_remote_copy(
                        src_ref=in_ref, dst_ref=out_ref.at[4 * px + 2 * py + pc], send_sem=send_sems.at[k],
                        recv_sem=recv_sems.at[k], device_id=(px, py, pc), device_id_type=MESH))
                    k += 1
        for cp in recvs:
            cp.wait_recv()
        for cp in sends:
            cp.wait_send()
        local.wait()

    return pl.pallas_call(
        body, name=name, in_specs=[ANY], out_specs=ANY,
        out_shape=jax.ShapeDtypeStruct((N_DEV,) + block.shape, block.dtype),
        scratch_shapes=[pltpu.SemaphoreType.DMA((N_DEV - 1,)), pltpu.SemaphoreType.DMA((N_DEV - 1,)),
                        pltpu.SemaphoreType.DMA],
    )(block)


def _pack_rows(pieces, width):
    flat = jnp.concatenate([p.reshape(-1) for p in pieces])
    rows = -(-flat.shape[0] // width)
    rows = -(-rows // 8) * 8
    flat = jnp.pad(flat, (0, rows * width - flat.shape[0]))
    return flat.reshape(rows, width)


def _unpack_rows(packed, shapes):
    flat = packed.reshape(-1)
    out, off = [], 0
    for s in shapes:
        size = 1
        for d in s:
            size *= d
        out.append(flat[off:off + size].reshape(s))
        off += size
    return out


def kernel(x, ln_mix_pre, w_in, conv_w, pool_w, pool_scale, w_out, ln_mix_post, ln_ffn_pre, w_gate, w_up, w_down, ln_ffn_post, loss_target, m_ln_mix_pre, m_w_in, m_conv_w, m_pool_w, m_pool_scale, m_w_out, m_ln_mix_post, m_ln_ffn_pre, m_w_gate, m_w_up, m_w_down, m_ln_ffn_post, v_ln_mix_pre, v_w_in, v_conv_w, v_pool_w, v_pool_scale, v_w_out, v_ln_mix_post, v_ln_ffn_pre, v_w_gate, v_w_up, v_w_down, v_ln_ffn_post):
    t, d = x.shape[1], x.shape[2]
    e4 = w_in.shape[2]
    e = N_CHIPS * e4
    f4 = w_gate.shape[2]
    f = N_CHIPS * f4
    n_groups, dg4, dg = pool_w.shape[1], pool_w.shape[2], pool_w.shape[3]
    cw4 = conv_w.shape[2]
    chip = 2 * lax.axis_index("x") + lax.axis_index("y")
    xs, tgt = x[0], loss_target[0]

    big = {"w_in": w_in[0], "w_out": w_out[0], "w_gate": w_gate[0], "w_up": w_up[0], "w_down": w_down[0],
           "pool_w": pool_w[0].reshape(n_groups * dg4, dg)}
    names = ["w_in", "pool_w", "w_out", "w_gate", "w_up", "w_down"]
    chip_arr = chip.astype(jnp.int32).reshape(1)
    conv_all = _gather_devices("gather_conv_w", _pack_rows([conv_w[0]], 128))
    conv_full = jnp.concatenate(
        [conv_all[2 * j].reshape(-1)[:CONV_K * cw4].reshape(CONV_K, cw4) for j in range(N_CHIPS)], axis=1)
    in_flight, deps = {}, (conv_all,)
    for k in names:
        by_columns = k in ("w_gate", "w_up")
        in_flight[k] = _gather_start(
            "gather_start_" + k, _cast_to_slot("cast_" + k, big[k], chip_arr, by_columns), deps)
        deps = (in_flight[k][3],)

    def landed(ks, after):
        fulls = []
        for k in ks:
            send_sems, recv_sems, full, _ = in_flight[k]
            fulls.append(_gather_wait("gather_wait_" + k, send_sems, recv_sems, full, after))
            after = fulls[-1]
        return _forward_sibling("forward_" + ks[0], fulls)

    def rows3(tile, width):
        return pl.BlockSpec((tile, width), lambda i, j, k: (i, 0))

    gain3 = pl.BlockSpec((1, d), lambda i, j, k: (0, 0))
    f32_td = jax.ShapeDtypeStruct((t, d), jnp.float32)
    mm_td = jax.ShapeDtypeStruct((t, d), MM_DTYPE)
    f32_gain = jax.ShapeDtypeStruct((1, d), jnp.float32)

    h = _pre_norm(xs, ln_mix_pre, deps)
    (win_g,) = landed(["w_in"], h)
    tm = _tile(t, 1024)
    proj = _matmul(
        "in_proj", grid=(t // tm, N_CHIPS, 1), mode="nn",
        pairs=[[(h, win_g)]],
        pair_specs=[[(rows3(tm, d), pl.BlockSpec((None, d, e4), lambda i, j, k: (j, 0, 0)))]],
        acc_shapes=[(tm, e4)], out_shapes=[jax.ShapeDtypeStruct((t, e), jnp.float32)],
        out_specs=[pl.BlockSpec((tm, e4), lambda i, j, k: (i, j))], epilogue=_identity_epilogue)[0]
    pool_g, wout_full = landed(["pool_w", "w_out"], proj)
    pool_g, wout_full = pool_g.reshape(N_CHIPS, n_groups, dg4, dg), wout_full.reshape(d, d)
    mixed = _mixers_fwd(proj, conv_full, pool_g, pool_scale)

    def post_mix_epilogue(accs, extra_refs):
        x_ref, g2_ref, g3_ref = extra_refs
        mo = accs[0]
        x1 = x_ref[...] + mo * _rstd(mo) * g2_ref[...]
        return mo, x1, x1 * _rstd(x1) * g3_ref[...]

    tm_mix = _tile(t, 256)
    mix_out, x1, hf = _matmul(
        "out_proj", grid=(t // tm_mix, 1, 1), mode="nn",
        pairs=[[(mixed, wout_full)]],
        pair_specs=[[(rows3(tm_mix, d), pl.BlockSpec((d, d), lambda i, j, k: (0, 0)))]],
        acc_shapes=[(tm_mix, d)], extras=[xs, ln_mix_post, ln_ffn_pre],
        extra_specs=[rows3(tm_mix, d), gain3, gain3], out_shapes=[f32_td, f32_td, mm_td],
        out_specs=[rows3(tm_mix, d)] * 3, epilogue=post_mix_epilogue, epi_rows=EPILOGUE_ROWS)
    wg_full, wu_full = landed(["w_gate", "w_up"], hf)

    def gate_up_epilogue(accs, extra_refs):
        g, up = accs
        return g, up, g * jax.nn.sigmoid(g) * up

    tf = _tile(f, 512)
    ff_tile = jax.ShapeDtypeStruct((t, f), MM_DTYPE)
    ff_spec = pl.BlockSpec((tm, tf), lambda j, i, k: (i, j))
    hf_spec = pl.BlockSpec((tm, d), lambda j, i, k: (i, 0))
    g_act, up_act, act = _matmul(
        "gate_up", grid=(f // tf, t // tm, 1), mode="nn",
        pairs=[[(hf, wg_full)], [(hf, wu_full)]],
        pair_specs=[[(hf_spec, pl.BlockSpec((d, tf), lambda j, i, k: (0, j)))]] * 2,
        acc_shapes=[(tm, tf)] * 2, out_shapes=[ff_tile] * 3, out_specs=[ff_spec] * 3, epilogue=gate_up_epilogue)
    wdown_full = landed(["w_down"], act)[0].reshape(f, d)

    def loss_epilogue(accs, extra_refs):
        x1_ref, tg_ref, g_ref = extra_refs
        ff_v = accs[0]
        rstd = _rstd(ff_v)
        n = ff_v * rstd
        g = g_ref[...]
        err = x1_ref[...] + n * g - tg_ref[...]
        rows_loss = 0.5 * jnp.sum(jnp.mean(err * err, axis=-1, keepdims=True), axis=0, keepdims=True)
        dout = err / d
        return (dout, _norm_bwd(dout * g, n, rstd), jnp.broadcast_to(rows_loss, (8, 128)),
                jnp.sum(dout * n, axis=0, keepdims=True))

    tm_row = _tile(t, 512)
    dout, dff, loss_tile, dg_ffn_post = _matmul(
        "down_proj", grid=(t // tm_row, 1, f // tf), mode="nn",
        pairs=[[(act, wdown_full)]],
        pair_specs=[[(pl.BlockSpec((tm_row, tf), lambda i, j, k: (i, k)),
                      pl.BlockSpec((tf, d), lambda i, j, k: (k, 0)))]],
        acc_shapes=[(tm_row, d)], extras=[x1, tgt, ln_ffn_post], extra_specs=[rows3(tm_row, d)] * 2 + [gain3],
        out_shapes=[f32_td, mm_td, jax.ShapeDtypeStruct((8, 128), jnp.float32), f32_gain],
        out_specs=[rows3(tm_row, d)] * 2 + [pl.BlockSpec((8, 128), lambda i, j, k: (0, 0)), gain3],
        epilogue=loss_epilogue, epi_rows=EPILOGUE_ROWS, sum_outs=(2, 3))

    def dact_epilogue(accs, extra_refs):
        dact = accs[0]
        g = extra_refs[0][...].astype(jnp.float32)
        up = extra_refs[1][...].astype(jnp.float32)
        sig = jax.nn.sigmoid(g)
        silu = g * sig
        return dact * up * (sig + silu * (1.0 - sig)), dact * silu

    dg_act, dup_act = _matmul(
        "dact", grid=(f // tf, t // tm, 1), mode="nt",
        pairs=[[(dff, wdown_full)]],
        pair_specs=[[(hf_spec, pl.BlockSpec((tf, d), lambda j, i, k: (j, 0)))]],
        acc_shapes=[(tm, tf)], extras=[g_act, up_act], extra_specs=[ff_spec, ff_spec],
        out_shapes=[ff_tile] * 2, out_specs=[ff_spec] * 2, epilogue=dact_epilogue)
    tk = _tile(t, 1024)
    tn = _tile(d, 1024)
    dw_down = _matmul(
        "dw_down", grid=(N_CHIPS, d // tn, t // tk), mode="tn",
        pairs=[[(act, dff)]],
        pair_specs=[[(pl.BlockSpec((tk, f4), lambda i, j, k: (k, i)),
                      pl.BlockSpec((tk, tn), lambda i, j, k: (k, j)))]],
        acc_shapes=[(f4, tn)], out_shapes=[jax.ShapeDtypeStruct((N_CHIPS, f4, d), COMM_DTYPE)],
        out_specs=[pl.BlockSpec((None, f4, tn), lambda i, j, k: (i, 0, j))], epilogue=_identity_epilogue)[0]
    leaving = {"w_down": _scatter_start("scatter_start_w_down", dw_down)}

    def ffn_pre_epilogue(accs, extra_refs):
        x1_ref, dout_ref, mo_ref, g3_ref, g2_ref = extra_refs
        dhf_v = accs[0]
        x1_v = x1_ref[...]
        rstd3 = _rstd(x1_v)
        n3 = x1_v * rstd3
        dx1_v = dout_ref[...] + _norm_bwd(dhf_v * g3_ref[...], n3, rstd3)
        mo = mo_ref[...]
        rstd2 = _rstd(mo)
        n2 = mo * rstd2
        return (dx1_v, _norm_bwd(dx1_v * g2_ref[...], n2, rstd2), jnp.sum(dhf_v * n3, axis=0, keepdims=True),
                jnp.sum(dx1_v * n2, axis=0, keepdims=True))

    dx1, dmo, dg_ffn_pre, dg_mix_post = _matmul(
        "dhf", grid=(t // tm_row, 1, f // tf), mode="nt",
        pairs=[[(dg_act, wg_full), (dup_act, wu_full)]],
        pair_specs=[[(pl.BlockSpec((tm_row, tf), lambda i, j, k: (i, k)),
                      pl.BlockSpec((d, tf), lambda i, j, k: (0, k)))] * 2],
        acc_shapes=[(tm_row, d)], extras=[x1, dout, mix_out, ln_ffn_pre, ln_mix_post],
        extra_specs=[rows3(tm_row, d)] * 3 + [gain3] * 2, out_shapes=[f32_td, mm_td, f32_gain, f32_gain],
        out_specs=[rows3(tm_row, d)] * 2 + [gain3] * 2, epilogue=ffn_pre_epilogue, epi_rows=EPILOGUE_ROWS,
        sum_outs=(2, 3), deps=leaving["w_down"][4:])
    tmo = _tile(d, 2048)
    grad_ff = jax.ShapeDtypeStruct((d, f), COMM_DTYPE)
    dw_gate, dw_up = _matmul(
        "dw_gate_up", grid=(d // tmo, f // tf, t // tk), mode="tn",
        pairs=[[(hf, dg_act)], [(hf, dup_act)]],
        pair_specs=[[(pl.BlockSpec((tk, tmo), lambda i, j, k: (k, i)),
                      pl.BlockSpec((tk, tf), lambda i, j, k: (k, j)))]] * 2,
        acc_shapes=[(tmo, tf)] * 2, out_shapes=[grad_ff] * 2,
        out_specs=[pl.BlockSpec((tmo, tf), lambda i, j, k: (i, j))] * 2, epilogue=_identity_epilogue)
    leaving["w_gate"] = _scatter_start("scatter_start_w_gate", dw_gate)
    leaving["w_up"] = _scatter_start("scatter_start_w_up", dw_up, leaving["w_gate"][4:])

    dmixed = _matmul(
        "dmixed", grid=(t // tm, d // tn, 1), mode="nt",
        pairs=[[(dmo, wout_full)]],
        pair_specs=[[(rows3(tm, d), pl.BlockSpec((tn, d), lambda i, j, k: (j, 0)))]],
        acc_shapes=[(tm, tn)], out_shapes=[f32_td],
        out_specs=[pl.BlockSpec((tm, tn), lambda i, j, k: (i, j))], epilogue=_identity_epilogue,
        deps=leaving["w_up"][4:])[0]
    tmo = _tile(d, 1024)
    dw_out = _matmul(
        "dw_out", grid=(d // tmo, d // tn, t // tk), mode="tn",
        pairs=[[(mixed, dmo)]],
        pair_specs=[[(pl.BlockSpec((tk, tmo), lambda i, j, k: (k, i)),
                      pl.BlockSpec((tk, tn), lambda i, j, k: (k, j)))]],
        acc_shapes=[(tmo, tn)], out_shapes=[jax.ShapeDtypeStruct((d, d), COMM_DTYPE)],
        out_specs=[pl.BlockSpec((tmo, tn), lambda i, j, k: (i, j))], epilogue=_identity_epilogue)[0]
    leaving["w_out"] = _scatter_start("scatter_start_w_out", dw_out.reshape(N_CHIPS, d // N_CHIPS, d))
    dproj, dconv_full, dpool_g, dpool_scale = _mixers_bwd(proj, dmixed, conv_full, pool_g, pool_scale,
                                                          leaving["w_out"][4:])
    dpool_slots = _cast_rows("cast_dpool", dpool_g.reshape(N_CHIPS * n_groups * dg4, dg), COMM_DTYPE)
    leaving["pool_w"] = _scatter_start("scatter_start_pool_w", dpool_slots.reshape(N_CHIPS, n_groups * dg4, dg))
    dw_in = _matmul(
        "dw_in", grid=(d // tmo, N_CHIPS, t // tk), mode="tn",
        pairs=[[(h, dproj)]],
        pair_specs=[[(pl.BlockSpec((tk, tmo), lambda i, j, k: (k, i)),
                      pl.BlockSpec((tk, e4), lambda i, j, k: (k, j)))]],
        acc_shapes=[(tmo, e4)], out_shapes=[jax.ShapeDtypeStruct((N_CHIPS, d, e4), COMM_DTYPE)],
        out_specs=[pl.BlockSpec((None, tmo, e4), lambda i, j, k: (j, i, 0))], epilogue=_identity_epilogue,
        deps=leaving["pool_w"][4:])[0]
    leaving["w_in"] = _scatter_start("scatter_start_w_in", dw_in)

    def mix_pre_epilogue(accs, extra_refs):
        x_ref, dx1_ref, g_ref = extra_refs
        dh_v = accs[0]
        xv = x_ref[...]
        rstd = _rstd(xv)
        n = xv * rstd
        return dx1_ref[...] + _norm_bwd(dh_v * g_ref[...], n, rstd), jnp.sum(dh_v * n, axis=0, keepdims=True)

    grad_x, dg_mix_pre = _matmul(
        "dh", grid=(t // tm_row, 1, N_CHIPS), mode="nt",
        pairs=[[(dproj, win_g)]],
        pair_specs=[[(pl.BlockSpec((tm_row, e4), lambda i, j, k: (i, k)),
                      pl.BlockSpec((None, d, e4), lambda i, j, k: (k, 0, 0)))]],
        acc_shapes=[(tm_row, d)], extras=[xs, dx1, ln_mix_pre], extra_specs=[rows3(tm_row, d)] * 2 + [gain3],
        out_shapes=[f32_td, f32_gain], out_specs=[rows3(tm_row, d), gain3], epilogue=mix_pre_epilogue,
        epi_rows=EPILOGUE_ROWS, sum_outs=(1,), deps=leaving["w_in"][4:])

    names = ["w_down", "w_gate", "w_up", "w_out", "pool_w", "w_in"]
    partial, after = [], grad_x
    for k in names:
        send_sems, recv_sems, own, land, _ = leaving[k]
        own, land = _scatter_wait("scatter_wait_" + k, send_sems, recv_sems, own, land, after)
        partial.append(_sum_own_and_received("sum_" + k, own, land, chip_arr))
        after = land
    other = _swap_sibling("swap_grads", partial)
    moments = {"w_in": (m_w_in, v_w_in), "w_out": (m_w_out, v_w_out), "w_gate": (m_w_gate, v_w_gate),
               "w_up": (m_w_up, v_w_up), "w_down": (m_w_down, v_w_down), "pool_w": (m_pool_w, v_pool_w)}
    result = {}
    for k, mine, theirs in zip(names, partial, other):
        shape = moments[k][0].shape
        two_d = big[k].shape
        outs = _adamw("adamw_" + k, big[k], moments[k][0].reshape(two_d), moments[k][1].reshape(two_d),
                      [mine, theirs])
        result[k] = [o.reshape(shape) for o in outs]

    small_shapes = [(1, d)] * 4 + [pool_scale.shape, (CONV_K, N_CHIPS * cw4)]
    packed = _pack_rows([dg_mix_pre, dg_mix_post, dg_ffn_pre, dg_ffn_post, dpool_scale, dconv_full], 1024)
    summed = _sum_slots("sum_small", _gather_devices("gather_small", packed))
    g_mix_pre, g_mix_post, g_ffn_pre, g_ffn_post, g_pool_scale, g_conv_full = _unpack_rows(summed, small_shapes)
    g_conv = lax.dynamic_slice(g_conv_full, (0, chip * cw4), (CONV_K, cw4))[None]
    small = [("ln_mix_pre", ln_mix_pre, m_ln_mix_pre, v_ln_mix_pre, g_mix_pre),
             ("conv_w", conv_w, m_conv_w, v_conv_w, g_conv),
             ("pool_scale", pool_scale, m_pool_scale, v_pool_scale, g_pool_scale),
             ("ln_mix_post", ln_mix_post, m_ln_mix_post, v_ln_mix_post, g_mix_post),
             ("ln_ffn_pre", ln_ffn_pre, m_ln_ffn_pre, v_ln_ffn_pre, g_ffn_pre),
             ("ln_ffn_post", ln_ffn_post, m_ln_ffn_post, v_ln_ffn_post, g_ffn_post)]
    shapes_small = [s[1].shape for s in small]
    packs = [_pack_rows([s[q] for s in small], 128) for q in (1, 2, 3, 4)]
    outs = _adamw("adamw_small", packs[0], packs[1], packs[2], [packs[3]])
    unpacked = [_unpack_rows(o, shapes_small) for o in outs]
    for idx, s in enumerate(small):
        result[s[0]] = [u[idx] for u in unpacked]

    loss = lax.psum(loss_tile[0, 0], ("x", "y", "c"))
    order = ["ln_mix_pre", "w_in", "conv_w", "pool_w", "pool_scale", "w_out", "ln_mix_post", "ln_ffn_pre",
             "w_gate", "w_up", "w_down", "ln_ffn_post"]
    return (loss, grad_x[None], *[result[k][0] for k in order], *[result[k][1] for k in order],
            *[result[k][2] for k in order], *[result[k][3] for k in order])
```

```python
import functools

import jax
import jax.numpy as jnp
from jax import lax
from jax.experimental import pallas as pl
from jax.experimental.pallas import tpu as pltpu

EPS = 1e-6
CONV_HEAD_DIM = 128
CONV_K = 3
POOL_WINDOWS = (2, 4, 8, 16)
HALO = 16
EPILOGUE_ROWS = 64
N_CHIPS = 4
N_DEV = 8

ADAM_LR = 0.001
ADAM_B1 = 0.9
ADAM_B2 = 0.999
ADAM_EPS = 1e-08
ADAM_WD = 0.01
ADAM_STEP = 10

MM_DTYPE = jnp.bfloat16
COMM_DTYPE = jnp.bfloat16
VMEM_LIMIT = 62 * 1024 * 1024
MESH = pl.DeviceIdType.MESH
ANY = pl.BlockSpec(memory_space=pl.ANY)
STREAM = "stream"


def _tile(n, pref):
    t = min(pref, n)
    while n % t:
        t //= 2
    return t


def _params(sem):
    return pltpu.CompilerParams(dimension_semantics=sem, vmem_limit_bytes=VMEM_LIMIT)


def _rstd(x):
    return lax.rsqrt(jnp.mean(x * x, axis=-1, keepdims=True) + EPS)


def _norm_bwd(dn, n, rstd):
    return rstd * (dn - n * jnp.mean(dn * n, axis=-1, keepdims=True))


_DOT_DIMS = {
    "nn": (((1,), (0,)), ((), ())),
    "nt": (((1,), (1,)), ((), ())),
    "tn": (((0,), (0,)), ((), ())),
}


def _dot(a, b, mode):
    return lax.dot_general(a.astype(MM_DTYPE), b.astype(MM_DTYPE), _DOT_DIMS[mode],
                           preferred_element_type=jnp.float32)


def _matmul(name, *, grid, mode, pairs, pair_specs, acc_shapes, extras=(), extra_specs=(),
            out_shapes, out_specs, epilogue, deps=(), epi_rows=0, sum_outs=()):
    nk = grid[2]
    operands, operand_specs, where, counts = [], [], {}, []
    pair_index = []
    for ps, ss in zip(pairs, pair_specs):
        counts.append(len(ps))
        for arrays, specs in zip(ps, ss):
            for arr, spec in zip(arrays, specs):
                key = (id(arr), id(spec))
                if key not in where:
                    where[key] = len(operands)
                    operands.append(arr)
                    operand_specs.append(spec)
                pair_index.append(where[key])
    n_operands = len(operands)
    n_extra = len(extras)
    n_out = len(out_shapes)
    n_in = n_operands + n_extra + len(deps)
    in_streams = [q for q, s in enumerate(extra_specs) if s is STREAM]
    out_streams = [q for q, s in enumerate(out_specs) if s is STREAM]
    stream_bufs = ([pltpu.VMEM((2, epi_rows, extras[q].shape[1]), extras[q].dtype) for q in in_streams]
                   + [pltpu.VMEM((2, epi_rows, out_shapes[q].shape[1]), out_shapes[q].dtype) for q in out_streams])
    n_streams = len(stream_bufs)
    n_acc = 0 if nk == 1 else len(acc_shapes)

    def body(*refs):
        pair_refs = [refs[q] for q in pair_index]
        extra_refs = refs[n_operands:n_operands + n_extra]
        out_refs = refs[n_in:n_in + n_out]
        acc_refs = refs[n_in + n_out:n_in + n_out + n_acc]
        bufs = refs[n_in + n_out + n_acc:n_in + n_out + n_acc + n_streams]
        i, k = pl.program_id(0), pl.program_id(2)

        def streamed_finish(accs):
            sems = refs[-1]
            n_rows = accs[0].shape[0]
            n_chunks = n_rows // epi_rows

            def hbm_rows(ref, c):
                return ref.at[pl.ds(pl.multiple_of(i * n_rows + c * epi_rows, epi_rows), epi_rows)]

            def fetch(s, c):
                return pltpu.make_async_copy(hbm_rows(extra_refs[in_streams[s]], c), bufs[s].at[c % 2],
                                             sems.at[s, c % 2])

            def drain(s, c):
                return pltpu.make_async_copy(bufs[len(in_streams) + s].at[c % 2], hbm_rows(out_refs[out_streams[s]], c),
                                             sems.at[len(in_streams) + s, c % 2])

            for s in range(len(in_streams)):
                fetch(s, 0).start()
            for c in range(n_chunks):
                rows = slice(c * epi_rows, (c + 1) * epi_rows)
                for s in range(len(in_streams)):
                    if c + 1 < n_chunks:
                        fetch(s, c + 1).start()
                    fetch(s, c).wait()
                views = [bufs[in_streams.index(q)].at[c % 2] if q in in_streams else e
                         for q, e in enumerate(extra_refs)]
                outs = epilogue([a[rows, :] for a in accs], views)
                for q, (o_ref, o) in enumerate(zip(out_refs, outs)):
                    if q in out_streams:
                        s = out_streams.index(q)
                        if c >= 2:
                            drain(s, c - 2).wait()
                        bufs[len(in_streams) + s][c % 2] = o.astype(o_ref.dtype)
                        drain(s, c).start()
                    elif c == 0:
                        _accumulate(o_ref, o, i)
                    else:
                        o_ref[...] += o
            for s in range(len(out_streams)):
                for c in range(max(n_chunks - 2, 0), n_chunks):
                    drain(s, c).wait()

        def partial_sums():
            res, p = [], 0
            for cnt in counts:
                tot = None
                for _ in range(cnt):
                    d = _dot(pair_refs[p][...], pair_refs[p + 1][...], mode)
                    tot = d if tot is None else tot + d
                    p += 2
                res.append(tot)
            return res

        def finish(accs):
            n_rows = accs[0].shape[0]
            step = epi_rows or n_rows
            for r0 in range(0, n_rows, step):
                rows = slice(r0, r0 + step)
                outs = epilogue([a[rows, :] for a in accs], [e.at[rows] if e.shape[0] == n_rows else e
                                                             for e in extra_refs])
                for q, (o_ref, o) in enumerate(zip(out_refs, outs)):
                    if q not in sum_outs:
                        o_ref[rows, :] = o.astype(o_ref.dtype)
                    elif r0 == 0:
                        _accumulate(o_ref, o, i)
                    else:
                        o_ref[...] += o

        if nk == 1:
            finish(partial_sums())
        else:
            @pl.when(k == 0)
            def _():
                for acc_ref, s in zip(acc_refs, partial_sums()):
                    acc_ref[...] = s

            @pl.when(k > 0)
            def _():
                for acc_ref, s in zip(acc_refs, partial_sums()):
                    acc_ref[...] += s

            @pl.when(k == nk - 1)
            def _():
                (streamed_finish if n_streams else finish)(acc_refs)

    scratch = [] if nk == 1 else [pltpu.VMEM(s, jnp.float32) for s in acc_shapes]
    if n_streams:
        assert nk > 1 and epi_rows and not any(q in sum_outs for q in out_streams)
        scratch = scratch + stream_bufs + [pltpu.SemaphoreType.DMA((n_streams, 2))]
    return pl.pallas_call(
        body, name=name, grid=grid,
        in_specs=operand_specs + [ANY if s is STREAM else s for s in extra_specs] + [ANY] * len(deps),
        out_specs=[ANY if s is STREAM else s for s in out_specs],
        out_shape=list(out_shapes), scratch_shapes=scratch,
        compiler_params=_params(("arbitrary", "arbitrary", "arbitrary")),
    )(*operands, *extras, *deps)


def _identity_epilogue(accs, extra_refs):
    return tuple(accs)


def _row_spec(tr, n):
    return pl.BlockSpec((tr, n), lambda i: (i, 0))


def _const_spec(shape):
    return pl.BlockSpec(shape, lambda i: tuple(0 for _ in shape))


def _accumulate(ref, val, i):
    @pl.when(i == 0)
    def _():
        ref[...] = val

    @pl.when(i > 0)
    def _():
        ref[...] += val


def _pre_norm(x, gain, deps=()):
    t, d = x.shape
    tr = _tile(t, 512)

    def body(x_ref, g_ref, *rest):
        h_ref = rest[-1]
        xv = x_ref[...]
        h_ref[...] = (xv * _rstd(xv) * g_ref[...]).astype(h_ref.dtype)

    return pl.pallas_call(
        body, name="pre_norm", grid=(t // tr,),
        in_specs=[_row_spec(tr, d), _const_spec((1, d))] + [ANY] * len(deps), out_specs=_row_spec(tr, d),
        out_shape=jax.ShapeDtypeStruct((t, d), MM_DTYPE), compiler_params=_params(("arbitrary",)),
    )(x, gain, *deps)


def _pool_matrix(pool_ref, g):
    return jnp.concatenate([pool_ref[c, g] for c in range(N_CHIPS)], axis=0)


def _inv_count(row0, n, w):
    pos = (row0 + lax.broadcasted_iota(jnp.int32, (n, 1), 0) + 1).astype(jnp.float32)
    return 1.0 / jnp.minimum(pos, float(w))


def _conv_piece(cu_buf, start, n, b_piece, convw_ref):
    conv = None
    for k in range(CONV_K):
        term = convw_ref[k:k + 1, :] * cu_buf[pl.ds(HALO + start + k - (CONV_K - 1), n), :]
        conv = term if conv is None else conv + term
    return conv, b_piece * conv


def _head_stats(a, width):
    return [_rstd(a[:, h * width:(h + 1) * width]) for h in range(a.shape[1] // width)]


def _pooled_piece(v_buf, start, n, v_piece, row0, dg):
    outs = []
    for gi, w in enumerate(POOL_WINDOWS):
        cols = slice(gi * dg, (gi + 1) * dg)
        win = None
        for k in range(w):
            term = v_buf[pl.ds(HALO + start - k, n), cols]
            win = term if win is None else win + term
        outs.append(win * _inv_count(row0 + start, n, w) - v_piece[:, cols])
    return outs


def _halo_specs(t, tr, width, col):
    per = tr // HALO
    last = t // HALO - 1
    prev = pl.BlockSpec((HALO, width), lambda i: (jnp.maximum(i * per - 1, 0), col))
    nxt = pl.BlockSpec((HALO, width), lambda i: (jnp.minimum((i + 1) * per, last), col))
    return prev, nxt


def _mixers_fwd(proj, conv_w, pool_g, pool_scale):
    t, e = proj.shape
    cw = e // 4
    dg = pool_g.shape[-1]
    tr = _tile(t, 256)

    def main(col):
        return pl.BlockSpec((tr, cw), lambda i: (i, col))

    def body(b_ref, c_ref, u_ref, v_ref, cp_ref, up_ref, vp_ref, convw_ref, pool_ref, scale_ref,
             out_ref, cu_buf, v_buf):
        i = pl.program_id(0)
        keep = (i > 0).astype(jnp.float32)
        cu_buf[pl.ds(0, HALO), :] = cp_ref[...] * up_ref[...] * keep
        cu_buf[pl.ds(HALO, tr), :] = c_ref[...] * u_ref[...]
        v_buf[pl.ds(0, HALO), :] = vp_ref[...] * keep
        v_buf[pl.ds(HALO, tr), :] = v_ref[...]
        _, a = _conv_piece(cu_buf, 0, tr, b_ref[...], convw_ref)
        for h, rstd in enumerate(_head_stats(a, CONV_HEAD_DIM)):
            cols = slice(h * CONV_HEAD_DIM, (h + 1) * CONV_HEAD_DIM)
            out_ref[:, cols] = (a[:, cols] * rstd).astype(out_ref.dtype)
        pooled = _pooled_piece(v_buf, 0, tr, v_ref[...], i * tr, dg)
        for gi, p in enumerate(pooled):
            z = _dot(p, _pool_matrix(pool_ref, gi), "nn")
            cols = slice(gi * dg, (gi + 1) * dg)
            out_ref[:, cw + gi * dg:cw + (gi + 1) * dg] = (z * _rstd(z) * scale_ref[:, cols]).astype(out_ref.dtype)

    prev_c, _ = _halo_specs(t, tr, cw, 1)
    prev_u, _ = _halo_specs(t, tr, cw, 2)
    prev_v, _ = _halo_specs(t, tr, cw, 3)
    return pl.pallas_call(
        body, name="mixers_fwd", grid=(t // tr,),
        in_specs=[main(0), main(1), main(2), main(3), prev_c, prev_u, prev_v,
                  _const_spec(conv_w.shape), _const_spec(pool_g.shape), _const_spec(pool_scale.shape)],
        out_specs=_row_spec(tr, 2 * cw),
        out_shape=jax.ShapeDtypeStruct((t, 2 * cw), MM_DTYPE),
        scratch_shapes=[pltpu.VMEM((tr + HALO, cw), jnp.float32), pltpu.VMEM((tr + HALO, cw), jnp.float32)],
        compiler_params=_params(("arbitrary",)),
    )(proj, proj, proj, proj, proj, proj, proj, conv_w, pool_g, pool_scale)


def _mixers_bwd(proj, dmixed, conv_w, pool_g, pool_scale, deps=()):
    t, e = proj.shape
    cw = e // 4
    dg = pool_g.shape[-1]
    n_groups = len(POOL_WINDOWS)
    tr = _tile(t, 256)
    n_tiles = t // tr
    ext = tr + 2 * HALO

    def main(col):
        return pl.BlockSpec((tr, cw), lambda i: (i, col))

    def body(b_ref, c_ref, u_ref, v_ref, dyc_ref, dyp_ref,
             cp_ref, up_ref, vp_ref,
             bn_ref, cn_ref, un_ref, vn_ref, dycn_ref, dypn_ref,
             convw_ref, pool_ref, scale_ref, *rest):
        (dproj_ref, dconvw_ref, dpool_ref, dscale_ref,
         cu_buf, v_buf, dconv_buf, dpn_buf, dpooled_buf) = rest[len(deps):]
        i = pl.program_id(0)
        keep_prev = (i > 0).astype(jnp.float32)
        keep_next = (i < n_tiles - 1).astype(jnp.float32)
        cu_buf[pl.ds(0, HALO), :] = cp_ref[...] * up_ref[...] * keep_prev
        cu_buf[pl.ds(HALO, tr), :] = c_ref[...] * u_ref[...]
        cu_buf[pl.ds(HALO + tr, HALO), :] = cn_ref[...] * un_ref[...]
        v_buf[pl.ds(0, HALO), :] = vp_ref[...] * keep_prev
        v_buf[pl.ds(HALO, tr), :] = v_ref[...]
        v_buf[pl.ds(HALO + tr, HALO), :] = vn_ref[...]

        def conv_piece(start, n, b_piece, dyc_piece, keep, is_main):
            conv, a = _conv_piece(cu_buf, start, n, b_piece, convw_ref)
            for h, rstd in enumerate(_head_stats(a, CONV_HEAD_DIM)):
                cols = slice(h * CONV_HEAD_DIM, (h + 1) * CONV_HEAD_DIM)
                da = _norm_bwd(dyc_piece[:, cols], a[:, cols] * rstd, rstd)
                dconv_buf[pl.ds(start, n), cols] = da * b_piece[:, cols] * keep
                if is_main:
                    dproj_ref[:, cols] = (da * conv[:, cols]).astype(dproj_ref.dtype)

        conv_piece(0, tr, b_ref[...], dyc_ref[...], 1.0, True)
        conv_piece(tr, HALO, bn_ref[...], dycn_ref[...], keep_next, False)

        dconv_main = dconv_buf[pl.ds(0, tr), :]
        dcu = None
        dw_rows = []
        for k in range(CONV_K):
            shift = CONV_K - 1 - k
            term = convw_ref[k:k + 1, :] * dconv_buf[pl.ds(shift, tr), :]
            dcu = term if dcu is None else dcu + term
            dw_rows.append(jnp.sum(dconv_main * cu_buf[pl.ds(HALO - shift, tr), :], axis=0, keepdims=True))
        dproj_ref[:, cw:2 * cw] = (dcu * u_ref[...]).astype(dproj_ref.dtype)
        dproj_ref[:, 2 * cw:3 * cw] = (dcu * c_ref[...]).astype(dproj_ref.dtype)
        _accumulate(dconvw_ref, jnp.concatenate(dw_rows, axis=0), i)

        def pool_piece(start, n, v_piece, dyp_piece, keep, is_main):
            pooled = _pooled_piece(v_buf, start, n, v_piece, i * tr, dg)
            dscale, dmats = [], []
            for gi, w in enumerate(POOL_WINDOWS):
                cols = slice(gi * dg, (gi + 1) * dg)
                mat = _pool_matrix(pool_ref, gi)
                z = _dot(pooled[gi], mat, "nn")
                rstd = _rstd(z)
                nz = z * rstd
                dyp_g = dyp_piece[:, cols]
                dz = _norm_bwd(dyp_g * scale_ref[:, cols], nz, rstd)
                dpooled = _dot(dz, mat, "nt") * keep
                dpn_buf[pl.ds(start, n), cols] = dpooled * _inv_count(i * tr + start, n, w)
                if is_main:
                    dpooled_buf[:, cols] = dpooled
                    dscale.append(jnp.sum(dyp_g * nz, axis=0, keepdims=True))
                    dmats.append(_dot(pooled[gi], dz, "tn"))
            return dscale, dmats

        dscale, dmats = pool_piece(0, tr, v_ref[...], dyp_ref[...], 1.0, True)
        pool_piece(tr, HALO, vn_ref[...], dypn_ref[...], keep_next, False)
        for gi, w in enumerate(POOL_WINDOWS):
            cols = slice(gi * dg, (gi + 1) * dg)
            back = None
            for k in range(w):
                term = dpn_buf[pl.ds(k, tr), cols]
                back = term if back is None else back + term
            dproj_ref[:, 3 * cw + gi * dg:3 * cw + (gi + 1) * dg] = (back - dpooled_buf[:, cols]).astype(dproj_ref.dtype)
        _accumulate(dscale_ref, jnp.concatenate(dscale, axis=1), i)
        rows = dg // N_CHIPS
        for gi in range(n_groups):
            for c in range(N_CHIPS):
                _accumulate(dpool_ref.at[c, gi], dmats[gi][c * rows:(c + 1) * rows, :], i)

    prev_c, next_c = _halo_specs(t, tr, cw, 1)
    prev_u, next_u = _halo_specs(t, tr, cw, 2)
    prev_v, next_v = _halo_specs(t, tr, cw, 3)
    _, next_b = _halo_specs(t, tr, cw, 0)
    _, next_dyc = _halo_specs(t, tr, cw, 0)
    _, next_dyp = _halo_specs(t, tr, cw, 1)
    return pl.pallas_call(
        body, name="mixers_bwd", grid=(n_tiles,),
        in_specs=[main(0), main(1), main(2), main(3), main(0), main(1),
                  prev_c, prev_u, prev_v,
                  next_b, next_c, next_u, next_v, next_dyc, next_dyp,
                  _const_spec(conv_w.shape), _const_spec(pool_g.shape), _const_spec(pool_scale.shape)]
        + [ANY] * len(deps),
        out_specs=[_row_spec(tr, e), _const_spec(conv_w.shape), _const_spec(pool_g.shape),
                   _const_spec(pool_scale.shape)],
        out_shape=[jax.ShapeDtypeStruct((t, e), MM_DTYPE), jax.ShapeDtypeStruct(conv_w.shape, jnp.float32),
                   jax.ShapeDtypeStruct(pool_g.shape, jnp.float32),
                   jax.ShapeDtypeStruct(pool_scale.shape, jnp.float32)],
        scratch_shapes=[pltpu.VMEM((ext, cw), jnp.float32), pltpu.VMEM((ext, cw), jnp.float32),
                        pltpu.VMEM((tr + HALO, cw), jnp.float32), pltpu.VMEM((tr + HALO, cw), jnp.float32),
                        pltpu.VMEM((tr, cw), jnp.float32)],
        compiler_params=_params(("arbitrary",)),
    )(proj, proj, proj, proj, dmixed, dmixed,
      proj, proj, proj,
      proj, proj, proj, proj, dmixed, dmixed,
      conv_w, pool_g, pool_scale, *deps)


def _cast_rows(name, w, dtype):
    r, c = w.shape
    tr = _tile(r, 512)

    def body(w_ref, o_ref):
        o_ref[...] = w_ref[...].astype(o_ref.dtype)

    return pl.pallas_call(
        body, name=name, grid=(r // tr,), in_specs=[_row_spec(tr, c)], out_specs=_row_spec(tr, c),
        out_shape=jax.ShapeDtypeStruct((r, c), dtype), compiler_params=_params(("arbitrary",)),
    )(w)


def _cast_to_slot(name, w, chip, by_columns=False):
    r, c = w.shape
    tr = _tile(r, 512)
    if by_columns:
        out_spec = pl.BlockSpec((tr, c), lambda i, chip_ref: (i, chip_ref[0]))
        out_shape = jax.ShapeDtypeStruct((r, N_CHIPS * c), MM_DTYPE)
    else:
        out_spec = pl.BlockSpec((None, tr, c), lambda i, chip_ref: (chip_ref[0], i, 0))
        out_shape = jax.ShapeDtypeStruct((N_CHIPS, r, c), MM_DTYPE)

    def body(chip_ref, w_ref, o_ref):
        o_ref[...] = w_ref[...].astype(o_ref.dtype)

    return pl.pallas_call(
        body, name=name,
        grid_spec=pltpu.PrefetchScalarGridSpec(
            num_scalar_prefetch=1, grid=(r // tr,),
            in_specs=[pl.BlockSpec((tr, c), lambda i, chip_ref: (i, 0))], out_specs=out_spec),
        out_shape=out_shape, compiler_params=_params(("arbitrary",)),
    )(chip, w)


def _sum_own_and_received(name, own, land, chip):
    _, r, c = land.shape
    tr = _tile(r, 256)

    def body(chip_ref, own_ref, a_ref, b_ref, c_ref, o_ref):
        tot = own_ref[...].astype(jnp.float32) + a_ref[...].astype(jnp.float32)
        tot = tot + b_ref[...].astype(jnp.float32)
        o_ref[...] = tot + c_ref[...].astype(jnp.float32)

    def slot(k):
        return pl.BlockSpec((None, tr, c), lambda i, chip_ref: ((chip_ref[0] + k) % N_CHIPS, i, 0))

    own_spec = slot(0) if len(own.shape) == 3 else pl.BlockSpec((tr, c), lambda i, chip_ref: (i, chip_ref[0]))
    return pl.pallas_call(
        body, name=name,
        grid_spec=pltpu.PrefetchScalarGridSpec(
            num_scalar_prefetch=1, grid=(r // tr,), in_specs=[own_spec, slot(1), slot(2), slot(3)],
            out_specs=pl.BlockSpec((tr, c), lambda i, chip_ref: (i, 0))),
        out_shape=jax.ShapeDtypeStruct((r, c), jnp.float32), compiler_params=_params(("arbitrary",)),
    )(chip, own, land, land, land)


def _sum_slots(name, slots):
    n, r, c = slots.shape
    tr = _tile(r, 256)

    def body(s_ref, o_ref):
        tot = s_ref[0].astype(jnp.float32)
        for s in range(1, n):
            tot = tot + s_ref[s].astype(jnp.float32)
        o_ref[...] = tot

    return pl.pallas_call(
        body, name=name, grid=(r // tr,),
        in_specs=[pl.BlockSpec((n, tr, c), lambda i: (0, i, 0))], out_specs=_row_spec(tr, c),
        out_shape=jax.ShapeDtypeStruct((r, c), jnp.float32), compiler_params=_params(("arbitrary",)),
    )(slots)


def _adamw_math(w, g, m, v):
    m = ADAM_B1 * m + (1.0 - ADAM_B1) * g
    v = ADAM_B2 * v + (1.0 - ADAM_B2) * (g * g)
    m_hat = m / (1.0 - ADAM_B1 ** ADAM_STEP)
    v_hat = v / (1.0 - ADAM_B2 ** ADAM_STEP)
    delta = -ADAM_LR * (m_hat / (jnp.sqrt(v_hat) + ADAM_EPS) + ADAM_WD * w)
    return delta, m, v


def _adamw(name, w, m, v, grad_parts):
    r, c = w.shape
    tr = _tile(r, 256)
    n_parts = len(grad_parts)

    def body(*refs):
        w_ref, m_ref, v_ref = refs[:3]
        part_refs = refs[3:3 + n_parts]
        g_ref, d_ref, nm_ref, nv_ref = refs[3 + n_parts:]
        g = part_refs[0][...]
        for p in part_refs[1:]:
            g = g + p[...]
        delta, nm, nv = _adamw_math(w_ref[...], g, m_ref[...], v_ref[...])
        g_ref[...] = g
        d_ref[...] = delta
        nm_ref[...] = nm
        nv_ref[...] = nv

    spec = _row_spec(tr, c)
    out = jax.ShapeDtypeStruct((r, c), jnp.float32)
    return pl.pallas_call(
        body, name=name, grid=(r // tr,), in_specs=[spec] * (3 + n_parts), out_specs=[spec] * 4,
        out_shape=[out] * 4, compiler_params=_params(("arbitrary",)),
    )(w, m, v, *grad_parts)


def _chip_peers():
    x, y, c = lax.axis_index("x"), lax.axis_index("y"), lax.axis_index("c")
    return x, y, c, [(1 - x, y), (x, 1 - y), (1 - x, 1 - y)]


HBM = pl.BlockSpec(memory_space=pltpu.HBM)
SEM = pl.BlockSpec(memory_space=pltpu.SEMAPHORE)
TOKEN = jax.ShapeDtypeStruct((8, 128), jnp.float32)
N_PEER_CHIPS = N_CHIPS - 1


def _in_flight():
    return pltpu.CompilerParams(has_side_effects=pltpu.SideEffectType.DATAFLOW_SIDE_EFFECTING)


def _slot(ref, slot):
    if len(ref.shape) == 3:
        return ref.at[slot]
    width = ref.shape[1] // N_CHIPS
    return ref.at[:, pl.ds(pl.multiple_of(slot * width, 128), width)]


def _half_slot(ref, slot, half):
    rows = ref.shape[-2] // 2
    if len(ref.shape) == 3:
        return ref.at[slot, pl.ds(half * rows, rows)]
    width = ref.shape[1] // N_CHIPS
    return ref.at[pl.ds(half * rows, rows), pl.ds(pl.multiple_of(slot * width, 128), width)]


def _slot_shape(shape):
    return shape[1:] if len(shape) == 3 else (shape[0], shape[1] // N_CHIPS)


def _gather_start(name, full, deps=()):
    def body(full_ref, *rest):
        send_sems, recv_sems, _, token_ref = rest[len(deps):]
        x, y, c, peers = _chip_peers()
        mine = _half_slot(full_ref, 2 * x + y, c)
        for p, (px, py) in enumerate(peers):
            pltpu.make_async_remote_copy(
                src_ref=mine, dst_ref=mine, send_sem=send_sems.at[p], recv_sem=recv_sems.at[p],
                device_id=(px, py, c), device_id_type=MESH).start()
        token_ref[...] = jnp.zeros_like(token_ref)

    return pl.pallas_call(
        body, name=name,
        out_shape=(pltpu.SemaphoreType.DMA((N_PEER_CHIPS,)), pltpu.SemaphoreType.DMA((N_PEER_CHIPS,)),
                   pltpu.HBM(full.shape, full.dtype), TOKEN),
        in_specs=[HBM] + [ANY] * len(deps), out_specs=(SEM, SEM, HBM, pl.BlockSpec(memory_space=pltpu.VMEM)),
        input_output_aliases={0: 2}, compiler_params=_in_flight(),
    )(pltpu.with_memory_space_constraint(full, pltpu.HBM), *deps)


def _gather_wait(name, send_sems, recv_sems, full, after):
    def body(full_ref, send_sems, recv_sems, after_ref, out_ref):
        x, y, c, peers = _chip_peers()
        for p, (px, py) in enumerate(peers):
            cp = pltpu.make_async_remote_copy(
                src_ref=_half_slot(full_ref, 2 * x + y, c), dst_ref=_half_slot(full_ref, 2 * px + py, c),
                send_sem=send_sems.at[p], recv_sem=recv_sems.at[p], device_id=(px, py, c), device_id_type=MESH)
            cp.wait_send()
            cp.wait_recv()

    return pl.pallas_call(
        body, name=name, out_shape=pltpu.HBM(full.shape, full.dtype),
        in_specs=(HBM, SEM, SEM, ANY), out_specs=HBM, input_output_aliases={0: 0}, compiler_params=_in_flight(),
    )(full, send_sems, recv_sems, after)


def _forward_sibling(name, fulls):
    n = len(fulls)

    def body(*refs):
        in_refs, out_refs = refs[:n], refs[n:2 * n]
        send_sems, recv_sems = refs[2 * n:]
        x, y, c, peers = _chip_peers()
        sends, recvs = [], []
        for a in range(n):
            for p, (px, py) in enumerate(peers):
                k = N_PEER_CHIPS * a + p
                slot = 2 * px + py
                cp = pltpu.make_async_remote_copy(
                    src_ref=_half_slot(in_refs[a], slot, c), dst_ref=_half_slot(out_refs[a], slot, c),
                    send_sem=send_sems.at[k], recv_sem=recv_sems.at[k], device_id=(x, y, 1 - c), device_id_type=MESH)
                cp.start()
                sends.append(cp)
                recvs.append(pltpu.make_async_remote_copy(
                    src_ref=_half_slot(in_refs[a], slot, c), dst_ref=_half_slot(out_refs[a], slot, 1 - c),
                    send_sem=send_sems.at[k], recv_sem=recv_sems.at[k], device_id=(x, y, 1 - c), device_id_type=MESH))
        for cp in recvs:
            cp.wait_recv()
        for cp in sends:
            cp.wait_send()

    return pl.pallas_call(
        body, name=name, in_specs=[ANY] * n, out_specs=[ANY] * n,
        out_shape=[jax.ShapeDtypeStruct(f.shape, f.dtype) for f in fulls],
        input_output_aliases={a: a for a in range(n)},
        scratch_shapes=[pltpu.SemaphoreType.DMA((N_PEER_CHIPS * n,)), pltpu.SemaphoreType.DMA((N_PEER_CHIPS * n,))],
    )(*fulls)


def _scatter_start(name, grads, deps=()):
    def body(g_ref, land_ref, *rest):
        send_sems, recv_sems, _, _, token_ref = rest[len(deps):]
        x, y, c, peers = _chip_peers()
        me = 2 * x + y
        for p, (px, py) in enumerate(peers):
            pltpu.make_async_remote_copy(
                src_ref=_slot(g_ref, 2 * px + py), dst_ref=land_ref.at[me], send_sem=send_sems.at[p],
                recv_sem=recv_sems.at[p], device_id=(px, py, c), device_id_type=MESH).start()
        token_ref[...] = jnp.zeros_like(token_ref)

    land = lax.empty((N_CHIPS,) + _slot_shape(grads.shape), grads.dtype)
    return pl.pallas_call(
        body, name=name,
        out_shape=(pltpu.SemaphoreType.DMA((N_PEER_CHIPS,)), pltpu.SemaphoreType.DMA((N_PEER_CHIPS,)),
                   pltpu.HBM(grads.shape, grads.dtype), pltpu.HBM(land.shape, land.dtype), TOKEN),
        in_specs=[HBM, HBM] + [ANY] * len(deps),
        out_specs=(SEM, SEM, HBM, HBM, pl.BlockSpec(memory_space=pltpu.VMEM)),
        input_output_aliases={0: 2, 1: 3}, compiler_params=_in_flight(),
    )(pltpu.with_memory_space_constraint(grads, pltpu.HBM), pltpu.with_memory_space_constraint(land, pltpu.HBM), *deps)


def _scatter_wait(name, send_sems, recv_sems, grads, land, after):
    def body(g_ref, land_ref, send_sems, recv_sems, after_ref, g_out, land_out):
        x, y, c, peers = _chip_peers()
        for p, (px, py) in enumerate(peers):
            cp = pltpu.make_async_remote_copy(
                src_ref=_slot(g_ref, 2 * px + py), dst_ref=land_ref.at[2 * px + py], send_sem=send_sems.at[p],
                recv_sem=recv_sems.at[p], device_id=(px, py, c), device_id_type=MESH)
            cp.wait_send()
            cp.wait_recv()

    return pl.pallas_call(
        body, name=name,
        out_shape=(pltpu.HBM(grads.shape, grads.dtype), pltpu.HBM(land.shape, land.dtype)),
        in_specs=(HBM, HBM, SEM, SEM, ANY), out_specs=(HBM, HBM), input_output_aliases={0: 0, 1: 1},
        compiler_params=_in_flight(),
    )(grads, land, send_sems, recv_sems, after)


def _swap_sibling(name, parts):
    n = len(parts)

    def body(*refs):
        in_refs, out_refs = refs[:n], refs[n:2 * n]
        send_sems, recv_sems = refs[2 * n:]
        x, y, c = lax.axis_index("x"), lax.axis_index("y"), lax.axis_index("c")
        copies = []
        for a in range(n):
            cp = pltpu.make_async_remote_copy(
                src_ref=in_refs[a], dst_ref=out_refs[a], send_sem=send_sems.at[a], recv_sem=recv_sems.at[a],
                device_id=(x, y, 1 - c), device_id_type=MESH)
            cp.start()
            copies.append(cp)
        for cp in copies:
            cp.wait_recv()
        for cp in copies:
            cp.wait_send()

    return pl.pallas_call(
        body, name=name, in_specs=[ANY] * n, out_specs=[ANY] * n,
        out_shape=[jax.ShapeDtypeStruct(p.shape, p.dtype) for p in parts],
        scratch_shapes=[pltpu.SemaphoreType.DMA((n,)), pltpu.SemaphoreType.DMA((n,))],
    )(*parts)


def _gather_devices(name, block):
    def body(in_ref, out_ref, send_sems, recv_sems, local_sem):
        x, y, c = lax.axis_index("x"), lax.axis_index("y"), lax.axis_index("c")
        me = 4 * x + 2 * y + c
        local = pltpu.make_async_copy(in_ref, out_ref.at[me], local_sem)
        local.start()
        sends, recvs = [], []
        k = 0
        for fx in range(2):
            for fy in range(2):
                for fc in range(2):
                    if fx == fy == fc == 0:
                        continue
                    px = x if fx == 0 else 1 - x
                    py = y if fy == 0 else 1 - y
                    pc = c if fc == 0 else 1 - c
                    cp = pltpu.make_async_remote_copy(
                        src_ref=in_ref, dst_ref=out_ref.at[me], send_sem=send_sems.at[k], recv_sem=recv_sems.at[k],
                        device_id=(px, py, pc), device_id_type=MESH)
                    cp.start()
                    sends.append(cp)
                    recvs.append(pltpu.make_async_remote_copy(
                        src_ref=in_ref, dst_ref=out_ref.at[4 * px + 2 * py + pc], send_sem=send_sems.at[k],
                        recv_sem=recv_sems.at[k], device_id=(px, py, pc), device_id_type=MESH))
                    k += 1
        for cp in recvs:
            cp.wait_recv()
        for cp in sends:
            cp.wait_send()
        local.wait()

    return pl.pallas_call(
        body, name=name, in_specs=[ANY], out_specs=ANY,
        out_shape=jax.ShapeDtypeStruct((N_DEV,) + block.shape, block.dtype),
        scratch_shapes=[pltpu.SemaphoreType.DMA((N_DEV - 1,)), pltpu.SemaphoreType.DMA((N_DEV - 1,)),
                        pltpu.SemaphoreType.DMA],
    )(block)


def _pack_rows(pieces, width):
    flat = jnp.concatenate([p.reshape(-1) for p in pieces])
    rows = -(-flat.shape[0] // width)
    rows = -(-rows // 8) * 8
    flat = jnp.pad(flat, (0, rows * width - flat.shape[0]))
    return flat.reshape(rows, width)


def _unpack_rows(packed, shapes):
    flat = packed.reshape(-1)
    out, off = [], 0
    for s in shapes:
        size = 1
        for d in s:
            size *= d
        out.append(flat[off:off + size].reshape(s))
        off += size
    return out


def kernel(x, ln_mix_pre, w_in, conv_w, pool_w, pool_scale, w_out, ln_mix_post, ln_ffn_pre, w_gate, w_up, w_down, ln_ffn_post, loss_target, m_ln_mix_pre, m_w_in, m_conv_w, m_pool_w, m_pool_scale, m_w_out, m_ln_mix_post, m_ln_ffn_pre, m_w_gate, m_w_up, m_w_down, m_ln_ffn_post, v_ln_mix_pre, v_w_in, v_conv_w, v_pool_w, v_pool_scale, v_w_out, v_ln_mix_post, v_ln_ffn_pre, v_w_gate, v_w_up, v_w_down, v_ln_ffn_post):
    t, d = x.shape[1], x.shape[2]
    e4 = w_in.shape[2]
    e = N_CHIPS * e4
    f4 = w_gate.shape[2]
    f = N_CHIPS * f4
    n_groups, dg4, dg = pool_w.shape[1], pool_w.shape[2], pool_w.shape[3]
    cw4 = conv_w.shape[2]
    chip = 2 * lax.axis_index("x") + lax.axis_index("y")
    xs, tgt = x[0], loss_target[0]

    big = {"w_in": w_in[0], "w_out": w_out[0], "w_gate": w_gate[0], "w_up": w_up[0], "w_down": w_down[0],
           "pool_w": pool_w[0].reshape(n_groups * dg4, dg)}
    names = ["w_in", "pool_w", "w_out", "w_gate", "w_up", "w_down"]
    chip_arr = chip.astype(jnp.int32).reshape(1)
    conv_all = _gather_devices("gather_conv_w", _pack_rows([conv_w[0]], 128))
    conv_full = jnp.concatenate(
        [conv_all[2 * j].reshape(-1)[:CONV_K * cw4].reshape(CONV_K, cw4) for j in range(N_CHIPS)], axis=1)
    in_flight, deps = {}, (conv_all,)
    for k in names:
        by_columns = k in ("w_gate", "w_up")
        in_flight[k] = _gather_start(
            "gather_start_" + k, _cast_to_slot("cast_" + k, big[k], chip_arr, by_columns), deps)
        deps = (in_flight[k][3],)

    def landed(ks, after):
        fulls = []
        for k in ks:
            send_sems, recv_sems, full, _ = in_flight[k]
            fulls.append(_gather_wait("gather_wait_" + k, send_sems, recv_sems, full, after))
            after = fulls[-1]
        return _forward_sibling("forward_" + ks[0], fulls)

    def rows3(tile, width):
        return pl.BlockSpec((tile, width), lambda i, j, k: (i, 0))

    gain3 = pl.BlockSpec((1, d), lambda i, j, k: (0, 0))
    f32_td = jax.ShapeDtypeStruct((t, d), jnp.float32)
    mm_td = jax.ShapeDtypeStruct((t, d), MM_DTYPE)
    f32_gain = jax.ShapeDtypeStruct((1, d), jnp.float32)

    h = _pre_norm(xs, ln_mix_pre, deps)
    (win_g,) = landed(["w_in"], h)
    tm = _tile(t, 1024)
    proj = _matmul(
        "in_proj", grid=(t // tm, N_CHIPS, 1), mode="nn",
        pairs=[[(h, win_g)]],
        pair_specs=[[(rows3(tm, d), pl.BlockSpec((None, d, e4), lambda i, j, k: (j, 0, 0)))]],
        acc_shapes=[(tm, e4)], out_shapes=[jax.ShapeDtypeStruct((t, e), jnp.float32)],
        out_specs=[pl.BlockSpec((tm, e4), lambda i, j, k: (i, j))], epilogue=_identity_epilogue)[0]
    pool_g, wout_full = landed(["pool_w", "w_out"], proj)
    pool_g, wout_full = pool_g.reshape(N_CHIPS, n_groups, dg4, dg), wout_full.reshape(d, d)
    mixed = _mixers_fwd(proj, conv_full, pool_g, pool_scale)

    def post_mix_epilogue(accs, extra_refs):
        x_ref, g2_ref, g3_ref = extra_refs
        mo = accs[0]
        x1 = x_ref[...] + mo * _rstd(mo) * g2_ref[...]
        return mo, x1, x1 * _rstd(x1) * g3_ref[...]

    tm_mix = _tile(t, 256)
    mix_out, x1, hf = _matmul(
        "out_proj", grid=(t // tm_mix, 1, 1), mode="nn",
        pairs=[[(mixed, wout_full)]],
        pair_specs=[[(rows3(tm_mix, d), pl.BlockSpec((d, d), lambda i, j, k: (0, 0)))]],
        acc_shapes=[(tm_mix, d)], extras=[xs, ln_mix_post, ln_ffn_pre],
        extra_specs=[rows3(tm_mix, d), gain3, gain3], out_shapes=[f32_td, f32_td, mm_td],
        out_specs=[rows3(tm_mix, d)] * 3, epilogue=post_mix_epilogue, epi_rows=EPILOGUE_ROWS)
    wg_full, wu_full = landed(["w_gate", "w_up"], hf)

    def gate_up_epilogue(accs, extra_refs):
        g, up = accs
        return g, up, g * jax.nn.sigmoid(g) * up

    tf = _tile(f, 512)
    ff_tile = jax.ShapeDtypeStruct((t, f), MM_DTYPE)
    ff_spec = pl.BlockSpec((tm, tf), lambda j, i, k: (i, j))
    hf_spec = pl.BlockSpec((tm, d), lambda j, i, k: (i, 0))
    g_act, up_act, act = _matmul(
        "gate_up", grid=(f // tf, t // tm, 1), mode="nn",
        pairs=[[(hf, wg_full)], [(hf, wu_full)]],
        pair_specs=[[(hf_spec, pl.BlockSpec((d, tf), lambda j, i, k: (0, j)))]] * 2,
        acc_shapes=[(tm, tf)] * 2, out_shapes=[ff_tile] * 3, out_specs=[ff_spec] * 3, epilogue=gate_up_epilogue)
    wdown_full = landed(["w_down"], act)[0].reshape(f, d)

    def loss_epilogue(accs, extra_refs):
        x1_ref, tg_ref, g_ref = extra_refs
        ff_v = accs[0]
        rstd = _rstd(ff_v)
        n = ff_v * rstd
        g = g_ref[...]
        err = x1_ref[...] + n * g - tg_ref[...]
        rows_loss = 0.5 * jnp.sum(jnp.mean(err * err, axis=-1, keepdims=True), axis=0, keepdims=True)
        dout = err / d
        return (dout, _norm_bwd(dout * g, n, rstd), jnp.broadcast_to(rows_loss, (8, 128)),
                jnp.sum(dout * n, axis=0, keepdims=True))

    tm_row = _tile(t, 1024)
    dout, dff, loss_tile, dg_ffn_post = _matmul(
        "down_proj", grid=(t // tm_row, 1, f // tf), mode="nn",
        pairs=[[(act, wdown_full)]],
        pair_specs=[[(pl.BlockSpec((tm_row, tf), lambda i, j, k: (i, k)),
                      pl.BlockSpec((tf, d), lambda i, j, k: (k, 0)))]],
        acc_shapes=[(tm_row, d)], extras=[x1, tgt, ln_ffn_post], extra_specs=[STREAM] * 2 + [gain3],
        out_shapes=[f32_td, mm_td, jax.ShapeDtypeStruct((8, 128), jnp.float32), f32_gain],
        out_specs=[STREAM] * 2 + [pl.BlockSpec((8, 128), lambda i, j, k: (0, 0)), gain3],
        epilogue=loss_epilogue, epi_rows=EPILOGUE_ROWS, sum_outs=(2, 3))

    def dact_epilogue(accs, extra_refs):
        dact = accs[0]
        g = extra_refs[0][...].astype(jnp.float32)
        up = extra_refs[1][...].astype(jnp.float32)
        sig = jax.nn.sigmoid(g)
        silu = g * sig
        return dact * up * (sig + silu * (1.0 - sig)), dact * silu

    ff_spec_ij = pl.BlockSpec((tm, tf), lambda i, j, k: (i, j))
    dg_act, dup_act = _matmul(
        "dact", grid=(t // tm, f // tf, 1), mode="nt",
        pairs=[[(dff, wdown_full)]],
        pair_specs=[[(rows3(tm, d), pl.BlockSpec((tf, d), lambda i, j, k: (j, 0)))]],
        acc_shapes=[(tm, tf)], extras=[g_act, up_act], extra_specs=[ff_spec_ij, ff_spec_ij],
        out_shapes=[ff_tile] * 2, out_specs=[ff_spec_ij] * 2, epilogue=dact_epilogue)
    tk = _tile(t, 1024)
    tn = _tile(d, 1024)
    dw_down = _matmul(
        "dw_down", grid=(N_CHIPS, d // tn, t // tk), mode="tn",
        pairs=[[(act, dff)]],
        pair_specs=[[(pl.BlockSpec((tk, f4), lambda i, j, k: (k, i)),
                      pl.BlockSpec((tk, tn), lambda i, j, k: (k, j)))]],
        acc_shapes=[(f4, tn)], out_shapes=[jax.ShapeDtypeStruct((N_CHIPS, f4, d), COMM_DTYPE)],
        out_specs=[pl.BlockSpec((None, f4, tn), lambda i, j, k: (i, 0, j))], epilogue=_identity_epilogue)[0]
    leaving = {"w_down": _scatter_start("scatter_start_w_down", dw_down)}

    def ffn_pre_epilogue(accs, extra_refs):
        x1_ref, dout_ref, mo_ref, g3_ref, g2_ref = extra_refs
        dhf_v = accs[0]
        x1_v = x1_ref[...]
        rstd3 = _rstd(x1_v)
        n3 = x1_v * rstd3
        dx1_v = dout_ref[...] + _norm_bwd(dhf_v * g3_ref[...], n3, rstd3)
        mo = mo_ref[...]
        rstd2 = _rstd(mo)
        n2 = mo * rstd2
        return (dx1_v, _norm_bwd(dx1_v * g2_ref[...], n2, rstd2), jnp.sum(dhf_v * n3, axis=0, keepdims=True),
                jnp.sum(dx1_v * n2, axis=0, keepdims=True))

    dx1, dmo, dg_ffn_pre, dg_mix_post = _matmul(
        "dhf", grid=(t // tm_row, 1, f // tf), mode="nt",
        pairs=[[(dg_act, wg_full), (dup_act, wu_full)]],
        pair_specs=[[(pl.BlockSpec((tm_row, tf), lambda i, j, k: (i, k)),
                      pl.BlockSpec((d, tf), lambda i, j, k: (0, k)))] * 2],
        acc_shapes=[(tm_row, d)], extras=[x1, dout, mix_out, ln_ffn_pre, ln_mix_post],
        extra_specs=[STREAM] * 3 + [gain3] * 2, out_shapes=[f32_td, mm_td, f32_gain, f32_gain],
        out_specs=[STREAM] * 2 + [gain3] * 2, epilogue=ffn_pre_epilogue, epi_rows=EPILOGUE_ROWS,
        sum_outs=(2, 3), deps=leaving["w_down"][4:])
    tmo = _tile(d, 2048)
    grad_ff = jax.ShapeDtypeStruct((d, f), COMM_DTYPE)
    dw_gate, dw_up = _matmul(
        "dw_gate_up", grid=(d // tmo, f // tf, t // tk), mode="tn",
        pairs=[[(hf, dg_act)], [(hf, dup_act)]],
        pair_specs=[[(pl.BlockSpec((tk, tmo), lambda i, j, k: (k, i)),
                      pl.BlockSpec((tk, tf), lambda i, j, k: (k, j)))]] * 2,
        acc_shapes=[(tmo, tf)] * 2, out_shapes=[grad_ff] * 2,
        out_specs=[pl.BlockSpec((tmo, tf), lambda i, j, k: (i, j))] * 2, epilogue=_identity_epilogue)
    leaving["w_gate"] = _scatter_start("scatter_start_w_gate", dw_gate)
    leaving["w_up"] = _scatter_start("scatter_start_w_up", dw_up, leaving["w_gate"][4:])

    dmixed = _matmul(
        "dmixed", grid=(t // tm, d // tn, 1), mode="nt",
        pairs=[[(dmo, wout_full)]],
        pair_specs=[[(rows3(tm, d), pl.BlockSpec((tn, d), lambda i, j, k: (j, 0)))]],
        acc_shapes=[(tm, tn)], out_shapes=[f32_td],
        out_specs=[pl.BlockSpec((tm, tn), lambda i, j, k: (i, j))], epilogue=_identity_epilogue,
        deps=leaving["w_up"][4:])[0]
    tmo = _tile(d, 1024)
    dw_out = _matmul(
        "dw_out", grid=(d // tmo, d // tn, t // tk), mode="tn",
        pairs=[[(mixed, dmo)]],
        pair_specs=[[(pl.BlockSpec((tk, tmo), lambda i, j, k: (k, i)),
                      pl.BlockSpec((tk, tn), lambda i, j, k: (k, j)))]],
        acc_shapes=[(tmo, tn)], out_shapes=[jax.ShapeDtypeStruct((d, d), COMM_DTYPE)],
        out_specs=[pl.BlockSpec((tmo, tn), lambda i, j, k: (i, j))], epilogue=_identity_epilogue)[0]
    leaving["w_out"] = _scatter_start("scatter_start_w_out", dw_out.reshape(N_CHIPS, d // N_CHIPS, d))
    dproj, dconv_full, dpool_g, dpool_scale = _mixers_bwd(proj, dmixed, conv_full, pool_g, pool_scale,
                                                          leaving["w_out"][4:])
    dpool_slots = _cast_rows("cast_dpool", dpool_g.reshape(N_CHIPS * n_groups * dg4, dg), COMM_DTYPE)
    leaving["pool_w"] = _scatter_start("scatter_start_pool_w", dpool_slots.reshape(N_CHIPS, n_groups * dg4, dg))
    dw_in = _matmul(
        "dw_in", grid=(d // tmo, N_CHIPS, t // tk), mode="tn",
        pairs=[[(h, dproj)]],
        pair_specs=[[(pl.BlockSpec((tk, tmo), lambda i, j, k: (k, i)),
                      pl.BlockSpec((tk, e4), lambda i, j, k: (k, j)))]],
        acc_shapes=[(tmo, e4)], out_shapes=[jax.ShapeDtypeStruct((N_CHIPS, d, e4), COMM_DTYPE)],
        out_specs=[pl.BlockSpec((None, tmo, e4), lambda i, j, k: (j, i, 0))], epilogue=_identity_epilogue,
        deps=leaving["pool_w"][4:])[0]
    leaving["w_in"] = _scatter_start("scatter_start_w_in", dw_in)

    def mix_pre_epilogue(accs, extra_refs):
        x_ref, dx1_ref, g_ref = extra_refs
        dh_v = accs[0]
        xv = x_ref[...]
        rstd = _rstd(xv)
        n = xv * rstd
        return dx1_ref[...] + _norm_bwd(dh_v * g_ref[...], n, rstd), jnp.sum(dh_v * n, axis=0, keepdims=True)

    grad_x, dg_mix_pre = _matmul(
        "dh", grid=(t // tm_row, 1, N_CHIPS), mode="nt",
        pairs=[[(dproj, win_g)]],
        pair_specs=[[(pl.BlockSpec((tm_row, e4), lambda i, j, k: (i, k)),
                      pl.BlockSpec((None, d, e4), lambda i, j, k: (k, 0, 0)))]],
        acc_shapes=[(tm_row, d)], extras=[xs, dx1, ln_mix_pre], extra_specs=[STREAM] * 2 + [gain3],
        out_shapes=[f32_td, f32_gain], out_specs=[STREAM, gain3], epilogue=mix_pre_epilogue,
        epi_rows=EPILOGUE_ROWS, sum_outs=(1,), deps=leaving["w_in"][4:])

    names = ["w_down", "w_gate", "w_up", "w_out", "pool_w", "w_in"]
    partial, after = [], grad_x
    for k in names:
        send_sems, recv_sems, own, land, _ = leaving[k]
        own, land = _scatter_wait("scatter_wait_" + k, send_sems, recv_sems, own, land, after)
        partial.append(_sum_own_and_received("sum_" + k, own, land, chip_arr))
        after = land
    other = _swap_sibling("swap_grads", partial)
    moments = {"w_in": (m_w_in, v_w_in), "w_out": (m_w_out, v_w_out), "w_gate": (m_w_gate, v_w_gate),
               "w_up": (m_w_up, v_w_up), "w_down": (m_w_down, v_w_down), "pool_w": (m_pool_w, v_pool_w)}
    result = {}
    for k, mine, theirs in zip(names, partial, other):
        shape = moments[k][0].shape
        two_d = big[k].shape
        outs = _adamw("adamw_" + k, big[k], moments[k][0].reshape(two_d), moments[k][1].reshape(two_d),
                      [mine, theirs])
        result[k] = [o.reshape(shape) for o in outs]

    small_shapes = [(1, d)] * 4 + [pool_scale.shape, (CONV_K, N_CHIPS * cw4)]
    packed = _pack_rows([dg_mix_pre, dg_mix_post, dg_ffn_pre, dg_ffn_post, dpool_scale, dconv_full], 1024)
    summed = _sum_slots("sum_small", _gather_devices("gather_small", packed))
    g_mix_pre, g_mix_post, g_ffn_pre, g_ffn_post, g_pool_scale, g_conv_full = _unpack_rows(summed, small_shapes)
    g_conv = lax.dynamic_slice(g_conv_full, (0, chip * cw4), (CONV_K, cw4))[None]
    small = [("ln_mix_pre", ln_mix_pre, m_ln_mix_pre, v_ln_mix_pre, g_mix_pre),
             ("conv_w", conv_w, m_conv_w, v_conv_w, g_conv),
             ("pool_scale", pool_scale, m_pool_scale, v_pool_scale, g_pool_scale),
             ("ln_mix_post", ln_mix_post, m_ln_mix_post, v_ln_mix_post, g_mix_post),
             ("ln_ffn_pre", ln_ffn_pre, m_ln_ffn_pre, v_ln_ffn_pre, g_ffn_pre),
             ("ln_ffn_post", ln_ffn_post, m_ln_ffn_post, v_ln_ffn_post, g_ffn_post)]
    shapes_small = [s[1].shape for s in small]
    packs = [_pack_rows([s[q] for s in small], 128) for q in (1, 2, 3, 4)]
    outs = _adamw("adamw_small", packs[0], packs[1], packs[2], [packs[3]])
    unpacked = [_unpack_rows(o, shapes_small) for o in outs]
    for idx, s in enumerate(small):
        result[s[0]] = [u[idx] for u in unpacked]

    loss = lax.psum(loss_tile[0, 0], ("x", "y", "c"))
    order = ["ln_mix_pre", "w_in", "conv_w", "pool_w", "pool_scale", "w_out", "ln_mix_post", "ln_ffn_pre",
             "w_gate", "w_up", "w_down", "ln_ffn_post"]
    return (loss, grad_x[None], *[result[k][0] for k in order], *[result[k][1] for k in order],
            *[result[k][2] for k in order], *[result[k][3] for k in order])
```

```python
import functools

import jax
import jax.numpy as jnp
from jax import lax
from jax.experimental import pallas as pl
from jax.experimental.pallas import tpu as pltpu

EPS = 1e-6
CONV_HEAD_DIM = 128
CONV_K = 3
POOL_WINDOWS = (2, 4, 8, 16)
HALO = 16
EPILOGUE_ROWS = 64
STREAM_ROWS = 256
N_CHIPS = 4
N_DEV = 8

ADAM_LR = 0.001
ADAM_B1 = 0.9
ADAM_B2 = 0.999
ADAM_EPS = 1e-08
ADAM_WD = 0.01
ADAM_STEP = 10

MM_DTYPE = jnp.bfloat16
COMM_DTYPE = jnp.bfloat16
VMEM_LIMIT = 62 * 1024 * 1024
MESH = pl.DeviceIdType.MESH
ANY = pl.BlockSpec(memory_space=pl.ANY)
STREAM = "stream"


def _tile(n, pref):
    t = min(pref, n)
    while n % t:
        t //= 2
    return t


def _params(sem):
    return pltpu.CompilerParams(dimension_semantics=sem, vmem_limit_bytes=VMEM_LIMIT)


def _rstd(x):
    return lax.rsqrt(jnp.mean(x * x, axis=-1, keepdims=True) + EPS)


def _norm_bwd(dn, n, rstd):
    return rstd * (dn - n * jnp.mean(dn * n, axis=-1, keepdims=True))


_DOT_DIMS = {
    "nn": (((1,), (0,)), ((), ())),
    "nt": (((1,), (1,)), ((), ())),
    "tn": (((0,), (0,)), ((), ())),
}


def _dot(a, b, mode):
    return lax.dot_general(a.astype(MM_DTYPE), b.astype(MM_DTYPE), _DOT_DIMS[mode],
                           preferred_element_type=jnp.float32)


def _matmul(name, *, grid, mode, pairs, pair_specs, acc_shapes, extras=(), extra_specs=(),
            out_shapes, out_specs, epilogue, deps=(), epi_rows=0, sum_outs=()):
    nk = grid[2]
    operands, operand_specs, where, counts = [], [], {}, []
    pair_index = []
    for ps, ss in zip(pairs, pair_specs):
        counts.append(len(ps))
        for arrays, specs in zip(ps, ss):
            for arr, spec in zip(arrays, specs):
                key = (id(arr), id(spec))
                if key not in where:
                    where[key] = len(operands)
                    operands.append(arr)
                    operand_specs.append(spec)
                pair_index.append(where[key])
    n_operands = len(operands)
    n_extra = len(extras)
    n_out = len(out_shapes)
    n_in = n_operands + n_extra + len(deps)
    in_streams = [q for q, s in enumerate(extra_specs) if s is STREAM]
    out_streams = [q for q, s in enumerate(out_specs) if s is STREAM]
    stream_bufs = ([pltpu.VMEM((2, STREAM_ROWS, extras[q].shape[1]), extras[q].dtype) for q in in_streams]
                   + [pltpu.VMEM((2, STREAM_ROWS, out_shapes[q].shape[1]), out_shapes[q].dtype) for q in out_streams])
    n_streams = len(stream_bufs)
    n_acc = 0 if nk == 1 else len(acc_shapes)

    def body(*refs):
        pair_refs = [refs[q] for q in pair_index]
        extra_refs = refs[n_operands:n_operands + n_extra]
        out_refs = refs[n_in:n_in + n_out]
        acc_refs = refs[n_in + n_out:n_in + n_out + n_acc]
        bufs = refs[n_in + n_out + n_acc:n_in + n_out + n_acc + n_streams]
        i, k = pl.program_id(0), pl.program_id(2)

        def stream_copies():
            sems = refs[-1]
            n_rows = acc_refs[0].shape[0]

            def hbm_rows(ref, c):
                return ref.at[pl.ds(pl.multiple_of(i * n_rows + c * STREAM_ROWS, STREAM_ROWS), STREAM_ROWS)]

            def fetch(s, c):
                return pltpu.make_async_copy(hbm_rows(extra_refs[in_streams[s]], c), bufs[s].at[c % 2],
                                             sems.at[s, c % 2])

            def drain(s, c):
                return pltpu.make_async_copy(bufs[len(in_streams) + s].at[c % 2], hbm_rows(out_refs[out_streams[s]], c),
                                             sems.at[len(in_streams) + s, c % 2])

            return fetch, drain

        def streamed_finish(accs):
            fetch, drain = stream_copies()
            n_chunks = accs[0].shape[0] // STREAM_ROWS
            for c in range(n_chunks):
                for s in range(len(in_streams)):
                    if c + 1 < n_chunks:
                        fetch(s, c + 1).start()
                    fetch(s, c).wait()
                for s in range(len(out_streams)):
                    if c >= 2:
                        drain(s, c - 2).wait()
                for r0 in range(0, STREAM_ROWS, epi_rows):
                    sub = slice(r0, r0 + epi_rows)
                    rows = slice(c * STREAM_ROWS + r0, c * STREAM_ROWS + r0 + epi_rows)
                    views = [bufs[in_streams.index(q)].at[c % 2, sub] if q in in_streams else e
                             for q, e in enumerate(extra_refs)]
                    outs = epilogue([a[rows, :] for a in accs], views)
                    for q, (o_ref, o) in enumerate(zip(out_refs, outs)):
                        if q in out_streams:
                            bufs[len(in_streams) + out_streams.index(q)][c % 2, sub, :] = o.astype(o_ref.dtype)
                        elif c == 0 and r0 == 0:
                            _accumulate(o_ref, o, i)
                        else:
                            o_ref[...] += o
                for s in range(len(out_streams)):
                    drain(s, c).start()
            for s in range(len(out_streams)):
                for c in range(max(n_chunks - 2, 0), n_chunks):
                    drain(s, c).wait()

        def partial_sums():
            res, p = [], 0
            for cnt in counts:
                tot = None
                for _ in range(cnt):
                    d = _dot(pair_refs[p][...], pair_refs[p + 1][...], mode)
                    tot = d if tot is None else tot + d
                    p += 2
                res.append(tot)
            return res

        def finish(accs):
            n_rows = accs[0].shape[0]
            step = epi_rows or n_rows
            for r0 in range(0, n_rows, step):
                rows = slice(r0, r0 + step)
                outs = epilogue([a[rows, :] for a in accs], [e.at[rows] if e.shape[0] == n_rows else e
                                                             for e in extra_refs])
                for q, (o_ref, o) in enumerate(zip(out_refs, outs)):
                    if q not in sum_outs:
                        o_ref[rows, :] = o.astype(o_ref.dtype)
                    elif r0 == 0:
                        _accumulate(o_ref, o, i)
                    else:
                        o_ref[...] += o

        if nk == 1:
            finish(partial_sums())
        else:
            if n_streams:
                @pl.when(k == nk - 1)
                def _():
                    for s in range(len(in_streams)):
                        stream_copies()[0](s, 0).start()

            @pl.when(k == 0)
            def _():
                for acc_ref, s in zip(acc_refs, partial_sums()):
                    acc_ref[...] = s

            @pl.when(k > 0)
            def _():
                for acc_ref, s in zip(acc_refs, partial_sums()):
                    acc_ref[...] += s

            @pl.when(k == nk - 1)
            def _():
                (streamed_finish if n_streams else finish)(acc_refs)

    scratch = [] if nk == 1 else [pltpu.VMEM(s, jnp.float32) for s in acc_shapes]
    if n_streams:
        assert nk > 1 and epi_rows and not any(q in sum_outs for q in out_streams)
        scratch = scratch + stream_bufs + [pltpu.SemaphoreType.DMA((n_streams, 2))]
    return pl.pallas_call(
        body, name=name, grid=grid,
        in_specs=operand_specs + [ANY if s is STREAM else s for s in extra_specs] + [ANY] * len(deps),
        out_specs=[ANY if s is STREAM else s for s in out_specs],
        out_shape=list(out_shapes), scratch_shapes=scratch,
        compiler_params=_params(("arbitrary", "arbitrary", "arbitrary")),
    )(*operands, *extras, *deps)


def _identity_epilogue(accs, extra_refs):
    return tuple(accs)


def _row_spec(tr, n):
    return pl.BlockSpec((tr, n), lambda i: (i, 0))


def _const_spec(shape):
    return pl.BlockSpec(shape, lambda i: tuple(0 for _ in shape))


def _accumulate(ref, val, i):
    @pl.when(i == 0)
    def _():
        ref[...] = val

    @pl.when(i > 0)
    def _():
        ref[...] += val


def _pre_norm(x, gain, deps=()):
    t, d = x.shape
    tr = _tile(t, 512)

    def body(x_ref, g_ref, *rest):
        h_ref = rest[-1]
        xv = x_ref[...]
        h_ref[...] = (xv * _rstd(xv) * g_ref[...]).astype(h_ref.dtype)

    return pl.pallas_call(
        body, name="pre_norm", grid=(t // tr,),
        in_specs=[_row_spec(tr, d), _const_spec((1, d))] + [ANY] * len(deps), out_specs=_row_spec(tr, d),
        out_shape=jax.ShapeDtypeStruct((t, d), MM_DTYPE), compiler_params=_params(("arbitrary",)),
    )(x, gain, *deps)


def _pool_matrix(pool_ref, g):
    return jnp.concatenate([pool_ref[c, g] for c in range(N_CHIPS)], axis=0)


def _inv_count(row0, n, w):
    pos = (row0 + lax.broadcasted_iota(jnp.int32, (n, 1), 0) + 1).astype(jnp.float32)
    return 1.0 / jnp.minimum(pos, float(w))


def _conv_piece(cu_buf, start, n, b_piece, convw_ref):
    conv = None
    for k in range(CONV_K):
        term = convw_ref[k:k + 1, :] * cu_buf[pl.ds(HALO + start + k - (CONV_K - 1), n), :]
        conv = term if conv is None else conv + term
    return conv, b_piece * conv


def _head_stats(a, width):
    return [_rstd(a[:, h * width:(h + 1) * width]) for h in range(a.shape[1] // width)]


def _pooled_piece(v_buf, start, n, v_piece, row0, dg):
    outs = []
    for gi, w in enumerate(POOL_WINDOWS):
        cols = slice(gi * dg, (gi + 1) * dg)
        win = None
        for k in range(w):
            term = v_buf[pl.ds(HALO + start - k, n), cols]
            win = term if win is None else win + term
        outs.append(win * _inv_count(row0 + start, n, w) - v_piece[:, cols])
    return outs


def _halo_specs(t, tr, width, col):
    per = tr // HALO
    last = t // HALO - 1
    prev = pl.BlockSpec((HALO, width), lambda i: (jnp.maximum(i * per - 1, 0), col))
    nxt = pl.BlockSpec((HALO, width), lambda i: (jnp.minimum((i + 1) * per, last), col))
    return prev, nxt


def _mixers_fwd(proj, conv_w, pool_g, pool_scale):
    t, e = proj.shape
    cw = e // 4
    dg = pool_g.shape[-1]
    tr = _tile(t, 256)

    def main(col):
        return pl.BlockSpec((tr, cw), lambda i: (i, col))

    def body(b_ref, c_ref, u_ref, v_ref, cp_ref, up_ref, vp_ref, convw_ref, pool_ref, scale_ref,
             out_ref, cu_buf, v_buf):
        i = pl.program_id(0)
        keep = (i > 0).astype(jnp.float32)
        cu_buf[pl.ds(0, HALO), :] = cp_ref[...] * up_ref[...] * keep
        cu_buf[pl.ds(HALO, tr), :] = c_ref[...] * u_ref[...]
        v_buf[pl.ds(0, HALO), :] = vp_ref[...] * keep
        v_buf[pl.ds(HALO, tr), :] = v_ref[...]
        _, a = _conv_piece(cu_buf, 0, tr, b_ref[...], convw_ref)
        for h, rstd in enumerate(_head_stats(a, CONV_HEAD_DIM)):
            cols = slice(h * CONV_HEAD_DIM, (h + 1) * CONV_HEAD_DIM)
            out_ref[:, cols] = (a[:, cols] * rstd).astype(out_ref.dtype)
        pooled = _pooled_piece(v_buf, 0, tr, v_ref[...], i * tr, dg)
        for gi, p in enumerate(pooled):
            z = _dot(p, _pool_matrix(pool_ref, gi), "nn")
            cols = slice(gi * dg, (gi + 1) * dg)
            out_ref[:, cw + gi * dg:cw + (gi + 1) * dg] = (z * _rstd(z) * scale_ref[:, cols]).astype(out_ref.dtype)

    prev_c, _ = _halo_specs(t, tr, cw, 1)
    prev_u, _ = _halo_specs(t, tr, cw, 2)
    prev_v, _ = _halo_specs(t, tr, cw, 3)
    return pl.pallas_call(
        body, name="mixers_fwd", grid=(t // tr,),
        in_specs=[main(0), main(1), main(2), main(3), prev_c, prev_u, prev_v,
                  _const_spec(conv_w.shape), _const_spec(pool_g.shape), _const_spec(pool_scale.shape)],
        out_specs=_row_spec(tr, 2 * cw),
        out_shape=jax.ShapeDtypeStruct((t, 2 * cw), MM_DTYPE),
        scratch_shapes=[pltpu.VMEM((tr + HALO, cw), jnp.float32), pltpu.VMEM((tr + HALO, cw), jnp.float32)],
        compiler_params=_params(("arbitrary",)),
    )(proj, proj, proj, proj, proj, proj, proj, conv_w, pool_g, pool_scale)


def _mixers_bwd(proj, dmixed, conv_w, pool_g, pool_scale, deps=()):
    t, e = proj.shape
    cw = e // 4
    dg = pool_g.shape[-1]
    n_groups = len(POOL_WINDOWS)
    tr = _tile(t, 256)
    n_tiles = t // tr
    ext = tr + 2 * HALO

    def main(col):
        return pl.BlockSpec((tr, cw), lambda i: (i, col))

    def body(b_ref, c_ref, u_ref, v_ref, dyc_ref, dyp_ref,
             cp_ref, up_ref, vp_ref,
             bn_ref, cn_ref, un_ref, vn_ref, dycn_ref, dypn_ref,
             convw_ref, pool_ref, scale_ref, *rest):
        (dproj_ref, dconvw_ref, dpool_ref, dscale_ref,
         cu_buf, v_buf, dconv_buf, dpn_buf, dpooled_buf) = rest[len(deps):]
        i = pl.program_id(0)
        keep_prev = (i > 0).astype(jnp.float32)
        keep_next = (i < n_tiles - 1).astype(jnp.float32)
        cu_buf[pl.ds(0, HALO), :] = cp_ref[...] * up_ref[...] * keep_prev
        cu_buf[pl.ds(HALO, tr), :] = c_ref[...] * u_ref[...]
        cu_buf[pl.ds(HALO + tr, HALO), :] = cn_ref[...] * un_ref[...]
        v_buf[pl.ds(0, HALO), :] = vp_ref[...] * keep_prev
        v_buf[pl.ds(HALO, tr), :] = v_ref[...]
        v_buf[pl.ds(HALO + tr, HALO), :] = vn_ref[...]

        def conv_piece(start, n, b_piece, dyc_piece, keep, is_main):
            conv, a = _conv_piece(cu_buf, start, n, b_piece, convw_ref)
            for h, rstd in enumerate(_head_stats(a, CONV_HEAD_DIM)):
                cols = slice(h * CONV_HEAD_DIM, (h + 1) * CONV_HEAD_DIM)
                da = _norm_bwd(dyc_piece[:, cols], a[:, cols] * rstd, rstd)
                dconv_buf[pl.ds(start, n), cols] = da * b_piece[:, cols] * keep
                if is_main:
                    dproj_ref[:, cols] = (da * conv[:, cols]).astype(dproj_ref.dtype)

        conv_piece(0, tr, b_ref[...], dyc_ref[...], 1.0, True)
        conv_piece(tr, HALO, bn_ref[...], dycn_ref[...], keep_next, False)

        dconv_main = dconv_buf[pl.ds(0, tr), :]
        dcu = None
        dw_rows = []
        for k in range(CONV_K):
            shift = CONV_K - 1 - k
            term = convw_ref[k:k + 1, :] * dconv_buf[pl.ds(shift, tr), :]
            dcu = term if dcu is None else dcu + term
            dw_rows.append(jnp.sum(dconv_main * cu_buf[pl.ds(HALO - shift, tr), :], axis=0, keepdims=True))
        dproj_ref[:, cw:2 * cw] = (dcu * u_ref[...]).astype(dproj_ref.dtype)
        dproj_ref[:, 2 * cw:3 * cw] = (dcu * c_ref[...]).astype(dproj_ref.dtype)
        _accumulate(dconvw_ref, jnp.concatenate(dw_rows, axis=0), i)

        def pool_piece(start, n, v_piece, dyp_piece, keep, is_main):
            pooled = _pooled_piece(v_buf, start, n, v_piece, i * tr, dg)
            dscale, dmats = [], []
            for gi, w in enumerate(POOL_WINDOWS):
                cols = slice(gi * dg, (gi + 1) * dg)
                mat = _pool_matrix(pool_ref, gi)
                z = _dot(pooled[gi], mat, "nn")
                rstd = _rstd(z)
                nz = z * rstd
                dyp_g = dyp_piece[:, cols]
                dz = _norm_bwd(dyp_g * scale_ref[:, cols], nz, rstd)
                dpooled = _dot(dz, mat, "nt") * keep
                dpn_buf[pl.ds(start, n), cols] = dpooled * _inv_count(i * tr + start, n, w)
                if is_main:
                    dpooled_buf[:, cols] = dpooled
                    dscale.append(jnp.sum(dyp_g * nz, axis=0, keepdims=True))
                    dmats.append(_dot(pooled[gi], dz, "tn"))
            return dscale, dmats

        dscale, dmats = pool_piece(0, tr, v_ref[...], dyp_ref[...], 1.0, True)
        pool_piece(tr, HALO, vn_ref[...], dypn_ref[...], keep_next, False)
        for gi, w in enumerate(POOL_WINDOWS):
            cols = slice(gi * dg, (gi + 1) * dg)
            back = None
            for k in range(w):
                term = dpn_buf[pl.ds(k, tr), cols]
                back = term if back is None else back + term
            dproj_ref[:, 3 * cw + gi * dg:3 * cw + (gi + 1) * dg] = (back - dpooled_buf[:, cols]).astype(dproj_ref.dtype)
        _accumulate(dscale_ref, jnp.concatenate(dscale, axis=1), i)
        rows = dg // N_CHIPS
        for gi in range(n_groups):
            for c in range(N_CHIPS):
                _accumulate(dpool_ref.at[c, gi], dmats[gi][c * rows:(c + 1) * rows, :], i)

    prev_c, next_c = _halo_specs(t, tr, cw, 1)
    prev_u, next_u = _halo_specs(t, tr, cw, 2)
    prev_v, next_v = _halo_specs(t, tr, cw, 3)
    _, next_b = _halo_specs(t, tr, cw, 0)
    _, next_dyc = _halo_specs(t, tr, cw, 0)
    _, next_dyp = _halo_specs(t, tr, cw, 1)
    return pl.pallas_call(
        body, name="mixers_bwd", grid=(n_tiles,),
        in_specs=[main(0), main(1), main(2), main(3), main(0), main(1),
                  prev_c, prev_u, prev_v,
                  next_b, next_c, next_u, next_v, next_dyc, next_dyp,
                  _const_spec(conv_w.shape), _const_spec(pool_g.shape), _const_spec(pool_scale.shape)]
        + [ANY] * len(deps),
        out_specs=[_row_spec(tr, e), _const_spec(conv_w.shape), _const_spec(pool_g.shape),
                   _const_spec(pool_scale.shape)],
        out_shape=[jax.ShapeDtypeStruct((t, e), MM_DTYPE), jax.ShapeDtypeStruct(conv_w.shape, jnp.float32),
                   jax.ShapeDtypeStruct(pool_g.shape, jnp.float32),
                   jax.ShapeDtypeStruct(pool_scale.shape, jnp.float32)],
        scratch_shapes=[pltpu.VMEM((ext, cw), jnp.float32), pltpu.VMEM((ext, cw), jnp.float32),
                        pltpu.VMEM((tr + HALO, cw), jnp.float32), pltpu.VMEM((tr + HALO, cw), jnp.float32),
                        pltpu.VMEM((tr, cw), jnp.float32)],
        compiler_params=_params(("arbitrary",)),
    )(proj, proj, proj, proj, dmixed, dmixed,
      proj, proj, proj,
      proj, proj, proj, proj, dmixed, dmixed,
      conv_w, pool_g, pool_scale, *deps)


def _cast_rows(name, w, dtype):
    r, c = w.shape
    tr = _tile(r, 512)

    def body(w_ref, o_ref):
        o_ref[...] = w_ref[...].astype(o_ref.dtype)

    return pl.pallas_call(
        body, name=name, grid=(r // tr,), in_specs=[_row_spec(tr, c)], out_specs=_row_spec(tr, c),
        out_shape=jax.ShapeDtypeStruct((r, c), dtype), compiler_params=_params(("arbitrary",)),
    )(w)


def _cast_to_slot(name, w, chip, by_columns=False):
    r, c = w.shape
    tr = _tile(r, 512)
    if by_columns:
        out_spec = pl.BlockSpec((tr, c), lambda i, chip_ref: (i, chip_ref[0]))
        out_shape = jax.ShapeDtypeStruct((r, N_CHIPS * c), MM_DTYPE)
    else:
        out_spec = pl.BlockSpec((None, tr, c), lambda i, chip_ref: (chip_ref[0], i, 0))
        out_shape = jax.ShapeDtypeStruct((N_CHIPS, r, c), MM_DTYPE)

    def body(chip_ref, w_ref, o_ref):
        o_ref[...] = w_ref[...].astype(o_ref.dtype)

    return pl.pallas_call(
        body, name=name,
        grid_spec=pltpu.PrefetchScalarGridSpec(
            num_scalar_prefetch=1, grid=(r // tr,),
            in_specs=[pl.BlockSpec((tr, c), lambda i, chip_ref: (i, 0))], out_specs=out_spec),
        out_shape=out_shape, compiler_params=_params(("arbitrary",)),
    )(chip, w)


def _sum_own_and_received(name, own, land, chip):
    _, r, c = land.shape
    tr = _tile(r, 256)

    def body(chip_ref, own_ref, a_ref, b_ref, c_ref, o_ref):
        tot = own_ref[...].astype(jnp.float32) + a_ref[...].astype(jnp.float32)
        tot = tot + b_ref[...].astype(jnp.float32)
        o_ref[...] = (tot + c_ref[...].astype(jnp.float32)).astype(o_ref.dtype)

    def slot(k):
        return pl.BlockSpec((None, tr, c), lambda i, chip_ref: ((chip_ref[0] + k) % N_CHIPS, i, 0))

    own_spec = slot(0) if len(own.shape) == 3 else pl.BlockSpec((tr, c), lambda i, chip_ref: (i, chip_ref[0]))
    return pl.pallas_call(
        body, name=name,
        grid_spec=pltpu.PrefetchScalarGridSpec(
            num_scalar_prefetch=1, grid=(r // tr,), in_specs=[own_spec, slot(1), slot(2), slot(3)],
            out_specs=pl.BlockSpec((tr, c), lambda i, chip_ref: (i, 0))),
        out_shape=jax.ShapeDtypeStruct((r, c), COMM_DTYPE), compiler_params=_params(("arbitrary",)),
    )(chip, own, land, land, land)


def _sum_slots(name, slots):
    n, r, c = slots.shape
    tr = _tile(r, 256)

    def body(s_ref, o_ref):
        tot = s_ref[0].astype(jnp.float32)
        for s in range(1, n):
            tot = tot + s_ref[s].astype(jnp.float32)
        o_ref[...] = tot

    return pl.pallas_call(
        body, name=name, grid=(r // tr,),
        in_specs=[pl.BlockSpec((n, tr, c), lambda i: (0, i, 0))], out_specs=_row_spec(tr, c),
        out_shape=jax.ShapeDtypeStruct((r, c), jnp.float32), compiler_params=_params(("arbitrary",)),
    )(slots)


def _adamw_math(w, g, m, v):
    m = ADAM_B1 * m + (1.0 - ADAM_B1) * g
    v = ADAM_B2 * v + (1.0 - ADAM_B2) * (g * g)
    m_hat = m / (1.0 - ADAM_B1 ** ADAM_STEP)
    v_hat = v / (1.0 - ADAM_B2 ** ADAM_STEP)
    delta = -ADAM_LR * (m_hat / (jnp.sqrt(v_hat) + ADAM_EPS) + ADAM_WD * w)
    return delta, m, v


def _adamw(name, w, m, v, grad_parts):
    r, c = w.shape
    tr = _tile(r, 256)
    n_parts = len(grad_parts)

    def body(*refs):
        w_ref, m_ref, v_ref = refs[:3]
        part_refs = refs[3:3 + n_parts]
        g_ref, d_ref, nm_ref, nv_ref = refs[3 + n_parts:]
        g = part_refs[0][...].astype(jnp.float32)
        for p in part_refs[1:]:
            g = g + p[...].astype(jnp.float32)
        delta, nm, nv = _adamw_math(w_ref[...], g, m_ref[...], v_ref[...])
        g_ref[...] = g
        d_ref[...] = delta
        nm_ref[...] = nm
        nv_ref[...] = nv

    spec = _row_spec(tr, c)
    out = jax.ShapeDtypeStruct((r, c), jnp.float32)
    return pl.pallas_call(
        body, name=name, grid=(r // tr,), in_specs=[spec] * (3 + n_parts), out_specs=[spec] * 4,
        out_shape=[out] * 4, compiler_params=_params(("arbitrary",)),
    )(w, m, v, *grad_parts)


def _chip_peers():
    x, y, c = lax.axis_index("x"), lax.axis_index("y"), lax.axis_index("c")
    return x, y, c, [(1 - x, y), (x, 1 - y), (1 - x, 1 - y)]


HBM = pl.BlockSpec(memory_space=pltpu.HBM)
SEM = pl.BlockSpec(memory_space=pltpu.SEMAPHORE)
TOKEN = jax.ShapeDtypeStruct((8, 128), jnp.float32)
N_PEER_CHIPS = N_CHIPS - 1


def _in_flight():
    return pltpu.CompilerParams(has_side_effects=pltpu.SideEffectType.DATAFLOW_SIDE_EFFECTING)


def _slot(ref, slot):
    if len(ref.shape) == 3:
        return ref.at[slot]
    width = ref.shape[1] // N_CHIPS
    return ref.at[:, pl.ds(pl.multiple_of(slot * width, 128), width)]


def _half_slot(ref, slot, half):
    rows = ref.shape[-2] // 2
    if len(ref.shape) == 3:
        return ref.at[slot, pl.ds(half * rows, rows)]
    width = ref.shape[1] // N_CHIPS
    return ref.at[pl.ds(half * rows, rows), pl.ds(pl.multiple_of(slot * width, 128), width)]


def _slot_shape(shape):
    return shape[1:] if len(shape) == 3 else (shape[0], shape[1] // N_CHIPS)


def _gather_start(name, full, deps=()):
    def body(full_ref, *rest):
        send_sems, recv_sems, _, token_ref = rest[len(deps):]
        x, y, c, peers = _chip_peers()
        mine = _half_slot(full_ref, 2 * x + y, c)
        for p, (px, py) in enumerate(peers):
            pltpu.make_async_remote_copy(
                src_ref=mine, dst_ref=mine, send_sem=send_sems.at[p], recv_sem=recv_sems.at[p],
                device_id=(px, py, c), device_id_type=MESH).start()
        token_ref[...] = jnp.zeros_like(token_ref)

    return pl.pallas_call(
        body, name=name,
        out_shape=(pltpu.SemaphoreType.DMA((N_PEER_CHIPS,)), pltpu.SemaphoreType.DMA((N_PEER_CHIPS,)),
                   pltpu.HBM(full.shape, full.dtype), TOKEN),
        in_specs=[HBM] + [ANY] * len(deps), out_specs=(SEM, SEM, HBM, pl.BlockSpec(memory_space=pltpu.VMEM)),
        input_output_aliases={0: 2}, compiler_params=_in_flight(),
    )(pltpu.with_memory_space_constraint(full, pltpu.HBM), *deps)


def _gather_wait(name, send_sems, recv_sems, full, after):
    def body(full_ref, send_sems, recv_sems, after_ref, out_ref):
        x, y, c, peers = _chip_peers()
        for p, (px, py) in enumerate(peers):
            cp = pltpu.make_async_remote_copy(
                src_ref=_half_slot(full_ref, 2 * x + y, c), dst_ref=_half_slot(full_ref, 2 * px + py, c),
                send_sem=send_sems.at[p], recv_sem=recv_sems.at[p], device_id=(px, py, c), device_id_type=MESH)
            cp.wait_send()
            cp.wait_recv()

    return pl.pallas_call(
        body, name=name, out_shape=pltpu.HBM(full.shape, full.dtype),
        in_specs=(HBM, SEM, SEM, ANY), out_specs=HBM, input_output_aliases={0: 0}, compiler_params=_in_flight(),
    )(full, send_sems, recv_sems, after)


def _forward_sibling(name, fulls):
    n = len(fulls)

    def body(*refs):
        in_refs, out_refs = refs[:n], refs[n:2 * n]
        send_sems, recv_sems = refs[2 * n:]
        x, y, c, peers = _chip_peers()
        sends, recvs = [], []
        for a in range(n):
            for p, (px, py) in enumerate(peers):
                k = N_PEER_CHIPS * a + p
                slot = 2 * px + py
                cp = pltpu.make_async_remote_copy(
                    src_ref=_half_slot(in_refs[a], slot, c), dst_ref=_half_slot(out_refs[a], slot, c),
                    send_sem=send_sems.at[k], recv_sem=recv_sems.at[k], device_id=(x, y, 1 - c), device_id_type=MESH)
                cp.start()
                sends.append(cp)
                recvs.append(pltpu.make_async_remote_copy(
                    src_ref=_half_slot(in_refs[a], slot, c), dst_ref=_half_slot(out_refs[a], slot, 1 - c),
                    send_sem=send_sems.at[k], recv_sem=recv_sems.at[k], device_id=(x, y, 1 - c), device_id_type=MESH))
        for cp in recvs:
            cp.wait_recv()
        for cp in sends:
            cp.wait_send()

    return pl.pallas_call(
        body, name=name, in_specs=[ANY] * n, out_specs=[ANY] * n,
        out_shape=[jax.ShapeDtypeStruct(f.shape, f.dtype) for f in fulls],
        input_output_aliases={a: a for a in range(n)},
        scratch_shapes=[pltpu.SemaphoreType.DMA((N_PEER_CHIPS * n,)), pltpu.SemaphoreType.DMA((N_PEER_CHIPS * n,))],
    )(*fulls)


def _scatter_start(name, grads, deps=()):
    def body(g_ref, land_ref, *rest):
        send_sems, recv_sems, _, _, token_ref = rest[len(deps):]
        x, y, c, peers = _chip_peers()
        me = 2 * x + y
        for p, (px, py) in enumerate(peers):
            pltpu.make_async_remote_copy(
                src_ref=_slot(g_ref, 2 * px + py), dst_ref=land_ref.at[me], send_sem=send_sems.at[p],
                recv_sem=recv_sems.at[p], device_id=(px, py, c), device_id_type=MESH).start()
        token_ref[...] = jnp.zeros_like(token_ref)

    land = lax.empty((N_CHIPS,) + _slot_shape(grads.shape), grads.dtype)
    return pl.pallas_call(
        body, name=name,
        out_shape=(pltpu.SemaphoreType.DMA((N_PEER_CHIPS,)), pltpu.SemaphoreType.DMA((N_PEER_CHIPS,)),
                   pltpu.HBM(grads.shape, grads.dtype), pltpu.HBM(land.shape, land.dtype), TOKEN),
        in_specs=[HBM, HBM] + [ANY] * len(deps),
        out_specs=(SEM, SEM, HBM, HBM, pl.BlockSpec(memory_space=pltpu.VMEM)),
        input_output_aliases={0: 2, 1: 3}, compiler_params=_in_flight(),
    )(pltpu.with_memory_space_constraint(grads, pltpu.HBM), pltpu.with_memory_space_constraint(land, pltpu.HBM), *deps)


def _scatter_wait(name, send_sems, recv_sems, grads, land, after):
    def body(g_ref, land_ref, send_sems, recv_sems, after_ref, g_out, land_out):
        x, y, c, peers = _chip_peers()
        for p, (px, py) in enumerate(peers):
            cp = pltpu.make_async_remote_copy(
                src_ref=_slot(g_ref, 2 * px + py), dst_ref=land_ref.at[2 * px + py], send_sem=send_sems.at[p],
                recv_sem=recv_sems.at[p], device_id=(px, py, c), device_id_type=MESH)
            cp.wait_send()
            cp.wait_recv()

    return pl.pallas_call(
        body, name=name,
        out_shape=(pltpu.HBM(grads.shape, grads.dtype), pltpu.HBM(land.shape, land.dtype)),
        in_specs=(HBM, HBM, SEM, SEM, ANY), out_specs=(HBM, HBM), input_output_aliases={0: 0, 1: 1},
        compiler_params=_in_flight(),
    )(grads, land, send_sems, recv_sems, after)


def _swap_sibling(name, parts):
    n = len(parts)

    def body(*refs):
        in_refs, out_refs = refs[:n], refs[n:2 * n]
        send_sems, recv_sems = refs[2 * n:]
        x, y, c = lax.axis_index("x"), lax.axis_index("y"), lax.axis_index("c")
        copies = []
        for a in range(n):
            cp = pltpu.make_async_remote_copy(
                src_ref=in_refs[a], dst_ref=out_refs[a], send_sem=send_sems.at[a], recv_sem=recv_sems.at[a],
                device_id=(x, y, 1 - c), device_id_type=MESH)
            cp.start()
            copies.append(cp)
        for cp in copies:
            cp.wait_recv()
        for cp in copies:
            cp.wait_send()

    return pl.pallas_call(
        body, name=name, in_specs=[ANY] * n, out_specs=[ANY] * n,
        out_shape=[jax.ShapeDtypeStruct(p.shape, p.dtype) for p in parts],
        scratch_shapes=[pltpu.SemaphoreType.DMA((n,)), pltpu.SemaphoreType.DMA((n,))],
    )(*parts)


def _gather_devices(name, block):
    def body(in_ref, out_ref, send_sems, recv_sems, local_sem):
        x, y, c = lax.axis_index("x"), lax.axis_index("y"), lax.axis_index("c")
        me = 4 * x + 2 * y + c
        local = pltpu.make_async_copy(in_ref, out_ref.at[me], local_sem)
        local.start()
        sends, recvs = [], []
        k = 0
        for fx in range(2):
            for fy in range(2):
                for fc in range(2):
                    if fx == fy == fc == 0:
                        continue
                    px = x if fx == 0 else 1 - x
                    py = y if fy == 0 else 1 - y
                    pc = c if fc == 0 else 1 - c
                    cp = pltpu.make_async_remote_copy(
                        src_ref=in_ref, dst_ref=out_ref.at[me], send_sem=send_sems.at[k], recv_sem=recv_sems.at[k],
                        device_id=(px, py, pc), device_id_type=MESH)
                    cp.start()
                    sends.append(cp)
                    recvs.append(pltpu.make_async_remote_copy(
                        src_ref=in_ref, dst_ref=out_ref.at[4 * px + 2 * py + pc], send_sem=send_sems.at[k],
                        recv_sem=recv_sems.at[k], device_id=(px, py, pc), device_id_type=MESH))
                    k += 1
        for cp in recvs:
            cp.wait_recv()
        for cp in sends:
            cp.wait_send()
        local.wait()

    return pl.pallas_call(
        body, name=name, in_specs=[ANY], out_specs=ANY,
        out_shape=jax.ShapeDtypeStruct((N_DEV,) + block.shape, block.dtype),
        scratch_shapes=[pltpu.SemaphoreType.DMA((N_DEV - 1,)), pltpu.SemaphoreType.DMA((N_DEV - 1,)),
                        pltpu.SemaphoreType.DMA],
    )(block)


def _pack_rows(pieces, width):
    flat = jnp.concatenate([p.reshape(-1) for p in pieces])
    rows = -(-flat.shape[0] // width)
    rows = -(-rows // 8) * 8
    flat = jnp.pad(flat, (0, rows * width - flat.shape[0]))
    return flat.reshape(rows, width)


def _unpack_rows(packed, shapes):
    flat = packed.reshape(-1)
    out, off = [], 0
    for s in shapes:
        size = 1
        for d in s:
            size *= d
        out.append(flat[off:off + size].reshape(s))
        off += size
    return out


def kernel(x, ln_mix_pre, w_in, conv_w, pool_w, pool_scale, w_out, ln_mix_post, ln_ffn_pre, w_gate, w_up, w_down, ln_ffn_post, loss_target, m_ln_mix_pre, m_w_in, m_conv_w, m_pool_w, m_pool_scale, m_w_out, m_ln_mix_post, m_ln_ffn_pre, m_w_gate, m_w_up, m_w_down, m_ln_ffn_post, v_ln_mix_pre, v_w_in, v_conv_w, v_pool_w, v_pool_scale, v_w_out, v_ln_mix_post, v_ln_ffn_pre, v_w_gate, v_w_up, v_w_down, v_ln_ffn_post):
    t, d = x.shape[1], x.shape[2]
    e4 = w_in.shape[2]
    e = N_CHIPS * e4
    f4 = w_gate.shape[2]
    f = N_CHIPS * f4
    n_groups, dg4, dg = pool_w.shape[1], pool_w.shape[2], pool_w.shape[3]
    cw4 = conv_w.shape[2]
    chip = 2 * lax.axis_index("x") + lax.axis_index("y")
    xs, tgt = x[0], loss_target[0]

    big = {"w_in": w_in[0], "w_out": w_out[0], "w_gate": w_gate[0], "w_up": w_up[0], "w_down": w_down[0],
           "pool_w": pool_w[0].reshape(n_groups * dg4, dg)}
    names = ["w_in", "pool_w", "w_out", "w_gate", "w_up", "w_down"]
    chip_arr = chip.astype(jnp.int32).reshape(1)
    conv_all = _gather_devices("gather_conv_w", _pack_rows([conv_w[0]], 128))
    conv_full = jnp.concatenate(
        [conv_all[2 * j].reshape(-1)[:CONV_K * cw4].reshape(CONV_K, cw4) for j in range(N_CHIPS)], axis=1)
    in_flight, deps = {}, (conv_all,)
    for k in names:
        by_columns = k in ("w_gate", "w_up")
        in_flight[k] = _gather_start(
            "gather_start_" + k, _cast_to_slot("cast_" + k, big[k], chip_arr, by_columns), deps)
        deps = (in_flight[k][3],)

    def landed(ks, after):
        fulls = []
        for k in ks:
            send_sems, recv_sems, full, _ = in_flight[k]
            fulls.append(_gather_wait("gather_wait_" + k, send_sems, recv_sems, full, after))
            after = fulls[-1]
        return _forward_sibling("forward_" + ks[0], fulls)

    def rows3(tile, width):
        return pl.BlockSpec((tile, width), lambda i, j, k: (i, 0))

    gain3 = pl.BlockSpec((1, d), lambda i, j, k: (0, 0))
    f32_td = jax.ShapeDtypeStruct((t, d), jnp.float32)
    mm_td = jax.ShapeDtypeStruct((t, d), MM_DTYPE)
    f32_gain = jax.ShapeDtypeStruct((1, d), jnp.float32)

    h = _pre_norm(xs, ln_mix_pre, deps)
    (win_g,) = landed(["w_in"], h)
    tm = _tile(t, 1024)
    proj = _matmul(
        "in_proj", grid=(t // tm, N_CHIPS, 1), mode="nn",
        pairs=[[(h, win_g)]],
        pair_specs=[[(rows3(tm, d), pl.BlockSpec((None, d, e4), lambda i, j, k: (j, 0, 0)))]],
        acc_shapes=[(tm, e4)], out_shapes=[jax.ShapeDtypeStruct((t, e), jnp.float32)],
        out_specs=[pl.BlockSpec((tm, e4), lambda i, j, k: (i, j))], epilogue=_identity_epilogue)[0]
    pool_g, wout_full = landed(["pool_w", "w_out"], proj)
    pool_g, wout_full = pool_g.reshape(N_CHIPS, n_groups, dg4, dg), wout_full.reshape(d, d)
    mixed = _mixers_fwd(proj, conv_full, pool_g, pool_scale)

    def post_mix_epilogue(accs, extra_refs):
        x_ref, g2_ref, g3_ref = extra_refs
        mo = accs[0]
        x1 = x_ref[...] + mo * _rstd(mo) * g2_ref[...]
        return mo, x1, x1 * _rstd(x1) * g3_ref[...]

    tm_mix = _tile(t, 256)
    mix_out, x1, hf = _matmul(
        "out_proj", grid=(t // tm_mix, 1, 1), mode="nn",
        pairs=[[(mixed, wout_full)]],
        pair_specs=[[(rows3(tm_mix, d), pl.BlockSpec((d, d), lambda i, j, k: (0, 0)))]],
        acc_shapes=[(tm_mix, d)], extras=[xs, ln_mix_post, ln_ffn_pre],
        extra_specs=[rows3(tm_mix, d), gain3, gain3], out_shapes=[f32_td, f32_td, mm_td],
        out_specs=[rows3(tm_mix, d)] * 3, epilogue=post_mix_epilogue, epi_rows=EPILOGUE_ROWS)
    wg_full, wu_full = landed(["w_gate", "w_up"], hf)

    def gate_up_epilogue(accs, extra_refs):
        g, up = accs
        return g, up, g * jax.nn.sigmoid(g) * up

    tf = _tile(f, 512)
    ff_tile = jax.ShapeDtypeStruct((t, f), MM_DTYPE)
    ff_spec = pl.BlockSpec((tm, tf), lambda j, i, k: (i, j))
    hf_spec = pl.BlockSpec((tm, d), lambda j, i, k: (i, 0))
    g_act, up_act, act = _matmul(
        "gate_up", grid=(f // tf, t // tm, 1), mode="nn",
        pairs=[[(hf, wg_full)], [(hf, wu_full)]],
        pair_specs=[[(hf_spec, pl.BlockSpec((d, tf), lambda j, i, k: (0, j)))]] * 2,
        acc_shapes=[(tm, tf)] * 2, out_shapes=[ff_tile] * 3, out_specs=[ff_spec] * 3, epilogue=gate_up_epilogue)
    wdown_full = landed(["w_down"], act)[0].reshape(f, d)

    def loss_epilogue(accs, extra_refs):
        x1_ref, tg_ref, g_ref = extra_refs
        ff_v = accs[0]
        rstd = _rstd(ff_v)
        n = ff_v * rstd
        g = g_ref[...]
        err = x1_ref[...] + n * g - tg_ref[...]
        rows_loss = 0.5 * jnp.sum(jnp.mean(err * err, axis=-1, keepdims=True), axis=0, keepdims=True)
        dout = err / d
        return (dout, _norm_bwd(dout * g, n, rstd), jnp.broadcast_to(rows_loss, (8, 128)),
                jnp.sum(dout * n, axis=0, keepdims=True))

    tm_row = _tile(t, 1024)
    dout, dff, loss_tile, dg_ffn_post = _matmul(
        "down_proj", grid=(t // tm_row, 1, f // tf), mode="nn",
        pairs=[[(act, wdown_full)]],
        pair_specs=[[(pl.BlockSpec((tm_row, tf), lambda i, j, k: (i, k)),
                      pl.BlockSpec((tf, d), lambda i, j, k: (k, 0)))]],
        acc_shapes=[(tm_row, d)], extras=[x1, tgt, ln_ffn_post], extra_specs=[STREAM] * 2 + [gain3],
        out_shapes=[f32_td, mm_td, jax.ShapeDtypeStruct((8, 128), jnp.float32), f32_gain],
        out_specs=[STREAM] * 2 + [pl.BlockSpec((8, 128), lambda i, j, k: (0, 0)), gain3],
        epilogue=loss_epilogue, epi_rows=EPILOGUE_ROWS, sum_outs=(2, 3))

    def dact_epilogue(accs, extra_refs):
        dact = accs[0]
        g = extra_refs[0][...].astype(jnp.float32)
        up = extra_refs[1][...].astype(jnp.float32)
        sig = jax.nn.sigmoid(g)
        silu = g * sig
        return dact * up * (sig + silu * (1.0 - sig)), dact * silu

    ff_spec_ij = pl.BlockSpec((tm, tf), lambda i, j, k: (i, j))
    dg_act, dup_act = _matmul(
        "dact", grid=(t // tm, f // tf, 1), mode="nt",
        pairs=[[(dff, wdown_full)]],
        pair_specs=[[(rows3(tm, d), pl.BlockSpec((tf, d), lambda i, j, k: (j, 0)))]],
        acc_shapes=[(tm, tf)], extras=[g_act, up_act], extra_specs=[ff_spec_ij, ff_spec_ij],
        out_shapes=[ff_tile] * 2, out_specs=[ff_spec_ij] * 2, epilogue=dact_epilogue)
    tk = _tile(t, 1024)
    tn = _tile(d, 1024)
    dw_down = _matmul(
        "dw_down", grid=(N_CHIPS, d // tn, t // tk), mode="tn",
        pairs=[[(act, dff)]],
        pair_specs=[[(pl.BlockSpec((tk, f4), lambda i, j, k: (k, i)),
                      pl.BlockSpec((tk, tn), lambda i, j, k: (k, j)))]],
        acc_shapes=[(f4, tn)], out_shapes=[jax.ShapeDtypeStruct((N_CHIPS, f4, d), COMM_DTYPE)],
        out_specs=[pl.BlockSpec((None, f4, tn), lambda i, j, k: (i, 0, j))], epilogue=_identity_epilogue)[0]
    leaving = {"w_down": _scatter_start("scatter_start_w_down", dw_down)}

    def ffn_pre_epilogue(accs, extra_refs):
        x1_ref, dout_ref, mo_ref, g3_ref, g2_ref = extra_refs
        dhf_v = accs[0]
        x1_v = x1_ref[...]
        rstd3 = _rstd(x1_v)
        n3 = x1_v * rstd3
        dx1_v = dout_ref[...] + _norm_bwd(dhf_v * g3_ref[...], n3, rstd3)
        mo = mo_ref[...]
        rstd2 = _rstd(mo)
        n2 = mo * rstd2
        return (dx1_v, _norm_bwd(dx1_v * g2_ref[...], n2, rstd2), jnp.sum(dhf_v * n3, axis=0, keepdims=True),
                jnp.sum(dx1_v * n2, axis=0, keepdims=True))

    dx1, dmo, dg_ffn_pre, dg_mix_post = _matmul(
        "dhf", grid=(t // tm_row, 1, f // tf), mode="nt",
        pairs=[[(dg_act, wg_full), (dup_act, wu_full)]],
        pair_specs=[[(pl.BlockSpec((tm_row, tf), lambda i, j, k: (i, k)),
                      pl.BlockSpec((d, tf), lambda i, j, k: (0, k)))] * 2],
        acc_shapes=[(tm_row, d)], extras=[x1, dout, mix_out, ln_ffn_pre, ln_mix_post],
        extra_specs=[STREAM] * 3 + [gain3] * 2, out_shapes=[f32_td, mm_td, f32_gain, f32_gain],
        out_specs=[STREAM] * 2 + [gain3] * 2, epilogue=ffn_pre_epilogue, epi_rows=EPILOGUE_ROWS,
        sum_outs=(2, 3), deps=leaving["w_down"][4:])
    tmo = _tile(d, 2048)
    grad_ff = jax.ShapeDtypeStruct((d, f), COMM_DTYPE)
    dw_gate, dw_up = _matmul(
        "dw_gate_up", grid=(d // tmo, f // tf, t // tk), mode="tn",
        pairs=[[(hf, dg_act)], [(hf, dup_act)]],
        pair_specs=[[(pl.BlockSpec((tk, tmo), lambda i, j, k: (k, i)),
                      pl.BlockSpec((tk, tf), lambda i, j, k: (k, j)))]] * 2,
        acc_shapes=[(tmo, tf)] * 2, out_shapes=[grad_ff] * 2,
        out_specs=[pl.BlockSpec((tmo, tf), lambda i, j, k: (i, j))] * 2, epilogue=_identity_epilogue)
    leaving["w_gate"] = _scatter_start("scatter_start_w_gate", dw_gate)
    leaving["w_up"] = _scatter_start("scatter_start_w_up", dw_up, leaving["w_gate"][4:])

    dmixed = _matmul(
        "dmixed", grid=(t // tm, d // tn, 1), mode="nt",
        pairs=[[(dmo, wout_full)]],
        pair_specs=[[(rows3(tm, d), pl.BlockSpec((tn, d), lambda i, j, k: (j, 0)))]],
        acc_shapes=[(tm, tn)], out_shapes=[f32_td],
        out_specs=[pl.BlockSpec((tm, tn), lambda i, j, k: (i, j))], epilogue=_identity_epilogue,
        deps=leaving["w_up"][4:])[0]
    tmo = _tile(d, 1024)
    dw_out = _matmul(
        "dw_out", grid=(d // tmo, d // tn, t // tk), mode="tn",
        pairs=[[(mixed, dmo)]],
        pair_specs=[[(pl.BlockSpec((tk, tmo), lambda i, j, k: (k, i)),
                      pl.BlockSpec((tk, tn), lambda i, j, k: (k, j)))]],
        acc_shapes=[(tmo, tn)], out_shapes=[jax.ShapeDtypeStruct((d, d), COMM_DTYPE)],
        out_specs=[pl.BlockSpec((tmo, tn), lambda i, j, k: (i, j))], epilogue=_identity_epilogue)[0]
    leaving["w_out"] = _scatter_start("scatter_start_w_out", dw_out.reshape(N_CHIPS, d // N_CHIPS, d))
    dproj, dconv_full, dpool_g, dpool_scale = _mixers_bwd(proj, dmixed, conv_full, pool_g, pool_scale,
                                                          leaving["w_out"][4:])
    dpool_slots = _cast_rows("cast_dpool", dpool_g.reshape(N_CHIPS * n_groups * dg4, dg), COMM_DTYPE)
    leaving["pool_w"] = _scatter_start("scatter_start_pool_w", dpool_slots.reshape(N_CHIPS, n_groups * dg4, dg))
    dw_in = _matmul(
        "dw_in", grid=(d // tmo, N_CHIPS, t // tk), mode="tn",
        pairs=[[(h, dproj)]],
        pair_specs=[[(pl.BlockSpec((tk, tmo), lambda i, j, k: (k, i)),
                      pl.BlockSpec((tk, e4), lambda i, j, k: (k, j)))]],
        acc_shapes=[(tmo, e4)], out_shapes=[jax.ShapeDtypeStruct((N_CHIPS, d, e4), COMM_DTYPE)],
        out_specs=[pl.BlockSpec((None, tmo, e4), lambda i, j, k: (j, i, 0))], epilogue=_identity_epilogue,
        deps=leaving["pool_w"][4:])[0]
    leaving["w_in"] = _scatter_start("scatter_start_w_in", dw_in)

    def mix_pre_epilogue(accs, extra_refs):
        x_ref, dx1_ref, g_ref = extra_refs
        dh_v = accs[0]
        xv = x_ref[...]
        rstd = _rstd(xv)
        n = xv * rstd
        return dx1_ref[...] + _norm_bwd(dh_v * g_ref[...], n, rstd), jnp.sum(dh_v * n, axis=0, keepdims=True)

    grad_x, dg_mix_pre = _matmul(
        "dh", grid=(t // tm_row, 1, N_CHIPS), mode="nt",
        pairs=[[(dproj, win_g)]],
        pair_specs=[[(pl.BlockSpec((tm_row, e4), lambda i, j, k: (i, k)),
                      pl.BlockSpec((None, d, e4), lambda i, j, k: (k, 0, 0)))]],
        acc_shapes=[(tm_row, d)], extras=[xs, dx1, ln_mix_pre], extra_specs=[STREAM] * 2 + [gain3],
        out_shapes=[f32_td, f32_gain], out_specs=[STREAM, gain3], epilogue=mix_pre_epilogue,
        epi_rows=EPILOGUE_ROWS, sum_outs=(1,), deps=leaving["w_in"][4:])

    names = ["w_down", "w_gate", "w_up", "w_out", "pool_w", "w_in"]
    partial, after = [], grad_x
    for k in names:
        send_sems, recv_sems, own, land, _ = leaving[k]
        own, land = _scatter_wait("scatter_wait_" + k, send_sems, recv_sems, own, land, after)
        partial.append(_sum_own_and_received("sum_" + k, own, land, chip_arr))
        after = land
    other = _swap_sibling("swap_grads", partial)
    moments = {"w_in": (m_w_in, v_w_in), "w_out": (m_w_out, v_w_out), "w_gate": (m_w_gate, v_w_gate),
               "w_up": (m_w_up, v_w_up), "w_down": (m_w_down, v_w_down), "pool_w": (m_pool_w, v_pool_w)}
    result = {}
    for k, mine, theirs in zip(names, partial, other):
        shape = moments[k][0].shape
        two_d = big[k].shape
        outs = _adamw("adamw_" + k, big[k], moments[k][0].reshape(two_d), moments[k][1].reshape(two_d),
                      [mine, theirs])
        result[k] = [o.reshape(shape) for o in outs]

    small_shapes = [(1, d)] * 4 + [pool_scale.shape, (CONV_K, N_CHIPS * cw4)]
    packed = _pack_rows([dg_mix_pre, dg_mix_post, dg_ffn_pre, dg_ffn_post, dpool_scale, dconv_full], 1024)
    summed = _sum_slots("sum_small", _gather_devices("gather_small", packed))
    g_mix_pre, g_mix_post, g_ffn_pre, g_ffn_post, g_pool_scale, g_conv_full = _unpack_rows(summed, small_shapes)
    g_conv = lax.dynamic_slice(g_conv_full, (0, chip * cw4), (CONV_K, cw4))[None]
    small = [("ln_mix_pre", ln_mix_pre, m_ln_mix_pre, v_ln_mix_pre, g_mix_pre),
             ("conv_w", conv_w, m_conv_w, v_conv_w, g_conv),
             ("pool_scale", pool_scale, m_pool_scale, v_pool_scale, g_pool_scale),
             ("ln_mix_post", ln_mix_post, m_ln_mix_post, v_ln_mix_post, g_mix_post),
             ("ln_ffn_pre", ln_ffn_pre, m_ln_ffn_pre, v_ln_ffn_pre, g_ffn_pre),
             ("ln_ffn_post", ln_ffn_post, m_ln_ffn_post, v_ln_ffn_post, g_ffn_post)]
    shapes_small = [s[1].shape for s in small]
    packs = [_pack_rows([s[q] for s in small], 128) for q in (1, 2, 3, 4)]
    outs = _adamw("adamw_small", packs[0], packs[1], packs[2], [packs[3]])
    unpacked = [_unpack_rows(o, shapes_small) for o in outs]
    for idx, s in enumerate(small):
        result[s[0]] = [u[idx] for u in unpacked]

    loss = lax.psum(loss_tile[0, 0], ("x", "y", "c"))
    order = ["ln_mix_pre", "w_in", "conv_w", "pool_w", "pool_scale", "w_out", "ln_mix_post", "ln_ffn_pre",
             "w_gate", "w_up", "w_down", "ln_ffn_post"]
    return (loss, grad_x[None], *[result[k][0] for k in order], *[result[k][1] for k in order],
            *[result[k][2] for k in order], *[result[k][3] for k in order])
```

```python
import functools

import jax
import jax.numpy as jnp
from jax import lax
from jax.experimental import pallas as pl
from jax.experimental.pallas import tpu as pltpu

EPS = 1e-6
CONV_HEAD_DIM = 128
CONV_K = 3
POOL_WINDOWS = (2, 4, 8, 16)
HALO = 16
EPILOGUE_ROWS = 64
STREAM_ROWS = 256
N_CHIPS = 4
N_DEV = 8

ADAM_LR = 0.001
ADAM_B1 = 0.9
ADAM_B2 = 0.999
ADAM_EPS = 1e-08
ADAM_WD = 0.01
ADAM_STEP = 10

MM_DTYPE = jnp.bfloat16
COMM_DTYPE = jnp.bfloat16
VMEM_LIMIT = 62 * 1024 * 1024
MESH = pl.DeviceIdType.MESH
ANY = pl.BlockSpec(memory_space=pl.ANY)
STREAM = "stream"


def _tile(n, pref):
    t = min(pref, n)
    while n % t:
        t //= 2
    return t


def _params(sem):
    return pltpu.CompilerParams(dimension_semantics=sem, vmem_limit_bytes=VMEM_LIMIT)


def _rstd(x):
    return lax.rsqrt(jnp.mean(x * x, axis=-1, keepdims=True) + EPS)


def _norm_bwd(dn, n, rstd):
    return rstd * (dn - n * jnp.mean(dn * n, axis=-1, keepdims=True))


_DOT_DIMS = {
    "nn": (((1,), (0,)), ((), ())),
    "nt": (((1,), (1,)), ((), ())),
    "tn": (((0,), (0,)), ((), ())),
}


def _dot(a, b, mode):
    return lax.dot_general(a.astype(MM_DTYPE), b.astype(MM_DTYPE), _DOT_DIMS[mode],
                           preferred_element_type=jnp.float32)


def _matmul(name, *, grid, mode, pairs, pair_specs, acc_shapes, extras=(), extra_specs=(),
            out_shapes, out_specs, epilogue, deps=(), epi_rows=0, sum_outs=()):
    nk = grid[2]
    operands, operand_specs, where, counts = [], [], {}, []
    pair_index = []
    for ps, ss in zip(pairs, pair_specs):
        counts.append(len(ps))
        for arrays, specs in zip(ps, ss):
            for arr, spec in zip(arrays, specs):
                key = (id(arr), id(spec))
                if key not in where:
                    where[key] = len(operands)
                    operands.append(arr)
                    operand_specs.append(spec)
                pair_index.append(where[key])
    n_operands = len(operands)
    n_extra = len(extras)
    n_out = len(out_shapes)
    n_in = n_operands + n_extra + len(deps)
    in_streams = [q for q, s in enumerate(extra_specs) if s is STREAM]
    out_streams = [q for q, s in enumerate(out_specs) if s is STREAM]
    stream_bufs = ([pltpu.VMEM((2, STREAM_ROWS, extras[q].shape[1]), extras[q].dtype) for q in in_streams]
                   + [pltpu.VMEM((2, STREAM_ROWS, out_shapes[q].shape[1]), out_shapes[q].dtype) for q in out_streams])
    n_streams = len(stream_bufs)
    n_acc = 0 if nk == 1 else len(acc_shapes)

    def body(*refs):
        pair_refs = [refs[q] for q in pair_index]
        extra_refs = refs[n_operands:n_operands + n_extra]
        out_refs = refs[n_in:n_in + n_out]
        acc_refs = refs[n_in + n_out:n_in + n_out + n_acc]
        bufs = refs[n_in + n_out + n_acc:n_in + n_out + n_acc + n_streams]
        i, k = pl.program_id(0), pl.program_id(2)

        def stream_copies():
            sems = refs[-1]
            n_rows = acc_refs[0].shape[0]

            def hbm_rows(ref, c):
                return ref.at[pl.ds(pl.multiple_of(i * n_rows + c * STREAM_ROWS, STREAM_ROWS), STREAM_ROWS)]

            def fetch(s, c):
                return pltpu.make_async_copy(hbm_rows(extra_refs[in_streams[s]], c), bufs[s].at[c % 2],
                                             sems.at[s, c % 2])

            def drain(s, c):
                return pltpu.make_async_copy(bufs[len(in_streams) + s].at[c % 2], hbm_rows(out_refs[out_streams[s]], c),
                                             sems.at[len(in_streams) + s, c % 2])

            return fetch, drain

        def streamed_finish(accs):
            fetch, drain = stream_copies()
            n_chunks = accs[0].shape[0] // STREAM_ROWS
            for c in range(n_chunks):
                for s in range(len(in_streams)):
                    if c + 1 < n_chunks:
                        fetch(s, c + 1).start()
                    fetch(s, c).wait()
                for s in range(len(out_streams)):
                    if c >= 2:
                        drain(s, c - 2).wait()
                for r0 in range(0, STREAM_ROWS, epi_rows):
                    sub = slice(r0, r0 + epi_rows)
                    rows = slice(c * STREAM_ROWS + r0, c * STREAM_ROWS + r0 + epi_rows)
                    views = [bufs[in_streams.index(q)].at[c % 2, sub] if q in in_streams else e
                             for q, e in enumerate(extra_refs)]
                    outs = epilogue([a[rows, :] for a in accs], views)
                    for q, (o_ref, o) in enumerate(zip(out_refs, outs)):
                        if q in out_streams:
                            bufs[len(in_streams) + out_streams.index(q)][c % 2, sub, :] = o.astype(o_ref.dtype)
                        elif c == 0 and r0 == 0:
                            _accumulate(o_ref, o, i)
                        else:
                            o_ref[...] += o
                for s in range(len(out_streams)):
                    drain(s, c).start()
            for s in range(len(out_streams)):
                for c in range(max(n_chunks - 2, 0), n_chunks):
                    drain(s, c).wait()

        def partial_sums():
            res, p = [], 0
            for cnt in counts:
                tot = None
                for _ in range(cnt):
                    d = _dot(pair_refs[p][...], pair_refs[p + 1][...], mode)
                    tot = d if tot is None else tot + d
                    p += 2
                res.append(tot)
            return res

        def finish(accs):
            n_rows = accs[0].shape[0]
            step = epi_rows or n_rows
            for r0 in range(0, n_rows, step):
                rows = slice(r0, r0 + step)
                outs = epilogue([a[rows, :] for a in accs], [e.at[rows] if e.shape[0] == n_rows else e
                                                             for e in extra_refs])
                for q, (o_ref, o) in enumerate(zip(out_refs, outs)):
                    if q not in sum_outs:
                        o_ref[rows, :] = o.astype(o_ref.dtype)
                    elif r0 == 0:
                        _accumulate(o_ref, o, i)
                    else:
                        o_ref[...] += o

        if nk == 1:
            finish(partial_sums())
        else:
            if n_streams:
                @pl.when(k == nk - 1)
                def _():
                    for s in range(len(in_streams)):
                        stream_copies()[0](s, 0).start()

            @pl.when(k == 0)
            def _():
                for acc_ref, s in zip(acc_refs, partial_sums()):
                    acc_ref[...] = s

            @pl.when(k > 0)
            def _():
                for acc_ref, s in zip(acc_refs, partial_sums()):
                    acc_ref[...] += s

            @pl.when(k == nk - 1)
            def _():
                (streamed_finish if n_streams else finish)(acc_refs)

    scratch = [] if nk == 1 else [pltpu.VMEM(s, jnp.float32) for s in acc_shapes]
    if n_streams:
        assert nk > 1 and epi_rows and not any(q in sum_outs for q in out_streams)
        scratch = scratch + stream_bufs + [pltpu.SemaphoreType.DMA((n_streams, 2))]
    return pl.pallas_call(
        body, name=name, grid=grid,
        in_specs=operand_specs + [ANY if s is STREAM else s for s in extra_specs] + [ANY] * len(deps),
        out_specs=[ANY if s is STREAM else s for s in out_specs],
        out_shape=list(out_shapes), scratch_shapes=scratch,
        compiler_params=_params(("arbitrary", "arbitrary", "arbitrary")),
    )(*operands, *extras, *deps)


def _identity_epilogue(accs, extra_refs):
    return tuple(accs)


def _row_spec(tr, n):
    return pl.BlockSpec((tr, n), lambda i: (i, 0))


def _const_spec(shape):
    return pl.BlockSpec(shape, lambda i: tuple(0 for _ in shape))


def _accumulate(ref, val, i):
    @pl.when(i == 0)
    def _():
        ref[...] = val

    @pl.when(i > 0)
    def _():
        ref[...] += val


def _pre_norm(x, gain, deps=()):
    t, d = x.shape
    tr = _tile(t, 512)

    def body(x_ref, g_ref, *rest):
        h_ref = rest[-1]
        xv = x_ref[...]
        h_ref[...] = (xv * _rstd(xv) * g_ref[...]).astype(h_ref.dtype)

    return pl.pallas_call(
        body, name="pre_norm", grid=(t // tr,),
        in_specs=[_row_spec(tr, d), _const_spec((1, d))] + [ANY] * len(deps), out_specs=_row_spec(tr, d),
        out_shape=jax.ShapeDtypeStruct((t, d), MM_DTYPE), compiler_params=_params(("arbitrary",)),
    )(x, gain, *deps)


def _pool_matrix(pool_ref, g):
    return jnp.concatenate([pool_ref[c, g] for c in range(N_CHIPS)], axis=0)


def _inv_count(row0, n, w):
    pos = (row0 + lax.broadcasted_iota(jnp.int32, (n, 1), 0) + 1).astype(jnp.float32)
    return 1.0 / jnp.minimum(pos, float(w))


def _conv_piece(cu_buf, start, n, b_piece, convw_ref):
    conv = None
    for k in range(CONV_K):
        term = convw_ref[k:k + 1, :] * cu_buf[pl.ds(HALO + start + k - (CONV_K - 1), n), :]
        conv = term if conv is None else conv + term
    return conv, b_piece * conv


def _head_stats(a, width):
    return [_rstd(a[:, h * width:(h + 1) * width]) for h in range(a.shape[1] // width)]


def _pooled_piece(v_buf, start, n, v_piece, row0, dg):
    outs = []
    for gi, w in enumerate(POOL_WINDOWS):
        cols = slice(gi * dg, (gi + 1) * dg)
        win = None
        for k in range(w):
            term = v_buf[pl.ds(HALO + start - k, n), cols]
            win = term if win is None else win + term
        outs.append(win * _inv_count(row0 + start, n, w) - v_piece[:, cols])
    return outs


def _halo_specs(t, tr, width, col):
    per = tr // HALO
    last = t // HALO - 1
    prev = pl.BlockSpec((HALO, width), lambda i: (jnp.maximum(i * per - 1, 0), col))
    nxt = pl.BlockSpec((HALO, width), lambda i: (jnp.minimum((i + 1) * per, last), col))
    return prev, nxt


def _mixers_fwd(proj, conv_w, pool_g, pool_scale):
    t, e = proj.shape
    cw = e // 4
    dg = pool_g.shape[-1]
    tr = _tile(t, 256)

    def main(col):
        return pl.BlockSpec((tr, cw), lambda i: (i, col))

    def body(b_ref, c_ref, u_ref, v_ref, cp_ref, up_ref, vp_ref, convw_ref, pool_ref, scale_ref,
             out_ref, cu_buf, v_buf):
        i = pl.program_id(0)
        keep = (i > 0).astype(jnp.float32)
        cu_buf[pl.ds(0, HALO), :] = cp_ref[...] * up_ref[...] * keep
        cu_buf[pl.ds(HALO, tr), :] = c_ref[...] * u_ref[...]
        v_buf[pl.ds(0, HALO), :] = vp_ref[...] * keep
        v_buf[pl.ds(HALO, tr), :] = v_ref[...]
        _, a = _conv_piece(cu_buf, 0, tr, b_ref[...], convw_ref)
        for h, rstd in enumerate(_head_stats(a, CONV_HEAD_DIM)):
            cols = slice(h * CONV_HEAD_DIM, (h + 1) * CONV_HEAD_DIM)
            out_ref[:, cols] = (a[:, cols] * rstd).astype(out_ref.dtype)
        pooled = _pooled_piece(v_buf, 0, tr, v_ref[...], i * tr, dg)
        for gi, p in enumerate(pooled):
            z = _dot(p, _pool_matrix(pool_ref, gi), "nn")
            cols = slice(gi * dg, (gi + 1) * dg)
            out_ref[:, cw + gi * dg:cw + (gi + 1) * dg] = (z * _rstd(z) * scale_ref[:, cols]).astype(out_ref.dtype)

    prev_c, _ = _halo_specs(t, tr, cw, 1)
    prev_u, _ = _halo_specs(t, tr, cw, 2)
    prev_v, _ = _halo_specs(t, tr, cw, 3)
    return pl.pallas_call(
        body, name="mixers_fwd", grid=(t // tr,),
        in_specs=[main(0), main(1), main(2), main(3), prev_c, prev_u, prev_v,
                  _const_spec(conv_w.shape), _const_spec(pool_g.shape), _const_spec(pool_scale.shape)],
        out_specs=_row_spec(tr, 2 * cw),
        out_shape=jax.ShapeDtypeStruct((t, 2 * cw), MM_DTYPE),
        scratch_shapes=[pltpu.VMEM((tr + HALO, cw), jnp.float32), pltpu.VMEM((tr + HALO, cw), jnp.float32)],
        compiler_params=_params(("arbitrary",)),
    )(proj, proj, proj, proj, proj, proj, proj, conv_w, pool_g, pool_scale)


def _mixers_bwd(proj, dmixed, conv_w, pool_g, pool_scale, deps=()):
    t, e = proj.shape
    cw = e // 4
    dg = pool_g.shape[-1]
    n_groups = len(POOL_WINDOWS)
    tr = _tile(t, 256)
    n_tiles = t // tr
    ext = tr + 2 * HALO

    def main(col):
        return pl.BlockSpec((tr, cw), lambda i: (i, col))

    def body(b_ref, c_ref, u_ref, v_ref, dyc_ref, dyp_ref,
             cp_ref, up_ref, vp_ref,
             bn_ref, cn_ref, un_ref, vn_ref, dycn_ref, dypn_ref,
             convw_ref, pool_ref, scale_ref, *rest):
        (dproj_ref, dconvw_ref, dpool_ref, dscale_ref,
         cu_buf, v_buf, dconv_buf, dpn_buf, dpooled_buf) = rest[len(deps):]
        i = pl.program_id(0)
        keep_prev = (i > 0).astype(jnp.float32)
        keep_next = (i < n_tiles - 1).astype(jnp.float32)
        cu_buf[pl.ds(0, HALO), :] = cp_ref[...] * up_ref[...] * keep_prev
        cu_buf[pl.ds(HALO, tr), :] = c_ref[...] * u_ref[...]
        cu_buf[pl.ds(HALO + tr, HALO), :] = cn_ref[...] * un_ref[...]
        v_buf[pl.ds(0, HALO), :] = vp_ref[...] * keep_prev
        v_buf[pl.ds(HALO, tr), :] = v_ref[...]
        v_buf[pl.ds(HALO + tr, HALO), :] = vn_ref[...]

        def conv_piece(start, n, b_piece, dyc_piece, keep, is_main):
            conv, a = _conv_piece(cu_buf, start, n, b_piece, convw_ref)
            for h, rstd in enumerate(_head_stats(a, CONV_HEAD_DIM)):
                cols = slice(h * CONV_HEAD_DIM, (h + 1) * CONV_HEAD_DIM)
                da = _norm_bwd(dyc_piece[:, cols], a[:, cols] * rstd, rstd)
                dconv_buf[pl.ds(start, n), cols] = da * b_piece[:, cols] * keep
                if is_main:
                    dproj_ref[:, cols] = (da * conv[:, cols]).astype(dproj_ref.dtype)

        conv_piece(0, tr, b_ref[...], dyc_ref[...], 1.0, True)
        conv_piece(tr, HALO, bn_ref[...], dycn_ref[...], keep_next, False)

        dconv_main = dconv_buf[pl.ds(0, tr), :]
        dcu = None
        dw_rows = []
        for k in range(CONV_K):
            shift = CONV_K - 1 - k
            term = convw_ref[k:k + 1, :] * dconv_buf[pl.ds(shift, tr), :]
            dcu = term if dcu is None else dcu + term
            dw_rows.append(jnp.sum(dconv_main * cu_buf[pl.ds(HALO - shift, tr), :], axis=0, keepdims=True))
        dproj_ref[:, cw:2 * cw] = (dcu * u_ref[...]).astype(dproj_ref.dtype)
        dproj_ref[:, 2 * cw:3 * cw] = (dcu * c_ref[...]).astype(dproj_ref.dtype)
        _accumulate(dconvw_ref, jnp.concatenate(dw_rows, axis=0), i)

        def pool_piece(start, n, v_piece, dyp_piece, keep, is_main):
            pooled = _pooled_piece(v_buf, start, n, v_piece, i * tr, dg)
            dscale, dmats = [], []
            for gi, w in enumerate(POOL_WINDOWS):
                cols = slice(gi * dg, (gi + 1) * dg)
                mat = _pool_matrix(pool_ref, gi)
                z = _dot(pooled[gi], mat, "nn")
                rstd = _rstd(z)
                nz = z * rstd
                dyp_g = dyp_piece[:, cols]
                dz = _norm_bwd(dyp_g * scale_ref[:, cols], nz, rstd)
                dpooled = _dot(dz, mat, "nt") * keep
                dpn_buf[pl.ds(start, n), cols] = dpooled * _inv_count(i * tr + start, n, w)
                if is_main:
                    dpooled_buf[:, cols] = dpooled
                    dscale.append(jnp.sum(dyp_g * nz, axis=0, keepdims=True))
                    dmats.append(_dot(pooled[gi], dz, "tn"))
            return dscale, dmats

        dscale, dmats = pool_piece(0, tr, v_ref[...], dyp_ref[...], 1.0, True)
        pool_piece(tr, HALO, vn_ref[...], dypn_ref[...], keep_next, False)
        for gi, w in enumerate(POOL_WINDOWS):
            cols = slice(gi * dg, (gi + 1) * dg)
            back = None
            for k in range(w):
                term = dpn_buf[pl.ds(k, tr), cols]
                back = term if back is None else back + term
            dproj_ref[:, 3 * cw + gi * dg:3 * cw + (gi + 1) * dg] = (back - dpooled_buf[:, cols]).astype(dproj_ref.dtype)
        _accumulate(dscale_ref, jnp.concatenate(dscale, axis=1), i)
        rows = dg // N_CHIPS
        for gi in range(n_groups):
            for c in range(N_CHIPS):
                _accumulate(dpool_ref.at[c, gi], dmats[gi][c * rows:(c + 1) * rows, :], i)

    prev_c, next_c = _halo_specs(t, tr, cw, 1)
    prev_u, next_u = _halo_specs(t, tr, cw, 2)
    prev_v, next_v = _halo_specs(t, tr, cw, 3)
    _, next_b = _halo_specs(t, tr, cw, 0)
    _, next_dyc = _halo_specs(t, tr, cw, 0)
    _, next_dyp = _halo_specs(t, tr, cw, 1)
    return pl.pallas_call(
        body, name="mixers_bwd", grid=(n_tiles,),
        in_specs=[main(0), main(1), main(2), main(3), main(0), main(1),
                  prev_c, prev_u, prev_v,
                  next_b, next_c, next_u, next_v, next_dyc, next_dyp,
                  _const_spec(conv_w.shape), _const_spec(pool_g.shape), _const_spec(pool_scale.shape)]
        + [ANY] * len(deps),
        out_specs=[_row_spec(tr, e), _const_spec(conv_w.shape), _const_spec(pool_g.shape),
                   _const_spec(pool_scale.shape)],
        out_shape=[jax.ShapeDtypeStruct((t, e), MM_DTYPE), jax.ShapeDtypeStruct(conv_w.shape, jnp.float32),
                   jax.ShapeDtypeStruct(pool_g.shape, jnp.float32),
                   jax.ShapeDtypeStruct(pool_scale.shape, jnp.float32)],
        scratch_shapes=[pltpu.VMEM((ext, cw), jnp.float32), pltpu.VMEM((ext, cw), jnp.float32),
                        pltpu.VMEM((tr + HALO, cw), jnp.float32), pltpu.VMEM((tr + HALO, cw), jnp.float32),
                        pltpu.VMEM((tr, cw), jnp.float32)],
        compiler_params=_params(("arbitrary",)),
    )(proj, proj, proj, proj, dmixed, dmixed,
      proj, proj, proj,
      proj, proj, proj, proj, dmixed, dmixed,
      conv_w, pool_g, pool_scale, *deps)


def _cast_rows(name, w, dtype):
    r, c = w.shape
    tr = _tile(r, 512)

    def body(w_ref, o_ref):
        o_ref[...] = w_ref[...].astype(o_ref.dtype)

    return pl.pallas_call(
        body, name=name, grid=(r // tr,), in_specs=[_row_spec(tr, c)], out_specs=_row_spec(tr, c),
        out_shape=jax.ShapeDtypeStruct((r, c), dtype), compiler_params=_params(("arbitrary",)),
    )(w)


def _cast_to_slot(name, w, chip, by_columns=False):
    r, c = w.shape
    tr = _tile(r, 512)
    if by_columns:
        out_spec = pl.BlockSpec((tr, c), lambda i, chip_ref: (i, chip_ref[0]))
        out_shape = jax.ShapeDtypeStruct((r, N_CHIPS * c), MM_DTYPE)
    else:
        out_spec = pl.BlockSpec((None, tr, c), lambda i, chip_ref: (chip_ref[0], i, 0))
        out_shape = jax.ShapeDtypeStruct((N_CHIPS, r, c), MM_DTYPE)

    def body(chip_ref, w_ref, o_ref):
        o_ref[...] = w_ref[...].astype(o_ref.dtype)

    return pl.pallas_call(
        body, name=name,
        grid_spec=pltpu.PrefetchScalarGridSpec(
            num_scalar_prefetch=1, grid=(r // tr,),
            in_specs=[pl.BlockSpec((tr, c), lambda i, chip_ref: (i, 0))], out_specs=out_spec),
        out_shape=out_shape, compiler_params=_params(("arbitrary",)),
    )(chip, w)


def _sum_own_and_received(name, own, land, chip):
    _, r, c = land.shape
    tr = _tile(r, 256)

    def body(chip_ref, own_ref, a_ref, b_ref, c_ref, o_ref):
        tot = own_ref[...].astype(jnp.float32) + a_ref[...].astype(jnp.float32)
        tot = tot + b_ref[...].astype(jnp.float32)
        o_ref[...] = (tot + c_ref[...].astype(jnp.float32)).astype(o_ref.dtype)

    def slot(k):
        return pl.BlockSpec((None, tr, c), lambda i, chip_ref: ((chip_ref[0] + k) % N_CHIPS, i, 0))

    own_spec = slot(0) if len(own.shape) == 3 else pl.BlockSpec((tr, c), lambda i, chip_ref: (i, chip_ref[0]))
    return pl.pallas_call(
        body, name=name,
        grid_spec=pltpu.PrefetchScalarGridSpec(
            num_scalar_prefetch=1, grid=(r // tr,), in_specs=[own_spec, slot(1), slot(2), slot(3)],
            out_specs=pl.BlockSpec((tr, c), lambda i, chip_ref: (i, 0))),
        out_shape=jax.ShapeDtypeStruct((r, c), COMM_DTYPE), compiler_params=_params(("arbitrary",)),
    )(chip, own, land, land, land)


def _sum_slots(name, slots):
    n, r, c = slots.shape
    tr = _tile(r, 256)

    def body(s_ref, o_ref):
        tot = s_ref[0].astype(jnp.float32)
        for s in range(1, n):
            tot = tot + s_ref[s].astype(jnp.float32)
        o_ref[...] = tot

    return pl.pallas_call(
        body, name=name, grid=(r // tr,),
        in_specs=[pl.BlockSpec((n, tr, c), lambda i: (0, i, 0))], out_specs=_row_spec(tr, c),
        out_shape=jax.ShapeDtypeStruct((r, c), jnp.float32), compiler_params=_params(("arbitrary",)),
    )(slots)


def _adamw_math(w, g, m, v):
    m = ADAM_B1 * m + (1.0 - ADAM_B1) * g
    v = ADAM_B2 * v + (1.0 - ADAM_B2) * (g * g)
    m_hat = m / (1.0 - ADAM_B1 ** ADAM_STEP)
    v_hat = v / (1.0 - ADAM_B2 ** ADAM_STEP)
    delta = -ADAM_LR * (m_hat / (jnp.sqrt(v_hat) + ADAM_EPS) + ADAM_WD * w)
    return delta, m, v


def _adamw(name, w, m, v, grad_parts):
    r, c = w.shape
    tr = _tile(r, 256)
    n_parts = len(grad_parts)

    def body(*refs):
        w_ref, m_ref, v_ref = refs[:3]
        part_refs = refs[3:3 + n_parts]
        g_ref, d_ref, nm_ref, nv_ref = refs[3 + n_parts:]
        g = part_refs[0][...].astype(jnp.float32)
        for p in part_refs[1:]:
            g = g + p[...].astype(jnp.float32)
        delta, nm, nv = _adamw_math(w_ref[...], g, m_ref[...], v_ref[...])
        g_ref[...] = g
        d_ref[...] = delta
        nm_ref[...] = nm
        nv_ref[...] = nv

    spec = _row_spec(tr, c)
    out = jax.ShapeDtypeStruct((r, c), jnp.float32)
    return pl.pallas_call(
        body, name=name, grid=(r // tr,), in_specs=[spec] * (3 + n_parts), out_specs=[spec] * 4,
        out_shape=[out] * 4, compiler_params=_params(("arbitrary",)),
    )(w, m, v, *grad_parts)


def _chip_peers():
    x, y, c = lax.axis_index("x"), lax.axis_index("y"), lax.axis_index("c")
    return x, y, c, [(1 - x, y), (x, 1 - y), (1 - x, 1 - y)]


HBM = pl.BlockSpec(memory_space=pltpu.HBM)
SEM = pl.BlockSpec(memory_space=pltpu.SEMAPHORE)
TOKEN = jax.ShapeDtypeStruct((8, 128), jnp.float32)
N_PEER_CHIPS = N_CHIPS - 1


def _in_flight():
    return pltpu.CompilerParams(has_side_effects=pltpu.SideEffectType.DATAFLOW_SIDE_EFFECTING)


def _slot(ref, slot):
    if len(ref.shape) == 3:
        return ref.at[slot]
    width = ref.shape[1] // N_CHIPS
    return ref.at[:, pl.ds(pl.multiple_of(slot * width, 128), width)]


def _half_slot(ref, slot, half):
    rows = ref.shape[-2] // 2
    if len(ref.shape) == 3:
        return ref.at[slot, pl.ds(half * rows, rows)]
    width = ref.shape[1] // N_CHIPS
    return ref.at[pl.ds(half * rows, rows), pl.ds(pl.multiple_of(slot * width, 128), width)]


def _slot_shape(shape):
    return shape[1:] if len(shape) == 3 else (shape[0], shape[1] // N_CHIPS)


def _gather_start(name, full, deps=()):
    def body(full_ref, *rest):
        send_sems, recv_sems, _, token_ref = rest[len(deps):]
        x, y, c, peers = _chip_peers()
        mine = _half_slot(full_ref, 2 * x + y, c)
        for p, (px, py) in enumerate(peers):
            pltpu.make_async_remote_copy(
                src_ref=mine, dst_ref=mine, send_sem=send_sems.at[p], recv_sem=recv_sems.at[p],
                device_id=(px, py, c), device_id_type=MESH).start()
        token_ref[...] = jnp.zeros_like(token_ref)

    return pl.pallas_call(
        body, name=name,
        out_shape=(pltpu.SemaphoreType.DMA((N_PEER_CHIPS,)), pltpu.SemaphoreType.DMA((N_PEER_CHIPS,)),
                   pltpu.HBM(full.shape, full.dtype), TOKEN),
        in_specs=[HBM] + [ANY] * len(deps), out_specs=(SEM, SEM, HBM, pl.BlockSpec(memory_space=pltpu.VMEM)),
        input_output_aliases={0: 2}, compiler_params=_in_flight(),
    )(pltpu.with_memory_space_constraint(full, pltpu.HBM), *deps)


def _gather_wait(name, send_sems, recv_sems, full, after):
    def body(full_ref, send_sems, recv_sems, after_ref, out_ref):
        x, y, c, peers = _chip_peers()
        for p, (px, py) in enumerate(peers):
            cp = pltpu.make_async_remote_copy(
                src_ref=_half_slot(full_ref, 2 * x + y, c), dst_ref=_half_slot(full_ref, 2 * px + py, c),
                send_sem=send_sems.at[p], recv_sem=recv_sems.at[p], device_id=(px, py, c), device_id_type=MESH)
            cp.wait_send()
            cp.wait_recv()

    return pl.pallas_call(
        body, name=name, out_shape=pltpu.HBM(full.shape, full.dtype),
        in_specs=(HBM, SEM, SEM, ANY), out_specs=HBM, input_output_aliases={0: 0}, compiler_params=_in_flight(),
    )(full, send_sems, recv_sems, after)


def _forward_sibling(name, fulls):
    n = len(fulls)

    def body(*refs):
        in_refs, out_refs = refs[:n], refs[n:2 * n]
        send_sems, recv_sems = refs[2 * n:]
        x, y, c, peers = _chip_peers()
        sends, recvs = [], []
        for a in range(n):
            for p, (px, py) in enumerate(peers):
                k = N_PEER_CHIPS * a + p
                slot = 2 * px + py
                cp = pltpu.make_async_remote_copy(
                    src_ref=_half_slot(in_refs[a], slot, c), dst_ref=_half_slot(out_refs[a], slot, c),
                    send_sem=send_sems.at[k], recv_sem=recv_sems.at[k], device_id=(x, y, 1 - c), device_id_type=MESH)
                cp.start()
                sends.append(cp)
                recvs.append(pltpu.make_async_remote_copy(
                    src_ref=_half_slot(in_refs[a], slot, c), dst_ref=_half_slot(out_refs[a], slot, 1 - c),
                    send_sem=send_sems.at[k], recv_sem=recv_sems.at[k], device_id=(x, y, 1 - c), device_id_type=MESH))
        for cp in recvs:
            cp.wait_recv()
        for cp in sends:
            cp.wait_send()

    return pl.pallas_call(
        body, name=name, in_specs=[ANY] * n, out_specs=[ANY] * n,
        out_shape=[jax.ShapeDtypeStruct(f.shape, f.dtype) for f in fulls],
        input_output_aliases={a: a for a in range(n)},
        scratch_shapes=[pltpu.SemaphoreType.DMA((N_PEER_CHIPS * n,)), pltpu.SemaphoreType.DMA((N_PEER_CHIPS * n,))],
    )(*fulls)


def _scatter_start(name, grads, deps=()):
    def body(g_ref, land_ref, *rest):
        send_sems, recv_sems, _, _, token_ref = rest[len(deps):]
        x, y, c, peers = _chip_peers()
        me = 2 * x + y
        for p, (px, py) in enumerate(peers):
            pltpu.make_async_remote_copy(
                src_ref=_slot(g_ref, 2 * px + py), dst_ref=land_ref.at[me], send_sem=send_sems.at[p],
                recv_sem=recv_sems.at[p], device_id=(px, py, c), device_id_type=MESH).start()
        token_ref[...] = jnp.zeros_like(token_ref)

    land = lax.empty((N_CHIPS,) + _slot_shape(grads.shape), grads.dtype)
    return pl.pallas_call(
        body, name=name,
        out_shape=(pltpu.SemaphoreType.DMA((N_PEER_CHIPS,)), pltpu.SemaphoreType.DMA((N_PEER_CHIPS,)),
                   pltpu.HBM(grads.shape, grads.dtype), pltpu.HBM(land.shape, land.dtype), TOKEN),
        in_specs=[HBM, HBM] + [ANY] * len(deps),
        out_specs=(SEM, SEM, HBM, HBM, pl.BlockSpec(memory_space=pltpu.VMEM)),
        input_output_aliases={0: 2, 1: 3}, compiler_params=_in_flight(),
    )(pltpu.with_memory_space_constraint(grads, pltpu.HBM), pltpu.with_memory_space_constraint(land, pltpu.HBM), *deps)


def _scatter_wait(name, send_sems, recv_sems, grads, land, after):
    def body(g_ref, land_ref, send_sems, recv_sems, after_ref, g_out, land_out):
        x, y, c, peers = _chip_peers()
        for p, (px, py) in enumerate(peers):
            cp = pltpu.make_async_remote_copy(
                src_ref=_slot(g_ref, 2 * px + py), dst_ref=land_ref.at[2 * px + py], send_sem=send_sems.at[p],
                recv_sem=recv_sems.at[p], device_id=(px, py, c), device_id_type=MESH)
            cp.wait_send()
            cp.wait_recv()

    return pl.pallas_call(
        body, name=name,
        out_shape=(pltpu.HBM(grads.shape, grads.dtype), pltpu.HBM(land.shape, land.dtype)),
        in_specs=(HBM, HBM, SEM, SEM, ANY), out_specs=(HBM, HBM), input_output_aliases={0: 0, 1: 1},
        compiler_params=_in_flight(),
    )(grads, land, send_sems, recv_sems, after)


def _swap_sibling(name, parts):
    n = len(parts)

    def body(*refs):
        in_refs, out_refs = refs[:n], refs[n:2 * n]
        send_sems, recv_sems = refs[2 * n:]
        x, y, c = lax.axis_index("x"), lax.axis_index("y"), lax.axis_index("c")
        copies = []
        for a in range(n):
            cp = pltpu.make_async_remote_copy(
                src_ref=in_refs[a], dst_ref=out_refs[a], send_sem=send_sems.at[a], recv_sem=recv_sems.at[a],
                device_id=(x, y, 1 - c), device_id_type=MESH)
            cp.start()
            copies.append(cp)
        for cp in copies:
            cp.wait_recv()
        for cp in copies:
            cp.wait_send()

    return pl.pallas_call(
        body, name=name, in_specs=[ANY] * n, out_specs=[ANY] * n,
        out_shape=[jax.ShapeDtypeStruct(p.shape, p.dtype) for p in parts],
        scratch_shapes=[pltpu.SemaphoreType.DMA((n,)), pltpu.SemaphoreType.DMA((n,))],
    )(*parts)


def _gather_devices(name, block):
    def body(in_ref, out_ref, send_sems, recv_sems, local_sem):
        x, y, c = lax.axis_index("x"), lax.axis_index("y"), lax.axis_index("c")
        me = 4 * x + 2 * y + c
        local = pltpu.make_async_copy(in_ref, out_ref.at[me], local_sem)
        local.start()
        sends, recvs = [], []
        k = 0
        for fx in range(2):
            for fy in range(2):
                for fc in range(2):
                    if fx == fy == fc == 0:
                        continue
                    px = x if fx == 0 else 1 - x
                    py = y if fy == 0 else 1 - y
                    pc = c if fc == 0 else 1 - c
                    cp = pltpu.make_async_remote_copy(
                        src_ref=in_ref, dst_ref=out_ref.at[me], send_sem=send_sems.at[k], recv_sem=recv_sems.at[k],
                        device_id=(px, py, pc), device_id_type=MESH)
                    cp.start()
                    sends.append(cp)
                    recvs.append(pltpu.make_async_remote_copy(
                        src_ref=in_ref, dst_ref=out_ref.at[4 * px + 2 * py + pc], send_sem=send_sems.at[k],
                        recv_sem=recv_sems.at[k], device_id=(px, py, pc), device_id_type=MESH))
                    k += 1
        for cp in recvs:
            cp.wait_recv()
        for cp in sends:
            cp.wait_send()
        local.wait()

    return pl.pallas_call(
        body, name=name, in_specs=[ANY], out_specs=ANY,
        out_shape=jax.ShapeDtypeStruct((N_DEV,) + block.shape, block.dtype),
        scratch_shapes=[pltpu.SemaphoreType.DMA((N_DEV - 1,)), pltpu.SemaphoreType.DMA((N_DEV - 1,)),
                        pltpu.SemaphoreType.DMA],
    )(block)


def _pack_rows(pieces, width):
    flat = jnp.concatenate([p.reshape(-1) for p in pieces])
    rows = -(-flat.shape[0] // width)
    rows = -(-rows // 8) * 8
    flat = jnp.pad(flat, (0, rows * width - flat.shape[0]))
    return flat.reshape(rows, width)


def _unpack_rows(packed, shapes):
    flat = packed.reshape(-1)
    out, off = [], 0
    for s in shapes:
        size = 1
        for d in s:
            size *= d
        out.append(flat[off:off + size].reshape(s))
        off += size
    return out


def kernel(x, ln_mix_pre, w_in, conv_w, pool_w, pool_scale, w_out, ln_mix_post, ln_ffn_pre, w_gate, w_up, w_down, ln_ffn_post, loss_target, m_ln_mix_pre, m_w_in, m_conv_w, m_pool_w, m_pool_scale, m_w_out, m_ln_mix_post, m_ln_ffn_pre, m_w_gate, m_w_up, m_w_down, m_ln_ffn_post, v_ln_mix_pre, v_w_in, v_conv_w, v_pool_w, v_pool_scale, v_w_out, v_ln_mix_post, v_ln_ffn_pre, v_w_gate, v_w_up, v_w_down, v_ln_ffn_post):
    t, d = x.shape[1], x.shape[2]
    e4 = w_in.shape[2]
    e = N_CHIPS * e4
    f4 = w_gate.shape[2]
    f = N_CHIPS * f4
    n_groups, dg4, dg = pool_w.shape[1], pool_w.shape[2], pool_w.shape[3]
    cw4 = conv_w.shape[2]
    chip = 2 * lax.axis_index("x") + lax.axis_index("y")
    xs, tgt = x[0], loss_target[0]

    big = {"w_in": w_in[0], "w_out": w_out[0], "w_gate": w_gate[0], "w_up": w_up[0], "w_down": w_down[0],
           "pool_w": pool_w[0].reshape(n_groups * dg4, dg)}
    names = ["w_in", "pool_w", "w_out", "w_gate", "w_up", "w_down"]
    chip_arr = chip.astype(jnp.int32).reshape(1)
    conv_all = _gather_devices("gather_conv_w", _pack_rows([conv_w[0]], 128))
    conv_full = jnp.concatenate(
        [conv_all[2 * j].reshape(-1)[:CONV_K * cw4].reshape(CONV_K, cw4) for j in range(N_CHIPS)], axis=1)
    in_flight, deps = {}, (conv_all,)
    for k in names:
        by_columns = k in ("w_gate", "w_up")
        in_flight[k] = _gather_start(
            "gather_start_" + k, _cast_to_slot("cast_" + k, big[k], chip_arr, by_columns), deps)
        deps = (in_flight[k][3],)

    def landed(ks, after):
        fulls = []
        for k in ks:
            send_sems, recv_sems, full, _ = in_flight[k]
            fulls.append(_gather_wait("gather_wait_" + k, send_sems, recv_sems, full, after))
            after = fulls[-1]
        return _forward_sibling("forward_" + ks[0], fulls)

    def rows3(tile, width):
        return pl.BlockSpec((tile, width), lambda i, j, k: (i, 0))

    gain3 = pl.BlockSpec((1, d), lambda i, j, k: (0, 0))
    f32_td = jax.ShapeDtypeStruct((t, d), jnp.float32)
    mm_td = jax.ShapeDtypeStruct((t, d), MM_DTYPE)
    f32_gain = jax.ShapeDtypeStruct((1, d), jnp.float32)

    h = _pre_norm(xs, ln_mix_pre, deps)
    (win_g,) = landed(["w_in"], h)
    tm = _tile(t, 1024)
    tm2 = _tile(t, 2048)
    proj = _matmul(
        "in_proj", grid=(t // tm2, N_CHIPS, 1), mode="nn",
        pairs=[[(h, win_g)]],
        pair_specs=[[(rows3(tm2, d), pl.BlockSpec((None, d, e4), lambda i, j, k: (j, 0, 0)))]],
        acc_shapes=[(tm2, e4)], out_shapes=[jax.ShapeDtypeStruct((t, e), jnp.float32)],
        out_specs=[pl.BlockSpec((tm2, e4), lambda i, j, k: (i, j))], epilogue=_identity_epilogue)[0]
    pool_g, wout_full = landed(["pool_w", "w_out"], proj)
    pool_g, wout_full = pool_g.reshape(N_CHIPS, n_groups, dg4, dg), wout_full.reshape(d, d)
    mixed = _mixers_fwd(proj, conv_full, pool_g, pool_scale)

    def post_mix_epilogue(accs, extra_refs):
        x_ref, g2_ref, g3_ref = extra_refs
        mo = accs[0]
        x1 = x_ref[...] + mo * _rstd(mo) * g2_ref[...]
        return mo, x1, x1 * _rstd(x1) * g3_ref[...]

    tm_mix = _tile(t, 256)
    mix_out, x1, hf = _matmul(
        "out_proj", grid=(t // tm_mix, 1, 1), mode="nn",
        pairs=[[(mixed, wout_full)]],
        pair_specs=[[(rows3(tm_mix, d), pl.BlockSpec((d, d), lambda i, j, k: (0, 0)))]],
        acc_shapes=[(tm_mix, d)], extras=[xs, ln_mix_post, ln_ffn_pre],
        extra_specs=[rows3(tm_mix, d), gain3, gain3], out_shapes=[f32_td, f32_td, mm_td],
        out_specs=[rows3(tm_mix, d)] * 3, epilogue=post_mix_epilogue, epi_rows=EPILOGUE_ROWS)
    wg_full, wu_full = landed(["w_gate", "w_up"], hf)

    def gate_up_epilogue(accs, extra_refs):
        g, up = accs
        sig = jax.nn.sigmoid(g)
        silu = g * sig
        return up * (sig + silu * (1.0 - sig)), silu, silu * up

    tf = _tile(f, 512)
    ff_tile = jax.ShapeDtypeStruct((t, f), MM_DTYPE)
    act_by_g, act_by_up, act = _matmul(
        "gate_up", grid=(f // tf, t // tm, 1), mode="nn",
        pairs=[[(hf, wg_full)], [(hf, wu_full)]],
        pair_specs=[[(pl.BlockSpec((tm, d), lambda j, i, k: (i, 0)),
                      pl.BlockSpec((d, tf), lambda j, i, k: (0, j)))]] * 2,
        acc_shapes=[(tm, tf)] * 2, out_shapes=[ff_tile] * 3,
        out_specs=[pl.BlockSpec((tm, tf), lambda j, i, k: (i, j))] * 3, epilogue=gate_up_epilogue)
    wdown_full = landed(["w_down"], act)[0].reshape(f, d)

    def loss_epilogue(accs, extra_refs):
        x1_ref, tg_ref, g_ref = extra_refs
        ff_v = accs[0]
        rstd = _rstd(ff_v)
        n = ff_v * rstd
        g = g_ref[...]
        err = x1_ref[...] + n * g - tg_ref[...]
        rows_loss = 0.5 * jnp.sum(jnp.mean(err * err, axis=-1, keepdims=True), axis=0, keepdims=True)
        dout = err / d
        return (dout, _norm_bwd(dout * g, n, rstd), jnp.broadcast_to(rows_loss, (8, 128)),
                jnp.sum(dout * n, axis=0, keepdims=True))

    tm_row = _tile(t, 1024)
    dout, dff, loss_tile, dg_ffn_post = _matmul(
        "down_proj", grid=(t // tm_row, 1, f // tf), mode="nn",
        pairs=[[(act, wdown_full)]],
        pair_specs=[[(pl.BlockSpec((tm_row, tf), lambda i, j, k: (i, k)),
                      pl.BlockSpec((tf, d), lambda i, j, k: (k, 0)))]],
        acc_shapes=[(tm_row, d)], extras=[x1, tgt, ln_ffn_post], extra_specs=[STREAM] * 2 + [gain3],
        out_shapes=[f32_td, mm_td, jax.ShapeDtypeStruct((8, 128), jnp.float32), f32_gain],
        out_specs=[STREAM] * 2 + [pl.BlockSpec((8, 128), lambda i, j, k: (0, 0)), gain3],
        epilogue=loss_epilogue, epi_rows=EPILOGUE_ROWS, sum_outs=(2, 3))

    def dact_epilogue(accs, extra_refs):
        dact = accs[0]
        return dact * extra_refs[0][...].astype(jnp.float32), dact * extra_refs[1][...].astype(jnp.float32)

    ff_spec_ij = pl.BlockSpec((tm, tf), lambda i, j, k: (i, j))
    dg_act, dup_act = _matmul(
        "dact", grid=(t // tm, f // tf, 1), mode="nt",
        pairs=[[(dff, wdown_full)]],
        pair_specs=[[(rows3(tm, d), pl.BlockSpec((tf, d), lambda i, j, k: (j, 0)))]],
        acc_shapes=[(tm, tf)], extras=[act_by_g, act_by_up], extra_specs=[ff_spec_ij, ff_spec_ij],
        out_shapes=[ff_tile] * 2, out_specs=[ff_spec_ij] * 2, epilogue=dact_epilogue)
    tk = _tile(t, 1024)
    dw_down = _matmul(
        "dw_down", grid=(N_CHIPS, 1, t // tk), mode="tn",
        pairs=[[(act, dff)]],
        pair_specs=[[(pl.BlockSpec((tk, f4), lambda i, j, k: (k, i)),
                      pl.BlockSpec((tk, d), lambda i, j, k: (k, 0)))]],
        acc_shapes=[(f4, d)], out_shapes=[jax.ShapeDtypeStruct((N_CHIPS, f4, d), COMM_DTYPE)],
        out_specs=[pl.BlockSpec((None, f4, d), lambda i, j, k: (i, 0, 0))], epilogue=_identity_epilogue)[0]
    tn = _tile(d, 1024)
    leaving = {"w_down": _scatter_start("scatter_start_w_down", dw_down)}

    def ffn_pre_epilogue(accs, extra_refs):
        x1_ref, dout_ref, mo_ref, g3_ref, g2_ref = extra_refs
        dhf_v = accs[0]
        x1_v = x1_ref[...]
        rstd3 = _rstd(x1_v)
        n3 = x1_v * rstd3
        dx1_v = dout_ref[...] + _norm_bwd(dhf_v * g3_ref[...], n3, rstd3)
        mo = mo_ref[...]
        rstd2 = _rstd(mo)
        n2 = mo * rstd2
        return (dx1_v, _norm_bwd(dx1_v * g2_ref[...], n2, rstd2), jnp.sum(dhf_v * n3, axis=0, keepdims=True),
                jnp.sum(dx1_v * n2, axis=0, keepdims=True))

    dx1, dmo, dg_ffn_pre, dg_mix_post = _matmul(
        "dhf", grid=(t // tm_row, 1, f // tf), mode="nt",
        pairs=[[(dg_act, wg_full), (dup_act, wu_full)]],
        pair_specs=[[(pl.BlockSpec((tm_row, tf), lambda i, j, k: (i, k)),
                      pl.BlockSpec((d, tf), lambda i, j, k: (0, k)))] * 2],
        acc_shapes=[(tm_row, d)], extras=[x1, dout, mix_out, ln_ffn_pre, ln_mix_post],
        extra_specs=[STREAM] * 3 + [gain3] * 2, out_shapes=[f32_td, mm_td, f32_gain, f32_gain],
        out_specs=[STREAM] * 2 + [gain3] * 2, epilogue=ffn_pre_epilogue, epi_rows=EPILOGUE_ROWS,
        sum_outs=(2, 3), deps=leaving["w_down"][4:])
    tmo = _tile(d, 2048)
    grad_ff = jax.ShapeDtypeStruct((d, f), COMM_DTYPE)
    dw_gate, dw_up = _matmul(
        "dw_gate_up", grid=(d // tmo, f // tf, t // tm2), mode="tn",
        pairs=[[(hf, dg_act)], [(hf, dup_act)]],
        pair_specs=[[(pl.BlockSpec((tm2, tmo), lambda i, j, k: (k, i)),
                      pl.BlockSpec((tm2, tf), lambda i, j, k: (k, j)))]] * 2,
        acc_shapes=[(tmo, tf)] * 2, out_shapes=[grad_ff] * 2,
        out_specs=[pl.BlockSpec((tmo, tf), lambda i, j, k: (i, j))] * 2, epilogue=_identity_epilogue)
    leaving["w_gate"] = _scatter_start("scatter_start_w_gate", dw_gate)
    leaving["w_up"] = _scatter_start("scatter_start_w_up", dw_up, leaving["w_gate"][4:])

    dmixed = _matmul(
        "dmixed", grid=(t // tm, 1, 1), mode="nt",
        pairs=[[(dmo, wout_full)]],
        pair_specs=[[(rows3(tm, d), pl.BlockSpec((d, d), lambda i, j, k: (0, 0)))]],
        acc_shapes=[(tm, d)], out_shapes=[f32_td],
        out_specs=[rows3(tm, d)], epilogue=_identity_epilogue,
        deps=leaving["w_up"][4:])[0]
    tmo = _tile(d, 2048)
    dw_out = _matmul(
        "dw_out", grid=(d // tmo, d // tn, t // tk), mode="tn",
        pairs=[[(mixed, dmo)]],
        pair_specs=[[(pl.BlockSpec((tk, tmo), lambda i, j, k: (k, i)),
                      pl.BlockSpec((tk, tn), lambda i, j, k: (k, j)))]],
        acc_shapes=[(tmo, tn)], out_shapes=[jax.ShapeDtypeStruct((d, d), COMM_DTYPE)],
        out_specs=[pl.BlockSpec((tmo, tn), lambda i, j, k: (i, j))], epilogue=_identity_epilogue)[0]
    leaving["w_out"] = _scatter_start("scatter_start_w_out", dw_out.reshape(N_CHIPS, d // N_CHIPS, d))
    dproj, dconv_full, dpool_g, dpool_scale = _mixers_bwd(proj, dmixed, conv_full, pool_g, pool_scale,
                                                          leaving["w_out"][4:])
    dpool_slots = _cast_rows("cast_dpool", dpool_g.reshape(N_CHIPS * n_groups * dg4, dg), COMM_DTYPE)
    leaving["pool_w"] = _scatter_start("scatter_start_pool_w", dpool_slots.reshape(N_CHIPS, n_groups * dg4, dg))
    dw_in = _matmul(
        "dw_in", grid=(d // tmo, N_CHIPS, t // tk), mode="tn",
        pairs=[[(h, dproj)]],
        pair_specs=[[(pl.BlockSpec((tk, tmo), lambda i, j, k: (k, i)),
                      pl.BlockSpec((tk, e4), lambda i, j, k: (k, j)))]],
        acc_shapes=[(tmo, e4)], out_shapes=[jax.ShapeDtypeStruct((N_CHIPS, d, e4), COMM_DTYPE)],
        out_specs=[pl.BlockSpec((None, tmo, e4), lambda i, j, k: (j, i, 0))], epilogue=_identity_epilogue,
        deps=leaving["pool_w"][4:])[0]
    leaving["w_in"] = _scatter_start("scatter_start_w_in", dw_in)

    def mix_pre_epilogue(accs, extra_refs):
        x_ref, dx1_ref, g_ref = extra_refs
        dh_v = accs[0]
        xv = x_ref[...]
        rstd = _rstd(xv)
        n = xv * rstd
        return dx1_ref[...] + _norm_bwd(dh_v * g_ref[...], n, rstd), jnp.sum(dh_v * n, axis=0, keepdims=True)

    grad_x, dg_mix_pre = _matmul(
        "dh", grid=(t // tm_row, 1, N_CHIPS), mode="nt",
        pairs=[[(dproj, win_g)]],
        pair_specs=[[(pl.BlockSpec((tm_row, e4), lambda i, j, k: (i, k)),
                      pl.BlockSpec((None, d, e4), lambda i, j, k: (k, 0, 0)))]],
        acc_shapes=[(tm_row, d)], extras=[xs, dx1, ln_mix_pre], extra_specs=[STREAM] * 2 + [gain3],
        out_shapes=[f32_td, f32_gain], out_specs=[STREAM, gain3], epilogue=mix_pre_epilogue,
        epi_rows=EPILOGUE_ROWS, sum_outs=(1,), deps=leaving["w_in"][4:])

    names = ["w_down", "w_gate", "w_up", "w_out", "pool_w", "w_in"]
    partial, after = [], grad_x
    for k in names:
        send_sems, recv_sems, own, land, _ = leaving[k]
        own, land = _scatter_wait("scatter_wait_" + k, send_sems, recv_sems, own, land, after)
        partial.append(_sum_own_and_received("sum_" + k, own, land, chip_arr))
        after = land
    other = _swap_sibling("swap_grads", partial)
    moments = {"w_in": (m_w_in, v_w_in), "w_out": (m_w_out, v_w_out), "w_gate": (m_w_gate, v_w_gate),
               "w_up": (m_w_up, v_w_up), "w_down": (m_w_down, v_w_down), "pool_w": (m_pool_w, v_pool_w)}
    result = {}
    for k, mine, theirs in zip(names, partial, other):
        shape = moments[k][0].shape
        two_d = big[k].shape
        outs = _adamw("adamw_" + k, big[k], moments[k][0].reshape(two_d), moments[k][1].reshape(two_d),
                      [mine, theirs])
        result[k] = [o.reshape(shape) for o in outs]

    small_shapes = [(1, d)] * 4 + [pool_scale.shape, (CONV_K, N_CHIPS * cw4)]
    packed = _pack_rows([dg_mix_pre, dg_mix_post, dg_ffn_pre, dg_ffn_post, dpool_scale, dconv_full], 1024)
    summed = _sum_slots("sum_small", _gather_devices("gather_small", packed))
    g_mix_pre, g_mix_post, g_ffn_pre, g_ffn_post, g_pool_scale, g_conv_full = _unpack_rows(summed, small_shapes)
    g_conv = lax.dynamic_slice(g_conv_full, (0, chip * cw4), (CONV_K, cw4))[None]
    small = [("ln_mix_pre", ln_mix_pre, m_ln_mix_pre, v_ln_mix_pre, g_mix_pre),
             ("conv_w", conv_w, m_conv_w, v_conv_w, g_conv),
             ("pool_scale", pool_scale, m_pool_scale, v_pool_scale, g_pool_scale),
             ("ln_mix_post", ln_mix_post, m_ln_mix_post, v_ln_mix_post, g_mix_post),
             ("ln_ffn_pre", ln_ffn_pre, m_ln_ffn_pre, v_ln_ffn_pre, g_ffn_pre),
             ("ln_ffn_post", ln_ffn_post, m_ln_ffn_post, v_ln_ffn_post, g_ffn_post)]
    shapes_small = [s[1].shape for s in small]
    packs = [_pack_rows([s[q] for s in small], 128) for q in (1, 2, 3, 4)]
    outs = _adamw("adamw_small", packs[0], packs[1], packs[2], [packs[3]])
    unpacked = [_unpack_rows(o, shapes_small) for o in outs]
    for idx, s in enumerate(small):
        result[s[0]] = [u[idx] for u in unpacked]

    loss = lax.psum(loss_tile[0, 0], ("x", "y", "c"))
    order = ["ln_mix_pre", "w_in", "conv_w", "pool_w", "pool_scale", "w_out", "ln_mix_post", "ln_ffn_pre",
             "w_gate", "w_up", "w_down", "ln_ffn_post"]
    return (loss, grad_x[None], *[result[k][0] for k in order], *[result[k][1] for k in order],
            *[result[k][2] for k in order], *[result[k][3] for k in order])
```

```python
import functools

import jax
import jax.numpy as jnp
from jax import lax
from jax.experimental import pallas as pl
from jax.experimental.pallas import tpu as pltpu

EPS = 1e-6
CONV_HEAD_DIM = 128
CONV_K = 3
POOL_WINDOWS = (2, 4, 8, 16)
HALO = 16
EPILOGUE_ROWS = 64
STREAM_ROWS = 256
N_CHIPS = 4
N_DEV = 8

ADAM_LR = 0.001
ADAM_B1 = 0.9
ADAM_B2 = 0.999
ADAM_EPS = 1e-08
ADAM_WD = 0.01
ADAM_STEP = 10

MM_DTYPE = jnp.bfloat16
COMM_DTYPE = jnp.bfloat16
VMEM_LIMIT = 62 * 1024 * 1024
MESH = pl.DeviceIdType.MESH
ANY = pl.BlockSpec(memory_space=pl.ANY)
STREAM = "stream"


def _tile(n, pref):
    t = min(pref, n)
    while n % t:
        t //= 2
    return t


def _params(sem):
    return pltpu.CompilerParams(dimension_semantics=sem, vmem_limit_bytes=VMEM_LIMIT)


def _rstd(x):
    return lax.rsqrt(jnp.mean(x * x, axis=-1, keepdims=True) + EPS)


def _norm_bwd(dn, n, rstd):
    return rstd * (dn - n * jnp.mean(dn * n, axis=-1, keepdims=True))


_DOT_DIMS = {
    "nn": (((1,), (0,)), ((), ())),
    "nt": (((1,), (1,)), ((), ())),
    "tn": (((0,), (0,)), ((), ())),
}


def _dot(a, b, mode):
    return lax.dot_general(a.astype(MM_DTYPE), b.astype(MM_DTYPE), _DOT_DIMS[mode],
                           preferred_element_type=jnp.float32)


def _matmul(name, *, grid, mode, pairs, pair_specs, acc_shapes, extras=(), extra_specs=(),
            out_shapes, out_specs, epilogue, deps=(), epi_rows=0, sum_outs=(), prefetch_extras=False):
    nk = grid[2]
    operands, operand_specs, where, counts = [], [], {}, []
    pair_index = []
    for ps, ss in zip(pairs, pair_specs):
        counts.append(len(ps))
        for arrays, specs in zip(ps, ss):
            for arr, spec in zip(arrays, specs):
                key = (id(arr), id(spec))
                if key not in where:
                    where[key] = len(operands)
                    operands.append(arr)
                    operand_specs.append(spec)
                pair_index.append(where[key])
    n_operands = len(operands)
    n_extra = len(extras)
    n_out = len(out_shapes)
    n_in = n_operands + n_extra + len(deps)
    in_streams = [q for q, s in enumerate(extra_specs) if s is STREAM]
    out_streams = [q for q, s in enumerate(out_specs) if s is STREAM]
    in_buf_rows = (acc_shapes[0][0],) if prefetch_extras else (2, STREAM_ROWS)
    stream_bufs = ([pltpu.VMEM(in_buf_rows + (extras[q].shape[1],), extras[q].dtype) for q in in_streams]
                   + [pltpu.VMEM((2, STREAM_ROWS, out_shapes[q].shape[1]), out_shapes[q].dtype) for q in out_streams])
    n_streams = len(stream_bufs)
    n_acc = 0 if nk == 1 else len(acc_shapes)

    def body(*refs):
        pair_refs = [refs[q] for q in pair_index]
        extra_refs = refs[n_operands:n_operands + n_extra]
        out_refs = refs[n_in:n_in + n_out]
        acc_refs = refs[n_in + n_out:n_in + n_out + n_acc]
        bufs = refs[n_in + n_out + n_acc:n_in + n_out + n_acc + n_streams]
        i, k = pl.program_id(0), pl.program_id(2)

        def stream_copies():
            sems = refs[-1]
            n_rows = acc_refs[0].shape[0]

            def hbm_rows(ref, c):
                return ref.at[pl.ds(pl.multiple_of(i * n_rows + c * STREAM_ROWS, STREAM_ROWS), STREAM_ROWS)]

            def fetch(s, c):
                if prefetch_extras:
                    return pltpu.make_async_copy(
                        extra_refs[in_streams[s]].at[pl.ds(pl.multiple_of(i * n_rows, STREAM_ROWS), n_rows)],
                        bufs[s], sems.at[s, 0])
                return pltpu.make_async_copy(hbm_rows(extra_refs[in_streams[s]], c), bufs[s].at[c % 2],
                                             sems.at[s, c % 2])

            def drain(s, c):
                return pltpu.make_async_copy(bufs[len(in_streams) + s].at[c % 2], hbm_rows(out_refs[out_streams[s]], c),
                                             sems.at[len(in_streams) + s, c % 2])

            return fetch, drain

        def streamed_finish(accs):
            fetch, drain = stream_copies()
            n_chunks = accs[0].shape[0] // STREAM_ROWS
            for c in range(n_chunks):
                for s in range(len(in_streams)):
                    if c + 1 < n_chunks and not prefetch_extras:
                        fetch(s, c + 1).start()
                    if c == 0 or not prefetch_extras:
                        fetch(s, c).wait()
                for s in range(len(out_streams)):
                    if c >= 2:
                        drain(s, c - 2).wait()
                for r0 in range(0, STREAM_ROWS, epi_rows):
                    sub = slice(r0, r0 + epi_rows)
                    rows = slice(c * STREAM_ROWS + r0, c * STREAM_ROWS + r0 + epi_rows)
                    views = [(bufs[in_streams.index(q)].at[rows] if prefetch_extras
                              else bufs[in_streams.index(q)].at[c % 2, sub]) if q in in_streams else e
                             for q, e in enumerate(extra_refs)]
                    outs = epilogue([a[rows, :] for a in accs], views)
                    for q, (o_ref, o) in enumerate(zip(out_refs, outs)):
                        if q in out_streams:
                            bufs[len(in_streams) + out_streams.index(q)][c % 2, sub, :] = o.astype(o_ref.dtype)
                        elif c == 0 and r0 == 0:
                            _accumulate(o_ref, o, i)
                        else:
                            o_ref[...] += o
                for s in range(len(out_streams)):
                    drain(s, c).start()
            for s in range(len(out_streams)):
                for c in range(max(n_chunks - 2, 0), n_chunks):
                    drain(s, c).wait()

        def partial_sums():
            res, p = [], 0
            for cnt in counts:
                tot = None
                for _ in range(cnt):
                    d = _dot(pair_refs[p][...], pair_refs[p + 1][...], mode)
                    tot = d if tot is None else tot + d
                    p += 2
                res.append(tot)
            return res

        def finish(accs):
            n_rows = accs[0].shape[0]
            step = epi_rows or n_rows
            for r0 in range(0, n_rows, step):
                rows = slice(r0, r0 + step)
                outs = epilogue([a[rows, :] for a in accs], [e.at[rows] if e.shape[0] == n_rows else e
                                                             for e in extra_refs])
                for q, (o_ref, o) in enumerate(zip(out_refs, outs)):
                    if q not in sum_outs:
                        o_ref[rows, :] = o.astype(o_ref.dtype)
                    elif r0 == 0:
                        _accumulate(o_ref, o, i)
                    else:
                        o_ref[...] += o

        if nk == 1:
            finish(partial_sums())
        else:
            if n_streams:
                @pl.when(k == (0 if prefetch_extras else nk - 1))
                def _():
                    for s in range(len(in_streams)):
                        stream_copies()[0](s, 0).start()

            @pl.when(k == 0)
            def _():
                for acc_ref, s in zip(acc_refs, partial_sums()):
                    acc_ref[...] = s

            @pl.when(k > 0)
            def _():
                for acc_ref, s in zip(acc_refs, partial_sums()):
                    acc_ref[...] += s

            @pl.when(k == nk - 1)
            def _():
                (streamed_finish if n_streams else finish)(acc_refs)

    scratch = [] if nk == 1 else [pltpu.VMEM(s, jnp.float32) for s in acc_shapes]
    if n_streams:
        assert nk > 1 and epi_rows and not any(q in sum_outs for q in out_streams)
        scratch = scratch + stream_bufs + [pltpu.SemaphoreType.DMA((n_streams, 2))]
    return pl.pallas_call(
        body, name=name, grid=grid,
        in_specs=operand_specs + [ANY if s is STREAM else s for s in extra_specs] + [ANY] * len(deps),
        out_specs=[ANY if s is STREAM else s for s in out_specs],
        out_shape=list(out_shapes), scratch_shapes=scratch,
        compiler_params=_params(("arbitrary", "arbitrary", "arbitrary")),
    )(*operands, *extras, *deps)


def _identity_epilogue(accs, extra_refs):
    return tuple(accs)


def _row_spec(tr, n):
    return pl.BlockSpec((tr, n), lambda i: (i, 0))


def _const_spec(shape):
    return pl.BlockSpec(shape, lambda i: tuple(0 for _ in shape))


def _accumulate(ref, val, i):
    @pl.when(i == 0)
    def _():
        ref[...] = val

    @pl.when(i > 0)
    def _():
        ref[...] += val


def _pre_norm(x, gain, deps=()):
    t, d = x.shape
    tr = _tile(t, 512)

    def body(x_ref, g_ref, *rest):
        h_ref = rest[-1]
        xv = x_ref[...]
        h_ref[...] = (xv * _rstd(xv) * g_ref[...]).astype(h_ref.dtype)

    return pl.pallas_call(
        body, name="pre_norm", grid=(t // tr,),
        in_specs=[_row_spec(tr, d), _const_spec((1, d))] + [ANY] * len(deps), out_specs=_row_spec(tr, d),
        out_shape=jax.ShapeDtypeStruct((t, d), MM_DTYPE), compiler_params=_params(("arbitrary",)),
    )(x, gain, *deps)


def _pool_matrix(pool_ref, g):
    return jnp.concatenate([pool_ref[c, g] for c in range(N_CHIPS)], axis=0)


def _inv_count(row0, n, w):
    pos = (row0 + lax.broadcasted_iota(jnp.int32, (n, 1), 0) + 1).astype(jnp.float32)
    return 1.0 / jnp.minimum(pos, float(w))


def _conv_piece(cu_buf, start, n, b_piece, convw_ref):
    conv = None
    for k in range(CONV_K):
        term = convw_ref[k:k + 1, :] * cu_buf[pl.ds(HALO + start + k - (CONV_K - 1), n), :]
        conv = term if conv is None else conv + term
    return conv, b_piece * conv


def _head_stats(a, width):
    return [_rstd(a[:, h * width:(h + 1) * width]) for h in range(a.shape[1] // width)]


def _pooled_piece(v_buf, start, n, v_piece, row0, dg):
    outs = []
    for gi, w in enumerate(POOL_WINDOWS):
        cols = slice(gi * dg, (gi + 1) * dg)
        win = None
        for k in range(w):
            term = v_buf[pl.ds(HALO + start - k, n), cols]
            win = term if win is None else win + term
        outs.append(win * _inv_count(row0 + start, n, w) - v_piece[:, cols])
    return outs


def _halo_specs(t, tr, width, col):
    per = tr // HALO
    last = t // HALO - 1
    prev = pl.BlockSpec((HALO, width), lambda i: (jnp.maximum(i * per - 1, 0), col))
    nxt = pl.BlockSpec((HALO, width), lambda i: (jnp.minimum((i + 1) * per, last), col))
    return prev, nxt


def _mixers_fwd(proj, conv_w, pool_g, pool_scale):
    t, e = proj.shape
    cw = e // 4
    dg = pool_g.shape[-1]
    tr = _tile(t, 256)

    def main(col):
        return pl.BlockSpec((tr, cw), lambda i: (i, col))

    def body(b_ref, c_ref, u_ref, v_ref, cp_ref, up_ref, vp_ref, convw_ref, pool_ref, scale_ref,
             out_ref, cu_buf, v_buf):
        i = pl.program_id(0)
        keep = (i > 0).astype(jnp.float32)
        cu_buf[pl.ds(0, HALO), :] = cp_ref[...] * up_ref[...] * keep
        cu_buf[pl.ds(HALO, tr), :] = c_ref[...] * u_ref[...]
        v_buf[pl.ds(0, HALO), :] = vp_ref[...] * keep
        v_buf[pl.ds(HALO, tr), :] = v_ref[...]
        _, a = _conv_piece(cu_buf, 0, tr, b_ref[...], convw_ref)
        for h, rstd in enumerate(_head_stats(a, CONV_HEAD_DIM)):
            cols = slice(h * CONV_HEAD_DIM, (h + 1) * CONV_HEAD_DIM)
            out_ref[:, cols] = (a[:, cols] * rstd).astype(out_ref.dtype)
        pooled = _pooled_piece(v_buf, 0, tr, v_ref[...], i * tr, dg)
        for gi, p in enumerate(pooled):
            z = _dot(p, _pool_matrix(pool_ref, gi), "nn")
            cols = slice(gi * dg, (gi + 1) * dg)
            out_ref[:, cw + gi * dg:cw + (gi + 1) * dg] = (z * _rstd(z) * scale_ref[:, cols]).astype(out_ref.dtype)

    prev_c, _ = _halo_specs(t, tr, cw, 1)
    prev_u, _ = _halo_specs(t, tr, cw, 2)
    prev_v, _ = _halo_specs(t, tr, cw, 3)
    return pl.pallas_call(
        body, name="mixers_fwd", grid=(t // tr,),
        in_specs=[main(0), main(1), main(2), main(3), prev_c, prev_u, prev_v,
                  _const_spec(conv_w.shape), _const_spec(pool_g.shape), _const_spec(pool_scale.shape)],
        out_specs=_row_spec(tr, 2 * cw),
        out_shape=jax.ShapeDtypeStruct((t, 2 * cw), MM_DTYPE),
        scratch_shapes=[pltpu.VMEM((tr + HALO, cw), jnp.float32), pltpu.VMEM((tr + HALO, cw), jnp.float32)],
        compiler_params=_params(("arbitrary",)),
    )(proj, proj, proj, proj, proj, proj, proj, conv_w, pool_g, pool_scale)


def _mixers_bwd(proj, dmixed, conv_w, pool_g, pool_scale, deps=()):
    t, e = proj.shape
    cw = e // 4
    dg = pool_g.shape[-1]
    n_groups = len(POOL_WINDOWS)
    tr = _tile(t, 256)
    n_tiles = t // tr
    ext = tr + 2 * HALO

    def main(col):
        return pl.BlockSpec((tr, cw), lambda i: (i, col))

    def body(b_ref, c_ref, u_ref, v_ref, dyc_ref, dyp_ref,
             cp_ref, up_ref, vp_ref,
             bn_ref, cn_ref, un_ref, vn_ref, dycn_ref, dypn_ref,
             convw_ref, pool_ref, scale_ref, *rest):
        (dproj_ref, dconvw_ref, dpool_ref, dscale_ref,
         cu_buf, v_buf, dconv_buf, dpn_buf, dpooled_buf) = rest[len(deps):]
        i = pl.program_id(0)
        keep_prev = (i > 0).astype(jnp.float32)
        keep_next = (i < n_tiles - 1).astype(jnp.float32)
        cu_buf[pl.ds(0, HALO), :] = cp_ref[...] * up_ref[...] * keep_prev
        cu_buf[pl.ds(HALO, tr), :] = c_ref[...] * u_ref[...]
        cu_buf[pl.ds(HALO + tr, HALO), :] = cn_ref[...] * un_ref[...]
        v_buf[pl.ds(0, HALO), :] = vp_ref[...] * keep_prev
        v_buf[pl.ds(HALO, tr), :] = v_ref[...]
        v_buf[pl.ds(HALO + tr, HALO), :] = vn_ref[...]

        def conv_piece(start, n, b_piece, dyc_piece, keep, is_main):
            conv, a = _conv_piece(cu_buf, start, n, b_piece, convw_ref)
            for h, rstd in enumerate(_head_stats(a, CONV_HEAD_DIM)):
                cols = slice(h * CONV_HEAD_DIM, (h + 1) * CONV_HEAD_DIM)
                da = _norm_bwd(dyc_piece[:, cols], a[:, cols] * rstd, rstd)
                dconv_buf[pl.ds(start, n), cols] = da * b_piece[:, cols] * keep
                if is_main:
                    dproj_ref[:, cols] = (da * conv[:, cols]).astype(dproj_ref.dtype)

        conv_piece(0, tr, b_ref[...], dyc_ref[...], 1.0, True)
        conv_piece(tr, HALO, bn_ref[...], dycn_ref[...], keep_next, False)

        dconv_main = dconv_buf[pl.ds(0, tr), :]
        dcu = None
        dw_rows = []
        for k in range(CONV_K):
            shift = CONV_K - 1 - k
            term = convw_ref[k:k + 1, :] * dconv_buf[pl.ds(shift, tr), :]
            dcu = term if dcu is None else dcu + term
            dw_rows.append(jnp.sum(dconv_main * cu_buf[pl.ds(HALO - shift, tr), :], axis=0, keepdims=True))
        dproj_ref[:, cw:2 * cw] = (dcu * u_ref[...]).astype(dproj_ref.dtype)
        dproj_ref[:, 2 * cw:3 * cw] = (dcu * c_ref[...]).astype(dproj_ref.dtype)
        _accumulate(dconvw_ref, jnp.concatenate(dw_rows, axis=0), i)

        def pool_piece(start, n, v_piece, dyp_piece, keep, is_main):
            pooled = _pooled_piece(v_buf, start, n, v_piece, i * tr, dg)
            dscale, dmats = [], []
            for gi, w in enumerate(POOL_WINDOWS):
                cols = slice(gi * dg, (gi + 1) * dg)
                mat = _pool_matrix(pool_ref, gi)
                z = _dot(pooled[gi], mat, "nn")
                rstd = _rstd(z)
                nz = z * rstd
                dyp_g = dyp_piece[:, cols]
                dz = _norm_bwd(dyp_g * scale_ref[:, cols], nz, rstd)
                dpooled = _dot(dz, mat, "nt") * keep
                dpn_buf[pl.ds(start, n), cols] = dpooled * _inv_count(i * tr + start, n, w)
                if is_main:
                    dpooled_buf[:, cols] = dpooled
                    dscale.append(jnp.sum(dyp_g * nz, axis=0, keepdims=True))
                    dmats.append(_dot(pooled[gi], dz, "tn"))
            return dscale, dmats

        dscale, dmats = pool_piece(0, tr, v_ref[...], dyp_ref[...], 1.0, True)
        pool_piece(tr, HALO, vn_ref[...], dypn_ref[...], keep_next, False)
        for gi, w in enumerate(POOL_WINDOWS):
            cols = slice(gi * dg, (gi + 1) * dg)
            back = None
            for k in range(w):
                term = dpn_buf[pl.ds(k, tr), cols]
                back = term if back is None else back + term
            dproj_ref[:, 3 * cw + gi * dg:3 * cw + (gi + 1) * dg] = (back - dpooled_buf[:, cols]).astype(dproj_ref.dtype)
        _accumulate(dscale_ref, jnp.concatenate(dscale, axis=1), i)
        rows = dg // N_CHIPS
        for gi in range(n_groups):
            for c in range(N_CHIPS):
                _accumulate(dpool_ref.at[c, gi], dmats[gi][c * rows:(c + 1) * rows, :], i)

    prev_c, next_c = _halo_specs(t, tr, cw, 1)
    prev_u, next_u = _halo_specs(t, tr, cw, 2)
    prev_v, next_v = _halo_specs(t, tr, cw, 3)
    _, next_b = _halo_specs(t, tr, cw, 0)
    _, next_dyc = _halo_specs(t, tr, cw, 0)
    _, next_dyp = _halo_specs(t, tr, cw, 1)
    return pl.pallas_call(
        body, name="mixers_bwd", grid=(n_tiles,),
        in_specs=[main(0), main(1), main(2), main(3), main(0), main(1),
                  prev_c, prev_u, prev_v,
                  next_b, next_c, next_u, next_v, next_dyc, next_dyp,
                  _const_spec(conv_w.shape), _const_spec(pool_g.shape), _const_spec(pool_scale.shape)]
        + [ANY] * len(deps),
        out_specs=[_row_spec(tr, e), _const_spec(conv_w.shape), _const_spec(pool_g.shape),
                   _const_spec(pool_scale.shape)],
        out_shape=[jax.ShapeDtypeStruct((t, e), MM_DTYPE), jax.ShapeDtypeStruct(conv_w.shape, jnp.float32),
                   jax.ShapeDtypeStruct(pool_g.shape, jnp.float32),
                   jax.ShapeDtypeStruct(pool_scale.shape, jnp.float32)],
        scratch_shapes=[pltpu.VMEM((ext, cw), jnp.float32), pltpu.VMEM((ext, cw), jnp.float32),
                        pltpu.VMEM((tr + HALO, cw), jnp.float32), pltpu.VMEM((tr + HALO, cw), jnp.float32),
                        pltpu.VMEM((tr, cw), jnp.float32)],
        compiler_params=_params(("arbitrary",)),
    )(proj, proj, proj, proj, dmixed, dmixed,
      proj, proj, proj,
      proj, proj, proj, proj, dmixed, dmixed,
      conv_w, pool_g, pool_scale, *deps)


def _cast_rows(name, w, dtype):
    r, c = w.shape
    tr = _tile(r, 512)

    def body(w_ref, o_ref):
        o_ref[...] = w_ref[...].astype(o_ref.dtype)

    return pl.pallas_call(
        body, name=name, grid=(r // tr,), in_specs=[_row_spec(tr, c)], out_specs=_row_spec(tr, c),
        out_shape=jax.ShapeDtypeStruct((r, c), dtype), compiler_params=_params(("arbitrary",)),
    )(w)


def _cast_to_slot(name, w, chip, by_columns=False):
    r, c = w.shape
    tr = _tile(r, 512)
    if by_columns:
        out_spec = pl.BlockSpec((tr, c), lambda i, chip_ref: (i, chip_ref[0]))
        out_shape = jax.ShapeDtypeStruct((r, N_CHIPS * c), MM_DTYPE)
    else:
        out_spec = pl.BlockSpec((None, tr, c), lambda i, chip_ref: (chip_ref[0], i, 0))
        out_shape = jax.ShapeDtypeStruct((N_CHIPS, r, c), MM_DTYPE)

    def body(chip_ref, w_ref, o_ref):
        o_ref[...] = w_ref[...].astype(o_ref.dtype)

    return pl.pallas_call(
        body, name=name,
        grid_spec=pltpu.PrefetchScalarGridSpec(
            num_scalar_prefetch=1, grid=(r // tr,),
            in_specs=[pl.BlockSpec((tr, c), lambda i, chip_ref: (i, 0))], out_specs=out_spec),
        out_shape=out_shape, compiler_params=_params(("arbitrary",)),
    )(chip, w)


def _sum_own_and_received(name, own, land, chip):
    _, r, c = land.shape
    tr = _tile(r, 256)

    def body(chip_ref, own_ref, a_ref, b_ref, c_ref, o_ref):
        tot = own_ref[...].astype(jnp.float32) + a_ref[...].astype(jnp.float32)
        tot = tot + b_ref[...].astype(jnp.float32)
        o_ref[...] = (tot + c_ref[...].astype(jnp.float32)).astype(o_ref.dtype)

    def slot(k):
        return pl.BlockSpec((None, tr, c), lambda i, chip_ref: ((chip_ref[0] + k) % N_CHIPS, i, 0))

    own_spec = slot(0) if len(own.shape) == 3 else pl.BlockSpec((tr, c), lambda i, chip_ref: (i, chip_ref[0]))
    return pl.pallas_call(
        body, name=name,
        grid_spec=pltpu.PrefetchScalarGridSpec(
            num_scalar_prefetch=1, grid=(r // tr,), in_specs=[own_spec, slot(1), slot(2), slot(3)],
            out_specs=pl.BlockSpec((tr, c), lambda i, chip_ref: (i, 0))),
        out_shape=jax.ShapeDtypeStruct((r, c), COMM_DTYPE), compiler_params=_params(("arbitrary",)),
    )(chip, own, land, land, land)


def _sum_slots(name, slots):
    n, r, c = slots.shape
    tr = _tile(r, 256)

    def body(s_ref, o_ref):
        tot = s_ref[0].astype(jnp.float32)
        for s in range(1, n):
            tot = tot + s_ref[s].astype(jnp.float32)
        o_ref[...] = tot

    return pl.pallas_call(
        body, name=name, grid=(r // tr,),
        in_specs=[pl.BlockSpec((n, tr, c), lambda i: (0, i, 0))], out_specs=_row_spec(tr, c),
        out_shape=jax.ShapeDtypeStruct((r, c), jnp.float32), compiler_params=_params(("arbitrary",)),
    )(slots)


def _adamw_math(w, g, m, v):
    m = ADAM_B1 * m + (1.0 - ADAM_B1) * g
    v = ADAM_B2 * v + (1.0 - ADAM_B2) * (g * g)
    m_hat = m / (1.0 - ADAM_B1 ** ADAM_STEP)
    v_hat = v / (1.0 - ADAM_B2 ** ADAM_STEP)
    delta = -ADAM_LR * (m_hat / (jnp.sqrt(v_hat) + ADAM_EPS) + ADAM_WD * w)
    return delta, m, v


def _adamw(name, w, m, v, grad_parts):
    r, c = w.shape
    tr = _tile(r, 256)
    n_parts = len(grad_parts)

    def body(*refs):
        w_ref, m_ref, v_ref = refs[:3]
        part_refs = refs[3:3 + n_parts]
        g_ref, d_ref, nm_ref, nv_ref = refs[3 + n_parts:]
        g = part_refs[0][...].astype(jnp.float32)
        for p in part_refs[1:]:
            g = g + p[...].astype(jnp.float32)
        delta, nm, nv = _adamw_math(w_ref[...], g, m_ref[...], v_ref[...])
        g_ref[...] = g
        d_ref[...] = delta
        nm_ref[...] = nm
        nv_ref[...] = nv

    spec = _row_spec(tr, c)
    out = jax.ShapeDtypeStruct((r, c), jnp.float32)
    return pl.pallas_call(
        body, name=name, grid=(r // tr,), in_specs=[spec] * (3 + n_parts), out_specs=[spec] * 4,
        out_shape=[out] * 4, compiler_params=_params(("arbitrary",)),
    )(w, m, v, *grad_parts)


def _chip_peers():
    x, y, c = lax.axis_index("x"), lax.axis_index("y"), lax.axis_index("c")
    return x, y, c, [(1 - x, y), (x, 1 - y), (1 - x, 1 - y)]


HBM = pl.BlockSpec(memory_space=pltpu.HBM)
SEM = pl.BlockSpec(memory_space=pltpu.SEMAPHORE)
TOKEN = jax.ShapeDtypeStruct((8, 128), jnp.float32)
N_PEER_CHIPS = N_CHIPS - 1


def _in_flight():
    return pltpu.CompilerParams(has_side_effects=pltpu.SideEffectType.DATAFLOW_SIDE_EFFECTING)


def _slot(ref, slot):
    if len(ref.shape) == 3:
        return ref.at[slot]
    width = ref.shape[1] // N_CHIPS
    return ref.at[:, pl.ds(pl.multiple_of(slot * width, 128), width)]


def _half_slot(ref, slot, half):
    rows = ref.shape[-2] // 2
    if len(ref.shape) == 3:
        return ref.at[slot, pl.ds(half * rows, rows)]
    width = ref.shape[1] // N_CHIPS
    return ref.at[pl.ds(half * rows, rows), pl.ds(pl.multiple_of(slot * width, 128), width)]


def _slot_shape(shape):
    return shape[1:] if len(shape) == 3 else (shape[0], shape[1] // N_CHIPS)


def _gather_start(name, full, deps=()):
    def body(full_ref, *rest):
        send_sems, recv_sems, _, token_ref = rest[len(deps):]
        x, y, c, peers = _chip_peers()
        mine = _half_slot(full_ref, 2 * x + y, c)
        for p, (px, py) in enumerate(peers):
            pltpu.make_async_remote_copy(
                src_ref=mine, dst_ref=mine, send_sem=send_sems.at[p], recv_sem=recv_sems.at[p],
                device_id=(px, py, c), device_id_type=MESH).start()
        token_ref[...] = jnp.zeros_like(token_ref)

    return pl.pallas_call(
        body, name=name,
        out_shape=(pltpu.SemaphoreType.DMA((N_PEER_CHIPS,)), pltpu.SemaphoreType.DMA((N_PEER_CHIPS,)),
                   pltpu.HBM(full.shape, full.dtype), TOKEN),
        in_specs=[HBM] + [ANY] * len(deps), out_specs=(SEM, SEM, HBM, pl.BlockSpec(memory_space=pltpu.VMEM)),
        input_output_aliases={0: 2}, compiler_params=_in_flight(),
    )(pltpu.with_memory_space_constraint(full, pltpu.HBM), *deps)


def _gather_wait(name, send_sems, recv_sems, full, after):
    def body(full_ref, send_sems, recv_sems, after_ref, out_ref):
        x, y, c, peers = _chip_peers()
        for p, (px, py) in enumerate(peers):
            cp = pltpu.make_async_remote_copy(
                src_ref=_half_slot(full_ref, 2 * x + y, c), dst_ref=_half_slot(full_ref, 2 * px + py, c),
                send_sem=send_sems.at[p], recv_sem=recv_sems.at[p], device_id=(px, py, c), device_id_type=MESH)
            cp.wait_send()
            cp.wait_recv()

    return pl.pallas_call(
        body, name=name, out_shape=pltpu.HBM(full.shape, full.dtype),
        in_specs=(HBM, SEM, SEM, ANY), out_specs=HBM, input_output_aliases={0: 0}, compiler_params=_in_flight(),
    )(full, send_sems, recv_sems, after)


def _forward_sibling(name, fulls):
    n = len(fulls)

    def body(*refs):
        in_refs, out_refs = refs[:n], refs[n:2 * n]
        send_sems, recv_sems = refs[2 * n:]
        x, y, c, peers = _chip_peers()
        sends, recvs = [], []
        for a in range(n):
            for p, (px, py) in enumerate(peers):
                k = N_PEER_CHIPS * a + p
                slot = 2 * px + py
                cp = pltpu.make_async_remote_copy(
                    src_ref=_half_slot(in_refs[a], slot, c), dst_ref=_half_slot(out_refs[a], slot, c),
                    send_sem=send_sems.at[k], recv_sem=recv_sems.at[k], device_id=(x, y, 1 - c), device_id_type=MESH)
                cp.start()
                sends.append(cp)
                recvs.append(pltpu.make_async_remote_copy(
                    src_ref=_half_slot(in_refs[a], slot, c), dst_ref=_half_slot(out_refs[a], slot, 1 - c),
                    send_sem=send_sems.at[k], recv_sem=recv_sems.at[k], device_id=(x, y, 1 - c), device_id_type=MESH))
        for cp in recvs:
            cp.wait_recv()
        for cp in sends:
            cp.wait_send()

    return pl.pallas_call(
        body, name=name, in_specs=[ANY] * n, out_specs=[ANY] * n,
        out_shape=[jax.ShapeDtypeStruct(f.shape, f.dtype) for f in fulls],
        input_output_aliases={a: a for a in range(n)},
        scratch_shapes=[pltpu.SemaphoreType.DMA((N_PEER_CHIPS * n,)), pltpu.SemaphoreType.DMA((N_PEER_CHIPS * n,))],
    )(*fulls)


def _scatter_start(name, grads, deps=()):
    def body(g_ref, land_ref, *rest):
        send_sems, recv_sems, _, _, token_ref = rest[len(deps):]
        x, y, c, peers = _chip_peers()
        me = 2 * x + y
        for p, (px, py) in enumerate(peers):
            pltpu.make_async_remote_copy(
                src_ref=_slot(g_ref, 2 * px + py), dst_ref=land_ref.at[me], send_sem=send_sems.at[p],
                recv_sem=recv_sems.at[p], device_id=(px, py, c), device_id_type=MESH).start()
        token_ref[...] = jnp.zeros_like(token_ref)

    land = lax.empty((N_CHIPS,) + _slot_shape(grads.shape), grads.dtype)
    return pl.pallas_call(
        body, name=name,
        out_shape=(pltpu.SemaphoreType.DMA((N_PEER_CHIPS,)), pltpu.SemaphoreType.DMA((N_PEER_CHIPS,)),
                   pltpu.HBM(grads.shape, grads.dtype), pltpu.HBM(land.shape, land.dtype), TOKEN),
        in_specs=[HBM, HBM] + [ANY] * len(deps),
        out_specs=(SEM, SEM, HBM, HBM, pl.BlockSpec(memory_space=pltpu.VMEM)),
        input_output_aliases={0: 2, 1: 3}, compiler_params=_in_flight(),
    )(pltpu.with_memory_space_constraint(grads, pltpu.HBM), pltpu.with_memory_space_constraint(land, pltpu.HBM), *deps)


def _scatter_wait(name, send_sems, recv_sems, grads, land, after):
    def body(g_ref, land_ref, send_sems, recv_sems, after_ref, g_out, land_out):
        x, y, c, peers = _chip_peers()
        for p, (px, py) in enumerate(peers):
            cp = pltpu.make_async_remote_copy(
                src_ref=_slot(g_ref, 2 * px + py), dst_ref=land_ref.at[2 * px + py], send_sem=send_sems.at[p],
                recv_sem=recv_sems.at[p], device_id=(px, py, c), device_id_type=MESH)
            cp.wait_send()
            cp.wait_recv()

    return pl.pallas_call(
        body, name=name,
        out_shape=(pltpu.HBM(grads.shape, grads.dtype), pltpu.HBM(land.shape, land.dtype)),
        in_specs=(HBM, HBM, SEM, SEM, ANY), out_specs=(HBM, HBM), input_output_aliases={0: 0, 1: 1},
        compiler_params=_in_flight(),
    )(grads, land, send_sems, recv_sems, after)


def _swap_sibling(name, parts):
    n = len(parts)

    def body(*refs):
        in_refs, out_refs = refs[:n], refs[n:2 * n]
        send_sems, recv_sems = refs[2 * n:]
        x, y, c = lax.axis_index("x"), lax.axis_index("y"), lax.axis_index("c")
        copies = []
        for a in range(n):
            cp = pltpu.make_async_remote_copy(
                src_ref=in_refs[a], dst_ref=out_refs[a], send_sem=send_sems.at[a], recv_sem=recv_sems.at[a],
                device_id=(x, y, 1 - c), device_id_type=MESH)
            cp.start()
            copies.append(cp)
        for cp in copies:
            cp.wait_recv()
        for cp in copies:
            cp.wait_send()

    return pl.pallas_call(
        body, name=name, in_specs=[ANY] * n, out_specs=[ANY] * n,
        out_shape=[jax.ShapeDtypeStruct(p.shape, p.dtype) for p in parts],
        scratch_shapes=[pltpu.SemaphoreType.DMA((n,)), pltpu.SemaphoreType.DMA((n,))],
    )(*parts)


def _gather_devices(name, block):
    def body(in_ref, out_ref, send_sems, recv_sems, local_sem):
        x, y, c = lax.axis_index("x"), lax.axis_index("y"), lax.axis_index("c")
        me = 4 * x + 2 * y + c
        local = pltpu.make_async_copy(in_ref, out_ref.at[me], local_sem)
        local.start()
        sends, recvs = [], []
        k = 0
        for fx in range(2):
            for fy in range(2):
                for fc in range(2):
                    if fx == fy == fc == 0:
                        continue
                    px = x if fx == 0 else 1 - x
                    py = y if fy == 0 else 1 - y
                    pc = c if fc == 0 else 1 - c
                    cp = pltpu.make_async_remote_copy(
                        src_ref=in_ref, dst_ref=out_ref.at[me], send_sem=send_sems.at[k], recv_sem=recv_sems.at[k],
                        device_id=(px, py, pc), device_id_type=MESH)
                    cp.start()
                    sends.append(cp)
                    recvs.append(pltpu.make_async_remote_copy(
                        src_ref=in_ref, dst_ref=out_ref.at[4 * px + 2 * py + pc], send_sem=send_sems.at[k],
                        recv_sem=recv_sems.at[k], device_id=(px, py, pc), device_id_type=MESH))
                    k += 1
        for cp in recvs:
            cp.wait_recv()
        for cp in sends:
            cp.wait_send()
        local.wait()

    return pl.pallas_call(
        body, name=name, in_specs=[ANY], out_specs=ANY,
        out_shape=jax.ShapeDtypeStruct((N_DEV,) + block.shape, block.dtype),
        scratch_shapes=[pltpu.SemaphoreType.DMA((N_DEV - 1,)), pltpu.SemaphoreType.DMA((N_DEV - 1,)),
                        pltpu.SemaphoreType.DMA],
    )(block)


def _pack_rows(pieces, width):
    flat = jnp.concatenate([p.reshape(-1) for p in pieces])
    rows = -(-flat.shape[0] // width)
    rows = -(-rows // 8) * 8
    flat = jnp.pad(flat, (0, rows * width - flat.shape[0]))
    return flat.reshape(rows, width)


def _unpack_rows(packed, shapes):
    flat = packed.reshape(-1)
    out, off = [], 0
    for s in shapes:
        size = 1
        for d in s:
            size *= d
        out.append(flat[off:off + size].reshape(s))
        off += size
    return out


def kernel(x, ln_mix_pre, w_in, conv_w, pool_w, pool_scale, w_out, ln_mix_post, ln_ffn_pre, w_gate, w_up, w_down, ln_ffn_post, loss_target, m_ln_mix_pre, m_w_in, m_conv_w, m_pool_w, m_pool_scale, m_w_out, m_ln_mix_post, m_ln_ffn_pre, m_w_gate, m_w_up, m_w_down, m_ln_ffn_post, v_ln_mix_pre, v_w_in, v_conv_w, v_pool_w, v_pool_scale, v_w_out, v_ln_mix_post, v_ln_ffn_pre, v_w_gate, v_w_up, v_w_down, v_ln_ffn_post):
    t, d = x.shape[1], x.shape[2]
    e4 = w_in.shape[2]
    e = N_CHIPS * e4
    f4 = w_gate.shape[2]
    f = N_CHIPS * f4
    n_groups, dg4, dg = pool_w.shape[1], pool_w.shape[2], pool_w.shape[3]
    cw4 = conv_w.shape[2]
    chip = 2 * lax.axis_index("x") + lax.axis_index("y")
    xs, tgt = x[0], loss_target[0]

    big = {"w_in": w_in[0], "w_out": w_out[0], "w_gate": w_gate[0], "w_up": w_up[0], "w_down": w_down[0],
           "pool_w": pool_w[0].reshape(n_groups * dg4, dg)}
    names = ["w_in", "pool_w", "w_out", "w_gate", "w_up", "w_down"]
    chip_arr = chip.astype(jnp.int32).reshape(1)
    conv_all = _gather_devices("gather_conv_w", _pack_rows([conv_w[0]], 128))
    conv_full = jnp.concatenate(
        [conv_all[2 * j].reshape(-1)[:CONV_K * cw4].reshape(CONV_K, cw4) for j in range(N_CHIPS)], axis=1)
    in_flight, deps = {}, (conv_all,)
    for k in names:
        by_columns = k in ("w_gate", "w_up")
        in_flight[k] = _gather_start(
            "gather_start_" + k, _cast_to_slot("cast_" + k, big[k], chip_arr, by_columns), deps)
        deps = (in_flight[k][3],)

    def landed(ks, after):
        fulls = []
        for k in ks:
            send_sems, recv_sems, full, _ = in_flight[k]
            fulls.append(_gather_wait("gather_wait_" + k, send_sems, recv_sems, full, after))
            after = fulls[-1]
        return _forward_sibling("forward_" + ks[0], fulls)

    def rows3(tile, width):
        return pl.BlockSpec((tile, width), lambda i, j, k: (i, 0))

    gain3 = pl.BlockSpec((1, d), lambda i, j, k: (0, 0))
    f32_td = jax.ShapeDtypeStruct((t, d), jnp.float32)
    mm_td = jax.ShapeDtypeStruct((t, d), MM_DTYPE)
    f32_gain = jax.ShapeDtypeStruct((1, d), jnp.float32)

    h = _pre_norm(xs, ln_mix_pre, deps)
    (win_g,) = landed(["w_in"], h)
    tm = _tile(t, 1024)
    tm2 = _tile(t, 2048)
    proj = _matmul(
        "in_proj", grid=(t // tm2, N_CHIPS, 1), mode="nn",
        pairs=[[(h, win_g)]],
        pair_specs=[[(rows3(tm2, d), pl.BlockSpec((None, d, e4), lambda i, j, k: (j, 0, 0)))]],
        acc_shapes=[(tm2, e4)], out_shapes=[jax.ShapeDtypeStruct((t, e), jnp.float32)],
        out_specs=[pl.BlockSpec((tm2, e4), lambda i, j, k: (i, j))], epilogue=_identity_epilogue)[0]
    pool_g, wout_full = landed(["pool_w", "w_out"], proj)
    pool_g, wout_full = pool_g.reshape(N_CHIPS, n_groups, dg4, dg), wout_full.reshape(d, d)
    mixed = _mixers_fwd(proj, conv_full, pool_g, pool_scale)

    def post_mix_epilogue(accs, extra_refs):
        x_ref, g2_ref, g3_ref = extra_refs
        mo = accs[0]
        x1 = x_ref[...] + mo * _rstd(mo) * g2_ref[...]
        return mo, x1, x1 * _rstd(x1) * g3_ref[...]

    tm_mix = _tile(t, 256)
    mix_out, x1, hf = _matmul(
        "out_proj", grid=(t // tm_mix, 1, 1), mode="nn",
        pairs=[[(mixed, wout_full)]],
        pair_specs=[[(rows3(tm_mix, d), pl.BlockSpec((d, d), lambda i, j, k: (0, 0)))]],
        acc_shapes=[(tm_mix, d)], extras=[xs, ln_mix_post, ln_ffn_pre],
        extra_specs=[rows3(tm_mix, d), gain3, gain3], out_shapes=[f32_td, f32_td, mm_td],
        out_specs=[rows3(tm_mix, d)] * 3, epilogue=post_mix_epilogue, epi_rows=EPILOGUE_ROWS)
    wg_full, wu_full = landed(["w_gate", "w_up"], hf)

    def gate_up_epilogue(accs, extra_refs):
        g, up = accs
        sig = jax.nn.sigmoid(g)
        silu = g * sig
        return up * (sig + silu * (1.0 - sig)), silu, silu * up

    tf = _tile(f, 512)
    ff_tile = jax.ShapeDtypeStruct((t, f), MM_DTYPE)
    act_by_g, act_by_up, act = _matmul(
        "gate_up", grid=(f // tf, t // tm, 1), mode="nn",
        pairs=[[(hf, wg_full)], [(hf, wu_full)]],
        pair_specs=[[(pl.BlockSpec((tm, d), lambda j, i, k: (i, 0)),
                      pl.BlockSpec((d, tf), lambda j, i, k: (0, j)))]] * 2,
        acc_shapes=[(tm, tf)] * 2, out_shapes=[ff_tile] * 3,
        out_specs=[pl.BlockSpec((tm, tf), lambda j, i, k: (i, j))] * 3, epilogue=gate_up_epilogue)
    wdown_full = landed(["w_down"], act)[0].reshape(f, d)

    def loss_epilogue(accs, extra_refs):
        x1_ref, tg_ref, g_ref = extra_refs
        ff_v = accs[0]
        rstd = _rstd(ff_v)
        n = ff_v * rstd
        g = g_ref[...]
        err = x1_ref[...] + n * g - tg_ref[...]
        rows_loss = 0.5 * jnp.sum(jnp.mean(err * err, axis=-1, keepdims=True), axis=0, keepdims=True)
        dout = err / d
        return (dout, _norm_bwd(dout * g, n, rstd), jnp.broadcast_to(rows_loss, (8, 128)),
                jnp.sum(dout * n, axis=0, keepdims=True))

    tm_row = _tile(t, 1024)
    dout, dff, loss_tile, dg_ffn_post = _matmul(
        "down_proj", grid=(t // tm_row, 1, f // tf), mode="nn",
        pairs=[[(act, wdown_full)]],
        pair_specs=[[(pl.BlockSpec((tm_row, tf), lambda i, j, k: (i, k)),
                      pl.BlockSpec((tf, d), lambda i, j, k: (k, 0)))]],
        acc_shapes=[(tm_row, d)], extras=[x1, tgt, ln_ffn_post], extra_specs=[STREAM] * 2 + [gain3],
        out_shapes=[f32_td, mm_td, jax.ShapeDtypeStruct((8, 128), jnp.float32), f32_gain],
        out_specs=[STREAM] * 2 + [pl.BlockSpec((8, 128), lambda i, j, k: (0, 0)), gain3],
        epilogue=loss_epilogue, epi_rows=EPILOGUE_ROWS, sum_outs=(2, 3))

    def dact_epilogue(accs, extra_refs):
        dact = accs[0]
        return dact * extra_refs[0][...].astype(jnp.float32), dact * extra_refs[1][...].astype(jnp.float32)

    ff_spec_ij = pl.BlockSpec((tm, tf), lambda i, j, k: (i, j))
    dg_act, dup_act = _matmul(
        "dact", grid=(t // tm, f // tf, 1), mode="nt",
        pairs=[[(dff, wdown_full)]],
        pair_specs=[[(rows3(tm, d), pl.BlockSpec((tf, d), lambda i, j, k: (j, 0)))]],
        acc_shapes=[(tm, tf)], extras=[act_by_g, act_by_up], extra_specs=[ff_spec_ij, ff_spec_ij],
        out_shapes=[ff_tile] * 2, out_specs=[ff_spec_ij] * 2, epilogue=dact_epilogue)
    tk = _tile(t, 1024)
    dw_down = _matmul(
        "dw_down", grid=(N_CHIPS, 1, t // tk), mode="tn",
        pairs=[[(act, dff)]],
        pair_specs=[[(pl.BlockSpec((tk, f4), lambda i, j, k: (k, i)),
                      pl.BlockSpec((tk, d), lambda i, j, k: (k, 0)))]],
        acc_shapes=[(f4, d)], out_shapes=[jax.ShapeDtypeStruct((N_CHIPS, f4, d), COMM_DTYPE)],
        out_specs=[pl.BlockSpec((None, f4, d), lambda i, j, k: (i, 0, 0))], epilogue=_identity_epilogue)[0]
    tn = _tile(d, 1024)
    leaving = {"w_down": _scatter_start("scatter_start_w_down", dw_down)}

    def ffn_pre_epilogue(accs, extra_refs):
        x1_ref, dout_ref, mo_ref, g3_ref, g2_ref = extra_refs
        dhf_v = accs[0]
        x1_v = x1_ref[...]
        rstd3 = _rstd(x1_v)
        n3 = x1_v * rstd3
        dx1_v = dout_ref[...] + _norm_bwd(dhf_v * g3_ref[...], n3, rstd3)
        mo = mo_ref[...]
        rstd2 = _rstd(mo)
        n2 = mo * rstd2
        return (dx1_v, _norm_bwd(dx1_v * g2_ref[...], n2, rstd2), jnp.sum(dhf_v * n3, axis=0, keepdims=True),
                jnp.sum(dx1_v * n2, axis=0, keepdims=True))

    dx1, dmo, dg_ffn_pre, dg_mix_post = _matmul(
        "dhf", grid=(t // tm_row, 1, f // tf), mode="nt",
        pairs=[[(dg_act, wg_full), (dup_act, wu_full)]],
        pair_specs=[[(pl.BlockSpec((tm_row, tf), lambda i, j, k: (i, k)),
                      pl.BlockSpec((d, tf), lambda i, j, k: (0, k)))] * 2],
        acc_shapes=[(tm_row, d)], extras=[x1, dout, mix_out, ln_ffn_pre, ln_mix_post],
        extra_specs=[STREAM] * 3 + [gain3] * 2, out_shapes=[f32_td, mm_td, f32_gain, f32_gain],
        out_specs=[STREAM] * 2 + [gain3] * 2, epilogue=ffn_pre_epilogue, epi_rows=EPILOGUE_ROWS,
        sum_outs=(2, 3), deps=leaving["w_down"][4:])
    tmo = _tile(d, 2048)
    grad_ff = jax.ShapeDtypeStruct((d, f), COMM_DTYPE)
    dw_gate, dw_up = _matmul(
        "dw_gate_up", grid=(d // tmo, f // tf, t // tm2), mode="tn",
        pairs=[[(hf, dg_act)], [(hf, dup_act)]],
        pair_specs=[[(pl.BlockSpec((tm2, tmo), lambda i, j, k: (k, i)),
                      pl.BlockSpec((tm2, tf), lambda i, j, k: (k, j)))]] * 2,
        acc_shapes=[(tmo, tf)] * 2, out_shapes=[grad_ff] * 2,
        out_specs=[pl.BlockSpec((tmo, tf), lambda i, j, k: (i, j))] * 2, epilogue=_identity_epilogue)
    leaving["w_gate"] = _scatter_start("scatter_start_w_gate", dw_gate)
    leaving["w_up"] = _scatter_start("scatter_start_w_up", dw_up, leaving["w_gate"][4:])

    dmixed = _matmul(
        "dmixed", grid=(t // tm, 1, 1), mode="nt",
        pairs=[[(dmo, wout_full)]],
        pair_specs=[[(rows3(tm, d), pl.BlockSpec((d, d), lambda i, j, k: (0, 0)))]],
        acc_shapes=[(tm, d)], out_shapes=[f32_td],
        out_specs=[rows3(tm, d)], epilogue=_identity_epilogue,
        deps=leaving["w_up"][4:])[0]
    tmo = _tile(d, 2048)
    dw_out = _matmul(
        "dw_out", grid=(d // tmo, d // tn, t // tk), mode="tn",
        pairs=[[(mixed, dmo)]],
        pair_specs=[[(pl.BlockSpec((tk, tmo), lambda i, j, k: (k, i)),
                      pl.BlockSpec((tk, tn), lambda i, j, k: (k, j)))]],
        acc_shapes=[(tmo, tn)], out_shapes=[jax.ShapeDtypeStruct((d, d), COMM_DTYPE)],
        out_specs=[pl.BlockSpec((tmo, tn), lambda i, j, k: (i, j))], epilogue=_identity_epilogue)[0]
    leaving["w_out"] = _scatter_start("scatter_start_w_out", dw_out.reshape(N_CHIPS, d // N_CHIPS, d))
    dproj, dconv_full, dpool_g, dpool_scale = _mixers_bwd(proj, dmixed, conv_full, pool_g, pool_scale,
                                                          leaving["w_out"][4:])
    dpool_slots = _cast_rows("cast_dpool", dpool_g.reshape(N_CHIPS * n_groups * dg4, dg), COMM_DTYPE)
    leaving["pool_w"] = _scatter_start("scatter_start_pool_w", dpool_slots.reshape(N_CHIPS, n_groups * dg4, dg))
    dw_in = _matmul(
        "dw_in", grid=(d // tmo, N_CHIPS, t // tk), mode="tn",
        pairs=[[(h, dproj)]],
        pair_specs=[[(pl.BlockSpec((tk, tmo), lambda i, j, k: (k, i)),
                      pl.BlockSpec((tk, e4), lambda i, j, k: (k, j)))]],
        acc_shapes=[(tmo, e4)], out_shapes=[jax.ShapeDtypeStruct((N_CHIPS, d, e4), COMM_DTYPE)],
        out_specs=[pl.BlockSpec((None, tmo, e4), lambda i, j, k: (j, i, 0))], epilogue=_identity_epilogue,
        deps=leaving["pool_w"][4:])[0]
    leaving["w_in"] = _scatter_start("scatter_start_w_in", dw_in)

    def mix_pre_epilogue(accs, extra_refs):
        x_ref, dx1_ref, g_ref = extra_refs
        dh_v = accs[0]
        xv = x_ref[...]
        rstd = _rstd(xv)
        n = xv * rstd
        return dx1_ref[...] + _norm_bwd(dh_v * g_ref[...], n, rstd), jnp.sum(dh_v * n, axis=0, keepdims=True)

    grad_x, dg_mix_pre = _matmul(
        "dh", grid=(t // tm_row, 1, N_CHIPS), mode="nt",
        pairs=[[(dproj, win_g)]],
        pair_specs=[[(pl.BlockSpec((tm_row, e4), lambda i, j, k: (i, k)),
                      pl.BlockSpec((None, d, e4), lambda i, j, k: (k, 0, 0)))]],
        acc_shapes=[(tm_row, d)], extras=[xs, dx1, ln_mix_pre], extra_specs=[STREAM] * 2 + [gain3],
        out_shapes=[f32_td, f32_gain], out_specs=[STREAM, gain3], epilogue=mix_pre_epilogue,
        epi_rows=EPILOGUE_ROWS, sum_outs=(1,), deps=leaving["w_in"][4:], prefetch_extras=True)

    names = ["w_down", "w_gate", "w_up", "w_out", "pool_w", "w_in"]
    partial, after = [], grad_x
    for k in names:
        send_sems, recv_sems, own, land, _ = leaving[k]
        own, land = _scatter_wait("scatter_wait_" + k, send_sems, recv_sems, own, land, after)
        partial.append(_sum_own_and_received("sum_" + k, own, land, chip_arr))
        after = land
    other = _swap_sibling("swap_grads", partial)
    moments = {"w_in": (m_w_in, v_w_in), "w_out": (m_w_out, v_w_out), "w_gate": (m_w_gate, v_w_gate),
               "w_up": (m_w_up, v_w_up), "w_down": (m_w_down, v_w_down), "pool_w": (m_pool_w, v_pool_w)}
    result = {}
    for k, mine, theirs in zip(names, partial, other):
        shape = moments[k][0].shape
        two_d = big[k].shape
        outs = _adamw("adamw_" + k, big[k], moments[k][0].reshape(two_d), moments[k][1].reshape(two_d),
                      [mine, theirs])
        result[k] = [o.reshape(shape) for o in outs]

    small_shapes = [(1, d)] * 4 + [pool_scale.shape, (CONV_K, N_CHIPS * cw4)]
    packed = _pack_rows([dg_mix_pre, dg_mix_post, dg_ffn_pre, dg_ffn_post, dpool_scale, dconv_full], 1024)
    summed = _sum_slots("sum_small", _gather_devices("gather_small", packed))
    g_mix_pre, g_mix_post, g_ffn_pre, g_ffn_post, g_pool_scale, g_conv_full = _unpack_rows(summed, small_shapes)
    g_conv = lax.dynamic_slice(g_conv_full, (0, chip * cw4), (CONV_K, cw4))[None]
    small = [("ln_mix_pre", ln_mix_pre, m_ln_mix_pre, v_ln_mix_pre, g_mix_pre),
             ("conv_w", conv_w, m_conv_w, v_conv_w, g_conv),
             ("pool_scale", pool_scale, m_pool_scale, v_pool_scale, g_pool_scale),
             ("ln_mix_post", ln_mix_post, m_ln_mix_post, v_ln_mix_post, g_mix_post),
             ("ln_ffn_pre", ln_ffn_pre, m_ln_ffn_pre, v_ln_ffn_pre, g_ffn_pre),
             ("ln_ffn_post", ln_ffn_post, m_ln_ffn_post, v_ln_ffn_post, g_ffn_post)]
    shapes_small = [s[1].shape for s in small]
    packs = [_pack_rows([s[q] for s in small], 128) for q in (1, 2, 3, 4)]
    outs = _adamw("adamw_small", packs[0], packs[1], packs[2], [packs[3]])
    unpacked = [_unpack_rows(o, shapes_small) for o in outs]
    for idx, s in enumerate(small):
        result[s[0]] = [u[idx] for u in unpacked]

    loss = lax.psum(loss_tile[0, 0], ("x", "y", "c"))
    order = ["ln_mix_pre", "w_in", "conv_w", "pool_w", "pool_scale", "w_out", "ln_mix_post", "ln_ffn_pre",
             "w_gate", "w_up", "w_down", "ln_ffn_post"]
    return (loss, grad_x[None], *[result[k][0] for k in order], *[result[k][1] for k in order],
            *[result[k][2] for k in order], *[result[k][3] for k in order])
```

```python
import functools

import jax
import jax.numpy as jnp
from jax import lax
from jax.experimental import pallas as pl
from jax.experimental.pallas import tpu as pltpu

EPS = 1e-6
CONV_HEAD_DIM = 128
CONV_K = 3
POOL_WINDOWS = (2, 4, 8, 16)
HALO = 16
EPILOGUE_ROWS = 64
STREAM_ROWS = 256
N_CHIPS = 4
N_DEV = 8

ADAM_LR = 0.001
ADAM_B1 = 0.9
ADAM_B2 = 0.999
ADAM_EPS = 1e-08
ADAM_WD = 0.01
ADAM_STEP = 10

MM_DTYPE = jnp.bfloat16
COMM_DTYPE = jnp.bfloat16
VMEM_LIMIT = 62 * 1024 * 1024
MESH = pl.DeviceIdType.MESH
ANY = pl.BlockSpec(memory_space=pl.ANY)
STREAM = "stream"


def _tile(n, pref):
    t = min(pref, n)
    while n % t:
        t //= 2
    return t


def _params(sem):
    return pltpu.CompilerParams(dimension_semantics=sem, vmem_limit_bytes=VMEM_LIMIT)


def _rstd(x):
    return lax.rsqrt(jnp.mean(x * x, axis=-1, keepdims=True) + EPS)


def _norm_bwd(dn, n, rstd):
    return rstd * (dn - n * jnp.mean(dn * n, axis=-1, keepdims=True))


_DOT_DIMS = {
    "nn": (((1,), (0,)), ((), ())),
    "nt": (((1,), (1,)), ((), ())),
    "tn": (((0,), (0,)), ((), ())),
}


def _dot(a, b, mode):
    return lax.dot_general(a.astype(MM_DTYPE), b.astype(MM_DTYPE), _DOT_DIMS[mode],
                           preferred_element_type=jnp.float32)


def _matmul(name, *, grid, mode, pairs, pair_specs, acc_shapes, extras=(), extra_specs=(),
            out_shapes, out_specs, epilogue, deps=(), epi_rows=0, sum_outs=(), prefetch_extras=False):
    nk = grid[2]
    operands, operand_specs, where, counts = [], [], {}, []
    pair_index = []
    for ps, ss in zip(pairs, pair_specs):
        counts.append(len(ps))
        for arrays, specs in zip(ps, ss):
            for arr, spec in zip(arrays, specs):
                key = (id(arr), id(spec))
                if key not in where:
                    where[key] = len(operands)
                    operands.append(arr)
                    operand_specs.append(spec)
                pair_index.append(where[key])
    n_operands = len(operands)
    n_extra = len(extras)
    n_out = len(out_shapes)
    n_in = n_operands + n_extra + len(deps)
    in_streams = [q for q, s in enumerate(extra_specs) if s is STREAM]
    out_streams = [q for q, s in enumerate(out_specs) if s is STREAM]
    in_buf_rows = (acc_shapes[0][0],) if prefetch_extras else (2, STREAM_ROWS)
    stream_bufs = ([pltpu.VMEM(in_buf_rows + (extras[q].shape[1],), extras[q].dtype) for q in in_streams]
                   + [pltpu.VMEM((2, STREAM_ROWS, out_shapes[q].shape[1]), out_shapes[q].dtype) for q in out_streams])
    n_streams = len(stream_bufs)
    n_acc = 0 if nk == 1 else len(acc_shapes)

    def body(*refs):
        pair_refs = [refs[q] for q in pair_index]
        extra_refs = refs[n_operands:n_operands + n_extra]
        out_refs = refs[n_in:n_in + n_out]
        acc_refs = refs[n_in + n_out:n_in + n_out + n_acc]
        bufs = refs[n_in + n_out + n_acc:n_in + n_out + n_acc + n_streams]
        i, k = pl.program_id(0), pl.program_id(2)

        def stream_copies():
            sems = refs[-1]
            n_rows = acc_refs[0].shape[0]

            def hbm_rows(ref, c):
                return ref.at[pl.ds(pl.multiple_of(i * n_rows + c * STREAM_ROWS, STREAM_ROWS), STREAM_ROWS)]

            def fetch(s, c):
                if prefetch_extras:
                    return pltpu.make_async_copy(
                        extra_refs[in_streams[s]].at[pl.ds(pl.multiple_of(i * n_rows, STREAM_ROWS), n_rows)],
                        bufs[s], sems.at[s, 0])
                return pltpu.make_async_copy(hbm_rows(extra_refs[in_streams[s]], c), bufs[s].at[c % 2],
                                             sems.at[s, c % 2])

            def drain(s, c):
                return pltpu.make_async_copy(bufs[len(in_streams) + s].at[c % 2], hbm_rows(out_refs[out_streams[s]], c),
                                             sems.at[len(in_streams) + s, c % 2])

            return fetch, drain

        def streamed_finish(accs):
            fetch, drain = stream_copies()
            n_chunks = accs[0].shape[0] // STREAM_ROWS
            for c in range(n_chunks):
                for s in range(len(in_streams)):
                    if c + 1 < n_chunks and not prefetch_extras:
                        fetch(s, c + 1).start()
                    if c == 0 or not prefetch_extras:
                        fetch(s, c).wait()
                for s in range(len(out_streams)):
                    if c >= 2:
                        drain(s, c - 2).wait()
                for r0 in range(0, STREAM_ROWS, epi_rows):
                    sub = slice(r0, r0 + epi_rows)
                    rows = slice(c * STREAM_ROWS + r0, c * STREAM_ROWS + r0 + epi_rows)
                    views = [(bufs[in_streams.index(q)].at[rows] if prefetch_extras
                              else bufs[in_streams.index(q)].at[c % 2, sub]) if q in in_streams else e
                             for q, e in enumerate(extra_refs)]
                    outs = epilogue([a[rows, :] for a in accs], views)
                    for q, (o_ref, o) in enumerate(zip(out_refs, outs)):
                        if q in out_streams:
                            bufs[len(in_streams) + out_streams.index(q)][c % 2, sub, :] = o.astype(o_ref.dtype)
                        elif c == 0 and r0 == 0:
                            _accumulate(o_ref, o, i)
                        else:
                            o_ref[...] += o
                for s in range(len(out_streams)):
                    drain(s, c).start()
            for s in range(len(out_streams)):
                for c in range(max(n_chunks - 2, 0), n_chunks):
                    drain(s, c).wait()

        def partial_sums():
            res, p = [], 0
            for cnt in counts:
                tot = None
                for _ in range(cnt):
                    d = _dot(pair_refs[p][...], pair_refs[p + 1][...], mode)
                    tot = d if tot is None else tot + d
                    p += 2
                res.append(tot)
            return res

        def finish(accs):
            n_rows = accs[0].shape[0]
            step = epi_rows or n_rows
            for r0 in range(0, n_rows, step):
                rows = slice(r0, r0 + step)
                outs = epilogue([a[rows, :] for a in accs], [e.at[rows] if e.shape[0] == n_rows else e
                                                             for e in extra_refs])
                for q, (o_ref, o) in enumerate(zip(out_refs, outs)):
                    if q not in sum_outs:
                        o_ref[rows, :] = o.astype(o_ref.dtype)
                    elif r0 == 0:
                        _accumulate(o_ref, o, i)
                    else:
                        o_ref[...] += o

        if nk == 1:
            finish(partial_sums())
        else:
            if n_streams:
                @pl.when(k == (0 if prefetch_extras else nk - 1))
                def _():
                    for s in range(len(in_streams)):
                        stream_copies()[0](s, 0).start()

            @pl.when(k == 0)
            def _():
                for acc_ref, s in zip(acc_refs, partial_sums()):
                    acc_ref[...] = s

            @pl.when(k > 0)
            def _():
                for acc_ref, s in zip(acc_refs, partial_sums()):
                    acc_ref[...] += s

            @pl.when(k == nk - 1)
            def _():
                (streamed_finish if n_streams else finish)(acc_refs)

    scratch = [] if nk == 1 else [pltpu.VMEM(s, jnp.float32) for s in acc_shapes]
    if n_streams:
        assert nk > 1 and epi_rows and not any(q in sum_outs for q in out_streams)
        scratch = scratch + stream_bufs + [pltpu.SemaphoreType.DMA((n_streams, 2))]
    return pl.pallas_call(
        body, name=name, grid=grid,
        in_specs=operand_specs + [ANY if s is STREAM else s for s in extra_specs] + [ANY] * len(deps),
        out_specs=[ANY if s is STREAM else s for s in out_specs],
        out_shape=list(out_shapes), scratch_shapes=scratch,
        compiler_params=_params(("arbitrary", "arbitrary", "arbitrary")),
    )(*operands, *extras, *deps)


def _identity_epilogue(accs, extra_refs):
    return tuple(accs)


def _row_spec(tr, n):
    return pl.BlockSpec((tr, n), lambda i: (i, 0))


def _const_spec(shape):
    return pl.BlockSpec(shape, lambda i: tuple(0 for _ in shape))


def _accumulate(ref, val, i):
    @pl.when(i == 0)
    def _():
        ref[...] = val

    @pl.when(i > 0)
    def _():
        ref[...] += val


def _pre_norm(x, gain, deps=()):
    t, d = x.shape
    tr = _tile(t, 512)

    def body(x_ref, g_ref, *rest):
        h_ref = rest[-1]
        xv = x_ref[...]
        h_ref[...] = (xv * _rstd(xv) * g_ref[...]).astype(h_ref.dtype)

    return pl.pallas_call(
        body, name="pre_norm", grid=(t // tr,),
        in_specs=[_row_spec(tr, d), _const_spec((1, d))] + [ANY] * len(deps), out_specs=_row_spec(tr, d),
        out_shape=jax.ShapeDtypeStruct((t, d), MM_DTYPE), compiler_params=_params(("arbitrary",)),
    )(x, gain, *deps)


def _pool_matrix(pool_ref, g):
    return jnp.concatenate([pool_ref[c, g] for c in range(N_CHIPS)], axis=0)


def _inv_count(row0, n, w):
    pos = (row0 + lax.broadcasted_iota(jnp.int32, (n, 1), 0) + 1).astype(jnp.float32)
    return 1.0 / jnp.minimum(pos, float(w))


def _conv_piece(cu_buf, start, n, b_piece, convw_ref):
    conv = None
    for k in range(CONV_K):
        term = convw_ref[k:k + 1, :] * cu_buf[pl.ds(HALO + start + k - (CONV_K - 1), n), :]
        conv = term if conv is None else conv + term
    return conv, b_piece * conv


def _head_stats(a, width):
    return [_rstd(a[:, h * width:(h + 1) * width]) for h in range(a.shape[1] // width)]


def _pooled_piece(v_buf, start, n, v_piece, row0, dg):
    outs = []
    for gi, w in enumerate(POOL_WINDOWS):
        cols = slice(gi * dg, (gi + 1) * dg)
        win = None
        for k in range(w):
            term = v_buf[pl.ds(HALO + start - k, n), cols]
            win = term if win is None else win + term
        outs.append(win * _inv_count(row0 + start, n, w) - v_piece[:, cols])
    return outs


def _halo_specs(t, tr, width, col):
    per = tr // HALO
    last = t // HALO - 1
    prev = pl.BlockSpec((HALO, width), lambda i: (jnp.maximum(i * per - 1, 0), col))
    nxt = pl.BlockSpec((HALO, width), lambda i: (jnp.minimum((i + 1) * per, last), col))
    return prev, nxt


def _mixers_fwd(proj, conv_w, pool_g, pool_scale):
    t, e = proj.shape
    cw = e // 4
    dg = pool_g.shape[-1]
    tr = _tile(t, 256)

    def main(col):
        return pl.BlockSpec((tr, cw), lambda i: (i, col))

    def body(b_ref, c_ref, u_ref, v_ref, cp_ref, up_ref, vp_ref, convw_ref, pool_ref, scale_ref,
             out_ref, cu_buf, v_buf):
        i = pl.program_id(0)
        keep = (i > 0).astype(jnp.float32)
        cu_buf[pl.ds(0, HALO), :] = cp_ref[...] * up_ref[...] * keep
        cu_buf[pl.ds(HALO, tr), :] = c_ref[...] * u_ref[...]
        v_buf[pl.ds(0, HALO), :] = vp_ref[...] * keep
        v_buf[pl.ds(HALO, tr), :] = v_ref[...]
        _, a = _conv_piece(cu_buf, 0, tr, b_ref[...], convw_ref)
        for h, rstd in enumerate(_head_stats(a, CONV_HEAD_DIM)):
            cols = slice(h * CONV_HEAD_DIM, (h + 1) * CONV_HEAD_DIM)
            out_ref[:, cols] = (a[:, cols] * rstd).astype(out_ref.dtype)
        pooled = _pooled_piece(v_buf, 0, tr, v_ref[...], i * tr, dg)
        for gi, p in enumerate(pooled):
            z = _dot(p, _pool_matrix(pool_ref, gi), "nn")
            cols = slice(gi * dg, (gi + 1) * dg)
            out_ref[:, cw + gi * dg:cw + (gi + 1) * dg] = (z * _rstd(z) * scale_ref[:, cols]).astype(out_ref.dtype)

    prev_c, _ = _halo_specs(t, tr, cw, 1)
    prev_u, _ = _halo_specs(t, tr, cw, 2)
    prev_v, _ = _halo_specs(t, tr, cw, 3)
    return pl.pallas_call(
        body, name="mixers_fwd", grid=(t // tr,),
        in_specs=[main(0), main(1), main(2), main(3), prev_c, prev_u, prev_v,
                  _const_spec(conv_w.shape), _const_spec(pool_g.shape), _const_spec(pool_scale.shape)],
        out_specs=_row_spec(tr, 2 * cw),
        out_shape=jax.ShapeDtypeStruct((t, 2 * cw), MM_DTYPE),
        scratch_shapes=[pltpu.VMEM((tr + HALO, cw), jnp.float32), pltpu.VMEM((tr + HALO, cw), jnp.float32)],
        compiler_params=_params(("arbitrary",)),
    )(proj, proj, proj, proj, proj, proj, proj, conv_w, pool_g, pool_scale)


def _mixers_bwd(proj, dmixed, conv_w, pool_g, pool_scale, deps=()):
    t, e = proj.shape
    cw = e // 4
    dg = pool_g.shape[-1]
    n_groups = len(POOL_WINDOWS)
    tr = _tile(t, 256)
    n_tiles = t // tr
    ext = tr + 2 * HALO

    def main(col):
        return pl.BlockSpec((tr, cw), lambda i: (i, col))

    def body(b_ref, c_ref, u_ref, v_ref, dyc_ref, dyp_ref,
             cp_ref, up_ref, vp_ref,
             bn_ref, cn_ref, un_ref, vn_ref, dycn_ref, dypn_ref,
             convw_ref, pool_ref, scale_ref, *rest):
        (dproj_ref, dconvw_ref, dpool_ref, dscale_ref,
         cu_buf, v_buf, dconv_buf, dpn_buf, dpooled_buf) = rest[len(deps):]
        i = pl.program_id(0)
        keep_prev = (i > 0).astype(jnp.float32)
        keep_next = (i < n_tiles - 1).astype(jnp.float32)
        cu_buf[pl.ds(0, HALO), :] = cp_ref[...] * up_ref[...] * keep_prev
        cu_buf[pl.ds(HALO, tr), :] = c_ref[...] * u_ref[...]
        cu_buf[pl.ds(HALO + tr, HALO), :] = cn_ref[...] * un_ref[...]
        v_buf[pl.ds(0, HALO), :] = vp_ref[...] * keep_prev
        v_buf[pl.ds(HALO, tr), :] = v_ref[...]
        v_buf[pl.ds(HALO + tr, HALO), :] = vn_ref[...]

        def conv_piece(start, n, b_piece, dyc_piece, keep, is_main):
            conv, a = _conv_piece(cu_buf, start, n, b_piece, convw_ref)
            for h, rstd in enumerate(_head_stats(a, CONV_HEAD_DIM)):
                cols = slice(h * CONV_HEAD_DIM, (h + 1) * CONV_HEAD_DIM)
                da = _norm_bwd(dyc_piece[:, cols], a[:, cols] * rstd, rstd)
                dconv_buf[pl.ds(start, n), cols] = da * b_piece[:, cols] * keep
                if is_main:
                    dproj_ref[:, cols] = (da * conv[:, cols]).astype(dproj_ref.dtype)

        conv_piece(0, tr, b_ref[...], dyc_ref[...], 1.0, True)
        conv_piece(tr, HALO, bn_ref[...], dycn_ref[...], keep_next, False)

        dconv_main = dconv_buf[pl.ds(0, tr), :]
        dcu = None
        dw_rows = []
        for k in range(CONV_K):
            shift = CONV_K - 1 - k
            term = convw_ref[k:k + 1, :] * dconv_buf[pl.ds(shift, tr), :]
            dcu = term if dcu is None else dcu + term
            dw_rows.append(jnp.sum(dconv_main * cu_buf[pl.ds(HALO - shift, tr), :], axis=0, keepdims=True))
        dproj_ref[:, cw:2 * cw] = (dcu * u_ref[...]).astype(dproj_ref.dtype)
        dproj_ref[:, 2 * cw:3 * cw] = (dcu * c_ref[...]).astype(dproj_ref.dtype)
        _accumulate(dconvw_ref, jnp.concatenate(dw_rows, axis=0), i)

        def pool_piece(start, n, v_piece, dyp_piece, keep, is_main):
            pooled = _pooled_piece(v_buf, start, n, v_piece, i * tr, dg)
            dscale, dmats = [], []
            for gi, w in enumerate(POOL_WINDOWS):
                cols = slice(gi * dg, (gi + 1) * dg)
                mat = _pool_matrix(pool_ref, gi)
                z = _dot(pooled[gi], mat, "nn")
                rstd = _rstd(z)
                nz = z * rstd
                dyp_g = dyp_piece[:, cols]
                dz = _norm_bwd(dyp_g * scale_ref[:, cols], nz, rstd)
                dpooled = _dot(dz, mat, "nt") * keep
                dpn_buf[pl.ds(start, n), cols] = dpooled * _inv_count(i * tr + start, n, w)
                if is_main:
                    dpooled_buf[:, cols] = dpooled
                    dscale.append(jnp.sum(dyp_g * nz, axis=0, keepdims=True))
                    dmats.append(_dot(pooled[gi], dz, "tn"))
            return dscale, dmats

        dscale, dmats = pool_piece(0, tr, v_ref[...], dyp_ref[...], 1.0, True)
        pool_piece(tr, HALO, vn_ref[...], dypn_ref[...], keep_next, False)
        for gi, w in enumerate(POOL_WINDOWS):
            cols = slice(gi * dg, (gi + 1) * dg)
            back = None
            for k in range(w):
                term = dpn_buf[pl.ds(k, tr), cols]
                back = term if back is None else back + term
            dproj_ref[:, 3 * cw + gi * dg:3 * cw + (gi + 1) * dg] = (back - dpooled_buf[:, cols]).astype(dproj_ref.dtype)
        _accumulate(dscale_ref, jnp.concatenate(dscale, axis=1), i)
        rows = dg // N_CHIPS
        for gi in range(n_groups):
            for c in range(N_CHIPS):
                _accumulate(dpool_ref.at[c, gi], dmats[gi][c * rows:(c + 1) * rows, :], i)

    prev_c, next_c = _halo_specs(t, tr, cw, 1)
    prev_u, next_u = _halo_specs(t, tr, cw, 2)
    prev_v, next_v = _halo_specs(t, tr, cw, 3)
    _, next_b = _halo_specs(t, tr, cw, 0)
    _, next_dyc = _halo_specs(t, tr, cw, 0)
    _, next_dyp = _halo_specs(t, tr, cw, 1)
    return pl.pallas_call(
        body, name="mixers_bwd", grid=(n_tiles,),
        in_specs=[main(0), main(1), main(2), main(3), main(0), main(1),
                  prev_c, prev_u, prev_v,
                  next_b, next_c, next_u, next_v, next_dyc, next_dyp,
                  _const_spec(conv_w.shape), _const_spec(pool_g.shape), _const_spec(pool_scale.shape)]
        + [ANY] * len(deps),
        out_specs=[_row_spec(tr, e), _const_spec(conv_w.shape), _const_spec(pool_g.shape),
                   _const_spec(pool_scale.shape)],
        out_shape=[jax.ShapeDtypeStruct((t, e), MM_DTYPE), jax.ShapeDtypeStruct(conv_w.shape, jnp.float32),
                   jax.ShapeDtypeStruct(pool_g.shape, jnp.float32),
                   jax.ShapeDtypeStruct(pool_scale.shape, jnp.float32)],
        scratch_shapes=[pltpu.VMEM((ext, cw), jnp.float32), pltpu.VMEM((ext, cw), jnp.float32),
                        pltpu.VMEM((tr + HALO, cw), jnp.float32), pltpu.VMEM((tr + HALO, cw), jnp.float32),
                        pltpu.VMEM((tr, cw), jnp.float32)],
        compiler_params=_params(("arbitrary",)),
    )(proj, proj, proj, proj, dmixed, dmixed,
      proj, proj, proj,
      proj, proj, proj, proj, dmixed, dmixed,
      conv_w, pool_g, pool_scale, *deps)


def _cast_rows(name, w, dtype):
    r, c = w.shape
    tr = _tile(r, 512)

    def body(w_ref, o_ref):
        o_ref[...] = w_ref[...].astype(o_ref.dtype)

    return pl.pallas_call(
        body, name=name, grid=(r // tr,), in_specs=[_row_spec(tr, c)], out_specs=_row_spec(tr, c),
        out_shape=jax.ShapeDtypeStruct((r, c), dtype), compiler_params=_params(("arbitrary",)),
    )(w)


def _cast_to_slot(name, w, chip, by_columns=False):
    r, c = w.shape
    tr = _tile(r, 512)
    if by_columns:
        out_spec = pl.BlockSpec((tr, c), lambda i, chip_ref: (i, chip_ref[0]))
        out_shape = jax.ShapeDtypeStruct((r, N_CHIPS * c), MM_DTYPE)
    else:
        out_spec = pl.BlockSpec((None, tr, c), lambda i, chip_ref: (chip_ref[0], i, 0))
        out_shape = jax.ShapeDtypeStruct((N_CHIPS, r, c), MM_DTYPE)

    def body(chip_ref, w_ref, o_ref):
        o_ref[...] = w_ref[...].astype(o_ref.dtype)

    return pl.pallas_call(
        body, name=name,
        grid_spec=pltpu.PrefetchScalarGridSpec(
            num_scalar_prefetch=1, grid=(r // tr,),
            in_specs=[pl.BlockSpec((tr, c), lambda i, chip_ref: (i, 0))], out_specs=out_spec),
        out_shape=out_shape, compiler_params=_params(("arbitrary",)),
    )(chip, w)


def _sum_own_and_received(name, own, land, chip):
    _, r, c = land.shape
    tr = _tile(r, 256)

    def body(chip_ref, own_ref, a_ref, b_ref, c_ref, o_ref):
        tot = own_ref[...].astype(jnp.float32) + a_ref[...].astype(jnp.float32)
        tot = tot + b_ref[...].astype(jnp.float32)
        o_ref[...] = (tot + c_ref[...].astype(jnp.float32)).astype(o_ref.dtype)

    def slot(k):
        return pl.BlockSpec((None, tr, c), lambda i, chip_ref: ((chip_ref[0] + k) % N_CHIPS, i, 0))

    own_spec = slot(0) if len(own.shape) == 3 else pl.BlockSpec((tr, c), lambda i, chip_ref: (i, chip_ref[0]))
    return pl.pallas_call(
        body, name=name,
        grid_spec=pltpu.PrefetchScalarGridSpec(
            num_scalar_prefetch=1, grid=(r // tr,), in_specs=[own_spec, slot(1), slot(2), slot(3)],
            out_specs=pl.BlockSpec((tr, c), lambda i, chip_ref: (i, 0))),
        out_shape=jax.ShapeDtypeStruct((r, c), COMM_DTYPE), compiler_params=_params(("arbitrary",)),
    )(chip, own, land, land, land)


def _sum_slots(name, slots):
    n, r, c = slots.shape
    tr = _tile(r, 256)

    def body(s_ref, o_ref):
        tot = s_ref[0].astype(jnp.float32)
        for s in range(1, n):
            tot = tot + s_ref[s].astype(jnp.float32)
        o_ref[...] = tot

    return pl.pallas_call(
        body, name=name, grid=(r // tr,),
        in_specs=[pl.BlockSpec((n, tr, c), lambda i: (0, i, 0))], out_specs=_row_spec(tr, c),
        out_shape=jax.ShapeDtypeStruct((r, c), jnp.float32), compiler_params=_params(("arbitrary",)),
    )(slots)


def _adamw_math(w, g, m, v):
    m = ADAM_B1 * m + (1.0 - ADAM_B1) * g
    v = ADAM_B2 * v + (1.0 - ADAM_B2) * (g * g)
    m_hat = m / (1.0 - ADAM_B1 ** ADAM_STEP)
    v_hat = v / (1.0 - ADAM_B2 ** ADAM_STEP)
    delta = -ADAM_LR * (m_hat / (jnp.sqrt(v_hat) + ADAM_EPS) + ADAM_WD * w)
    return delta, m, v


def _adamw(name, w, m, v, grad_parts):
    r, c = w.shape
    tr = _tile(r, 256)
    n_parts = len(grad_parts)

    def body(*refs):
        w_ref, m_ref, v_ref = refs[:3]
        part_refs = refs[3:3 + n_parts]
        g_ref, d_ref, nm_ref, nv_ref = refs[3 + n_parts:]
        g = part_refs[0][...].astype(jnp.float32)
        for p in part_refs[1:]:
            g = g + p[...].astype(jnp.float32)
        delta, nm, nv = _adamw_math(w_ref[...], g, m_ref[...], v_ref[...])
        g_ref[...] = g
        d_ref[...] = delta
        nm_ref[...] = nm
        nv_ref[...] = nv

    spec = _row_spec(tr, c)
    out = jax.ShapeDtypeStruct((r, c), jnp.float32)
    return pl.pallas_call(
        body, name=name, grid=(r // tr,), in_specs=[spec] * (3 + n_parts), out_specs=[spec] * 4,
        out_shape=[out] * 4, compiler_params=_params(("arbitrary",)),
    )(w, m, v, *grad_parts)


def _chip_peers():
    x, y, c = lax.axis_index("x"), lax.axis_index("y"), lax.axis_index("c")
    return x, y, c, [(1 - x, y), (x, 1 - y), (1 - x, 1 - y)]


HBM = pl.BlockSpec(memory_space=pltpu.HBM)
SEM = pl.BlockSpec(memory_space=pltpu.SEMAPHORE)
TOKEN = jax.ShapeDtypeStruct((8, 128), jnp.float32)
N_PEER_CHIPS = N_CHIPS - 1


def _in_flight():
    return pltpu.CompilerParams(has_side_effects=pltpu.SideEffectType.DATAFLOW_SIDE_EFFECTING)


def _slot(ref, slot):
    if len(ref.shape) == 3:
        return ref.at[slot]
    width = ref.shape[1] // N_CHIPS
    return ref.at[:, pl.ds(pl.multiple_of(slot * width, 128), width)]


def _half_slot(ref, slot, half):
    rows = ref.shape[-2] // 2
    if len(ref.shape) == 3:
        return ref.at[slot, pl.ds(half * rows, rows)]
    width = ref.shape[1] // N_CHIPS
    return ref.at[pl.ds(half * rows, rows), pl.ds(pl.multiple_of(slot * width, 128), width)]


def _slot_shape(shape):
    return shape[1:] if len(shape) == 3 else (shape[0], shape[1] // N_CHIPS)


def _gather_start(name, full, deps=()):
    def body(full_ref, *rest):
        send_sems, recv_sems, _, token_ref = rest[len(deps):]
        x, y, c, peers = _chip_peers()
        mine = _half_slot(full_ref, 2 * x + y, c)
        for p, (px, py) in enumerate(peers):
            pltpu.make_async_remote_copy(
                src_ref=mine, dst_ref=mine, send_sem=send_sems.at[p], recv_sem=recv_sems.at[p],
                device_id=(px, py, c), device_id_type=MESH).start()
        token_ref[...] = jnp.zeros_like(token_ref)

    return pl.pallas_call(
        body, name=name,
        out_shape=(pltpu.SemaphoreType.DMA((N_PEER_CHIPS,)), pltpu.SemaphoreType.DMA((N_PEER_CHIPS,)),
                   pltpu.HBM(full.shape, full.dtype), TOKEN),
        in_specs=[HBM] + [ANY] * len(deps), out_specs=(SEM, SEM, HBM, pl.BlockSpec(memory_space=pltpu.VMEM)),
        input_output_aliases={0: 2}, compiler_params=_in_flight(),
    )(pltpu.with_memory_space_constraint(full, pltpu.HBM), *deps)


def _gather_wait(name, send_sems, recv_sems, full, after):
    def body(full_ref, send_sems, recv_sems, after_ref, out_ref):
        x, y, c, peers = _chip_peers()
        for p, (px, py) in enumerate(peers):
            cp = pltpu.make_async_remote_copy(
                src_ref=_half_slot(full_ref, 2 * x + y, c), dst_ref=_half_slot(full_ref, 2 * px + py, c),
                send_sem=send_sems.at[p], recv_sem=recv_sems.at[p], device_id=(px, py, c), device_id_type=MESH)
            cp.wait_send()
            cp.wait_recv()

    return pl.pallas_call(
        body, name=name, out_shape=pltpu.HBM(full.shape, full.dtype),
        in_specs=(HBM, SEM, SEM, ANY), out_specs=HBM, input_output_aliases={0: 0}, compiler_params=_in_flight(),
    )(full, send_sems, recv_sems, after)


def _forward_sibling(name, fulls):
    n = len(fulls)

    def body(*refs):
        in_refs, out_refs = refs[:n], refs[n:2 * n]
        send_sems, recv_sems = refs[2 * n:]
        x, y, c, peers = _chip_peers()
        sends, recvs = [], []
        for a in range(n):
            for p, (px, py) in enumerate(peers):
                k = N_PEER_CHIPS * a + p
                slot = 2 * px + py
                cp = pltpu.make_async_remote_copy(
                    src_ref=_half_slot(in_refs[a], slot, c), dst_ref=_half_slot(out_refs[a], slot, c),
                    send_sem=send_sems.at[k], recv_sem=recv_sems.at[k], device_id=(x, y, 1 - c), device_id_type=MESH)
                cp.start()
                sends.append(cp)
                recvs.append(pltpu.make_async_remote_copy(
                    src_ref=_half_slot(in_refs[a], slot, c), dst_ref=_half_slot(out_refs[a], slot, 1 - c),
                    send_sem=send_sems.at[k], recv_sem=recv_sems.at[k], device_id=(x, y, 1 - c), device_id_type=MESH))
        for cp in recvs:
            cp.wait_recv()
        for cp in sends:
            cp.wait_send()

    return pl.pallas_call(
        body, name=name, in_specs=[ANY] * n, out_specs=[ANY] * n,
        out_shape=[jax.ShapeDtypeStruct(f.shape, f.dtype) for f in fulls],
        input_output_aliases={a: a for a in range(n)},
        scratch_shapes=[pltpu.SemaphoreType.DMA((N_PEER_CHIPS * n,)), pltpu.SemaphoreType.DMA((N_PEER_CHIPS * n,))],
    )(*fulls)


def _scatter_start(name, grads, deps=()):
    def body(g_ref, land_ref, *rest):
        send_sems, recv_sems, _, _, token_ref = rest[len(deps):]
        x, y, c, peers = _chip_peers()
        me = 2 * x + y
        for p, (px, py) in enumerate(peers):
            pltpu.make_async_remote_copy(
                src_ref=_slot(g_ref, 2 * px + py), dst_ref=land_ref.at[me], send_sem=send_sems.at[p],
                recv_sem=recv_sems.at[p], device_id=(px, py, c), device_id_type=MESH).start()
        token_ref[...] = jnp.zeros_like(token_ref)

    land = lax.empty((N_CHIPS,) + _slot_shape(grads.shape), grads.dtype)
    return pl.pallas_call(
        body, name=name,
        out_shape=(pltpu.SemaphoreType.DMA((N_PEER_CHIPS,)), pltpu.SemaphoreType.DMA((N_PEER_CHIPS,)),
                   pltpu.HBM(grads.shape, grads.dtype), pltpu.HBM(land.shape, land.dtype), TOKEN),
        in_specs=[HBM, HBM] + [ANY] * len(deps),
        out_specs=(SEM, SEM, HBM, HBM, pl.BlockSpec(memory_space=pltpu.VMEM)),
        input_output_aliases={0: 2, 1: 3}, compiler_params=_in_flight(),
    )(pltpu.with_memory_space_constraint(grads, pltpu.HBM), pltpu.with_memory_space_constraint(land, pltpu.HBM), *deps)


def _scatter_wait(name, send_sems, recv_sems, grads, land, after):
    def body(g_ref, land_ref, send_sems, recv_sems, after_ref, g_out, land_out):
        x, y, c, peers = _chip_peers()
        for p, (px, py) in enumerate(peers):
            cp = pltpu.make_async_remote_copy(
                src_ref=_slot(g_ref, 2 * px + py), dst_ref=land_ref.at[2 * px + py], send_sem=send_sems.at[p],
                recv_sem=recv_sems.at[p], device_id=(px, py, c), device_id_type=MESH)
            cp.wait_send()
            cp.wait_recv()

    return pl.pallas_call(
        body, name=name,
        out_shape=(pltpu.HBM(grads.shape, grads.dtype), pltpu.HBM(land.shape, land.dtype)),
        in_specs=(HBM, HBM, SEM, SEM, ANY), out_specs=(HBM, HBM), input_output_aliases={0: 0, 1: 1},
        compiler_params=_in_flight(),
    )(grads, land, send_sems, recv_sems, after)


def _swap_sibling(name, parts):
    n = len(parts)

    def body(*refs):
        in_refs, out_refs = refs[:n], refs[n:2 * n]
        send_sems, recv_sems = refs[2 * n:]
        x, y, c = lax.axis_index("x"), lax.axis_index("y"), lax.axis_index("c")
        copies = []
        for a in range(n):
            cp = pltpu.make_async_remote_copy(
                src_ref=in_refs[a], dst_ref=out_refs[a], send_sem=send_sems.at[a], recv_sem=recv_sems.at[a],
                device_id=(x, y, 1 - c), device_id_type=MESH)
            cp.start()
            copies.append(cp)
        for cp in copies:
            cp.wait_recv()
        for cp in copies:
            cp.wait_send()

    return pl.pallas_call(
        body, name=name, in_specs=[ANY] * n, out_specs=[ANY] * n,
        out_shape=[jax.ShapeDtypeStruct(p.shape, p.dtype) for p in parts],
        scratch_shapes=[pltpu.SemaphoreType.DMA((n,)), pltpu.SemaphoreType.DMA((n,))],
    )(*parts)


def _gather_devices(name, block):
    def body(in_ref, out_ref, send_sems, recv_sems, local_sem):
        x, y, c = lax.axis_index("x"), lax.axis_index("y"), lax.axis_index("c")
        me = 4 * x + 2 * y + c
        local = pltpu.make_async_copy(in_ref, out_ref.at[me], local_sem)
        local.start()
        sends, recvs = [], []
        k = 0
        for fx in range(2):
            for fy in range(2):
                for fc in range(2):
                    if fx == fy == fc == 0:
                        continue
                    px = x if fx == 0 else 1 - x
                    py = y if fy == 0 else 1 - y
                    pc = c if fc == 0 else 1 - c
                    cp = pltpu.make_async_remote_copy(
                        src_ref=in_ref, dst_ref=out_ref.at[me], send_sem=send_sems.at[k], recv_sem=recv_sems.at[k],
                        device_id=(px, py, pc), device_id_type=MESH)
                    cp.start()
                    sends.append(cp)
                    recvs.append(pltpu.make_async_remote_copy(
                        src_ref=in_ref, dst_ref=out_ref.at[4 * px + 2 * py + pc], send_sem=send_sems.at[k],
                        recv_sem=recv_sems.at[k], device_id=(px, py, pc), device_id_type=MESH))
                    k += 1
        for cp in recvs:
            cp.wait_recv()
        for cp in sends:
            cp.wait_send()
        local.wait()

    return pl.pallas_call(
        body, name=name, in_specs=[ANY], out_specs=ANY,
        out_shape=jax.ShapeDtypeStruct((N_DEV,) + block.shape, block.dtype),
        scratch_shapes=[pltpu.SemaphoreType.DMA((N_DEV - 1,)), pltpu.SemaphoreType.DMA((N_DEV - 1,)),
                        pltpu.SemaphoreType.DMA],
    )(block)


def _pack_rows(pieces, width):
    flat = jnp.concatenate([p.reshape(-1) for p in pieces])
    rows = -(-flat.shape[0] // width)
    rows = -(-rows // 8) * 8
    flat = jnp.pad(flat, (0, rows * width - flat.shape[0]))
    return flat.reshape(rows, width)


def _unpack_rows(packed, shapes):
    flat = packed.reshape(-1)
    out, off = [], 0
    for s in shapes:
        size = 1
        for d in s:
            size *= d
        out.append(flat[off:off + size].reshape(s))
        off += size
    return out


def kernel(x, ln_mix_pre, w_in, conv_w, pool_w, pool_scale, w_out, ln_mix_post, ln_ffn_pre, w_gate, w_up, w_down, ln_ffn_post, loss_target, m_ln_mix_pre, m_w_in, m_conv_w, m_pool_w, m_pool_scale, m_w_out, m_ln_mix_post, m_ln_ffn_pre, m_w_gate, m_w_up, m_w_down, m_ln_ffn_post, v_ln_mix_pre, v_w_in, v_conv_w, v_pool_w, v_pool_scale, v_w_out, v_ln_mix_post, v_ln_ffn_pre, v_w_gate, v_w_up, v_w_down, v_ln_ffn_post):
    t, d = x.shape[1], x.shape[2]
    e4 = w_in.shape[2]
    e = N_CHIPS * e4
    f4 = w_gate.shape[2]
    f = N_CHIPS * f4
    n_groups, dg4, dg = pool_w.shape[1], pool_w.shape[2], pool_w.shape[3]
    cw4 = conv_w.shape[2]
    chip = 2 * lax.axis_index("x") + lax.axis_index("y")
    xs, tgt = x[0], loss_target[0]

    big = {"w_in": w_in[0], "w_out": w_out[0], "w_gate": w_gate[0], "w_up": w_up[0], "w_down": w_down[0],
           "pool_w": pool_w[0].reshape(n_groups * dg4, dg)}
    names = ["w_in", "pool_w", "w_out", "w_gate", "w_up", "w_down"]
    chip_arr = chip.astype(jnp.int32).reshape(1)
    conv_all = _gather_devices("gather_conv_w", _pack_rows([conv_w[0]], 128))
    conv_full = jnp.concatenate(
        [conv_all[2 * j].reshape(-1)[:CONV_K * cw4].reshape(CONV_K, cw4) for j in range(N_CHIPS)], axis=1)
    in_flight, deps = {}, (conv_all,)
    for k in names:
        by_columns = k in ("w_gate", "w_up")
        in_flight[k] = _gather_start(
            "gather_start_" + k, _cast_to_slot("cast_" + k, big[k], chip_arr, by_columns), deps)
        deps = (in_flight[k][3],)

    def landed(ks, after):
        fulls = []
        for k in ks:
            send_sems, recv_sems, full, _ = in_flight[k]
            fulls.append(_gather_wait("gather_wait_" + k, send_sems, recv_sems, full, after))
            after = fulls[-1]
        return _forward_sibling("forward_" + ks[0], fulls)

    def rows3(tile, width):
        return pl.BlockSpec((tile, width), lambda i, j, k: (i, 0))

    gain3 = pl.BlockSpec((1, d), lambda i, j, k: (0, 0))
    f32_td = jax.ShapeDtypeStruct((t, d), jnp.float32)
    mm_td = jax.ShapeDtypeStruct((t, d), MM_DTYPE)
    f32_gain = jax.ShapeDtypeStruct((1, d), jnp.float32)

    h = _pre_norm(xs, ln_mix_pre, deps)
    (win_g,) = landed(["w_in"], h)
    tm = _tile(t, 1024)
    tm2 = _tile(t, 2048)
    proj = _matmul(
        "in_proj", grid=(t // tm2, N_CHIPS, 1), mode="nn",
        pairs=[[(h, win_g)]],
        pair_specs=[[(rows3(tm2, d), pl.BlockSpec((None, d, e4), lambda i, j, k: (j, 0, 0)))]],
        acc_shapes=[(tm2, e4)], out_shapes=[jax.ShapeDtypeStruct((t, e), jnp.float32)],
        out_specs=[pl.BlockSpec((tm2, e4), lambda i, j, k: (i, j))], epilogue=_identity_epilogue)[0]
    pool_g, wout_full = landed(["pool_w", "w_out"], proj)
    pool_g, wout_full = pool_g.reshape(N_CHIPS, n_groups, dg4, dg), wout_full.reshape(d, d)
    mixed = _mixers_fwd(proj, conv_full, pool_g, pool_scale)

    def post_mix_epilogue(accs, extra_refs):
        x_ref, g2_ref, g3_ref = extra_refs
        mo = accs[0]
        x1 = x_ref[...] + mo * _rstd(mo) * g2_ref[...]
        return mo, x1, x1 * _rstd(x1) * g3_ref[...]

    tm_mix = _tile(t, 256)
    mix_out, x1, hf = _matmul(
        "out_proj", grid=(t // tm_mix, 1, 1), mode="nn",
        pairs=[[(mixed, wout_full)]],
        pair_specs=[[(rows3(tm_mix, d), pl.BlockSpec((d, d), lambda i, j, k: (0, 0)))]],
        acc_shapes=[(tm_mix, d)], extras=[xs, ln_mix_post, ln_ffn_pre],
        extra_specs=[rows3(tm_mix, d), gain3, gain3], out_shapes=[f32_td, f32_td, mm_td],
        out_specs=[rows3(tm_mix, d)] * 3, epilogue=post_mix_epilogue, epi_rows=EPILOGUE_ROWS)
    wg_full, wu_full = landed(["w_gate", "w_up"], hf)

    def gate_up_epilogue(accs, extra_refs):
        g, up = accs
        sig = jax.nn.sigmoid(g)
        silu = g * sig
        return up * (sig + silu * (1.0 - sig)), silu, silu * up

    tf = _tile(f, 512)
    ff_tile = jax.ShapeDtypeStruct((t, f), MM_DTYPE)
    act_by_g, act_by_up, act = _matmul(
        "gate_up", grid=(f // tf, t // tm, 1), mode="nn",
        pairs=[[(hf, wg_full)], [(hf, wu_full)]],
        pair_specs=[[(pl.BlockSpec((tm, d), lambda j, i, k: (i, 0)),
                      pl.BlockSpec((d, tf), lambda j, i, k: (0, j)))]] * 2,
        acc_shapes=[(tm, tf)] * 2, out_shapes=[ff_tile] * 3,
        out_specs=[pl.BlockSpec((tm, tf), lambda j, i, k: (i, j))] * 3, epilogue=gate_up_epilogue)
    wdown_full = landed(["w_down"], act)[0].reshape(f, d)

    def loss_epilogue(accs, extra_refs):
        x1_ref, tg_ref, g_ref = extra_refs
        ff_v = accs[0]
        rstd = _rstd(ff_v)
        n = ff_v * rstd
        g = g_ref[...]
        err = x1_ref[...] + n * g - tg_ref[...]
        rows_loss = 0.5 * jnp.sum(jnp.mean(err * err, axis=-1, keepdims=True), axis=0, keepdims=True)
        dout = err / d
        return (dout, _norm_bwd(dout * g, n, rstd), jnp.broadcast_to(rows_loss, (8, 128)),
                jnp.sum(dout * n, axis=0, keepdims=True))

    tm_row = _tile(t, 1024)
    dout, dff, loss_tile, dg_ffn_post = _matmul(
        "down_proj", grid=(t // tm_row, 1, f // tf), mode="nn",
        pairs=[[(act, wdown_full)]],
        pair_specs=[[(pl.BlockSpec((tm_row, tf), lambda i, j, k: (i, k)),
                      pl.BlockSpec((tf, d), lambda i, j, k: (k, 0)))]],
        acc_shapes=[(tm_row, d)], extras=[x1, tgt, ln_ffn_post], extra_specs=[STREAM] * 2 + [gain3],
        out_shapes=[f32_td, mm_td, jax.ShapeDtypeStruct((8, 128), jnp.float32), f32_gain],
        out_specs=[STREAM] * 2 + [pl.BlockSpec((8, 128), lambda i, j, k: (0, 0)), gain3],
        epilogue=loss_epilogue, epi_rows=EPILOGUE_ROWS, sum_outs=(2, 3))

    def dact_epilogue(accs, extra_refs):
        dact = accs[0]
        return dact * extra_refs[0][...].astype(jnp.float32), dact * extra_refs[1][...].astype(jnp.float32)

    ff_spec_ij = pl.BlockSpec((tm2, tf), lambda i, j, k: (i, j))
    dg_act, dup_act = _matmul(
        "dact", grid=(t // tm2, f // tf, 1), mode="nt",
        pairs=[[(dff, wdown_full)]],
        pair_specs=[[(rows3(tm2, d), pl.BlockSpec((tf, d), lambda i, j, k: (j, 0)))]],
        acc_shapes=[(tm2, tf)], extras=[act_by_g, act_by_up], extra_specs=[ff_spec_ij, ff_spec_ij],
        out_shapes=[ff_tile] * 2, out_specs=[ff_spec_ij] * 2, epilogue=dact_epilogue)
    tk = _tile(t, 1024)
    dw_down = _matmul(
        "dw_down", grid=(N_CHIPS, 1, t // tk), mode="tn",
        pairs=[[(act, dff)]],
        pair_specs=[[(pl.BlockSpec((tk, f4), lambda i, j, k: (k, i)),
                      pl.BlockSpec((tk, d), lambda i, j, k: (k, 0)))]],
        acc_shapes=[(f4, d)], out_shapes=[jax.ShapeDtypeStruct((N_CHIPS, f4, d), COMM_DTYPE)],
        out_specs=[pl.BlockSpec((None, f4, d), lambda i, j, k: (i, 0, 0))], epilogue=_identity_epilogue)[0]
    tn = _tile(d, 1024)
    leaving = {"w_down": _scatter_start("scatter_start_w_down", dw_down)}

    def ffn_pre_epilogue(accs, extra_refs):
        x1_ref, dout_ref, mo_ref, g3_ref, g2_ref = extra_refs
        dhf_v = accs[0]
        x1_v = x1_ref[...]
        rstd3 = _rstd(x1_v)
        n3 = x1_v * rstd3
        dx1_v = dout_ref[...] + _norm_bwd(dhf_v * g3_ref[...], n3, rstd3)
        mo = mo_ref[...]
        rstd2 = _rstd(mo)
        n2 = mo * rstd2
        return (dx1_v, _norm_bwd(dx1_v * g2_ref[...], n2, rstd2), jnp.sum(dhf_v * n3, axis=0, keepdims=True),
                jnp.sum(dx1_v * n2, axis=0, keepdims=True))

    dx1, dmo, dg_ffn_pre, dg_mix_post = _matmul(
        "dhf", grid=(t // tm_row, 1, f // tf), mode="nt",
        pairs=[[(dg_act, wg_full), (dup_act, wu_full)]],
        pair_specs=[[(pl.BlockSpec((tm_row, tf), lambda i, j, k: (i, k)),
                      pl.BlockSpec((d, tf), lambda i, j, k: (0, k)))] * 2],
        acc_shapes=[(tm_row, d)], extras=[x1, dout, mix_out, ln_ffn_pre, ln_mix_post],
        extra_specs=[STREAM] * 3 + [gain3] * 2, out_shapes=[f32_td, mm_td, f32_gain, f32_gain],
        out_specs=[STREAM] * 2 + [gain3] * 2, epilogue=ffn_pre_epilogue, epi_rows=EPILOGUE_ROWS,
        sum_outs=(2, 3), deps=leaving["w_down"][4:])
    tmo = _tile(d, 2048)
    grad_ff = jax.ShapeDtypeStruct((d, f), COMM_DTYPE)
    dw_gate, dw_up = _matmul(
        "dw_gate_up", grid=(d // tmo, f // tf, t // tm2), mode="tn",
        pairs=[[(hf, dg_act)], [(hf, dup_act)]],
        pair_specs=[[(pl.BlockSpec((tm2, tmo), lambda i, j, k: (k, i)),
                      pl.BlockSpec((tm2, tf), lambda i, j, k: (k, j)))]] * 2,
        acc_shapes=[(tmo, tf)] * 2, out_shapes=[grad_ff] * 2,
        out_specs=[pl.BlockSpec((tmo, tf), lambda i, j, k: (i, j))] * 2, epilogue=_identity_epilogue)
    leaving["w_gate"] = _scatter_start("scatter_start_w_gate", dw_gate)
    leaving["w_up"] = _scatter_start("scatter_start_w_up", dw_up, leaving["w_gate"][4:])

    dmixed = _matmul(
        "dmixed", grid=(t // tm, 1, 1), mode="nt",
        pairs=[[(dmo, wout_full)]],
        pair_specs=[[(rows3(tm, d), pl.BlockSpec((d, d), lambda i, j, k: (0, 0)))]],
        acc_shapes=[(tm, d)], out_shapes=[f32_td],
        out_specs=[rows3(tm, d)], epilogue=_identity_epilogue,
        deps=leaving["w_up"][4:])[0]
    tmo = _tile(d, 2048)
    dw_out = _matmul(
        "dw_out", grid=(d // tmo, d // tn, t // tk), mode="tn",
        pairs=[[(mixed, dmo)]],
        pair_specs=[[(pl.BlockSpec((tk, tmo), lambda i, j, k: (k, i)),
                      pl.BlockSpec((tk, tn), lambda i, j, k: (k, j)))]],
        acc_shapes=[(tmo, tn)], out_shapes=[jax.ShapeDtypeStruct((d, d), COMM_DTYPE)],
        out_specs=[pl.BlockSpec((tmo, tn), lambda i, j, k: (i, j))], epilogue=_identity_epilogue)[0]
    leaving["w_out"] = _scatter_start("scatter_start_w_out", dw_out.reshape(N_CHIPS, d // N_CHIPS, d))
    dproj, dconv_full, dpool_g, dpool_scale = _mixers_bwd(proj, dmixed, conv_full, pool_g, pool_scale,
                                                          leaving["w_out"][4:])
    dpool_slots = _cast_rows("cast_dpool", dpool_g.reshape(N_CHIPS * n_groups * dg4, dg), COMM_DTYPE)
    leaving["pool_w"] = _scatter_start("scatter_start_pool_w", dpool_slots.reshape(N_CHIPS, n_groups * dg4, dg))
    dw_in = _matmul(
        "dw_in", grid=(d // tmo, N_CHIPS, t // tk), mode="tn",
        pairs=[[(h, dproj)]],
        pair_specs=[[(pl.BlockSpec((tk, tmo), lambda i, j, k: (k, i)),
                      pl.BlockSpec((tk, e4), lambda i, j, k: (k, j)))]],
        acc_shapes=[(tmo, e4)], out_shapes=[jax.ShapeDtypeStruct((N_CHIPS, d, e4), COMM_DTYPE)],
        out_specs=[pl.BlockSpec((None, tmo, e4), lambda i, j, k: (j, i, 0))], epilogue=_identity_epilogue,
        deps=leaving["pool_w"][4:])[0]
    leaving["w_in"] = _scatter_start("scatter_start_w_in", dw_in)

    def mix_pre_epilogue(accs, extra_refs):
        x_ref, dx1_ref, g_ref = extra_refs
        dh_v = accs[0]
        xv = x_ref[...]
        rstd = _rstd(xv)
        n = xv * rstd
        return dx1_ref[...] + _norm_bwd(dh_v * g_ref[...], n, rstd), jnp.sum(dh_v * n, axis=0, keepdims=True)

    grad_x, dg_mix_pre = _matmul(
        "dh", grid=(t // tm_row, 1, N_CHIPS), mode="nt",
        pairs=[[(dproj, win_g)]],
        pair_specs=[[(pl.BlockSpec((tm_row, e4), lambda i, j, k: (i, k)),
                      pl.BlockSpec((None, d, e4), lambda i, j, k: (k, 0, 0)))]],
        acc_shapes=[(tm_row, d)], extras=[xs, dx1, ln_mix_pre], extra_specs=[STREAM] * 2 + [gain3],
        out_shapes=[f32_td, f32_gain], out_specs=[STREAM, gain3], epilogue=mix_pre_epilogue,
        epi_rows=EPILOGUE_ROWS, sum_outs=(1,), deps=leaving["w_in"][4:], prefetch_extras=True)

    names = ["w_down", "w_gate", "w_up", "w_out", "pool_w", "w_in"]
    partial, after = [], grad_x
    for k in names:
        send_sems, recv_sems, own, land, _ = leaving[k]
        own, land = _scatter_wait("scatter_wait_" + k, send_sems, recv_sems, own, land, after)
        partial.append(_sum_own_and_received("sum_" + k, own, land, chip_arr))
        after = land
    other = _swap_sibling("swap_grads", partial)
    moments = {"w_in": (m_w_in, v_w_in), "w_out": (m_w_out, v_w_out), "w_gate": (m_w_gate, v_w_gate),
               "w_up": (m_w_up, v_w_up), "w_down": (m_w_down, v_w_down), "pool_w": (m_pool_w, v_pool_w)}
    result = {}
    for k, mine, theirs in zip(names, partial, other):
        shape = moments[k][0].shape
        two_d = big[k].shape
        outs = _adamw("adamw_" + k, big[k], moments[k][0].reshape(two_d), moments[k][1].reshape(two_d),
                      [mine, theirs])
        result[k] = [o.reshape(shape) for o in outs]

    small_shapes = [(1, d)] * 4 + [pool_scale.shape, (CONV_K, N_CHIPS * cw4)]
    packed = _pack_rows([dg_mix_pre, dg_mix_post, dg_ffn_pre, dg_ffn_post, dpool_scale, dconv_full], 1024)
    summed = _sum_slots("sum_small", _gather_devices("gather_small", packed))
    g_mix_pre, g_mix_post, g_ffn_pre, g_ffn_post, g_pool_scale, g_conv_full = _unpack_rows(summed, small_shapes)
    g_conv = lax.dynamic_slice(g_conv_full, (0, chip * cw4), (CONV_K, cw4))[None]
    small = [("ln_mix_pre", ln_mix_pre, m_ln_mix_pre, v_ln_mix_pre, g_mix_pre),
             ("conv_w", conv_w, m_conv_w, v_conv_w, g_conv),
             ("pool_scale", pool_scale, m_pool_scale, v_pool_scale, g_pool_scale),
             ("ln_mix_post", ln_mix_post, m_ln_mix_post, v_ln_mix_post, g_mix_post),
             ("ln_ffn_pre", ln_ffn_pre, m_ln_ffn_pre, v_ln_ffn_pre, g_ffn_pre),
             ("ln_ffn_post", ln_ffn_post, m_ln_ffn_post, v_ln_ffn_post, g_ffn_post)]
    shapes_small = [s[1].shape for s in small]
    packs = [_pack_rows([s[q] for s in small], 128) for q in (1, 2, 3, 4)]
    outs = _adamw("adamw_small", packs[0], packs[1], packs[2], [packs[3]])
    unpacked = [_unpack_rows(o, shapes_small) for o in outs]
    for idx, s in enumerate(small):
        result[s[0]] = [u[idx] for u in unpacked]

    loss = lax.psum(loss_tile[0, 0], ("x", "y", "c"))
    order = ["ln_mix_pre", "w_in", "conv_w", "pool_w", "pool_scale", "w_out", "ln_mix_post", "ln_ffn_pre",
             "w_gate", "w_up", "w_down", "ln_ffn_post"]
    return (loss, grad_x[None], *[result[k][0] for k in order], *[result[k][1] for k in order],
            *[result[k][2] for k in order], *[result[k][3] for k in order])
```

```python
import functools

import jax
import jax.numpy as jnp
from jax import lax
from jax.experimental import pallas as pl
from jax.experimental.pallas import tpu as pltpu

EPS = 1e-6
CONV_HEAD_DIM = 128
CONV_K = 3
POOL_WINDOWS = (2, 4, 8, 16)
HALO = 16
EPILOGUE_ROWS = 64
STREAM_ROWS = 256
N_CHIPS = 4
N_DEV = 8

ADAM_LR = 0.001
ADAM_B1 = 0.9
ADAM_B2 = 0.999
ADAM_EPS = 1e-08
ADAM_WD = 0.01
ADAM_STEP = 10

MM_DTYPE = jnp.bfloat16
COMM_DTYPE = jnp.bfloat16
VMEM_LIMIT = 62 * 1024 * 1024
MESH = pl.DeviceIdType.MESH
ANY = pl.BlockSpec(memory_space=pl.ANY)
STREAM = "stream"


def _tile(n, pref):
    t = min(pref, n)
    while n % t:
        t //= 2
    return t


def _params(sem):
    return pltpu.CompilerParams(dimension_semantics=sem, vmem_limit_bytes=VMEM_LIMIT)


def _rstd(x):
    return lax.rsqrt(jnp.mean(x * x, axis=-1, keepdims=True) + EPS)


def _norm_bwd(dn, n, rstd):
    return rstd * (dn - n * jnp.mean(dn * n, axis=-1, keepdims=True))


_DOT_DIMS = {
    "nn": (((1,), (0,)), ((), ())),
    "nt": (((1,), (1,)), ((), ())),
    "tn": (((0,), (0,)), ((), ())),
}


def _dot(a, b, mode):
    return lax.dot_general(a.astype(MM_DTYPE), b.astype(MM_DTYPE), _DOT_DIMS[mode],
                           preferred_element_type=jnp.float32)


def _matmul(name, *, grid, mode, pairs, pair_specs, acc_shapes, extras=(), extra_specs=(),
            out_shapes, out_specs, epilogue, deps=(), epi_rows=0, sum_outs=(), prefetch_extras=False):
    nk = grid[2]
    operands, operand_specs, where, counts = [], [], {}, []
    pair_index = []
    for ps, ss in zip(pairs, pair_specs):
        counts.append(len(ps))
        for arrays, specs in zip(ps, ss):
            for arr, spec in zip(arrays, specs):
                key = (id(arr), id(spec))
                if key not in where:
                    where[key] = len(operands)
                    operands.append(arr)
                    operand_specs.append(spec)
                pair_index.append(where[key])
    n_operands = len(operands)
    n_extra = len(extras)
    n_out = len(out_shapes)
    n_in = n_operands + n_extra + len(deps)
    in_streams = [q for q, s in enumerate(extra_specs) if s is STREAM]
    out_streams = [q for q, s in enumerate(out_specs) if s is STREAM]
    in_buf_rows = (acc_shapes[0][0],) if prefetch_extras else (2, STREAM_ROWS)
    stream_bufs = ([pltpu.VMEM(in_buf_rows + (extras[q].shape[1],), extras[q].dtype) for q in in_streams]
                   + [pltpu.VMEM((2, STREAM_ROWS, out_shapes[q].shape[1]), out_shapes[q].dtype) for q in out_streams])
    n_streams = len(stream_bufs)
    n_acc = 0 if nk == 1 else len(acc_shapes)

    def body(*refs):
        pair_refs = [refs[q] for q in pair_index]
        extra_refs = refs[n_operands:n_operands + n_extra]
        out_refs = refs[n_in:n_in + n_out]
        acc_refs = refs[n_in + n_out:n_in + n_out + n_acc]
        bufs = refs[n_in + n_out + n_acc:n_in + n_out + n_acc + n_streams]
        i, k = pl.program_id(0), pl.program_id(2)

        def stream_copies():
            sems = refs[-1]
            n_rows = acc_refs[0].shape[0]

            def hbm_rows(ref, c):
                return ref.at[pl.ds(pl.multiple_of(i * n_rows + c * STREAM_ROWS, STREAM_ROWS), STREAM_ROWS)]

            def fetch(s, c):
                if prefetch_extras:
                    return pltpu.make_async_copy(
                        extra_refs[in_streams[s]].at[pl.ds(pl.multiple_of(i * n_rows, STREAM_ROWS), n_rows)],
                        bufs[s], sems.at[s, 0])
                return pltpu.make_async_copy(hbm_rows(extra_refs[in_streams[s]], c), bufs[s].at[c % 2],
                                             sems.at[s, c % 2])

            def drain(s, c):
                return pltpu.make_async_copy(bufs[len(in_streams) + s].at[c % 2], hbm_rows(out_refs[out_streams[s]], c),
                                             sems.at[len(in_streams) + s, c % 2])

            return fetch, drain

        def streamed_finish(accs):
            fetch, drain = stream_copies()
            n_chunks = accs[0].shape[0] // STREAM_ROWS
            for c in range(n_chunks):
                for s in range(len(in_streams)):
                    if c + 1 < n_chunks and not prefetch_extras:
                        fetch(s, c + 1).start()
                    if c == 0 or not prefetch_extras:
                        fetch(s, c).wait()
                for s in range(len(out_streams)):
                    if c >= 2:
                        drain(s, c - 2).wait()
                for r0 in range(0, STREAM_ROWS, epi_rows):
                    sub = slice(r0, r0 + epi_rows)
                    rows = slice(c * STREAM_ROWS + r0, c * STREAM_ROWS + r0 + epi_rows)
                    views = [(bufs[in_streams.index(q)].at[rows] if prefetch_extras
                              else bufs[in_streams.index(q)].at[c % 2, sub]) if q in in_streams else e
                             for q, e in enumerate(extra_refs)]
                    outs = epilogue([a[rows, :] for a in accs], views)
                    for q, (o_ref, o) in enumerate(zip(out_refs, outs)):
                        if q in out_streams:
                            bufs[len(in_streams) + out_streams.index(q)][c % 2, sub, :] = o.astype(o_ref.dtype)
                        elif c == 0 and r0 == 0:
                            _accumulate(o_ref, o, i)
                        else:
                            o_ref[...] += o
                for s in range(len(out_streams)):
                    drain(s, c).start()
            for s in range(len(out_streams)):
                for c in range(max(n_chunks - 2, 0), n_chunks):
                    drain(s, c).wait()

        def partial_sums():
            res, p = [], 0
            for cnt in counts:
                tot = None
                for _ in range(cnt):
                    d = _dot(pair_refs[p][...], pair_refs[p + 1][...], mode)
                    tot = d if tot is None else tot + d
                    p += 2
                res.append(tot)
            return res

        def finish(accs):
            n_rows = accs[0].shape[0]
            step = epi_rows or n_rows
            for r0 in range(0, n_rows, step):
                rows = slice(r0, r0 + step)
                outs = epilogue([a[rows, :] for a in accs], [e.at[rows] if e.shape[0] == n_rows else e
                                                             for e in extra_refs])
                for q, (o_ref, o) in enumerate(zip(out_refs, outs)):
                    if q not in sum_outs:
                        o_ref[rows, :] = o.astype(o_ref.dtype)
                    elif r0 == 0:
                        _accumulate(o_ref, o, i)
                    else:
                        o_ref[...] += o

        if nk == 1:
            finish(partial_sums())
        else:
            if n_streams:
                @pl.when(k == (0 if prefetch_extras else nk - 1))
                def _():
                    for s in range(len(in_streams)):
                        stream_copies()[0](s, 0).start()

            @pl.when(k == 0)
            def _():
                for acc_ref, s in zip(acc_refs, partial_sums()):
                    acc_ref[...] = s

            @pl.when(k > 0)
            def _():
                for acc_ref, s in zip(acc_refs, partial_sums()):
                    acc_ref[...] += s

            @pl.when(k == nk - 1)
            def _():
                (streamed_finish if n_streams else finish)(acc_refs)

    scratch = [] if nk == 1 else [pltpu.VMEM(s, jnp.float32) for s in acc_shapes]
    if n_streams:
        assert nk > 1 and epi_rows and not any(q in sum_outs for q in out_streams)
        scratch = scratch + stream_bufs + [pltpu.SemaphoreType.DMA((n_streams, 2))]
    return pl.pallas_call(
        body, name=name, grid=grid,
        in_specs=operand_specs + [ANY if s is STREAM else s for s in extra_specs] + [ANY] * len(deps),
        out_specs=[ANY if s is STREAM else s for s in out_specs],
        out_shape=list(out_shapes), scratch_shapes=scratch,
        compiler_params=_params(("arbitrary", "arbitrary", "arbitrary")),
    )(*operands, *extras, *deps)


def _identity_epilogue(accs, extra_refs):
    return tuple(accs)


def _row_spec(tr, n):
    return pl.BlockSpec((tr, n), lambda i: (i, 0))


def _const_spec(shape):
    return pl.BlockSpec(shape, lambda i: tuple(0 for _ in shape))


def _accumulate(ref, val, i):
    @pl.when(i == 0)
    def _():
        ref[...] = val

    @pl.when(i > 0)
    def _():
        ref[...] += val


def _pre_norm(x, gain, deps=()):
    t, d = x.shape
    tr = _tile(t, 512)

    def body(x_ref, g_ref, *rest):
        h_ref = rest[-1]
        xv = x_ref[...]
        h_ref[...] = (xv * _rstd(xv) * g_ref[...]).astype(h_ref.dtype)

    return pl.pallas_call(
        body, name="pre_norm", grid=(t // tr,),
        in_specs=[_row_spec(tr, d), _const_spec((1, d))] + [ANY] * len(deps), out_specs=_row_spec(tr, d),
        out_shape=jax.ShapeDtypeStruct((t, d), MM_DTYPE), compiler_params=_params(("arbitrary",)),
    )(x, gain, *deps)


def _pool_matrix(pool_ref, g):
    return jnp.concatenate([pool_ref[c, g] for c in range(N_CHIPS)], axis=0)


def _inv_count(row0, n, w):
    pos = (row0 + lax.broadcasted_iota(jnp.int32, (n, 1), 0) + 1).astype(jnp.float32)
    return 1.0 / jnp.minimum(pos, float(w))


def _conv_piece(cu_buf, start, n, b_piece, convw_ref):
    conv = None
    for k in range(CONV_K):
        term = convw_ref[k:k + 1, :] * cu_buf[pl.ds(HALO + start + k - (CONV_K - 1), n), :]
        conv = term if conv is None else conv + term
    return conv, b_piece * conv


def _head_stats(a, width):
    return [_rstd(a[:, h * width:(h + 1) * width]) for h in range(a.shape[1] // width)]


def _pooled_piece(v_buf, start, n, v_piece, row0, dg):
    outs = []
    for gi, w in enumerate(POOL_WINDOWS):
        cols = slice(gi * dg, (gi + 1) * dg)
        win = None
        for k in range(w):
            term = v_buf[pl.ds(HALO + start - k, n), cols]
            win = term if win is None else win + term
        outs.append(win * _inv_count(row0 + start, n, w) - v_piece[:, cols])
    return outs


def _halo_specs(t, tr, width, col):
    per = tr // HALO
    last = t // HALO - 1
    prev = pl.BlockSpec((HALO, width), lambda i: (jnp.maximum(i * per - 1, 0), col))
    nxt = pl.BlockSpec((HALO, width), lambda i: (jnp.minimum((i + 1) * per, last), col))
    return prev, nxt


def _mixers_fwd(proj, conv_w, pool_g, pool_scale):
    t, e = proj.shape
    cw = e // 4
    dg = pool_g.shape[-1]
    tr = _tile(t, 256)

    def main(col):
        return pl.BlockSpec((tr, cw), lambda i: (i, col))

    def body(b_ref, c_ref, u_ref, v_ref, cp_ref, up_ref, vp_ref, convw_ref, pool_ref, scale_ref,
             out_ref, cu_buf, v_buf):
        i = pl.program_id(0)
        keep = (i > 0).astype(jnp.float32)
        cu_buf[pl.ds(0, HALO), :] = cp_ref[...] * up_ref[...] * keep
        cu_buf[pl.ds(HALO, tr), :] = c_ref[...] * u_ref[...]
        v_buf[pl.ds(0, HALO), :] = vp_ref[...] * keep
        v_buf[pl.ds(HALO, tr), :] = v_ref[...]
        _, a = _conv_piece(cu_buf, 0, tr, b_ref[...], convw_ref)
        for h, rstd in enumerate(_head_stats(a, CONV_HEAD_DIM)):
            cols = slice(h * CONV_HEAD_DIM, (h + 1) * CONV_HEAD_DIM)
            out_ref[:, cols] = (a[:, cols] * rstd).astype(out_ref.dtype)
        pooled = _pooled_piece(v_buf, 0, tr, v_ref[...], i * tr, dg)
        for gi, p in enumerate(pooled):
            z = _dot(p, _pool_matrix(pool_ref, gi), "nn")
            cols = slice(gi * dg, (gi + 1) * dg)
            out_ref[:, cw + gi * dg:cw + (gi + 1) * dg] = (z * _rstd(z) * scale_ref[:, cols]).astype(out_ref.dtype)

    prev_c, _ = _halo_specs(t, tr, cw, 1)
    prev_u, _ = _halo_specs(t, tr, cw, 2)
    prev_v, _ = _halo_specs(t, tr, cw, 3)
    return pl.pallas_call(
        body, name="mixers_fwd", grid=(t // tr,),
        in_specs=[main(0), main(1), main(2), main(3), prev_c, prev_u, prev_v,
                  _const_spec(conv_w.shape), _const_spec(pool_g.shape), _const_spec(pool_scale.shape)],
        out_specs=_row_spec(tr, 2 * cw),
        out_shape=jax.ShapeDtypeStruct((t, 2 * cw), MM_DTYPE),
        scratch_shapes=[pltpu.VMEM((tr + HALO, cw), jnp.float32), pltpu.VMEM((tr + HALO, cw), jnp.float32)],
        compiler_params=_params(("arbitrary",)),
    )(proj, proj, proj, proj, proj, proj, proj, conv_w, pool_g, pool_scale)


def _mixers_bwd(proj, dmixed, conv_w, pool_g, pool_scale, deps=()):
    t, e = proj.shape
    cw = e // 4
    dg = pool_g.shape[-1]
    n_groups = len(POOL_WINDOWS)
    tr = _tile(t, 256)
    n_tiles = t // tr
    ext = tr + 2 * HALO

    def main(col):
        return pl.BlockSpec((tr, cw), lambda i: (i, col))

    def body(b_ref, c_ref, u_ref, v_ref, dyc_ref, dyp_ref,
             cp_ref, up_ref, vp_ref,
             bn_ref, cn_ref, un_ref, vn_ref, dycn_ref, dypn_ref,
             convw_ref, pool_ref, scale_ref, *rest):
        (dproj_ref, dconvw_ref, dpool_ref, dscale_ref,
         cu_buf, v_buf, dconv_buf, dpn_buf, dpooled_buf) = rest[len(deps):]
        i = pl.program_id(0)
        keep_prev = (i > 0).astype(jnp.float32)
        keep_next = (i < n_tiles - 1).astype(jnp.float32)
        cu_buf[pl.ds(0, HALO), :] = cp_ref[...] * up_ref[...] * keep_prev
        cu_buf[pl.ds(HALO, tr), :] = c_ref[...] * u_ref[...]
        cu_buf[pl.ds(HALO + tr, HALO), :] = cn_ref[...] * un_ref[...]
        v_buf[pl.ds(0, HALO), :] = vp_ref[...] * keep_prev
        v_buf[pl.ds(HALO, tr), :] = v_ref[...]
        v_buf[pl.ds(HALO + tr, HALO), :] = vn_ref[...]

        def conv_piece(start, n, b_piece, dyc_piece, keep, is_main):
            conv, a = _conv_piece(cu_buf, start, n, b_piece, convw_ref)
            for h, rstd in enumerate(_head_stats(a, CONV_HEAD_DIM)):
                cols = slice(h * CONV_HEAD_DIM, (h + 1) * CONV_HEAD_DIM)
                da = _norm_bwd(dyc_piece[:, cols], a[:, cols] * rstd, rstd)
                dconv_buf[pl.ds(start, n), cols] = da * b_piece[:, cols] * keep
                if is_main:
                    dproj_ref[:, cols] = (da * conv[:, cols]).astype(dproj_ref.dtype)

        conv_piece(0, tr, b_ref[...], dyc_ref[...], 1.0, True)
        conv_piece(tr, HALO, bn_ref[...], dycn_ref[...], keep_next, False)

        dconv_main = dconv_buf[pl.ds(0, tr), :]
        dcu = None
        dw_rows = []
        for k in range(CONV_K):
            shift = CONV_K - 1 - k
            term = convw_ref[k:k + 1, :] * dconv_buf[pl.ds(shift, tr), :]
            dcu = term if dcu is None else dcu + term
            dw_rows.append(jnp.sum(dconv_main * cu_buf[pl.ds(HALO - shift, tr), :], axis=0, keepdims=True))
        dproj_ref[:, cw:2 * cw] = (dcu * u_ref[...]).astype(dproj_ref.dtype)
        dproj_ref[:, 2 * cw:3 * cw] = (dcu * c_ref[...]).astype(dproj_ref.dtype)
        _accumulate(dconvw_ref, jnp.concatenate(dw_rows, axis=0), i)

        def pool_piece(start, n, v_piece, dyp_piece, keep, is_main):
            pooled = _pooled_piece(v_buf, start, n, v_piece, i * tr, dg)
            dscale, dmats = [], []
            for gi, w in enumerate(POOL_WINDOWS):
                cols = slice(gi * dg, (gi + 1) * dg)
                mat = _pool_matrix(pool_ref, gi)
                z = _dot(pooled[gi], mat, "nn")
                rstd = _rstd(z)
                nz = z * rstd
                dyp_g = dyp_piece[:, cols]
                dz = _norm_bwd(dyp_g * scale_ref[:, cols], nz, rstd)
                dpooled = _dot(dz, mat, "nt") * keep
                dpn_buf[pl.ds(start, n), cols] = dpooled * _inv_count(i * tr + start, n, w)
                if is_main:
                    dpooled_buf[:, cols] = dpooled
                    dscale.append(jnp.sum(dyp_g * nz, axis=0, keepdims=True))
                    dmats.append(_dot(pooled[gi], dz, "tn"))
            return dscale, dmats

        dscale, dmats = pool_piece(0, tr, v_ref[...], dyp_ref[...], 1.0, True)
        pool_piece(tr, HALO, vn_ref[...], dypn_ref[...], keep_next, False)
        for gi, w in enumerate(POOL_WINDOWS):
            cols = slice(gi * dg, (gi + 1) * dg)
            back = None
            for k in range(w):
                term = dpn_buf[pl.ds(k, tr), cols]
                back = term if back is None else back + term
            dproj_ref[:, 3 * cw + gi * dg:3 * cw + (gi + 1) * dg] = (back - dpooled_buf[:, cols]).astype(dproj_ref.dtype)
        _accumulate(dscale_ref, jnp.concatenate(dscale, axis=1), i)
        rows = dg // N_CHIPS
        for gi in range(n_groups):
            for c in range(N_CHIPS):
                _accumulate(dpool_ref.at[c, gi], dmats[gi][c * rows:(c + 1) * rows, :], i)

    prev_c, next_c = _halo_specs(t, tr, cw, 1)
    prev_u, next_u = _halo_specs(t, tr, cw, 2)
    prev_v, next_v = _halo_specs(t, tr, cw, 3)
    _, next_b = _halo_specs(t, tr, cw, 0)
    _, next_dyc = _halo_specs(t, tr, cw, 0)
    _, next_dyp = _halo_specs(t, tr, cw, 1)
    return pl.pallas_call(
        body, name="mixers_bwd", grid=(n_tiles,),
        in_specs=[main(0), main(1), main(2), main(3), main(0), main(1),
                  prev_c, prev_u, prev_v,
                  next_b, next_c, next_u, next_v, next_dyc, next_dyp,
                  _const_spec(conv_w.shape), _const_spec(pool_g.shape), _const_spec(pool_scale.shape)]
        + [ANY] * len(deps),
        out_specs=[_row_spec(tr, e), _const_spec(conv_w.shape), _const_spec(pool_g.shape),
                   _const_spec(pool_scale.shape)],
        out_shape=[jax.ShapeDtypeStruct((t, e), MM_DTYPE), jax.ShapeDtypeStruct(conv_w.shape, jnp.float32),
                   jax.ShapeDtypeStruct(pool_g.shape, jnp.float32),
                   jax.ShapeDtypeStruct(pool_scale.shape, jnp.float32)],
        scratch_shapes=[pltpu.VMEM((ext, cw), jnp.float32), pltpu.VMEM((ext, cw), jnp.float32),
                        pltpu.VMEM((tr + HALO, cw), jnp.float32), pltpu.VMEM((tr + HALO, cw), jnp.float32),
                        pltpu.VMEM((tr, cw), jnp.float32)],
        compiler_params=_params(("arbitrary",)),
    )(proj, proj, proj, proj, dmixed, dmixed,
      proj, proj, proj,
      proj, proj, proj, proj, dmixed, dmixed,
      conv_w, pool_g, pool_scale, *deps)


def _cast_rows(name, w, dtype):
    r, c = w.shape
    tr = _tile(r, 512)

    def body(w_ref, o_ref):
        o_ref[...] = w_ref[...].astype(o_ref.dtype)

    return pl.pallas_call(
        body, name=name, grid=(r // tr,), in_specs=[_row_spec(tr, c)], out_specs=_row_spec(tr, c),
        out_shape=jax.ShapeDtypeStruct((r, c), dtype), compiler_params=_params(("arbitrary",)),
    )(w)


def _cast_to_slot(name, w, chip, by_columns=False):
    r, c = w.shape
    tr = _tile(r, 512)
    if by_columns:
        out_spec = pl.BlockSpec((tr, c), lambda i, chip_ref: (i, chip_ref[0]))
        out_shape = jax.ShapeDtypeStruct((r, N_CHIPS * c), MM_DTYPE)
    else:
        out_spec = pl.BlockSpec((None, tr, c), lambda i, chip_ref: (chip_ref[0], i, 0))
        out_shape = jax.ShapeDtypeStruct((N_CHIPS, r, c), MM_DTYPE)

    def body(chip_ref, w_ref, o_ref):
        o_ref[...] = w_ref[...].astype(o_ref.dtype)

    return pl.pallas_call(
        body, name=name,
        grid_spec=pltpu.PrefetchScalarGridSpec(
            num_scalar_prefetch=1, grid=(r // tr,),
            in_specs=[pl.BlockSpec((tr, c), lambda i, chip_ref: (i, 0))], out_specs=out_spec),
        out_shape=out_shape, compiler_params=_params(("arbitrary",)),
    )(chip, w)


def _sum_own_and_received(name, own, land, chip):
    _, r, c = land.shape
    tr = _tile(r, 256)

    def body(chip_ref, own_ref, a_ref, b_ref, c_ref, o_ref):
        tot = own_ref[...].astype(jnp.float32) + a_ref[...].astype(jnp.float32)
        tot = tot + b_ref[...].astype(jnp.float32)
        o_ref[...] = (tot + c_ref[...].astype(jnp.float32)).astype(o_ref.dtype)

    def slot(k):
        return pl.BlockSpec((None, tr, c), lambda i, chip_ref: ((chip_ref[0] + k) % N_CHIPS, i, 0))

    own_spec = slot(0) if len(own.shape) == 3 else pl.BlockSpec((tr, c), lambda i, chip_ref: (i, chip_ref[0]))
    return pl.pallas_call(
        body, name=name,
        grid_spec=pltpu.PrefetchScalarGridSpec(
            num_scalar_prefetch=1, grid=(r // tr,), in_specs=[own_spec, slot(1), slot(2), slot(3)],
            out_specs=pl.BlockSpec((tr, c), lambda i, chip_ref: (i, 0))),
        out_shape=jax.ShapeDtypeStruct((r, c), COMM_DTYPE), compiler_params=_params(("arbitrary",)),
    )(chip, own, land, land, land)


def _sum_slots(name, slots):
    n, r, c = slots.shape
    tr = _tile(r, 256)

    def body(s_ref, o_ref):
        tot = s_ref[0].astype(jnp.float32)
        for s in range(1, n):
            tot = tot + s_ref[s].astype(jnp.float32)
        o_ref[...] = tot

    return pl.pallas_call(
        body, name=name, grid=(r // tr,),
        in_specs=[pl.BlockSpec((n, tr, c), lambda i: (0, i, 0))], out_specs=_row_spec(tr, c),
        out_shape=jax.ShapeDtypeStruct((r, c), jnp.float32), compiler_params=_params(("arbitrary",)),
    )(slots)


def _adamw_math(w, g, m, v):
    m = ADAM_B1 * m + (1.0 - ADAM_B1) * g
    v = ADAM_B2 * v + (1.0 - ADAM_B2) * (g * g)
    m_hat = m / (1.0 - ADAM_B1 ** ADAM_STEP)
    v_hat = v / (1.0 - ADAM_B2 ** ADAM_STEP)
    delta = -ADAM_LR * (m_hat / (jnp.sqrt(v_hat) + ADAM_EPS) + ADAM_WD * w)
    return delta, m, v


def _adamw(name, w, m, v, grad_parts):
    r, c = w.shape
    tr = _tile(r, 256)
    n_parts = len(grad_parts)

    def body(*refs):
        w_ref, m_ref, v_ref = refs[:3]
        part_refs = refs[3:3 + n_parts]
        g_ref, d_ref, nm_ref, nv_ref = refs[3 + n_parts:]
        g = part_refs[0][...].astype(jnp.float32)
        for p in part_refs[1:]:
            g = g + p[...].astype(jnp.float32)
        delta, nm, nv = _adamw_math(w_ref[...], g, m_ref[...], v_ref[...])
        g_ref[...] = g
        d_ref[...] = delta
        nm_ref[...] = nm
        nv_ref[...] = nv

    spec = _row_spec(tr, c)
    out = jax.ShapeDtypeStruct((r, c), jnp.float32)
    return pl.pallas_call(
        body, name=name, grid=(r // tr,), in_specs=[spec] * (3 + n_parts), out_specs=[spec] * 4,
        out_shape=[out] * 4, compiler_params=_params(("arbitrary",)),
    )(w, m, v, *grad_parts)


def _chip_peers():
    x, y, c = lax.axis_index("x"), lax.axis_index("y"), lax.axis_index("c")
    return x, y, c, [(1 - x, y), (x, 1 - y), (1 - x, 1 - y)]


HBM = pl.BlockSpec(memory_space=pltpu.HBM)
SEM = pl.BlockSpec(memory_space=pltpu.SEMAPHORE)
TOKEN = jax.ShapeDtypeStruct((8, 128), jnp.float32)
N_PEER_CHIPS = N_CHIPS - 1


def _in_flight():
    return pltpu.CompilerParams(has_side_effects=pltpu.SideEffectType.DATAFLOW_SIDE_EFFECTING)


def _slot(ref, slot):
    if len(ref.shape) == 3:
        return ref.at[slot]
    width = ref.shape[1] // N_CHIPS
    return ref.at[:, pl.ds(pl.multiple_of(slot * width, 128), width)]


def _half_slot(ref, slot, half):
    rows = ref.shape[-2] // 2
    if len(ref.shape) == 3:
        return ref.at[slot, pl.ds(half * rows, rows)]
    width = ref.shape[1] // N_CHIPS
    return ref.at[pl.ds(half * rows, rows), pl.ds(pl.multiple_of(slot * width, 128), width)]


def _slot_shape(shape):
    return shape[1:] if len(shape) == 3 else (shape[0], shape[1] // N_CHIPS)


def _gather_start(name, full, deps=()):
    def body(full_ref, *rest):
        send_sems, recv_sems, _, token_ref = rest[len(deps):]
        x, y, c, peers = _chip_peers()
        mine = _half_slot(full_ref, 2 * x + y, c)
        for p, (px, py) in enumerate(peers):
            pltpu.make_async_remote_copy(
                src_ref=mine, dst_ref=mine, send_sem=send_sems.at[p], recv_sem=recv_sems.at[p],
                device_id=(px, py, c), device_id_type=MESH).start()
        token_ref[...] = jnp.zeros_like(token_ref)

    return pl.pallas_call(
        body, name=name,
        out_shape=(pltpu.SemaphoreType.DMA((N_PEER_CHIPS,)), pltpu.SemaphoreType.DMA((N_PEER_CHIPS,)),
                   pltpu.HBM(full.shape, full.dtype), TOKEN),
        in_specs=[HBM] + [ANY] * len(deps), out_specs=(SEM, SEM, HBM, pl.BlockSpec(memory_space=pltpu.VMEM)),
        input_output_aliases={0: 2}, compiler_params=_in_flight(),
    )(pltpu.with_memory_space_constraint(full, pltpu.HBM), *deps)


def _gather_wait(name, send_sems, recv_sems, full, after):
    def body(full_ref, send_sems, recv_sems, after_ref, out_ref):
        x, y, c, peers = _chip_peers()
        for p, (px, py) in enumerate(peers):
            cp = pltpu.make_async_remote_copy(
                src_ref=_half_slot(full_ref, 2 * x + y, c), dst_ref=_half_slot(full_ref, 2 * px + py, c),
                send_sem=send_sems.at[p], recv_sem=recv_sems.at[p], device_id=(px, py, c), device_id_type=MESH)
            cp.wait_send()
            cp.wait_recv()

    return pl.pallas_call(
        body, name=name, out_shape=pltpu.HBM(full.shape, full.dtype),
        in_specs=(HBM, SEM, SEM, ANY), out_specs=HBM, input_output_aliases={0: 0}, compiler_params=_in_flight(),
    )(full, send_sems, recv_sems, after)


def _forward_sibling(name, fulls):
    n = len(fulls)

    def body(*refs):
        in_refs, out_refs = refs[:n], refs[n:2 * n]
        send_sems, recv_sems = refs[2 * n:]
        x, y, c, peers = _chip_peers()
        sends, recvs = [], []
        for a in range(n):
            for p, (px, py) in enumerate(peers):
                k = N_PEER_CHIPS * a + p
                slot = 2 * px + py
                cp = pltpu.make_async_remote_copy(
                    src_ref=_half_slot(in_refs[a], slot, c), dst_ref=_half_slot(out_refs[a], slot, c),
                    send_sem=send_sems.at[k], recv_sem=recv_sems.at[k], device_id=(x, y, 1 - c), device_id_type=MESH)
                cp.start()
                sends.append(cp)
                recvs.append(pltpu.make_async_remote_copy(
                    src_ref=_half_slot(in_refs[a], slot, c), dst_ref=_half_slot(out_refs[a], slot, 1 - c),
                    send_sem=send_sems.at[k], recv_sem=recv_sems.at[k], device_id=(x, y, 1 - c), device_id_type=MESH))
        for cp in recvs:
            cp.wait_recv()
        for cp in sends:
            cp.wait_send()

    return pl.pallas_call(
        body, name=name, in_specs=[ANY] * n, out_specs=[ANY] * n,
        out_shape=[jax.ShapeDtypeStruct(f.shape, f.dtype) for f in fulls],
        input_output_aliases={a: a for a in range(n)},
        scratch_shapes=[pltpu.SemaphoreType.DMA((N_PEER_CHIPS * n,)), pltpu.SemaphoreType.DMA((N_PEER_CHIPS * n,))],
    )(*fulls)


def _scatter_start(name, grads, deps=()):
    def body(g_ref, land_ref, *rest):
        send_sems, recv_sems, _, _, token_ref = rest[len(deps):]
        x, y, c, peers = _chip_peers()
        me = 2 * x + y
        for p, (px, py) in enumerate(peers):
            pltpu.make_async_remote_copy(
                src_ref=_slot(g_ref, 2 * px + py), dst_ref=land_ref.at[me], send_sem=send_sems.at[p],
                recv_sem=recv_sems.at[p], device_id=(px, py, c), device_id_type=MESH).start()
        token_ref[...] = jnp.zeros_like(token_ref)

    land = lax.empty((N_CHIPS,) + _slot_shape(grads.shape), grads.dtype)
    return pl.pallas_call(
        body, name=name,
        out_shape=(pltpu.SemaphoreType.DMA((N_PEER_CHIPS,)), pltpu.SemaphoreType.DMA((N_PEER_CHIPS,)),
                   pltpu.HBM(grads.shape, grads.dtype), pltpu.HBM(land.shape, land.dtype), TOKEN),
        in_specs=[HBM, HBM] + [ANY] * len(deps),
        out_specs=(SEM, SEM, HBM, HBM, pl.BlockSpec(memory_space=pltpu.VMEM)),
        input_output_aliases={0: 2, 1: 3}, compiler_params=_in_flight(),
    )(pltpu.with_memory_space_constraint(grads, pltpu.HBM), pltpu.with_memory_space_constraint(land, pltpu.HBM), *deps)


def _scatter_wait(name, send_sems, recv_sems, grads, land, after):
    def body(g_ref, land_ref, send_sems, recv_sems, after_ref, g_out, land_out):
        x, y, c, peers = _chip_peers()
        for p, (px, py) in enumerate(peers):
            cp = pltpu.make_async_remote_copy(
                src_ref=_slot(g_ref, 2 * px + py), dst_ref=land_ref.at[2 * px + py], send_sem=send_sems.at[p],
                recv_sem=recv_sems.at[p], device_id=(px, py, c), device_id_type=MESH)
            cp.wait_send()
            cp.wait_recv()

    return pl.pallas_call(
        body, name=name,
        out_shape=(pltpu.HBM(grads.shape, grads.dtype), pltpu.HBM(land.shape, land.dtype)),
        in_specs=(HBM, HBM, SEM, SEM, ANY), out_specs=(HBM, HBM), input_output_aliases={0: 0, 1: 1},
        compiler_params=_in_flight(),
    )(grads, land, send_sems, recv_sems, after)


def _swap_sibling(name, parts):
    n = len(parts)

    def body(*refs):
        in_refs, out_refs = refs[:n], refs[n:2 * n]
        send_sems, recv_sems = refs[2 * n:]
        x, y, c = lax.axis_index("x"), lax.axis_index("y"), lax.axis_index("c")
        copies = []
        for a in range(n):
            cp = pltpu.make_async_remote_copy(
                src_ref=in_refs[a], dst_ref=out_refs[a], send_sem=send_sems.at[a], recv_sem=recv_sems.at[a],
                device_id=(x, y, 1 - c), device_id_type=MESH)
            cp.start()
            copies.append(cp)
        for cp in copies:
            cp.wait_recv()
        for cp in copies:
            cp.wait_send()

    return pl.pallas_call(
        body, name=name, in_specs=[ANY] * n, out_specs=[ANY] * n,
        out_shape=[jax.ShapeDtypeStruct(p.shape, p.dtype) for p in parts],
        scratch_shapes=[pltpu.SemaphoreType.DMA((n,)), pltpu.SemaphoreType.DMA((n,))],
    )(*parts)


def _gather_devices(name, block):
    def body(in_ref, out_ref, send_sems, recv_sems, local_sem):
        x, y, c = lax.axis_index("x"), lax.axis_index("y"), lax.axis_index("c")
        me = 4 * x + 2 * y + c
        local = pltpu.make_async_copy(in_ref, out_ref.at[me], local_sem)
        local.start()
        sends, recvs = [], []
        k = 0
        for fx in range(2):
            for fy in range(2):
                for fc in range(2):
                    if fx == fy == fc == 0:
                        continue
                    px = x if fx == 0 else 1 - x
                    py = y if fy == 0 else 1 - y
                    pc = c if fc == 0 else 1 - c
                    cp = pltpu.make_async_remote_copy(
                        src_ref=in_ref, dst_ref=out_ref.at[me], send_sem=send_sems.at[k], recv_sem=recv_sems.at[k],
                        device_id=(px, py, pc), device_id_type=MESH)
                    cp.start()
                    sends.append(cp)
                    recvs.append(pltpu.make_async_remote_copy(
                        src_ref=in_ref, dst_ref=out_ref.at[4 * px + 2 * py + pc], send_sem=send_sems.at[k],
                        recv_sem=recv_sems.at[k], device_id=(px, py, pc), device_id_type=MESH))
                    k += 1
        for cp in recvs:
            cp.wait_recv()
        for cp in sends:
            cp.wait_send()
        local.wait()

    return pl.pallas_call(
        body, name=name, in_specs=[ANY], out_specs=ANY,
        out_shape=jax.ShapeDtypeStruct((N_DEV,) + block.shape, block.dtype),
        scratch_shapes=[pltpu.SemaphoreType.DMA((N_DEV - 1,)), pltpu.SemaphoreType.DMA((N_DEV - 1,)),
                        pltpu.SemaphoreType.DMA],
    )(block)


def _pack_rows(pieces, width):
    flat = jnp.concatenate([p.reshape(-1) for p in pieces])
    rows = -(-flat.shape[0] // width)
    rows = -(-rows // 8) * 8
    flat = jnp.pad(flat, (0, rows * width - flat.shape[0]))
    return flat.reshape(rows, width)


def _unpack_rows(packed, shapes):
    flat = packed.reshape(-1)
    out, off = [], 0
    for s in shapes:
        size = 1
        for d in s:
            size *= d
        out.append(flat[off:off + size].reshape(s))
        off += size
    return out


def kernel(x, ln_mix_pre, w_in, conv_w, pool_w, pool_scale, w_out, ln_mix_post, ln_ffn_pre, w_gate, w_up, w_down, ln_ffn_post, loss_target, m_ln_mix_pre, m_w_in, m_conv_w, m_pool_w, m_pool_scale, m_w_out, m_ln_mix_post, m_ln_ffn_pre, m_w_gate, m_w_up, m_w_down, m_ln_ffn_post, v_ln_mix_pre, v_w_in, v_conv_w, v_pool_w, v_pool_scale, v_w_out, v_ln_mix_post, v_ln_ffn_pre, v_w_gate, v_w_up, v_w_down, v_ln_ffn_post):
    t, d = x.shape[1], x.shape[2]
    e4 = w_in.shape[2]
    e = N_CHIPS * e4
    f4 = w_gate.shape[2]
    f = N_CHIPS * f4
    n_groups, dg4, dg = pool_w.shape[1], pool_w.shape[2], pool_w.shape[3]
    cw4 = conv_w.shape[2]
    chip = 2 * lax.axis_index("x") + lax.axis_index("y")
    xs, tgt = x[0], loss_target[0]

    big = {"w_in": w_in[0], "w_out": w_out[0], "w_gate": w_gate[0], "w_up": w_up[0], "w_down": w_down[0],
           "pool_w": pool_w[0].reshape(n_groups * dg4, dg)}
    names = ["w_in", "pool_w", "w_out", "w_gate", "w_up", "w_down"]
    chip_arr = chip.astype(jnp.int32).reshape(1)
    conv_all = _gather_devices("gather_conv_w", _pack_rows([conv_w[0]], 128))
    conv_full = jnp.concatenate(
        [conv_all[2 * j].reshape(-1)[:CONV_K * cw4].reshape(CONV_K, cw4) for j in range(N_CHIPS)], axis=1)
    in_flight, deps = {}, (conv_all,)
    for k in names:
        by_columns = k in ("w_gate", "w_up")
        in_flight[k] = _gather_start(
            "gather_start_" + k, _cast_to_slot("cast_" + k, big[k], chip_arr, by_columns), deps)
        deps = (in_flight[k][3],)

    def landed(ks, after):
        fulls = []
        for k in ks:
            send_sems, recv_sems, full, _ = in_flight[k]
            fulls.append(_gather_wait("gather_wait_" + k, send_sems, recv_sems, full, after))
            after = fulls[-1]
        return _forward_sibling("forward_" + ks[0], fulls)

    def rows3(tile, width):
        return pl.BlockSpec((tile, width), lambda i, j, k: (i, 0))

    gain3 = pl.BlockSpec((1, d), lambda i, j, k: (0, 0))
    f32_td = jax.ShapeDtypeStruct((t, d), jnp.float32)
    mm_td = jax.ShapeDtypeStruct((t, d), MM_DTYPE)
    f32_gain = jax.ShapeDtypeStruct((1, d), jnp.float32)

    h = _pre_norm(xs, ln_mix_pre, deps)
    (win_g,) = landed(["w_in"], h)
    tm = _tile(t, 1024)
    tm2 = _tile(t, 2048)
    proj = _matmul(
        "in_proj", grid=(t // tm2, N_CHIPS, 1), mode="nn",
        pairs=[[(h, win_g)]],
        pair_specs=[[(rows3(tm2, d), pl.BlockSpec((None, d, e4), lambda i, j, k: (j, 0, 0)))]],
        acc_shapes=[(tm2, e4)], out_shapes=[jax.ShapeDtypeStruct((t, e), jnp.float32)],
        out_specs=[pl.BlockSpec((tm2, e4), lambda i, j, k: (i, j))], epilogue=_identity_epilogue)[0]
    pool_g, wout_full = landed(["pool_w", "w_out"], proj)
    pool_g, wout_full = pool_g.reshape(N_CHIPS, n_groups, dg4, dg), wout_full.reshape(d, d)
    mixed = _mixers_fwd(proj, conv_full, pool_g, pool_scale)

    def post_mix_epilogue(accs, extra_refs):
        x_ref, g2_ref, g3_ref = extra_refs
        mo = accs[0]
        x1 = x_ref[...] + mo * _rstd(mo) * g2_ref[...]
        return mo, x1, x1 * _rstd(x1) * g3_ref[...]

    tm_mix = _tile(t, 512)
    mix_out, x1, hf = _matmul(
        "out_proj", grid=(t // tm_mix, 1, 1), mode="nn",
        pairs=[[(mixed, wout_full)]],
        pair_specs=[[(rows3(tm_mix, d), pl.BlockSpec((d, d), lambda i, j, k: (0, 0)))]],
        acc_shapes=[(tm_mix, d)], extras=[xs, ln_mix_post, ln_ffn_pre],
        extra_specs=[rows3(tm_mix, d), gain3, gain3], out_shapes=[f32_td, f32_td, mm_td],
        out_specs=[rows3(tm_mix, d)] * 3, epilogue=post_mix_epilogue, epi_rows=EPILOGUE_ROWS)
    wg_full, wu_full = landed(["w_gate", "w_up"], hf)

    def gate_up_epilogue(accs, extra_refs):
        g, up = accs
        sig = jax.nn.sigmoid(g)
        silu = g * sig
        return up * (sig + silu * (1.0 - sig)), silu, silu * up

    tf = _tile(f, 512)
    ff_tile = jax.ShapeDtypeStruct((t, f), MM_DTYPE)
    act_by_g, act_by_up, act = _matmul(
        "gate_up", grid=(f // tf, t // tm, 1), mode="nn",
        pairs=[[(hf, wg_full)], [(hf, wu_full)]],
        pair_specs=[[(pl.BlockSpec((tm, d), lambda j, i, k: (i, 0)),
                      pl.BlockSpec((d, tf), lambda j, i, k: (0, j)))]] * 2,
        acc_shapes=[(tm, tf)] * 2, out_shapes=[ff_tile] * 3,
        out_specs=[pl.BlockSpec((tm, tf), lambda j, i, k: (i, j))] * 3, epilogue=gate_up_epilogue)
    wdown_full = landed(["w_down"], act)[0].reshape(f, d)

    def loss_epilogue(accs, extra_refs):
        x1_ref, tg_ref, g_ref = extra_refs
        ff_v = accs[0]
        rstd = _rstd(ff_v)
        n = ff_v * rstd
        g = g_ref[...]
        err = x1_ref[...] + n * g - tg_ref[...]
        rows_loss = 0.5 * jnp.sum(jnp.mean(err * err, axis=-1, keepdims=True), axis=0, keepdims=True)
        dout = err / d
        return (dout, _norm_bwd(dout * g, n, rstd), jnp.broadcast_to(rows_loss, (8, 128)),
                jnp.sum(dout * n, axis=0, keepdims=True))

    tm_row = _tile(t, 1024)
    dout, dff, loss_tile, dg_ffn_post = _matmul(
        "down_proj", grid=(t // tm_row, 1, f // tf), mode="nn",
        pairs=[[(act, wdown_full)]],
        pair_specs=[[(pl.BlockSpec((tm_row, tf), lambda i, j, k: (i, k)),
                      pl.BlockSpec((tf, d), lambda i, j, k: (k, 0)))]],
        acc_shapes=[(tm_row, d)], extras=[x1, tgt, ln_ffn_post], extra_specs=[STREAM] * 2 + [gain3],
        out_shapes=[f32_td, mm_td, jax.ShapeDtypeStruct((8, 128), jnp.float32), f32_gain],
        out_specs=[STREAM] * 2 + [pl.BlockSpec((8, 128), lambda i, j, k: (0, 0)), gain3],
        epilogue=loss_epilogue, epi_rows=EPILOGUE_ROWS, sum_outs=(2, 3))

    def dact_epilogue(accs, extra_refs):
        dact = accs[0]
        return dact * extra_refs[0][...].astype(jnp.float32), dact * extra_refs[1][...].astype(jnp.float32)

    ff_spec_ij = pl.BlockSpec((tm2, tf), lambda i, j, k: (i, j))
    dg_act, dup_act = _matmul(
        "dact", grid=(t // tm2, f // tf, 1), mode="nt",
        pairs=[[(dff, wdown_full)]],
        pair_specs=[[(rows3(tm2, d), pl.BlockSpec((tf, d), lambda i, j, k: (j, 0)))]],
        acc_shapes=[(tm2, tf)], extras=[act_by_g, act_by_up], extra_specs=[ff_spec_ij, ff_spec_ij],
        out_shapes=[ff_tile] * 2, out_specs=[ff_spec_ij] * 2, epilogue=dact_epilogue)
    tk = _tile(t, 2048)
    dw_down = _matmul(
        "dw_down", grid=(N_CHIPS, 1, t // tk), mode="tn",
        pairs=[[(act, dff)]],
        pair_specs=[[(pl.BlockSpec((tk, f4), lambda i, j, k: (k, i)),
                      pl.BlockSpec((tk, d), lambda i, j, k: (k, 0)))]],
        acc_shapes=[(f4, d)], out_shapes=[jax.ShapeDtypeStruct((N_CHIPS, f4, d), COMM_DTYPE)],
        out_specs=[pl.BlockSpec((None, f4, d), lambda i, j, k: (i, 0, 0))], epilogue=_identity_epilogue)[0]
    tn = _tile(d, 1024)
    leaving = {"w_down": _scatter_start("scatter_start_w_down", dw_down)}

    def ffn_pre_epilogue(accs, extra_refs):
        x1_ref, dout_ref, mo_ref, g3_ref, g2_ref = extra_refs
        dhf_v = accs[0]
        x1_v = x1_ref[...]
        rstd3 = _rstd(x1_v)
        n3 = x1_v * rstd3
        dx1_v = dout_ref[...] + _norm_bwd(dhf_v * g3_ref[...], n3, rstd3)
        mo = mo_ref[...]
        rstd2 = _rstd(mo)
        n2 = mo * rstd2
        return (dx1_v, _norm_bwd(dx1_v * g2_ref[...], n2, rstd2), jnp.sum(dhf_v * n3, axis=0, keepdims=True),
                jnp.sum(dx1_v * n2, axis=0, keepdims=True))

    dx1, dmo, dg_ffn_pre, dg_mix_post = _matmul(
        "dhf", grid=(t // tm_row, 1, f // tf), mode="nt",
        pairs=[[(dg_act, wg_full), (dup_act, wu_full)]],
        pair_specs=[[(pl.BlockSpec((tm_row, tf), lambda i, j, k: (i, k)),
                      pl.BlockSpec((d, tf), lambda i, j, k: (0, k)))] * 2],
        acc_shapes=[(tm_row, d)], extras=[x1, dout, mix_out, ln_ffn_pre, ln_mix_post],
        extra_specs=[STREAM] * 3 + [gain3] * 2, out_shapes=[f32_td, mm_td, f32_gain, f32_gain],
        out_specs=[STREAM] * 2 + [gain3] * 2, epilogue=ffn_pre_epilogue, epi_rows=EPILOGUE_ROWS,
        sum_outs=(2, 3), deps=leaving["w_down"][4:])
    tmo = _tile(d, 2048)
    grad_ff = jax.ShapeDtypeStruct((d, f), COMM_DTYPE)
    dw_gate, dw_up = _matmul(
        "dw_gate_up", grid=(d // tmo, f // tf, t // tm2), mode="tn",
        pairs=[[(hf, dg_act)], [(hf, dup_act)]],
        pair_specs=[[(pl.BlockSpec((tm2, tmo), lambda i, j, k: (k, i)),
                      pl.BlockSpec((tm2, tf), lambda i, j, k: (k, j)))]] * 2,
        acc_shapes=[(tmo, tf)] * 2, out_shapes=[grad_ff] * 2,
        out_specs=[pl.BlockSpec((tmo, tf), lambda i, j, k: (i, j))] * 2, epilogue=_identity_epilogue)
    leaving["w_gate"] = _scatter_start("scatter_start_w_gate", dw_gate)
    leaving["w_up"] = _scatter_start("scatter_start_w_up", dw_up, leaving["w_gate"][4:])

    dmixed = _matmul(
        "dmixed", grid=(t // tm, 1, 1), mode="nt",
        pairs=[[(dmo, wout_full)]],
        pair_specs=[[(rows3(tm, d), pl.BlockSpec((d, d), lambda i, j, k: (0, 0)))]],
        acc_shapes=[(tm, d)], out_shapes=[f32_td],
        out_specs=[rows3(tm, d)], epilogue=_identity_epilogue,
        deps=leaving["w_up"][4:])[0]
    tmo = _tile(d, 2048)
    dw_out = _matmul(
        "dw_out", grid=(d // tmo, d // tn, t // tk), mode="tn",
        pairs=[[(mixed, dmo)]],
        pair_specs=[[(pl.BlockSpec((tk, tmo), lambda i, j, k: (k, i)),
                      pl.BlockSpec((tk, tn), lambda i, j, k: (k, j)))]],
        acc_shapes=[(tmo, tn)], out_shapes=[jax.ShapeDtypeStruct((d, d), COMM_DTYPE)],
        out_specs=[pl.BlockSpec((tmo, tn), lambda i, j, k: (i, j))], epilogue=_identity_epilogue)[0]
    leaving["w_out"] = _scatter_start("scatter_start_w_out", dw_out.reshape(N_CHIPS, d // N_CHIPS, d))
    dproj, dconv_full, dpool_g, dpool_scale = _mixers_bwd(proj, dmixed, conv_full, pool_g, pool_scale,
                                                          leaving["w_out"][4:])
    dpool_slots = _cast_rows("cast_dpool", dpool_g.reshape(N_CHIPS * n_groups * dg4, dg), COMM_DTYPE)
    leaving["pool_w"] = _scatter_start("scatter_start_pool_w", dpool_slots.reshape(N_CHIPS, n_groups * dg4, dg))
    dw_in = _matmul(
        "dw_in", grid=(d // tmo, N_CHIPS, t // tk), mode="tn",
        pairs=[[(h, dproj)]],
        pair_specs=[[(pl.BlockSpec((tk, tmo), lambda i, j, k: (k, i)),
                      pl.BlockSpec((tk, e4), lambda i, j, k: (k, j)))]],
        acc_shapes=[(tmo, e4)], out_shapes=[jax.ShapeDtypeStruct((N_CHIPS, d, e4), COMM_DTYPE)],
        out_specs=[pl.BlockSpec((None, tmo, e4), lambda i, j, k: (j, i, 0))], epilogue=_identity_epilogue,
        deps=leaving["pool_w"][4:])[0]
    leaving["w_in"] = _scatter_start("scatter_start_w_in", dw_in)

    def mix_pre_epilogue(accs, extra_refs):
        x_ref, dx1_ref, g_ref = extra_refs
        dh_v = accs[0]
        xv = x_ref[...]
        rstd = _rstd(xv)
        n = xv * rstd
        return dx1_ref[...] + _norm_bwd(dh_v * g_ref[...], n, rstd), jnp.sum(dh_v * n, axis=0, keepdims=True)

    grad_x, dg_mix_pre = _matmul(
        "dh", grid=(t // tm_row, 1, N_CHIPS), mode="nt",
        pairs=[[(dproj, win_g)]],
        pair_specs=[[(pl.BlockSpec((tm_row, e4), lambda i, j, k: (i, k)),
                      pl.BlockSpec((None, d, e4), lambda i, j, k: (k, 0, 0)))]],
        acc_shapes=[(tm_row, d)], extras=[xs, dx1, ln_mix_pre], extra_specs=[STREAM] * 2 + [gain3],
        out_shapes=[f32_td, f32_gain], out_specs=[STREAM, gain3], epilogue=mix_pre_epilogue,
        epi_rows=EPILOGUE_ROWS, sum_outs=(1,), deps=leaving["w_in"][4:], prefetch_extras=True)

    names = ["w_down", "w_gate", "w_up", "w_out", "pool_w", "w_in"]
    partial, after = [], grad_x
    for k in names:
        send_sems, recv_sems, own, land, _ = leaving[k]
        own, land = _scatter_wait("scatter_wait_" + k, send_sems, recv_sems, own, land, after)
        partial.append(_sum_own_and_received("sum_" + k, own, land, chip_arr))
        after = land
    other = _swap_sibling("swap_grads", partial)
    moments = {"w_in": (m_w_in, v_w_in), "w_out": (m_w_out, v_w_out), "w_gate": (m_w_gate, v_w_gate),
               "w_up": (m_w_up, v_w_up), "w_down": (m_w_down, v_w_down), "pool_w": (m_pool_w, v_pool_w)}
    result = {}
    for k, mine, theirs in zip(names, partial, other):
        shape = moments[k][0].shape
        two_d = big[k].shape
        outs = _adamw("adamw_" + k, big[k], moments[k][0].reshape(two_d), moments[k][1].reshape(two_d),
                      [mine, theirs])
        result[k] = [o.reshape(shape) for o in outs]

    small_shapes = [(1, d)] * 4 + [pool_scale.shape, (CONV_K, N_CHIPS * cw4)]
    packed = _pack_rows([dg_mix_pre, dg_mix_post, dg_ffn_pre, dg_ffn_post, dpool_scale, dconv_full], 1024)
    summed = _sum_slots("sum_small", _gather_devices("gather_small", packed))
    g_mix_pre, g_mix_post, g_ffn_pre, g_ffn_post, g_pool_scale, g_conv_full = _unpack_rows(summed, small_shapes)
    g_conv = lax.dynamic_slice(g_conv_full, (0, chip * cw4), (CONV_K, cw4))[None]
    small = [("ln_mix_pre", ln_mix_pre, m_ln_mix_pre, v_ln_mix_pre, g_mix_pre),
             ("conv_w", conv_w, m_conv_w, v_conv_w, g_conv),
             ("pool_scale", pool_scale, m_pool_scale, v_pool_scale, g_pool_scale),
             ("ln_mix_post", ln_mix_post, m_ln_mix_post, v_ln_mix_post, g_mix_post),
             ("ln_ffn_pre", ln_ffn_pre, m_ln_ffn_pre, v_ln_ffn_pre, g_ffn_pre),
             ("ln_ffn_post", ln_ffn_post, m_ln_ffn_post, v_ln_ffn_post, g_ffn_post)]
    shapes_small = [s[1].shape for s in small]
    packs = [_pack_rows([s[q] for s in small], 128) for q in (1, 2, 3, 4)]
    outs = _adamw("adamw_small", packs[0], packs[1], packs[2], [packs[3]])
    unpacked = [_unpack_rows(o, shapes_small) for o in outs]
    for idx, s in enumerate(small):
        result[s[0]] = [u[idx] for u in unpacked]

    loss = lax.psum(loss_tile[0, 0], ("x", "y", "c"))
    order = ["ln_mix_pre", "w_in", "conv_w", "pool_w", "pool_scale", "w_out", "ln_mix_post", "ln_ffn_pre",
             "w_gate", "w_up", "w_down", "ln_ffn_post"]
    return (loss, grad_x[None], *[result[k][0] for k in order], *[result[k][1] for k in order],
            *[result[k][2] for k in order], *[result[k][3] for k in order])
```

```python
import functools

import jax
import jax.numpy as jnp
from jax import lax
from jax.experimental import pallas as pl
from jax.experimental.pallas import tpu as pltpu

EPS = 1e-6
CONV_HEAD_DIM = 128
CONV_K = 3
POOL_WINDOWS = (2, 4, 8, 16)
HALO = 16
EPILOGUE_ROWS = 64
STREAM_ROWS = 256
N_CHIPS = 4
N_DEV = 8

ADAM_LR = 0.001
ADAM_B1 = 0.9
ADAM_B2 = 0.999
ADAM_EPS = 1e-08
ADAM_WD = 0.01
ADAM_STEP = 10

MM_DTYPE = jnp.bfloat16
COMM_DTYPE = jnp.bfloat16
VMEM_LIMIT = 62 * 1024 * 1024
MESH = pl.DeviceIdType.MESH
ANY = pl.BlockSpec(memory_space=pl.ANY)
STREAM = "stream"


def _tile(n, pref):
    t = min(pref, n)
    while n % t:
        t //= 2
    return t


def _params(sem):
    return pltpu.CompilerParams(dimension_semantics=sem, vmem_limit_bytes=VMEM_LIMIT)


def _rstd(x):
    return lax.rsqrt(jnp.mean(x * x, axis=-1, keepdims=True) + EPS)


def _norm_bwd(dn, n, rstd):
    return rstd * (dn - n * jnp.mean(dn * n, axis=-1, keepdims=True))


_DOT_DIMS = {
    "nn": (((1,), (0,)), ((), ())),
    "nt": (((1,), (1,)), ((), ())),
    "tn": (((0,), (0,)), ((), ())),
}


def _dot(a, b, mode):
    return lax.dot_general(a.astype(MM_DTYPE), b.astype(MM_DTYPE), _DOT_DIMS[mode],
                           preferred_element_type=jnp.float32)


def _matmul(name, *, grid, mode, pairs, pair_specs, acc_shapes, extras=(), extra_specs=(),
            out_shapes, out_specs, epilogue, deps=(), epi_rows=0, sum_outs=(), prefetch_extras=False):
    nk = grid[2]
    operands, operand_specs, where, counts = [], [], {}, []
    pair_index = []
    for ps, ss in zip(pairs, pair_specs):
        counts.append(len(ps))
        for arrays, specs in zip(ps, ss):
            for arr, spec in zip(arrays, specs):
                key = (id(arr), id(spec))
                if key not in where:
                    where[key] = len(operands)
                    operands.append(arr)
                    operand_specs.append(spec)
                pair_index.append(where[key])
    n_operands = len(operands)
    n_extra = len(extras)
    n_out = len(out_shapes)
    n_in = n_operands + n_extra + len(deps)
    in_streams = [q for q, s in enumerate(extra_specs) if s is STREAM]
    out_streams = [q for q, s in enumerate(out_specs) if s is STREAM]
    in_buf_rows = (acc_shapes[0][0],) if prefetch_extras else (2, STREAM_ROWS)
    stream_bufs = ([pltpu.VMEM(in_buf_rows + (extras[q].shape[1],), extras[q].dtype) for q in in_streams]
                   + [pltpu.VMEM((2, STREAM_ROWS, out_shapes[q].shape[1]), out_shapes[q].dtype) for q in out_streams])
    n_streams = len(stream_bufs)
    n_acc = 0 if nk == 1 else len(acc_shapes)

    def body(*refs):
        pair_refs = [refs[q] for q in pair_index]
        extra_refs = refs[n_operands:n_operands + n_extra]
        out_refs = refs[n_in:n_in + n_out]
        acc_refs = refs[n_in + n_out:n_in + n_out + n_acc]
        bufs = refs[n_in + n_out + n_acc:n_in + n_out + n_acc + n_streams]
        i, k = pl.program_id(0), pl.program_id(2)

        def stream_copies():
            sems = refs[-1]
            n_rows = acc_refs[0].shape[0]

            def hbm_rows(ref, c):
                return ref.at[pl.ds(pl.multiple_of(i * n_rows + c * STREAM_ROWS, STREAM_ROWS), STREAM_ROWS)]

            def fetch(s, c):
                if prefetch_extras:
                    return pltpu.make_async_copy(
                        extra_refs[in_streams[s]].at[pl.ds(pl.multiple_of(i * n_rows, STREAM_ROWS), n_rows)],
                        bufs[s], sems.at[s, 0])
                return pltpu.make_async_copy(hbm_rows(extra_refs[in_streams[s]], c), bufs[s].at[c % 2],
                                             sems.at[s, c % 2])

            def drain(s, c):
                return pltpu.make_async_copy(bufs[len(in_streams) + s].at[c % 2], hbm_rows(out_refs[out_streams[s]], c),
                                             sems.at[len(in_streams) + s, c % 2])

            return fetch, drain

        def streamed_finish(accs):
            fetch, drain = stream_copies()
            n_chunks = accs[0].shape[0] // STREAM_ROWS
            for c in range(n_chunks):
                for s in range(len(in_streams)):
                    if c + 1 < n_chunks and not prefetch_extras:
                        fetch(s, c + 1).start()
                    if c == 0 or not prefetch_extras:
                        fetch(s, c).wait()
                for s in range(len(out_streams)):
                    if c >= 2:
                        drain(s, c - 2).wait()
                for r0 in range(0, STREAM_ROWS, epi_rows):
                    sub = slice(r0, r0 + epi_rows)
                    rows = slice(c * STREAM_ROWS + r0, c * STREAM_ROWS + r0 + epi_rows)
                    views = [(bufs[in_streams.index(q)].at[rows] if prefetch_extras
                              else bufs[in_streams.index(q)].at[c % 2, sub]) if q in in_streams else e
                             for q, e in enumerate(extra_refs)]
                    outs = epilogue([a[rows, :] for a in accs], views)
                    for q, (o_ref, o) in enumerate(zip(out_refs, outs)):
                        if q in out_streams:
                            bufs[len(in_streams) + out_streams.index(q)][c % 2, sub, :] = o.astype(o_ref.dtype)
                        elif c == 0 and r0 == 0:
                            _accumulate(o_ref, o, i)
                        else:
                            o_ref[...] += o
                for s in range(len(out_streams)):
                    drain(s, c).start()
            for s in range(len(out_streams)):
                for c in range(max(n_chunks - 2, 0), n_chunks):
                    drain(s, c).wait()

        def partial_sums():
            res, p = [], 0
            for cnt in counts:
                tot = None
                for _ in range(cnt):
                    d = _dot(pair_refs[p][...], pair_refs[p + 1][...], mode)
                    tot = d if tot is None else tot + d
                    p += 2
                res.append(tot)
            return res

        def finish(accs):
            n_rows = accs[0].shape[0]
            step = epi_rows or n_rows
            for r0 in range(0, n_rows, step):
                rows = slice(r0, r0 + step)
                outs = epilogue([a[rows, :] for a in accs], [e.at[rows] if e.shape[0] == n_rows else e
                                                             for e in extra_refs])
                for q, (o_ref, o) in enumerate(zip(out_refs, outs)):
                    if q not in sum_outs:
                        o_ref[rows, :] = o.astype(o_ref.dtype)
                    elif r0 == 0:
                        _accumulate(o_ref, o, i)
                    else:
                        o_ref[...] += o

        if nk == 1:
            finish(partial_sums())
        else:
            if n_streams:
                @pl.when(k == (0 if prefetch_extras else nk - 1))
                def _():
                    for s in range(len(in_streams)):
                        stream_copies()[0](s, 0).start()

            @pl.when(k == 0)
            def _():
                for acc_ref, s in zip(acc_refs, partial_sums()):
                    acc_ref[...] = s

            @pl.when(k > 0)
            def _():
                for acc_ref, s in zip(acc_refs, partial_sums()):
                    acc_ref[...] += s

            @pl.when(k == nk - 1)
            def _():
                (streamed_finish if n_streams else finish)(acc_refs)

    scratch = [] if nk == 1 else [pltpu.VMEM(s, jnp.float32) for s in acc_shapes]
    if n_streams:
        assert nk > 1 and epi_rows and not any(q in sum_outs for q in out_streams)
        scratch = scratch + stream_bufs + [pltpu.SemaphoreType.DMA((n_streams, 2))]
    return pl.pallas_call(
        body, name=name, grid=grid,
        in_specs=operand_specs + [ANY if s is STREAM else s for s in extra_specs] + [ANY] * len(deps),
        out_specs=[ANY if s is STREAM else s for s in out_specs],
        out_shape=list(out_shapes), scratch_shapes=scratch,
        compiler_params=_params(("arbitrary", "arbitrary", "arbitrary")),
    )(*operands, *extras, *deps)


def _identity_epilogue(accs, extra_refs):
    return tuple(accs)


def _row_spec(tr, n):
    return pl.BlockSpec((tr, n), lambda i: (i, 0))


def _const_spec(shape):
    return pl.BlockSpec(shape, lambda i: tuple(0 for _ in shape))


def _accumulate(ref, val, i):
    @pl.when(i == 0)
    def _():
        ref[...] = val

    @pl.when(i > 0)
    def _():
        ref[...] += val


def _pre_norm(x, gain, deps=()):
    t, d = x.shape
    tr = _tile(t, 512)

    def body(x_ref, g_ref, *rest):
        h_ref = rest[-1]
        xv = x_ref[...]
        h_ref[...] = (xv * _rstd(xv) * g_ref[...]).astype(h_ref.dtype)

    return pl.pallas_call(
        body, name="pre_norm", grid=(t // tr,),
        in_specs=[_row_spec(tr, d), _const_spec((1, d))] + [ANY] * len(deps), out_specs=_row_spec(tr, d),
        out_shape=jax.ShapeDtypeStruct((t, d), MM_DTYPE), compiler_params=_params(("arbitrary",)),
    )(x, gain, *deps)


def _pool_matrix(pool_ref, g):
    return jnp.concatenate([pool_ref[c, g] for c in range(N_CHIPS)], axis=0)


def _inv_count(row0, n, w):
    pos = (row0 + lax.broadcasted_iota(jnp.int32, (n, 1), 0) + 1).astype(jnp.float32)
    return 1.0 / jnp.minimum(pos, float(w))


def _conv_piece(cu_buf, start, n, b_piece, convw_ref):
    conv = None
    for k in range(CONV_K):
        term = convw_ref[k:k + 1, :] * cu_buf[pl.ds(HALO + start + k - (CONV_K - 1), n), :]
        conv = term if conv is None else conv + term
    return conv, b_piece * conv


def _head_stats(a, width):
    return [_rstd(a[:, h * width:(h + 1) * width]) for h in range(a.shape[1] // width)]


def _pooled_piece(v_buf, start, n, v_piece, row0, dg):
    outs = []
    for gi, w in enumerate(POOL_WINDOWS):
        cols = slice(gi * dg, (gi + 1) * dg)
        win = None
        for k in range(w):
            term = v_buf[pl.ds(HALO + start - k, n), cols]
            win = term if win is None else win + term
        outs.append(win * _inv_count(row0 + start, n, w) - v_piece[:, cols])
    return outs


def _halo_specs(t, tr, width, col):
    per = tr // HALO
    last = t // HALO - 1
    prev = pl.BlockSpec((HALO, width), lambda i: (jnp.maximum(i * per - 1, 0), col))
    nxt = pl.BlockSpec((HALO, width), lambda i: (jnp.minimum((i + 1) * per, last), col))
    return prev, nxt


def _mixers_fwd(proj, conv_w, pool_g, pool_scale):
    t, e = proj.shape
    cw = e // 4
    dg = pool_g.shape[-1]
    tr = _tile(t, 512)

    def main(col):
        return pl.BlockSpec((tr, cw), lambda i: (i, col))

    def body(b_ref, c_ref, u_ref, v_ref, cp_ref, up_ref, vp_ref, convw_ref, pool_ref, scale_ref,
             out_ref, cu_buf, v_buf):
        i = pl.program_id(0)
        keep = (i > 0).astype(jnp.float32)
        cu_buf[pl.ds(0, HALO), :] = cp_ref[...] * up_ref[...] * keep
        cu_buf[pl.ds(HALO, tr), :] = c_ref[...] * u_ref[...]
        v_buf[pl.ds(0, HALO), :] = vp_ref[...] * keep
        v_buf[pl.ds(HALO, tr), :] = v_ref[...]
        _, a = _conv_piece(cu_buf, 0, tr, b_ref[...], convw_ref)
        for h, rstd in enumerate(_head_stats(a, CONV_HEAD_DIM)):
            cols = slice(h * CONV_HEAD_DIM, (h + 1) * CONV_HEAD_DIM)
            out_ref[:, cols] = (a[:, cols] * rstd).astype(out_ref.dtype)
        pooled = _pooled_piece(v_buf, 0, tr, v_ref[...], i * tr, dg)
        for gi, p in enumerate(pooled):
            z = _dot(p, _pool_matrix(pool_ref, gi), "nn")
            cols = slice(gi * dg, (gi + 1) * dg)
            out_ref[:, cw + gi * dg:cw + (gi + 1) * dg] = (z * _rstd(z) * scale_ref[:, cols]).astype(out_ref.dtype)

    prev_c, _ = _halo_specs(t, tr, cw, 1)
    prev_u, _ = _halo_specs(t, tr, cw, 2)
    prev_v, _ = _halo_specs(t, tr, cw, 3)
    return pl.pallas_call(
        body, name="mixers_fwd", grid=(t // tr,),
        in_specs=[main(0), main(1), main(2), main(3), prev_c, prev_u, prev_v,
                  _const_spec(conv_w.shape), _const_spec(pool_g.shape), _const_spec(pool_scale.shape)],
        out_specs=_row_spec(tr, 2 * cw),
        out_shape=jax.ShapeDtypeStruct((t, 2 * cw), MM_DTYPE),
        scratch_shapes=[pltpu.VMEM((tr + HALO, cw), jnp.float32), pltpu.VMEM((tr + HALO, cw), jnp.float32)],
        compiler_params=_params(("arbitrary",)),
    )(proj, proj, proj, proj, proj, proj, proj, conv_w, pool_g, pool_scale)


def _mixers_bwd(proj, dmixed, conv_w, pool_g, pool_scale, deps=()):
    t, e = proj.shape
    cw = e // 4
    dg = pool_g.shape[-1]
    n_groups = len(POOL_WINDOWS)
    tr = _tile(t, 512)
    n_tiles = t // tr
    ext = tr + 2 * HALO

    def main(col):
        return pl.BlockSpec((tr, cw), lambda i: (i, col))

    def body(b_ref, c_ref, u_ref, v_ref, dyc_ref, dyp_ref,
             cp_ref, up_ref, vp_ref,
             bn_ref, cn_ref, un_ref, vn_ref, dycn_ref, dypn_ref,
             convw_ref, pool_ref, scale_ref, *rest):
        (dproj_ref, dconvw_ref, dpool_ref, dscale_ref,
         cu_buf, v_buf, dconv_buf, dpn_buf, dpooled_buf) = rest[len(deps):]
        i = pl.program_id(0)
        keep_prev = (i > 0).astype(jnp.float32)
        keep_next = (i < n_tiles - 1).astype(jnp.float32)
        cu_buf[pl.ds(0, HALO), :] = cp_ref[...] * up_ref[...] * keep_prev
        cu_buf[pl.ds(HALO, tr), :] = c_ref[...] * u_ref[...]
        cu_buf[pl.ds(HALO + tr, HALO), :] = cn_ref[...] * un_ref[...]
        v_buf[pl.ds(0, HALO), :] = vp_ref[...] * keep_prev
        v_buf[pl.ds(HALO, tr), :] = v_ref[...]
        v_buf[pl.ds(HALO + tr, HALO), :] = vn_ref[...]

        def conv_piece(start, n, b_piece, dyc_piece, keep, is_main):
            conv, a = _conv_piece(cu_buf, start, n, b_piece, convw_ref)
            for h, rstd in enumerate(_head_stats(a, CONV_HEAD_DIM)):
                cols = slice(h * CONV_HEAD_DIM, (h + 1) * CONV_HEAD_DIM)
                da = _norm_bwd(dyc_piece[:, cols], a[:, cols] * rstd, rstd)
                dconv_buf[pl.ds(start, n), cols] = da * b_piece[:, cols] * keep
                if is_main:
                    dproj_ref[:, cols] = (da * conv[:, cols]).astype(dproj_ref.dtype)

        conv_piece(0, tr, b_ref[...], dyc_ref[...], 1.0, True)
        conv_piece(tr, HALO, bn_ref[...], dycn_ref[...], keep_next, False)

        dconv_main = dconv_buf[pl.ds(0, tr), :]
        dcu = None
        dw_rows = []
        for k in range(CONV_K):
            shift = CONV_K - 1 - k
            term = convw_ref[k:k + 1, :] * dconv_buf[pl.ds(shift, tr), :]
            dcu = term if dcu is None else dcu + term
            dw_rows.append(jnp.sum(dconv_main * cu_buf[pl.ds(HALO - shift, tr), :], axis=0, keepdims=True))
        dproj_ref[:, cw:2 * cw] = (dcu * u_ref[...]).astype(dproj_ref.dtype)
        dproj_ref[:, 2 * cw:3 * cw] = (dcu * c_ref[...]).astype(dproj_ref.dtype)
        _accumulate(dconvw_ref, jnp.concatenate(dw_rows, axis=0), i)

        def pool_piece(start, n, v_piece, dyp_piece, keep, is_main):
            pooled = _pooled_piece(v_buf, start, n, v_piece, i * tr, dg)
            dscale, dmats = [], []
            for gi, w in enumerate(POOL_WINDOWS):
                cols = slice(gi * dg, (gi + 1) * dg)
                mat = _pool_matrix(pool_ref, gi)
                z = _dot(pooled[gi], mat, "nn")
                rstd = _rstd(z)
                nz = z * rstd
                dyp_g = dyp_piece[:, cols]
                dz = _norm_bwd(dyp_g * scale_ref[:, cols], nz, rstd)
                dpooled = _dot(dz, mat, "nt") * keep
                dpn_buf[pl.ds(start, n), cols] = dpooled * _inv_count(i * tr + start, n, w)
                if is_main:
                    dpooled_buf[:, cols] = dpooled
                    dscale.append(jnp.sum(dyp_g * nz, axis=0, keepdims=True))
                    dmats.append(_dot(pooled[gi], dz, "tn"))
            return dscale, dmats

        dscale, dmats = pool_piece(0, tr, v_ref[...], dyp_ref[...], 1.0, True)
        pool_piece(tr, HALO, vn_ref[...], dypn_ref[...], keep_next, False)
        for gi, w in enumerate(POOL_WINDOWS):
            cols = slice(gi * dg, (gi + 1) * dg)
            back = None
            for k in range(w):
                term = dpn_buf[pl.ds(k, tr), cols]
                back = term if back is None else back + term
            dproj_ref[:, 3 * cw + gi * dg:3 * cw + (gi + 1) * dg] = (back - dpooled_buf[:, cols]).astype(dproj_ref.dtype)
        _accumulate(dscale_ref, jnp.concatenate(dscale, axis=1), i)
        rows = dg // N_CHIPS
        for gi in range(n_groups):
            for c in range(N_CHIPS):
                _accumulate(dpool_ref.at[c, gi], dmats[gi][c * rows:(c + 1) * rows, :], i)

    prev_c, next_c = _halo_specs(t, tr, cw, 1)
    prev_u, next_u = _halo_specs(t, tr, cw, 2)
    prev_v, next_v = _halo_specs(t, tr, cw, 3)
    _, next_b = _halo_specs(t, tr, cw, 0)
    _, next_dyc = _halo_specs(t, tr, cw, 0)
    _, next_dyp = _halo_specs(t, tr, cw, 1)
    return pl.pallas_call(
        body, name="mixers_bwd", grid=(n_tiles,),
        in_specs=[main(0), main(1), main(2), main(3), main(0), main(1),
                  prev_c, prev_u, prev_v,
                  next_b, next_c, next_u, next_v, next_dyc, next_dyp,
                  _const_spec(conv_w.shape), _const_spec(pool_g.shape), _const_spec(pool_scale.shape)]
        + [ANY] * len(deps),
        out_specs=[_row_spec(tr, e), _const_spec(conv_w.shape), _const_spec(pool_g.shape),
                   _const_spec(pool_scale.shape)],
        out_shape=[jax.ShapeDtypeStruct((t, e), MM_DTYPE), jax.ShapeDtypeStruct(conv_w.shape, jnp.float32),
                   jax.ShapeDtypeStruct(pool_g.shape, jnp.float32),
                   jax.ShapeDtypeStruct(pool_scale.shape, jnp.float32)],
        scratch_shapes=[pltpu.VMEM((ext, cw), jnp.float32), pltpu.VMEM((ext, cw), jnp.float32),
                        pltpu.VMEM((tr + HALO, cw), jnp.float32), pltpu.VMEM((tr + HALO, cw), jnp.float32),
                        pltpu.VMEM((tr, cw), jnp.float32)],
        compiler_params=_params(("arbitrary",)),
    )(proj, proj, proj, proj, dmixed, dmixed,
      proj, proj, proj,
      proj, proj, proj, proj, dmixed, dmixed,
      conv_w, pool_g, pool_scale, *deps)


def _cast_rows(name, w, dtype):
    r, c = w.shape
    tr = _tile(r, 512)

    def body(w_ref, o_ref):
        o_ref[...] = w_ref[...].astype(o_ref.dtype)

    return pl.pallas_call(
        body, name=name, grid=(r // tr,), in_specs=[_row_spec(tr, c)], out_specs=_row_spec(tr, c),
        out_shape=jax.ShapeDtypeStruct((r, c), dtype), compiler_params=_params(("arbitrary",)),
    )(w)


def _cast_to_slot(name, w, chip, by_columns=False):
    r, c = w.shape
    tr = _tile(r, 512)
    if by_columns:
        out_spec = pl.BlockSpec((tr, c), lambda i, chip_ref: (i, chip_ref[0]))
        out_shape = jax.ShapeDtypeStruct((r, N_CHIPS * c), MM_DTYPE)
    else:
        out_spec = pl.BlockSpec((None, tr, c), lambda i, chip_ref: (chip_ref[0], i, 0))
        out_shape = jax.ShapeDtypeStruct((N_CHIPS, r, c), MM_DTYPE)

    def body(chip_ref, w_ref, o_ref):
        o_ref[...] = w_ref[...].astype(o_ref.dtype)

    return pl.pallas_call(
        body, name=name,
        grid_spec=pltpu.PrefetchScalarGridSpec(
            num_scalar_prefetch=1, grid=(r // tr,),
            in_specs=[pl.BlockSpec((tr, c), lambda i, chip_ref: (i, 0))], out_specs=out_spec),
        out_shape=out_shape, compiler_params=_params(("arbitrary",)),
    )(chip, w)


def _sum_own_and_received(name, own, land, chip):
    _, r, c = land.shape
    tr = _tile(r, 256)

    def body(chip_ref, own_ref, a_ref, b_ref, c_ref, o_ref):
        tot = own_ref[...].astype(jnp.float32) + a_ref[...].astype(jnp.float32)
        tot = tot + b_ref[...].astype(jnp.float32)
        o_ref[...] = (tot + c_ref[...].astype(jnp.float32)).astype(o_ref.dtype)

    def slot(k):
        return pl.BlockSpec((None, tr, c), lambda i, chip_ref: ((chip_ref[0] + k) % N_CHIPS, i, 0))

    own_spec = slot(0) if len(own.shape) == 3 else pl.BlockSpec((tr, c), lambda i, chip_ref: (i, chip_ref[0]))
    return pl.pallas_call(
        body, name=name,
        grid_spec=pltpu.PrefetchScalarGridSpec(
            num_scalar_prefetch=1, grid=(r // tr,), in_specs=[own_spec, slot(1), slot(2), slot(3)],
            out_specs=pl.BlockSpec((tr, c), lambda i, chip_ref: (i, 0))),
        out_shape=jax.ShapeDtypeStruct((r, c), COMM_DTYPE), compiler_params=_params(("arbitrary",)),
    )(chip, own, land, land, land)


def _sum_slots(name, slots):
    n, r, c = slots.shape
    tr = _tile(r, 256)

    def body(s_ref, o_ref):
        tot = s_ref[0].astype(jnp.float32)
        for s in range(1, n):
            tot = tot + s_ref[s].astype(jnp.float32)
        o_ref[...] = tot

    return pl.pallas_call(
        body, name=name, grid=(r // tr,),
        in_specs=[pl.BlockSpec((n, tr, c), lambda i: (0, i, 0))], out_specs=_row_spec(tr, c),
        out_shape=jax.ShapeDtypeStruct((r, c), jnp.float32), compiler_params=_params(("arbitrary",)),
    )(slots)


def _adamw_math(w, g, m, v):
    m = ADAM_B1 * m + (1.0 - ADAM_B1) * g
    v = ADAM_B2 * v + (1.0 - ADAM_B2) * (g * g)
    m_hat = m / (1.0 - ADAM_B1 ** ADAM_STEP)
    v_hat = v / (1.0 - ADAM_B2 ** ADAM_STEP)
    delta = -ADAM_LR * (m_hat / (jnp.sqrt(v_hat) + ADAM_EPS) + ADAM_WD * w)
    return delta, m, v


def _adamw(name, w, m, v, grad_parts):
    r, c = w.shape
    tr = _tile(r, 256)
    n_parts = len(grad_parts)

    def body(*refs):
        w_ref, m_ref, v_ref = refs[:3]
        part_refs = refs[3:3 + n_parts]
        g_ref, d_ref, nm_ref, nv_ref = refs[3 + n_parts:]
        g = part_refs[0][...].astype(jnp.float32)
        for p in part_refs[1:]:
            g = g + p[...].astype(jnp.float32)
        delta, nm, nv = _adamw_math(w_ref[...], g, m_ref[...], v_ref[...])
        g_ref[...] = g
        d_ref[...] = delta
        nm_ref[...] = nm
        nv_ref[...] = nv

    spec = _row_spec(tr, c)
    out = jax.ShapeDtypeStruct((r, c), jnp.float32)
    return pl.pallas_call(
        body, name=name, grid=(r // tr,), in_specs=[spec] * (3 + n_parts), out_specs=[spec] * 4,
        out_shape=[out] * 4, compiler_params=_params(("arbitrary",)),
    )(w, m, v, *grad_parts)


def _chip_peers():
    x, y, c = lax.axis_index("x"), lax.axis_index("y"), lax.axis_index("c")
    return x, y, c, [(1 - x, y), (x, 1 - y), (1 - x, 1 - y)]


HBM = pl.BlockSpec(memory_space=pltpu.HBM)
SEM = pl.BlockSpec(memory_space=pltpu.SEMAPHORE)
TOKEN = jax.ShapeDtypeStruct((8, 128), jnp.float32)
N_PEER_CHIPS = N_CHIPS - 1


def _in_flight():
    return pltpu.CompilerParams(has_side_effects=pltpu.SideEffectType.DATAFLOW_SIDE_EFFECTING)


def _slot(ref, slot):
    if len(ref.shape) == 3:
        return ref.at[slot]
    width = ref.shape[1] // N_CHIPS
    return ref.at[:, pl.ds(pl.multiple_of(slot * width, 128), width)]


def _half_slot(ref, slot, half):
    rows = ref.shape[-2] // 2
    if len(ref.shape) == 3:
        return ref.at[slot, pl.ds(half * rows, rows)]
    width = ref.shape[1] // N_CHIPS
    return ref.at[pl.ds(half * rows, rows), pl.ds(pl.multiple_of(slot * width, 128), width)]


def _slot_shape(shape):
    return shape[1:] if len(shape) == 3 else (shape[0], shape[1] // N_CHIPS)


def _gather_start(name, full, deps=()):
    def body(full_ref, *rest):
        send_sems, recv_sems, _, token_ref = rest[len(deps):]
        x, y, c, peers = _chip_peers()
        mine = _half_slot(full_ref, 2 * x + y, c)
        for p, (px, py) in enumerate(peers):
            pltpu.make_async_remote_copy(
                src_ref=mine, dst_ref=mine, send_sem=send_sems.at[p], recv_sem=recv_sems.at[p],
                device_id=(px, py, c), device_id_type=MESH).start()
        token_ref[...] = jnp.zeros_like(token_ref)

    return pl.pallas_call(
        body, name=name,
        out_shape=(pltpu.SemaphoreType.DMA((N_PEER_CHIPS,)), pltpu.SemaphoreType.DMA((N_PEER_CHIPS,)),
                   pltpu.HBM(full.shape, full.dtype), TOKEN),
        in_specs=[HBM] + [ANY] * len(deps), out_specs=(SEM, SEM, HBM, pl.BlockSpec(memory_space=pltpu.VMEM)),
        input_output_aliases={0: 2}, compiler_params=_in_flight(),
    )(pltpu.with_memory_space_constraint(full, pltpu.HBM), *deps)


def _gather_wait(name, send_sems, recv_sems, full, after):
    def body(full_ref, send_sems, recv_sems, after_ref, out_ref):
        x, y, c, peers = _chip_peers()
        for p, (px, py) in enumerate(peers):
            cp = pltpu.make_async_remote_copy(
                src_ref=_half_slot(full_ref, 2 * x + y, c), dst_ref=_half_slot(full_ref, 2 * px + py, c),
                send_sem=send_sems.at[p], recv_sem=recv_sems.at[p], device_id=(px, py, c), device_id_type=MESH)
            cp.wait_send()
            cp.wait_recv()

    return pl.pallas_call(
        body, name=name, out_shape=pltpu.HBM(full.shape, full.dtype),
        in_specs=(HBM, SEM, SEM, ANY), out_specs=HBM, input_output_aliases={0: 0}, compiler_params=_in_flight(),
    )(full, send_sems, recv_sems, after)


def _forward_sibling(name, fulls):
    n = len(fulls)

    def body(*refs):
        in_refs, out_refs = refs[:n], refs[n:2 * n]
        send_sems, recv_sems = refs[2 * n:]
        x, y, c, peers = _chip_peers()
        sends, recvs = [], []
        for a in range(n):
            for p, (px, py) in enumerate(peers):
                k = N_PEER_CHIPS * a + p
                slot = 2 * px + py
                cp = pltpu.make_async_remote_copy(
                    src_ref=_half_slot(in_refs[a], slot, c), dst_ref=_half_slot(out_refs[a], slot, c),
                    send_sem=send_sems.at[k], recv_sem=recv_sems.at[k], device_id=(x, y, 1 - c), device_id_type=MESH)
                cp.start()
                sends.append(cp)
                recvs.append(pltpu.make_async_remote_copy(
                    src_ref=_half_slot(in_refs[a], slot, c), dst_ref=_half_slot(out_refs[a], slot, 1 - c),
                    send_sem=send_sems.at[k], recv_sem=recv_sems.at[k], device_id=(x, y, 1 - c), device_id_type=MESH))
        for cp in recvs:
            cp.wait_recv()
        for cp in sends:
            cp.wait_send()

    return pl.pallas_call(
        body, name=name, in_specs=[ANY] * n, out_specs=[ANY] * n,
        out_shape=[jax.ShapeDtypeStruct(f.shape, f.dtype) for f in fulls],
        input_output_aliases={a: a for a in range(n)},
        scratch_shapes=[pltpu.SemaphoreType.DMA((N_PEER_CHIPS * n,)), pltpu.SemaphoreType.DMA((N_PEER_CHIPS * n,))],
    )(*fulls)


def _scatter_start(name, grads, deps=()):
    def body(g_ref, land_ref, *rest):
        send_sems, recv_sems, _, _, token_ref = rest[len(deps):]
        x, y, c, peers = _chip_peers()
        me = 2 * x + y
        for p, (px, py) in enumerate(peers):
            pltpu.make_async_remote_copy(
                src_ref=_slot(g_ref, 2 * px + py), dst_ref=land_ref.at[me], send_sem=send_sems.at[p],
                recv_sem=recv_sems.at[p], device_id=(px, py, c), device_id_type=MESH).start()
        token_ref[...] = jnp.zeros_like(token_ref)

    land = lax.empty((N_CHIPS,) + _slot_shape(grads.shape), grads.dtype)
    return pl.pallas_call(
        body, name=name,
        out_shape=(pltpu.SemaphoreType.DMA((N_PEER_CHIPS,)), pltpu.SemaphoreType.DMA((N_PEER_CHIPS,)),
                   pltpu.HBM(grads.shape, grads.dtype), pltpu.HBM(land.shape, land.dtype), TOKEN),
        in_specs=[HBM, HBM] + [ANY] * len(deps),
        out_specs=(SEM, SEM, HBM, HBM, pl.BlockSpec(memory_space=pltpu.VMEM)),
        input_output_aliases={0: 2, 1: 3}, compiler_params=_in_flight(),
    )(pltpu.with_memory_space_constraint(grads, pltpu.HBM), pltpu.with_memory_space_constraint(land, pltpu.HBM), *deps)


def _scatter_wait(name, send_sems, recv_sems, grads, land, after):
    def body(g_ref, land_ref, send_sems, recv_sems, after_ref, g_out, land_out):
        x, y, c, peers = _chip_peers()
        for p, (px, py) in enumerate(peers):
            cp = pltpu.make_async_remote_copy(
                src_ref=_slot(g_ref, 2 * px + py), dst_ref=land_ref.at[2 * px + py], send_sem=send_sems.at[p],
                recv_sem=recv_sems.at[p], device_id=(px, py, c), device_id_type=MESH)
            cp.wait_send()
            cp.wait_recv()

    return pl.pallas_call(
        body, name=name,
        out_shape=(pltpu.HBM(grads.shape, grads.dtype), pltpu.HBM(land.shape, land.dtype)),
        in_specs=(HBM, HBM, SEM, SEM, ANY), out_specs=(HBM, HBM), input_output_aliases={0: 0, 1: 1},
        compiler_params=_in_flight(),
    )(grads, land, send_sems, recv_sems, after)


def _swap_sibling(name, parts):
    n = len(parts)

    def body(*refs):
        in_refs, out_refs = refs[:n], refs[n:2 * n]
        send_sems, recv_sems = refs[2 * n:]
        x, y, c = lax.axis_index("x"), lax.axis_index("y"), lax.axis_index("c")
        copies = []
        for a in range(n):
            cp = pltpu.make_async_remote_copy(
                src_ref=in_refs[a], dst_ref=out_refs[a], send_sem=send_sems.at[a], recv_sem=recv_sems.at[a],
                device_id=(x, y, 1 - c), device_id_type=MESH)
            cp.start()
            copies.append(cp)
        for cp in copies:
            cp.wait_recv()
        for cp in copies:
            cp.wait_send()

    return pl.pallas_call(
        body, name=name, in_specs=[ANY] * n, out_specs=[ANY] * n,
        out_shape=[jax.ShapeDtypeStruct(p.shape, p.dtype) for p in parts],
        scratch_shapes=[pltpu.SemaphoreType.DMA((n,)), pltpu.SemaphoreType.DMA((n,))],
    )(*parts)


def _gather_devices(name, block):
    def body(in_ref, out_ref, send_sems, recv_sems, local_sem):
        x, y, c = lax.axis_index("x"), lax.axis_index("y"), lax.axis_index("c")
        me = 4 * x + 2 * y + c
        local = pltpu.make_async_copy(in_ref, out_ref.at[me], local_sem)
        local.start()
        sends, recvs = [], []
        k = 0
        for fx in range(2):
            for fy in range(2):
                for fc in range(2):
                    if fx == fy == fc == 0:
                        continue
                    px = x if fx == 0 else 1 - x
                    py = y if fy == 0 else 1 - y
                    pc = c if fc == 0 else 1 - c
                    cp = pltpu.make_async_remote_copy(
                        src_ref=in_ref, dst_ref=out_ref.at[me], send_sem=send_sems.at[k], recv_sem=recv_sems.at[k],
                        device_id=(px, py, pc), device_id_type=MESH)
                    cp.start()
                    sends.append(cp)
                    recvs.append(pltpu.make_async_remote_copy(
                        src_ref=in_ref, dst_ref=out_ref.at[4 * px + 2 * py + pc], send_sem=send_sems.at[k],
                        recv_sem=recv_sems.at[k], device_id=(px, py, pc), device_id_type=MESH))
                    k += 1
        for cp in recvs:
            cp.wait_recv()
        for cp in sends:
            cp.wait_send()
        local.wait()

    return pl.pallas_call(
        body, name=name, in_specs=[ANY], out_specs=ANY,
        out_shape=jax.ShapeDtypeStruct((N_DEV,) + block.shape, block.dtype),
        scratch_shapes=[pltpu.SemaphoreType.DMA((N_DEV - 1,)), pltpu.SemaphoreType.DMA((N_DEV - 1,)),
                        pltpu.SemaphoreType.DMA],
    )(block)


def _pack_rows(pieces, width):
    flat = jnp.concatenate([p.reshape(-1) for p in pieces])
    rows = -(-flat.shape[0] // width)
    rows = -(-rows // 8) * 8
    flat = jnp.pad(flat, (0, rows * width - flat.shape[0]))
    return flat.reshape(rows, width)


def _unpack_rows(packed, shapes):
    flat = packed.reshape(-1)
    out, off = [], 0
    for s in shapes:
        size = 1
        for d in s:
            size *= d
        out.append(flat[off:off + size].reshape(s))
        off += size
    return out


def kernel(x, ln_mix_pre, w_in, conv_w, pool_w, pool_scale, w_out, ln_mix_post, ln_ffn_pre, w_gate, w_up, w_down, ln_ffn_post, loss_target, m_ln_mix_pre, m_w_in, m_conv_w, m_pool_w, m_pool_scale, m_w_out, m_ln_mix_post, m_ln_ffn_pre, m_w_gate, m_w_up, m_w_down, m_ln_ffn_post, v_ln_mix_pre, v_w_in, v_conv_w, v_pool_w, v_pool_scale, v_w_out, v_ln_mix_post, v_ln_ffn_pre, v_w_gate, v_w_up, v_w_down, v_ln_ffn_post):
    t, d = x.shape[1], x.shape[2]
    e4 = w_in.shape[2]
    e = N_CHIPS * e4
    f4 = w_gate.shape[2]
    f = N_CHIPS * f4
    n_groups, dg4, dg = pool_w.shape[1], pool_w.shape[2], pool_w.shape[3]
    cw4 = conv_w.shape[2]
    chip = 2 * lax.axis_index("x") + lax.axis_index("y")
    xs, tgt = x[0], loss_target[0]

    big = {"w_in": w_in[0], "w_out": w_out[0], "w_gate": w_gate[0], "w_up": w_up[0], "w_down": w_down[0],
           "pool_w": pool_w[0].reshape(n_groups * dg4, dg)}
    names = ["w_in", "pool_w", "w_out", "w_gate", "w_up", "w_down"]
    chip_arr = chip.astype(jnp.int32).reshape(1)
    conv_all = _gather_devices("gather_conv_w", _pack_rows([conv_w[0]], 128))
    conv_full = jnp.concatenate(
        [conv_all[2 * j].reshape(-1)[:CONV_K * cw4].reshape(CONV_K, cw4) for j in range(N_CHIPS)], axis=1)
    in_flight, deps = {}, (conv_all,)
    for k in names:
        by_columns = k in ("w_gate", "w_up")
        in_flight[k] = _gather_start(
            "gather_start_" + k, _cast_to_slot("cast_" + k, big[k], chip_arr, by_columns), deps)
        deps = (in_flight[k][3],)

    def landed(ks, after):
        fulls = []
        for k in ks:
            send_sems, recv_sems, full, _ = in_flight[k]
            fulls.append(_gather_wait("gather_wait_" + k, send_sems, recv_sems, full, after))
            after = fulls[-1]
        return _forward_sibling("forward_" + ks[0], fulls)

    def rows3(tile, width):
        return pl.BlockSpec((tile, width), lambda i, j, k: (i, 0))

    gain3 = pl.BlockSpec((1, d), lambda i, j, k: (0, 0))
    f32_td = jax.ShapeDtypeStruct((t, d), jnp.float32)
    mm_td = jax.ShapeDtypeStruct((t, d), MM_DTYPE)
    f32_gain = jax.ShapeDtypeStruct((1, d), jnp.float32)

    h = _pre_norm(xs, ln_mix_pre, deps)
    (win_g,) = landed(["w_in"], h)
    tm = _tile(t, 1024)
    tm2 = _tile(t, 2048)
    proj = _matmul(
        "in_proj", grid=(t // tm2, N_CHIPS, 1), mode="nn",
        pairs=[[(h, win_g)]],
        pair_specs=[[(rows3(tm2, d), pl.BlockSpec((None, d, e4), lambda i, j, k: (j, 0, 0)))]],
        acc_shapes=[(tm2, e4)], out_shapes=[jax.ShapeDtypeStruct((t, e), jnp.float32)],
        out_specs=[pl.BlockSpec((tm2, e4), lambda i, j, k: (i, j))], epilogue=_identity_epilogue)[0]
    pool_g, wout_full = landed(["pool_w", "w_out"], proj)
    pool_g, wout_full = pool_g.reshape(N_CHIPS, n_groups, dg4, dg), wout_full.reshape(d, d)
    mixed = _mixers_fwd(proj, conv_full, pool_g, pool_scale)

    def post_mix_epilogue(accs, extra_refs):
        x_ref, g2_ref, g3_ref = extra_refs
        mo = accs[0]
        x1 = x_ref[...] + mo * _rstd(mo) * g2_ref[...]
        return mo, x1, x1 * _rstd(x1) * g3_ref[...]

    tm_mix = _tile(t, 512)
    mix_out, x1, hf = _matmul(
        "out_proj", grid=(t // tm_mix, 1, 1), mode="nn",
        pairs=[[(mixed, wout_full)]],
        pair_specs=[[(rows3(tm_mix, d), pl.BlockSpec((d, d), lambda i, j, k: (0, 0)))]],
        acc_shapes=[(tm_mix, d)], extras=[xs, ln_mix_post, ln_ffn_pre],
        extra_specs=[rows3(tm_mix, d), gain3, gain3], out_shapes=[f32_td, f32_td, mm_td],
        out_specs=[rows3(tm_mix, d)] * 3, epilogue=post_mix_epilogue, epi_rows=EPILOGUE_ROWS)
    wg_full, wu_full = landed(["w_gate", "w_up"], hf)

    def gate_up_epilogue(accs, extra_refs):
        g, up = accs
        sig = jax.nn.sigmoid(g)
        silu = g * sig
        return up * (sig + silu * (1.0 - sig)), silu, silu * up

    tf = _tile(f, 512)
    ff_tile = jax.ShapeDtypeStruct((t, f), MM_DTYPE)
    act_by_g, act_by_up, act = _matmul(
        "gate_up", grid=(f // tf, t // tm, 1), mode="nn",
        pairs=[[(hf, wg_full)], [(hf, wu_full)]],
        pair_specs=[[(pl.BlockSpec((tm, d), lambda j, i, k: (i, 0)),
                      pl.BlockSpec((d, tf), lambda j, i, k: (0, j)))]] * 2,
        acc_shapes=[(tm, tf)] * 2, out_shapes=[ff_tile] * 3,
        out_specs=[pl.BlockSpec((tm, tf), lambda j, i, k: (i, j))] * 3, epilogue=gate_up_epilogue)
    wdown_full = landed(["w_down"], act)[0].reshape(f, d)

    def loss_epilogue(accs, extra_refs):
        x1_ref, tg_ref, g_ref = extra_refs
        ff_v = accs[0]
        rstd = _rstd(ff_v)
        n = ff_v * rstd
        g = g_ref[...]
        err = x1_ref[...] + n * g - tg_ref[...]
        rows_loss = 0.5 * jnp.sum(jnp.mean(err * err, axis=-1, keepdims=True), axis=0, keepdims=True)
        dout = err / d
        return (dout, _norm_bwd(dout * g, n, rstd), jnp.broadcast_to(rows_loss, (8, 128)),
                jnp.sum(dout * n, axis=0, keepdims=True))

    tm_row = _tile(t, 1024)
    dout, dff, loss_tile, dg_ffn_post = _matmul(
        "down_proj", grid=(t // tm_row, 1, f // tf), mode="nn",
        pairs=[[(act, wdown_full)]],
        pair_specs=[[(pl.BlockSpec((tm_row, tf), lambda i, j, k: (i, k)),
                      pl.BlockSpec((tf, d), lambda i, j, k: (k, 0)))]],
        acc_shapes=[(tm_row, d)], extras=[x1, tgt, ln_ffn_post], extra_specs=[STREAM] * 2 + [gain3],
        out_shapes=[f32_td, mm_td, jax.ShapeDtypeStruct((8, 128), jnp.float32), f32_gain],
        out_specs=[STREAM] * 2 + [pl.BlockSpec((8, 128), lambda i, j, k: (0, 0)), gain3],
        epilogue=loss_epilogue, epi_rows=EPILOGUE_ROWS, sum_outs=(2, 3))

    def dact_epilogue(accs, extra_refs):
        dact = accs[0]
        return dact * extra_refs[0][...].astype(jnp.float32), dact * extra_refs[1][...].astype(jnp.float32)

    ff_spec_ij = pl.BlockSpec((tm2, tf), lambda i, j, k: (i, j))
    dg_act, dup_act = _matmul(
        "dact", grid=(t // tm2, f // tf, 1), mode="nt",
        pairs=[[(dff, wdown_full)]],
        pair_specs=[[(rows3(tm2, d), pl.BlockSpec((tf, d), lambda i, j, k: (j, 0)))]],
        acc_shapes=[(tm2, tf)], extras=[act_by_g, act_by_up], extra_specs=[ff_spec_ij, ff_spec_ij],
        out_shapes=[ff_tile] * 2, out_specs=[ff_spec_ij] * 2, epilogue=dact_epilogue)
    tk = _tile(t, 2048)
    dw_down = _matmul(
        "dw_down", grid=(N_CHIPS, 1, t // tk), mode="tn",
        pairs=[[(act, dff)]],
        pair_specs=[[(pl.BlockSpec((tk, f4), lambda i, j, k: (k, i)),
                      pl.BlockSpec((tk, d), lambda i, j, k: (k, 0)))]],
        acc_shapes=[(f4, d)], out_shapes=[jax.ShapeDtypeStruct((N_CHIPS, f4, d), COMM_DTYPE)],
        out_specs=[pl.BlockSpec((None, f4, d), lambda i, j, k: (i, 0, 0))], epilogue=_identity_epilogue)[0]
    tn = _tile(d, 1024)
    leaving = {"w_down": _scatter_start("scatter_start_w_down", dw_down)}
    tmo = _tile(d, 2048)
    grad_ff = jax.ShapeDtypeStruct((d, f), COMM_DTYPE)
    dw_gate, dw_up = _matmul(
        "dw_gate_up", grid=(d // tmo, f // tf, t // tm2), mode="tn",
        pairs=[[(hf, dg_act)], [(hf, dup_act)]],
        pair_specs=[[(pl.BlockSpec((tm2, tmo), lambda i, j, k: (k, i)),
                      pl.BlockSpec((tm2, tf), lambda i, j, k: (k, j)))]] * 2,
        acc_shapes=[(tmo, tf)] * 2, out_shapes=[grad_ff] * 2,
        out_specs=[pl.BlockSpec((tmo, tf), lambda i, j, k: (i, j))] * 2, epilogue=_identity_epilogue,
        deps=leaving["w_down"][4:])
    leaving["w_gate"] = _scatter_start("scatter_start_w_gate", dw_gate)
    leaving["w_up"] = _scatter_start("scatter_start_w_up", dw_up, leaving["w_gate"][4:])

    def ffn_pre_epilogue(accs, extra_refs):
        x1_ref, dout_ref, mo_ref, g3_ref, g2_ref = extra_refs
        dhf_v = accs[0]
        x1_v = x1_ref[...]
        rstd3 = _rstd(x1_v)
        n3 = x1_v * rstd3
        dx1_v = dout_ref[...] + _norm_bwd(dhf_v * g3_ref[...], n3, rstd3)
        mo = mo_ref[...]
        rstd2 = _rstd(mo)
        n2 = mo * rstd2
        return (dx1_v, _norm_bwd(dx1_v * g2_ref[...], n2, rstd2), jnp.sum(dhf_v * n3, axis=0, keepdims=True),
                jnp.sum(dx1_v * n2, axis=0, keepdims=True))

    dx1, dmo, dg_ffn_pre, dg_mix_post = _matmul(
        "dhf", grid=(t // tm_row, 1, f // tf), mode="nt",
        pairs=[[(dg_act, wg_full), (dup_act, wu_full)]],
        pair_specs=[[(pl.BlockSpec((tm_row, tf), lambda i, j, k: (i, k)),
                      pl.BlockSpec((d, tf), lambda i, j, k: (0, k)))] * 2],
        acc_shapes=[(tm_row, d)], extras=[x1, dout, mix_out, ln_ffn_pre, ln_mix_post],
        extra_specs=[STREAM] * 3 + [gain3] * 2, out_shapes=[f32_td, mm_td, f32_gain, f32_gain],
        out_specs=[STREAM] * 2 + [gain3] * 2, epilogue=ffn_pre_epilogue, epi_rows=EPILOGUE_ROWS,
        sum_outs=(2, 3), deps=leaving["w_up"][4:])

    dmixed = _matmul(
        "dmixed", grid=(t // tm, 1, 1), mode="nt",
        pairs=[[(dmo, wout_full)]],
        pair_specs=[[(rows3(tm, d), pl.BlockSpec((d, d), lambda i, j, k: (0, 0)))]],
        acc_shapes=[(tm, d)], out_shapes=[f32_td],
        out_specs=[rows3(tm, d)], epilogue=_identity_epilogue,
        deps=leaving["w_up"][4:])[0]
    tmo = _tile(d, 2048)
    dw_out = _matmul(
        "dw_out", grid=(d // tmo, d // tn, t // tk), mode="tn",
        pairs=[[(mixed, dmo)]],
        pair_specs=[[(pl.BlockSpec((tk, tmo), lambda i, j, k: (k, i)),
                      pl.BlockSpec((tk, tn), lambda i, j, k: (k, j)))]],
        acc_shapes=[(tmo, tn)], out_shapes=[jax.ShapeDtypeStruct((d, d), COMM_DTYPE)],
        out_specs=[pl.BlockSpec((tmo, tn), lambda i, j, k: (i, j))], epilogue=_identity_epilogue)[0]
    leaving["w_out"] = _scatter_start("scatter_start_w_out", dw_out.reshape(N_CHIPS, d // N_CHIPS, d))
    dproj, dconv_full, dpool_g, dpool_scale = _mixers_bwd(proj, dmixed, conv_full, pool_g, pool_scale,
                                                          leaving["w_out"][4:])
    dpool_slots = _cast_rows("cast_dpool", dpool_g.reshape(N_CHIPS * n_groups * dg4, dg), COMM_DTYPE)
    leaving["pool_w"] = _scatter_start("scatter_start_pool_w", dpool_slots.reshape(N_CHIPS, n_groups * dg4, dg))
    dw_in = _matmul(
        "dw_in", grid=(d // tmo, N_CHIPS, t // tk), mode="tn",
        pairs=[[(h, dproj)]],
        pair_specs=[[(pl.BlockSpec((tk, tmo), lambda i, j, k: (k, i)),
                      pl.BlockSpec((tk, e4), lambda i, j, k: (k, j)))]],
        acc_shapes=[(tmo, e4)], out_shapes=[jax.ShapeDtypeStruct((N_CHIPS, d, e4), COMM_DTYPE)],
        out_specs=[pl.BlockSpec((None, tmo, e4), lambda i, j, k: (j, i, 0))], epilogue=_identity_epilogue,
        deps=leaving["pool_w"][4:])[0]
    leaving["w_in"] = _scatter_start("scatter_start_w_in", dw_in)

    def mix_pre_epilogue(accs, extra_refs):
        x_ref, dx1_ref, g_ref = extra_refs
        dh_v = accs[0]
        xv = x_ref[...]
        rstd = _rstd(xv)
        n = xv * rstd
        return dx1_ref[...] + _norm_bwd(dh_v * g_ref[...], n, rstd), jnp.sum(dh_v * n, axis=0, keepdims=True)

    grad_x, dg_mix_pre = _matmul(
        "dh", grid=(t // tm_row, 1, N_CHIPS), mode="nt",
        pairs=[[(dproj, win_g)]],
        pair_specs=[[(pl.BlockSpec((tm_row, e4), lambda i, j, k: (i, k)),
                      pl.BlockSpec((None, d, e4), lambda i, j, k: (k, 0, 0)))]],
        acc_shapes=[(tm_row, d)], extras=[xs, dx1, ln_mix_pre], extra_specs=[STREAM] * 2 + [gain3],
        out_shapes=[f32_td, f32_gain], out_specs=[STREAM, gain3], epilogue=mix_pre_epilogue,
        epi_rows=EPILOGUE_ROWS, sum_outs=(1,), deps=leaving["w_in"][4:], prefetch_extras=True)

    names = ["w_down", "w_gate", "w_up", "w_out", "pool_w", "w_in"]
    partial, after = [], grad_x
    for k in names:
        send_sems, recv_sems, own, land, _ = leaving[k]
        own, land = _scatter_wait("scatter_wait_" + k, send_sems, recv_sems, own, land, after)
        partial.append(_sum_own_and_received("sum_" + k, own, land, chip_arr))
        after = land
    other = _swap_sibling("swap_grads", partial)
    moments = {"w_in": (m_w_in, v_w_in), "w_out": (m_w_out, v_w_out), "w_gate": (m_w_gate, v_w_gate),
               "w_up": (m_w_up, v_w_up), "w_down": (m_w_down, v_w_down), "pool_w": (m_pool_w, v_pool_w)}
    result = {}
    for k, mine, theirs in zip(names, partial, other):
        shape = moments[k][0].shape
        two_d = big[k].shape
        outs = _adamw("adamw_" + k, big[k], moments[k][0].reshape(two_d), moments[k][1].reshape(two_d),
                      [mine, theirs])
        result[k] = [o.reshape(shape) for o in outs]

    small_shapes = [(1, d)] * 4 + [pool_scale.shape, (CONV_K, N_CHIPS * cw4)]
    packed = _pack_rows([dg_mix_pre, dg_mix_post, dg_ffn_pre, dg_ffn_post, dpool_scale, dconv_full], 1024)
    summed = _sum_slots("sum_small", _gather_devices("gather_small", packed))
    g_mix_pre, g_mix_post, g_ffn_pre, g_ffn_post, g_pool_scale, g_conv_full = _unpack_rows(summed, small_shapes)
    g_conv = lax.dynamic_slice(g_conv_full, (0, chip * cw4), (CONV_K, cw4))[None]
    small = [("ln_mix_pre", ln_mix_pre, m_ln_mix_pre, v_ln_mix_pre, g_mix_pre),
             ("conv_w", conv_w, m_conv_w, v_conv_w, g_conv),
             ("pool_scale", pool_scale, m_pool_scale, v_pool_scale, g_pool_scale),
             ("ln_mix_post", ln_mix_post, m_ln_mix_post, v_ln_mix_post, g_mix_post),
             ("ln_ffn_pre", ln_ffn_pre, m_ln_ffn_pre, v_ln_ffn_pre, g_ffn_pre),
             ("ln_ffn_post", ln_ffn_post, m_ln_ffn_post, v_ln_ffn_post, g_ffn_post)]
    shapes_small = [s[1].shape for s in small]
    packs = [_pack_rows([s[q] for s in small], 128) for q in (1, 2, 3, 4)]
    outs = _adamw("adamw_small", packs[0], packs[1], packs[2], [packs[3]])
    unpacked = [_unpack_rows(o, shapes_small) for o in outs]
    for idx, s in enumerate(small):
        result[s[0]] = [u[idx] for u in unpacked]

    loss = lax.psum(loss_tile[0, 0], ("x", "y", "c"))
    order = ["ln_mix_pre", "w_in", "conv_w", "pool_w", "pool_scale", "w_out", "ln_mix_post", "ln_ffn_pre",
             "w_gate", "w_up", "w_down", "ln_ffn_post"]
    return (loss, grad_x[None], *[result[k][0] for k in order], *[result[k][1] for k in order],
            *[result[k][2] for k in order], *[result[k][3] for k in order])
```

```python
import functools

import jax
import jax.numpy as jnp
from jax import lax
from jax.experimental import pallas as pl
from jax.experimental.pallas import tpu as pltpu

EPS = 1e-6
CONV_HEAD_DIM = 128
CONV_K = 3
POOL_WINDOWS = (2, 4, 8, 16)
HALO = 16
EPILOGUE_ROWS = 64
STREAM_ROWS = 256
N_CHIPS = 4
N_DEV = 8

ADAM_LR = 0.001
ADAM_B1 = 0.9
ADAM_B2 = 0.999
ADAM_EPS = 1e-08
ADAM_WD = 0.01
ADAM_STEP = 10

MM_DTYPE = jnp.bfloat16
COMM_DTYPE = jnp.bfloat16
VMEM_LIMIT = 62 * 1024 * 1024
MESH = pl.DeviceIdType.MESH
ANY = pl.BlockSpec(memory_space=pl.ANY)
STREAM = "stream"


def _tile(n, pref):
    t = min(pref, n)
    while n % t:
        t //= 2
    return t


def _params(sem):
    return pltpu.CompilerParams(dimension_semantics=sem, vmem_limit_bytes=VMEM_LIMIT)


def _rstd(x):
    return lax.rsqrt(jnp.mean(x * x, axis=-1, keepdims=True) + EPS)


def _norm_bwd(dn, n, rstd):
    return rstd * (dn - n * jnp.mean(dn * n, axis=-1, keepdims=True))


_DOT_DIMS = {
    "nn": (((1,), (0,)), ((), ())),
    "nt": (((1,), (1,)), ((), ())),
    "tn": (((0,), (0,)), ((), ())),
}


def _dot(a, b, mode):
    return lax.dot_general(a.astype(MM_DTYPE), b.astype(MM_DTYPE), _DOT_DIMS[mode],
                           preferred_element_type=jnp.float32)


def _matmul(name, *, grid, mode, pairs, pair_specs, acc_shapes, extras=(), extra_specs=(),
            out_shapes, out_specs, epilogue, deps=(), epi_rows=0, sum_outs=(), prefetch_extras=False):
    nk = grid[2]
    operands, operand_specs, where, counts = [], [], {}, []
    pair_index = []
    for ps, ss in zip(pairs, pair_specs):
        counts.append(len(ps))
        for arrays, specs in zip(ps, ss):
            for arr, spec in zip(arrays, specs):
                key = (id(arr), id(spec))
                if key not in where:
                    where[key] = len(operands)
                    operands.append(arr)
                    operand_specs.append(spec)
                pair_index.append(where[key])
    n_operands = len(operands)
    n_extra = len(extras)
    n_out = len(out_shapes)
    n_in = n_operands + n_extra + len(deps)
    in_streams = [q for q, s in enumerate(extra_specs) if s is STREAM]
    out_streams = [q for q, s in enumerate(out_specs) if s is STREAM]
    in_buf_rows = (acc_shapes[0][0],) if prefetch_extras else (2, STREAM_ROWS)
    stream_bufs = ([pltpu.VMEM(in_buf_rows + (extras[q].shape[1],), extras[q].dtype) for q in in_streams]
                   + [pltpu.VMEM((2, STREAM_ROWS, out_shapes[q].shape[1]), out_shapes[q].dtype) for q in out_streams])
    n_streams = len(stream_bufs)
    n_acc = 0 if nk == 1 else len(acc_shapes)

    def body(*refs):
        pair_refs = [refs[q] for q in pair_index]
        extra_refs = refs[n_operands:n_operands + n_extra]
        out_refs = refs[n_in:n_in + n_out]
        acc_refs = refs[n_in + n_out:n_in + n_out + n_acc]
        bufs = refs[n_in + n_out + n_acc:n_in + n_out + n_acc + n_streams]
        i, k = pl.program_id(0), pl.program_id(2)

        def stream_copies():
            sems = refs[-1]
            n_rows = acc_refs[0].shape[0]

            def hbm_rows(ref, c):
                return ref.at[pl.ds(pl.multiple_of(i * n_rows + c * STREAM_ROWS, STREAM_ROWS), STREAM_ROWS)]

            def fetch(s, c):
                if prefetch_extras:
                    return pltpu.make_async_copy(
                        extra_refs[in_streams[s]].at[pl.ds(pl.multiple_of(i * n_rows, STREAM_ROWS), n_rows)],
                        bufs[s], sems.at[s, 0])
                return pltpu.make_async_copy(hbm_rows(extra_refs[in_streams[s]], c), bufs[s].at[c % 2],
                                             sems.at[s, c % 2])

            def drain(s, c):
                return pltpu.make_async_copy(bufs[len(in_streams) + s].at[c % 2], hbm_rows(out_refs[out_streams[s]], c),
                                             sems.at[len(in_streams) + s, c % 2])

            return fetch, drain

        def streamed_finish(accs):
            fetch, drain = stream_copies()
            n_chunks = accs[0].shape[0] // STREAM_ROWS
            for c in range(n_chunks):
                for s in range(len(in_streams)):
                    if c + 1 < n_chunks and not prefetch_extras:
                        fetch(s, c + 1).start()
                    if c == 0 or not prefetch_extras:
                        fetch(s, c).wait()
                for s in range(len(out_streams)):
                    if c >= 2:
                        drain(s, c - 2).wait()
                for r0 in range(0, STREAM_ROWS, epi_rows):
                    sub = slice(r0, r0 + epi_rows)
                    rows = slice(c * STREAM_ROWS + r0, c * STREAM_ROWS + r0 + epi_rows)
                    views = [(bufs[in_streams.index(q)].at[rows] if prefetch_extras
                              else bufs[in_streams.index(q)].at[c % 2, sub]) if q in in_streams else e
                             for q, e in enumerate(extra_refs)]
                    outs = epilogue([a[rows, :] for a in accs], views)
                    for q, (o_ref, o) in enumerate(zip(out_refs, outs)):
                        if q in out_streams:
                            bufs[len(in_streams) + out_streams.index(q)][c % 2, sub, :] = o.astype(o_ref.dtype)
                        elif c == 0 and r0 == 0:
                            _accumulate(o_ref, o, i)
                        else:
                            o_ref[...] += o
                for s in range(len(out_streams)):
                    drain(s, c).start()
            for s in range(len(out_streams)):
                for c in range(max(n_chunks - 2, 0), n_chunks):
                    drain(s, c).wait()

        def partial_sums():
            res, p = [], 0
            for cnt in counts:
                tot = None
                for _ in range(cnt):
                    d = _dot(pair_refs[p][...], pair_refs[p + 1][...], mode)
                    tot = d if tot is None else tot + d
                    p += 2
                res.append(tot)
            return res

        def finish(accs):
            n_rows = accs[0].shape[0]
            step = epi_rows or n_rows
            for r0 in range(0, n_rows, step):
                rows = slice(r0, r0 + step)
                outs = epilogue([a[rows, :] for a in accs], [e.at[rows] if e.shape[0] == n_rows else e
                                                             for e in extra_refs])
                for q, (o_ref, o) in enumerate(zip(out_refs, outs)):
                    if q not in sum_outs:
                        o_ref[rows, :] = o.astype(o_ref.dtype)
                    elif r0 == 0:
                        _accumulate(o_ref, o, i)
                    else:
                        o_ref[...] += o

        if nk == 1:
            finish(partial_sums())
        else:
            if n_streams:
                @pl.when(k == (0 if prefetch_extras else nk - 1))
                def _():
                    for s in range(len(in_streams)):
                        stream_copies()[0](s, 0).start()

            @pl.when(k == 0)
            def _():
                for acc_ref, s in zip(acc_refs, partial_sums()):
                    acc_ref[...] = s

            @pl.when(k > 0)
            def _():
                for acc_ref, s in zip(acc_refs, partial_sums()):
                    acc_ref[...] += s

            @pl.when(k == nk - 1)
            def _():
                (streamed_finish if n_streams else finish)(acc_refs)

    scratch = [] if nk == 1 else [pltpu.VMEM(s, jnp.float32) for s in acc_shapes]
    if n_streams:
        assert nk > 1 and epi_rows and not any(q in sum_outs for q in out_streams)
        scratch = scratch + stream_bufs + [pltpu.SemaphoreType.DMA((n_streams, 2))]
    return pl.pallas_call(
        body, name=name, grid=grid,
        in_specs=operand_specs + [ANY if s is STREAM else s for s in extra_specs] + [ANY] * len(deps),
        out_specs=[ANY if s is STREAM else s for s in out_specs],
        out_shape=list(out_shapes), scratch_shapes=scratch,
        compiler_params=_params(("arbitrary", "arbitrary", "arbitrary")),
    )(*operands, *extras, *deps)


def _identity_epilogue(accs, extra_refs):
    return tuple(accs)


def _row_spec(tr, n):
    return pl.BlockSpec((tr, n), lambda i: (i, 0))


def _const_spec(shape):
    return pl.BlockSpec(shape, lambda i: tuple(0 for _ in shape))


def _accumulate(ref, val, i):
    @pl.when(i == 0)
    def _():
        ref[...] = val

    @pl.when(i > 0)
    def _():
        ref[...] += val


def _pre_norm(x, gain, deps=()):
    t, d = x.shape
    tr = _tile(t, 512)

    def body(x_ref, g_ref, *rest):
        h_ref = rest[-1]
        xv = x_ref[...]
        h_ref[...] = (xv * _rstd(xv) * g_ref[...]).astype(h_ref.dtype)

    return pl.pallas_call(
        body, name="pre_norm", grid=(t // tr,),
        in_specs=[_row_spec(tr, d), _const_spec((1, d))] + [ANY] * len(deps), out_specs=_row_spec(tr, d),
        out_shape=jax.ShapeDtypeStruct((t, d), MM_DTYPE), compiler_params=_params(("arbitrary",)),
    )(x, gain, *deps)


def _pool_matrix(pool_ref, g):
    return jnp.concatenate([pool_ref[c, g] for c in range(N_CHIPS)], axis=0)


def _inv_count(row0, n, w):
    pos = (row0 + lax.broadcasted_iota(jnp.int32, (n, 1), 0) + 1).astype(jnp.float32)
    return 1.0 / jnp.minimum(pos, float(w))


def _conv_piece(cu_buf, start, n, b_piece, convw_ref):
    conv = None
    for k in range(CONV_K):
        term = convw_ref[k:k + 1, :] * cu_buf[pl.ds(HALO + start + k - (CONV_K - 1), n), :]
        conv = term if conv is None else conv + term
    return conv, b_piece * conv


def _head_stats(a, width):
    return [_rstd(a[:, h * width:(h + 1) * width]) for h in range(a.shape[1] // width)]


def _pooled_piece(v_buf, start, n, v_piece, row0, dg):
    outs = []
    for gi, w in enumerate(POOL_WINDOWS):
        cols = slice(gi * dg, (gi + 1) * dg)
        win = None
        for k in range(w):
            term = v_buf[pl.ds(HALO + start - k, n), cols]
            win = term if win is None else win + term
        outs.append(win * _inv_count(row0 + start, n, w) - v_piece[:, cols])
    return outs


def _halo_specs(t, tr, width, col):
    per = tr // HALO
    last = t // HALO - 1
    prev = pl.BlockSpec((HALO, width), lambda i: (jnp.maximum(i * per - 1, 0), col))
    nxt = pl.BlockSpec((HALO, width), lambda i: (jnp.minimum((i + 1) * per, last), col))
    return prev, nxt


def _mixers_fwd(proj, conv_w, pool_g, pool_scale):
    t, e = proj.shape
    cw = e // 4
    dg = pool_g.shape[-1]
    tr = _tile(t, 512)

    def main(col):
        return pl.BlockSpec((tr, cw), lambda i: (i, col))

    def body(b_ref, c_ref, u_ref, v_ref, cp_ref, up_ref, vp_ref, convw_ref, pool_ref, scale_ref,
             out_ref, cu_buf, v_buf):
        i = pl.program_id(0)
        keep = (i > 0).astype(jnp.float32)
        cu_buf[pl.ds(0, HALO), :] = cp_ref[...] * up_ref[...] * keep
        cu_buf[pl.ds(HALO, tr), :] = c_ref[...] * u_ref[...]
        v_buf[pl.ds(0, HALO), :] = vp_ref[...] * keep
        v_buf[pl.ds(HALO, tr), :] = v_ref[...]
        _, a = _conv_piece(cu_buf, 0, tr, b_ref[...], convw_ref)
        for h, rstd in enumerate(_head_stats(a, CONV_HEAD_DIM)):
            cols = slice(h * CONV_HEAD_DIM, (h + 1) * CONV_HEAD_DIM)
            out_ref[:, cols] = (a[:, cols] * rstd).astype(out_ref.dtype)
        pooled = _pooled_piece(v_buf, 0, tr, v_ref[...], i * tr, dg)
        for gi, p in enumerate(pooled):
            z = _dot(p, _pool_matrix(pool_ref, gi), "nn")
            cols = slice(gi * dg, (gi + 1) * dg)
            out_ref[:, cw + gi * dg:cw + (gi + 1) * dg] = (z * _rstd(z) * scale_ref[:, cols]).astype(out_ref.dtype)

    prev_c, _ = _halo_specs(t, tr, cw, 1)
    prev_u, _ = _halo_specs(t, tr, cw, 2)
    prev_v, _ = _halo_specs(t, tr, cw, 3)
    return pl.pallas_call(
        body, name="mixers_fwd", grid=(t // tr,),
        in_specs=[main(0), main(1), main(2), main(3), prev_c, prev_u, prev_v,
                  _const_spec(conv_w.shape), _const_spec(pool_g.shape), _const_spec(pool_scale.shape)],
        out_specs=_row_spec(tr, 2 * cw),
        out_shape=jax.ShapeDtypeStruct((t, 2 * cw), MM_DTYPE),
        scratch_shapes=[pltpu.VMEM((tr + HALO, cw), jnp.float32), pltpu.VMEM((tr + HALO, cw), jnp.float32)],
        compiler_params=_params(("arbitrary",)),
    )(proj, proj, proj, proj, proj, proj, proj, conv_w, pool_g, pool_scale)


def _mixers_bwd(proj, dmixed, conv_w, pool_g, pool_scale, deps=()):
    t, e = proj.shape
    cw = e // 4
    dg = pool_g.shape[-1]
    n_groups = len(POOL_WINDOWS)
    tr = _tile(t, 512)
    n_tiles = t // tr
    ext = tr + 2 * HALO

    def main(col):
        return pl.BlockSpec((tr, cw), lambda i: (i, col))

    def body(b_ref, c_ref, u_ref, v_ref, dyc_ref, dyp_ref,
             cp_ref, up_ref, vp_ref,
             bn_ref, cn_ref, un_ref, vn_ref, dycn_ref, dypn_ref,
             convw_ref, pool_ref, scale_ref, *rest):
        (dproj_ref, dconvw_ref, dpool_ref, dscale_ref,
         cu_buf, v_buf, dconv_buf, dpn_buf, dpooled_buf) = rest[len(deps):]
        i = pl.program_id(0)
        keep_prev = (i > 0).astype(jnp.float32)
        keep_next = (i < n_tiles - 1).astype(jnp.float32)
        cu_buf[pl.ds(0, HALO), :] = cp_ref[...] * up_ref[...] * keep_prev
        cu_buf[pl.ds(HALO, tr), :] = c_ref[...] * u_ref[...]
        cu_buf[pl.ds(HALO + tr, HALO), :] = cn_ref[...] * un_ref[...]
        v_buf[pl.ds(0, HALO), :] = vp_ref[...] * keep_prev
        v_buf[pl.ds(HALO, tr), :] = v_ref[...]
        v_buf[pl.ds(HALO + tr, HALO), :] = vn_ref[...]

        def conv_piece(start, n, b_piece, dyc_piece, keep, is_main):
            conv, a = _conv_piece(cu_buf, start, n, b_piece, convw_ref)
            for h, rstd in enumerate(_head_stats(a, CONV_HEAD_DIM)):
                cols = slice(h * CONV_HEAD_DIM, (h + 1) * CONV_HEAD_DIM)
                da = _norm_bwd(dyc_piece[:, cols], a[:, cols] * rstd, rstd)
                dconv_buf[pl.ds(start, n), cols] = da * b_piece[:, cols] * keep
                if is_main:
                    dproj_ref[:, cols] = (da * conv[:, cols]).astype(dproj_ref.dtype)

        conv_piece(0, tr, b_ref[...], dyc_ref[...], 1.0, True)
        conv_piece(tr, HALO, bn_ref[...], dycn_ref[...], keep_next, False)

        dconv_main = dconv_buf[pl.ds(0, tr), :]
        dcu = None
        dw_rows = []
        for k in range(CONV_K):
            shift = CONV_K - 1 - k
            term = convw_ref[k:k + 1, :] * dconv_buf[pl.ds(shift, tr), :]
            dcu = term if dcu is None else dcu + term
            dw_rows.append(jnp.sum(dconv_main * cu_buf[pl.ds(HALO - shift, tr), :], axis=0, keepdims=True))
        dproj_ref[:, cw:2 * cw] = (dcu * u_ref[...]).astype(dproj_ref.dtype)
        dproj_ref[:, 2 * cw:3 * cw] = (dcu * c_ref[...]).astype(dproj_ref.dtype)
        _accumulate(dconvw_ref, jnp.concatenate(dw_rows, axis=0), i)

        def pool_piece(start, n, v_piece, dyp_piece, keep, is_main):
            pooled = _pooled_piece(v_buf, start, n, v_piece, i * tr, dg)
            dscale, dmats = [], []
            for gi, w in enumerate(POOL_WINDOWS):
                cols = slice(gi * dg, (gi + 1) * dg)
                mat = _pool_matrix(pool_ref, gi)
                z = _dot(pooled[gi], mat, "nn")
                rstd = _rstd(z)
                nz = z * rstd
                dyp_g = dyp_piece[:, cols]
                dz = _norm_bwd(dyp_g * scale_ref[:, cols], nz, rstd)
                dpooled = _dot(dz, mat, "nt") * keep
                dpn_buf[pl.ds(start, n), cols] = dpooled * _inv_count(i * tr + start, n, w)
                if is_main:
                    dpooled_buf[:, cols] = dpooled
                    dscale.append(jnp.sum(dyp_g * nz, axis=0, keepdims=True))
                    dmats.append(_dot(pooled[gi], dz, "tn"))
            return dscale, dmats

        dscale, dmats = pool_piece(0, tr, v_ref[...], dyp_ref[...], 1.0, True)
        pool_piece(tr, HALO, vn_ref[...], dypn_ref[...], keep_next, False)
        for gi, w in enumerate(POOL_WINDOWS):
            cols = slice(gi * dg, (gi + 1) * dg)
            back = None
            for k in range(w):
                term = dpn_buf[pl.ds(k, tr), cols]
                back = term if back is None else back + term
            dproj_ref[:, 3 * cw + gi * dg:3 * cw + (gi + 1) * dg] = (back - dpooled_buf[:, cols]).astype(dproj_ref.dtype)
        _accumulate(dscale_ref, jnp.concatenate(dscale, axis=1), i)
        rows = dg // N_CHIPS
        for gi in range(n_groups):
            for c in range(N_CHIPS):
                _accumulate(dpool_ref.at[c, gi], dmats[gi][c * rows:(c + 1) * rows, :], i)

    prev_c, next_c = _halo_specs(t, tr, cw, 1)
    prev_u, next_u = _halo_specs(t, tr, cw, 2)
    prev_v, next_v = _halo_specs(t, tr, cw, 3)
    _, next_b = _halo_specs(t, tr, cw, 0)
    _, next_dyc = _halo_specs(t, tr, cw, 0)
    _, next_dyp = _halo_specs(t, tr, cw, 1)
    return pl.pallas_call(
        body, name="mixers_bwd", grid=(n_tiles,),
        in_specs=[main(0), main(1), main(2), main(3), main(0), main(1),
                  prev_c, prev_u, prev_v,
                  next_b, next_c, next_u, next_v, next_dyc, next_dyp,
                  _const_spec(conv_w.shape), _const_spec(pool_g.shape), _const_spec(pool_scale.shape)]
        + [ANY] * len(deps),
        out_specs=[_row_spec(tr, e), _const_spec(conv_w.shape), _const_spec(pool_g.shape),
                   _const_spec(pool_scale.shape)],
        out_shape=[jax.ShapeDtypeStruct((t, e), MM_DTYPE), jax.ShapeDtypeStruct(conv_w.shape, jnp.float32),
                   jax.ShapeDtypeStruct(pool_g.shape, jnp.float32),
                   jax.ShapeDtypeStruct(pool_scale.shape, jnp.float32)],
        scratch_shapes=[pltpu.VMEM((ext, cw), jnp.float32), pltpu.VMEM((ext, cw), jnp.float32),
                        pltpu.VMEM((tr + HALO, cw), jnp.float32), pltpu.VMEM((tr + HALO, cw), jnp.float32),
                        pltpu.VMEM((tr, cw), jnp.float32)],
        compiler_params=_params(("arbitrary",)),
    )(proj, proj, proj, proj, dmixed, dmixed,
      proj, proj, proj,
      proj, proj, proj, proj, dmixed, dmixed,
      conv_w, pool_g, pool_scale, *deps)


def _cast_rows(name, w, dtype):
    r, c = w.shape
    tr = _tile(r, 512)

    def body(w_ref, o_ref):
        o_ref[...] = w_ref[...].astype(o_ref.dtype)

    return pl.pallas_call(
        body, name=name, grid=(r // tr,), in_specs=[_row_spec(tr, c)], out_specs=_row_spec(tr, c),
        out_shape=jax.ShapeDtypeStruct((r, c), dtype), compiler_params=_params(("arbitrary",)),
    )(w)


def _cast_to_slot(name, w, chip, by_columns=False):
    r, c = w.shape
    tr = _tile(r, 512)
    if by_columns:
        out_spec = pl.BlockSpec((tr, c), lambda i, chip_ref: (i, chip_ref[0]))
        out_shape = jax.ShapeDtypeStruct((r, N_CHIPS * c), MM_DTYPE)
    else:
        out_spec = pl.BlockSpec((None, tr, c), lambda i, chip_ref: (chip_ref[0], i, 0))
        out_shape = jax.ShapeDtypeStruct((N_CHIPS, r, c), MM_DTYPE)

    def body(chip_ref, w_ref, o_ref):
        o_ref[...] = w_ref[...].astype(o_ref.dtype)

    return pl.pallas_call(
        body, name=name,
        grid_spec=pltpu.PrefetchScalarGridSpec(
            num_scalar_prefetch=1, grid=(r // tr,),
            in_specs=[pl.BlockSpec((tr, c), lambda i, chip_ref: (i, 0))], out_specs=out_spec),
        out_shape=out_shape, compiler_params=_params(("arbitrary",)),
    )(chip, w)


def _sum_own_and_received(name, own, land, chip):
    _, r, c = land.shape
    tr = _tile(r, 256)

    def body(chip_ref, own_ref, a_ref, b_ref, c_ref, o_ref):
        tot = own_ref[...].astype(jnp.float32) + a_ref[...].astype(jnp.float32)
        tot = tot + b_ref[...].astype(jnp.float32)
        o_ref[...] = (tot + c_ref[...].astype(jnp.float32)).astype(o_ref.dtype)

    def slot(k):
        return pl.BlockSpec((None, tr, c), lambda i, chip_ref: ((chip_ref[0] + k) % N_CHIPS, i, 0))

    own_spec = slot(0) if len(own.shape) == 3 else pl.BlockSpec((tr, c), lambda i, chip_ref: (i, chip_ref[0]))
    return pl.pallas_call(
        body, name=name,
        grid_spec=pltpu.PrefetchScalarGridSpec(
            num_scalar_prefetch=1, grid=(r // tr,), in_specs=[own_spec, slot(1), slot(2), slot(3)],
            out_specs=pl.BlockSpec((tr, c), lambda i, chip_ref: (i, 0))),
        out_shape=jax.ShapeDtypeStruct((r, c), COMM_DTYPE), compiler_params=_params(("arbitrary",)),
    )(chip, own, land, land, land)


def _sum_slots(name, slots):
    n, r, c = slots.shape
    tr = _tile(r, 256)

    def body(s_ref, o_ref):
        tot = s_ref[0].astype(jnp.float32)
        for s in range(1, n):
            tot = tot + s_ref[s].astype(jnp.float32)
        o_ref[...] = tot

    return pl.pallas_call(
        body, name=name, grid=(r // tr,),
        in_specs=[pl.BlockSpec((n, tr, c), lambda i: (0, i, 0))], out_specs=_row_spec(tr, c),
        out_shape=jax.ShapeDtypeStruct((r, c), jnp.float32), compiler_params=_params(("arbitrary",)),
    )(slots)


def _adamw_math(w, g, m, v):
    m = ADAM_B1 * m + (1.0 - ADAM_B1) * g
    v = ADAM_B2 * v + (1.0 - ADAM_B2) * (g * g)
    m_hat = m / (1.0 - ADAM_B1 ** ADAM_STEP)
    v_hat = v / (1.0 - ADAM_B2 ** ADAM_STEP)
    delta = -ADAM_LR * (m_hat / (jnp.sqrt(v_hat) + ADAM_EPS) + ADAM_WD * w)
    return delta, m, v


def _adamw(name, w, m, v, grad_parts):
    r, c = w.shape
    tr = _tile(r, 256)
    n_parts = len(grad_parts)

    def body(*refs):
        w_ref, m_ref, v_ref = refs[:3]
        part_refs = refs[3:3 + n_parts]
        g_ref, d_ref, nm_ref, nv_ref = refs[3 + n_parts:]
        g = part_refs[0][...].astype(jnp.float32)
        for p in part_refs[1:]:
            g = g + p[...].astype(jnp.float32)
        delta, nm, nv = _adamw_math(w_ref[...], g, m_ref[...], v_ref[...])
        g_ref[...] = g
        d_ref[...] = delta
        nm_ref[...] = nm
        nv_ref[...] = nv

    spec = _row_spec(tr, c)
    out = jax.ShapeDtypeStruct((r, c), jnp.float32)
    return pl.pallas_call(
        body, name=name, grid=(r // tr,), in_specs=[spec] * (3 + n_parts), out_specs=[spec] * 4,
        out_shape=[out] * 4, compiler_params=_params(("arbitrary",)),
    )(w, m, v, *grad_parts)


def _chip_peers():
    x, y, c = lax.axis_index("x"), lax.axis_index("y"), lax.axis_index("c")
    return x, y, c, [(1 - x, y), (x, 1 - y), (1 - x, 1 - y)]


HBM = pl.BlockSpec(memory_space=pltpu.HBM)
SEM = pl.BlockSpec(memory_space=pltpu.SEMAPHORE)
TOKEN = jax.ShapeDtypeStruct((8, 128), jnp.float32)
N_PEER_CHIPS = N_CHIPS - 1


def _in_flight():
    return pltpu.CompilerParams(has_side_effects=pltpu.SideEffectType.DATAFLOW_SIDE_EFFECTING)


def _slot(ref, slot):
    if len(ref.shape) == 3:
        return ref.at[slot]
    width = ref.shape[1] // N_CHIPS
    return ref.at[:, pl.ds(pl.multiple_of(slot * width, 128), width)]


def _half_slot(ref, slot, half):
    rows = ref.shape[-2] // 2
    if len(ref.shape) == 3:
        return ref.at[slot, pl.ds(half * rows, rows)]
    width = ref.shape[1] // N_CHIPS
    return ref.at[pl.ds(half * rows, rows), pl.ds(pl.multiple_of(slot * width, 128), width)]


def _slot_shape(shape):
    return shape[1:] if len(shape) == 3 else (shape[0], shape[1] // N_CHIPS)


def _gather_start(name, full, deps=()):
    def body(full_ref, *rest):
        send_sems, recv_sems, _, token_ref = rest[len(deps):]
        x, y, c, peers = _chip_peers()
        mine = _half_slot(full_ref, 2 * x + y, c)
        for p, (px, py) in enumerate(peers):
            pltpu.make_async_remote_copy(
                src_ref=mine, dst_ref=mine, send_sem=send_sems.at[p], recv_sem=recv_sems.at[p],
                device_id=(px, py, c), device_id_type=MESH).start()
        token_ref[...] = jnp.zeros_like(token_ref)

    return pl.pallas_call(
        body, name=name,
        out_shape=(pltpu.SemaphoreType.DMA((N_PEER_CHIPS,)), pltpu.SemaphoreType.DMA((N_PEER_CHIPS,)),
                   pltpu.HBM(full.shape, full.dtype), TOKEN),
        in_specs=[HBM] + [ANY] * len(deps), out_specs=(SEM, SEM, HBM, pl.BlockSpec(memory_space=pltpu.VMEM)),
        input_output_aliases={0: 2}, compiler_params=_in_flight(),
    )(pltpu.with_memory_space_constraint(full, pltpu.HBM), *deps)


def _gather_wait(name, send_sems, recv_sems, full, after):
    def body(full_ref, send_sems, recv_sems, after_ref, out_ref):
        x, y, c, peers = _chip_peers()
        for p, (px, py) in enumerate(peers):
            cp = pltpu.make_async_remote_copy(
                src_ref=_half_slot(full_ref, 2 * x + y, c), dst_ref=_half_slot(full_ref, 2 * px + py, c),
                send_sem=send_sems.at[p], recv_sem=recv_sems.at[p], device_id=(px, py, c), device_id_type=MESH)
            cp.wait_send()
            cp.wait_recv()

    return pl.pallas_call(
        body, name=name, out_shape=pltpu.HBM(full.shape, full.dtype),
        in_specs=(HBM, SEM, SEM, ANY), out_specs=HBM, input_output_aliases={0: 0}, compiler_params=_in_flight(),
    )(full, send_sems, recv_sems, after)


def _forward_sibling(name, fulls):
    n = len(fulls)

    def body(*refs):
        in_refs, out_refs = refs[:n], refs[n:2 * n]
        send_sems, recv_sems = refs[2 * n:]
        x, y, c, peers = _chip_peers()
        sends, recvs = [], []
        for a in range(n):
            for p, (px, py) in enumerate(peers):
                k = N_PEER_CHIPS * a + p
                slot = 2 * px + py
                cp = pltpu.make_async_remote_copy(
                    src_ref=_half_slot(in_refs[a], slot, c), dst_ref=_half_slot(out_refs[a], slot, c),
                    send_sem=send_sems.at[k], recv_sem=recv_sems.at[k], device_id=(x, y, 1 - c), device_id_type=MESH)
                cp.start()
                sends.append(cp)
                recvs.append(pltpu.make_async_remote_copy(
                    src_ref=_half_slot(in_refs[a], slot, c), dst_ref=_half_slot(out_refs[a], slot, 1 - c),
                    send_sem=send_sems.at[k], recv_sem=recv_sems.at[k], device_id=(x, y, 1 - c), device_id_type=MESH))
        for cp in recvs:
            cp.wait_recv()
        for cp in sends:
            cp.wait_send()

    return pl.pallas_call(
        body, name=name, in_specs=[ANY] * n, out_specs=[ANY] * n,
        out_shape=[jax.ShapeDtypeStruct(f.shape, f.dtype) for f in fulls],
        input_output_aliases={a: a for a in range(n)},
        scratch_shapes=[pltpu.SemaphoreType.DMA((N_PEER_CHIPS * n,)), pltpu.SemaphoreType.DMA((N_PEER_CHIPS * n,))],
    )(*fulls)


def _scatter_start(name, grads, deps=()):
    def body(g_ref, land_ref, *rest):
        send_sems, recv_sems, _, _, token_ref = rest[len(deps):]
        x, y, c, peers = _chip_peers()
        me = 2 * x + y
        for p, (px, py) in enumerate(peers):
            pltpu.make_async_remote_copy(
                src_ref=_slot(g_ref, 2 * px + py), dst_ref=land_ref.at[me], send_sem=send_sems.at[p],
                recv_sem=recv_sems.at[p], device_id=(px, py, c), device_id_type=MESH).start()
        token_ref[...] = jnp.zeros_like(token_ref)

    land = lax.empty((N_CHIPS,) + _slot_shape(grads.shape), grads.dtype)
    return pl.pallas_call(
        body, name=name,
        out_shape=(pltpu.SemaphoreType.DMA((N_PEER_CHIPS,)), pltpu.SemaphoreType.DMA((N_PEER_CHIPS,)),
                   pltpu.HBM(grads.shape, grads.dtype), pltpu.HBM(land.shape, land.dtype), TOKEN),
        in_specs=[HBM, HBM] + [ANY] * len(deps),
        out_specs=(SEM, SEM, HBM, HBM, pl.BlockSpec(memory_space=pltpu.VMEM)),
        input_output_aliases={0: 2, 1: 3}, compiler_params=_in_flight(),
    )(pltpu.with_memory_space_constraint(grads, pltpu.HBM), pltpu.with_memory_space_constraint(land, pltpu.HBM), *deps)


def _scatter_wait(name, send_sems, recv_sems, grads, land, after):
    def body(g_ref, land_ref, send_sems, recv_sems, after_ref, g_out, land_out):
        x, y, c, peers = _chip_peers()
        for p, (px, py) in enumerate(peers):
            cp = pltpu.make_async_remote_copy(
                src_ref=_slot(g_ref, 2 * px + py), dst_ref=land_ref.at[2 * px + py], send_sem=send_sems.at[p],
                recv_sem=recv_sems.at[p], device_id=(px, py, c), device_id_type=MESH)
            cp.wait_send()
            cp.wait_recv()

    return pl.pallas_call(
        body, name=name,
        out_shape=(pltpu.HBM(grads.shape, grads.dtype), pltpu.HBM(land.shape, land.dtype)),
        in_specs=(HBM, HBM, SEM, SEM, ANY), out_specs=(HBM, HBM), input_output_aliases={0: 0, 1: 1},
        compiler_params=_in_flight(),
    )(grads, land, send_sems, recv_sems, after)


def _swap_sibling(name, parts):
    n = len(parts)

    def body(*refs):
        in_refs, out_refs = refs[:n], refs[n:2 * n]
        send_sems, recv_sems = refs[2 * n:]
        x, y, c = lax.axis_index("x"), lax.axis_index("y"), lax.axis_index("c")
        copies = []
        for a in range(n):
            cp = pltpu.make_async_remote_copy(
                src_ref=in_refs[a], dst_ref=out_refs[a], send_sem=send_sems.at[a], recv_sem=recv_sems.at[a],
                device_id=(x, y, 1 - c), device_id_type=MESH)
            cp.start()
            copies.append(cp)
        for cp in copies:
            cp.wait_recv()
        for cp in copies:
            cp.wait_send()

    return pl.pallas_call(
        body, name=name, in_specs=[ANY] * n, out_specs=[ANY] * n,
        out_shape=[jax.ShapeDtypeStruct(p.shape, p.dtype) for p in parts],
        scratch_shapes=[pltpu.SemaphoreType.DMA((n,)), pltpu.SemaphoreType.DMA((n,))],
    )(*parts)


def _gather_devices(name, block):
    def body(in_ref, out_ref, send_sems, recv_sems, local_sem):
        x, y, c = lax.axis_index("x"), lax.axis_index("y"), lax.axis_index("c")
        me = 4 * x + 2 * y + c
        local = pltpu.make_async_copy(in_ref, out_ref.at[me], local_sem)
        local.start()
        sends, recvs = [], []
        k = 0
        for fx in range(2):
            for fy in range(2):
                for fc in range(2):
                    if fx == fy == fc == 0:
                        continue
                    px = x if fx == 0 else 1 - x
                    py = y if fy == 0 else 1 - y
                    pc = c if fc == 0 else 1 - c
                    cp = pltpu.make_async_remote_copy(
                        src_ref=in_ref, dst_ref=out_ref.at[me], send_sem=send_sems.at[k], recv_sem=recv_sems.at[k],
                        device_id=(px, py, pc), device_id_type=MESH)
                    cp.start()
                    sends.append(cp)
                    recvs.append(pltpu.make_async_remote_copy(
                        src_ref=in_ref, dst_ref=out_ref.at[4 * px + 2 * py + pc], send_sem=send_sems.at[k],
                        recv_sem=recv_sems.at[k], device_id=(px, py, pc), device_id_type=MESH))
                    k += 1
        for cp in recvs:
            cp.wait_recv()
        for cp in sends:
            cp.wait_send()
        local.wait()

    return pl.pallas_call(
        body, name=name, in_specs=[ANY], out_specs=ANY,
        out_shape=jax.ShapeDtypeStruct((N_DEV,) + block.shape, block.dtype),
        scratch_shapes=[pltpu.SemaphoreType.DMA((N_DEV - 1,)), pltpu.SemaphoreType.DMA((N_DEV - 1,)),
                        pltpu.SemaphoreType.DMA],
    )(block)


def _pack_rows(pieces, width):
    flat = jnp.concatenate([p.reshape(-1) for p in pieces])
    rows = -(-flat.shape[0] // width)
    rows = -(-rows // 8) * 8
    flat = jnp.pad(flat, (0, rows * width - flat.shape[0]))
    return flat.reshape(rows, width)


def _unpack_rows(packed, shapes):
    flat = packed.reshape(-1)
    out, off = [], 0
    for s in shapes:
        size = 1
        for d in s:
            size *= d
        out.append(flat[off:off + size].reshape(s))
        off += size
    return out


def kernel(x, ln_mix_pre, w_in, conv_w, pool_w, pool_scale, w_out, ln_mix_post, ln_ffn_pre, w_gate, w_up, w_down, ln_ffn_post, loss_target, m_ln_mix_pre, m_w_in, m_conv_w, m_pool_w, m_pool_scale, m_w_out, m_ln_mix_post, m_ln_ffn_pre, m_w_gate, m_w_up, m_w_down, m_ln_ffn_post, v_ln_mix_pre, v_w_in, v_conv_w, v_pool_w, v_pool_scale, v_w_out, v_ln_mix_post, v_ln_ffn_pre, v_w_gate, v_w_up, v_w_down, v_ln_ffn_post):
    t, d = x.shape[1], x.shape[2]
    e4 = w_in.shape[2]
    e = N_CHIPS * e4
    f4 = w_gate.shape[2]
    f = N_CHIPS * f4
    n_groups, dg4, dg = pool_w.shape[1], pool_w.shape[2], pool_w.shape[3]
    cw4 = conv_w.shape[2]
    chip = 2 * lax.axis_index("x") + lax.axis_index("y")
    xs, tgt = x[0], loss_target[0]

    big = {"w_in": w_in[0], "w_out": w_out[0], "w_gate": w_gate[0], "w_up": w_up[0], "w_down": w_down[0],
           "pool_w": pool_w[0].reshape(n_groups * dg4, dg)}
    names = ["w_in", "pool_w", "w_out", "w_gate", "w_up", "w_down"]
    chip_arr = chip.astype(jnp.int32).reshape(1)
    conv_all = _gather_devices("gather_conv_w", _pack_rows([conv_w[0]], 128))
    conv_full = jnp.concatenate(
        [conv_all[2 * j].reshape(-1)[:CONV_K * cw4].reshape(CONV_K, cw4) for j in range(N_CHIPS)], axis=1)
    in_flight, deps = {}, (conv_all,)
    for k in names:
        by_columns = k in ("w_gate", "w_up")
        in_flight[k] = _gather_start(
            "gather_start_" + k, _cast_to_slot("cast_" + k, big[k], chip_arr, by_columns), deps)
        deps = (in_flight[k][3],)

    def landed(ks, after):
        fulls = []
        for k in ks:
            send_sems, recv_sems, full, _ = in_flight[k]
            fulls.append(_gather_wait("gather_wait_" + k, send_sems, recv_sems, full, after))
            after = fulls[-1]
        return _forward_sibling("forward_" + ks[0], fulls)

    def rows3(tile, width):
        return pl.BlockSpec((tile, width), lambda i, j, k: (i, 0))

    gain3 = pl.BlockSpec((1, d), lambda i, j, k: (0, 0))
    f32_td = jax.ShapeDtypeStruct((t, d), jnp.float32)
    mm_td = jax.ShapeDtypeStruct((t, d), MM_DTYPE)
    f32_gain = jax.ShapeDtypeStruct((1, d), jnp.float32)

    h = _pre_norm(xs, ln_mix_pre, deps)
    (win_g,) = landed(["w_in"], h)
    tm = _tile(t, 1024)
    tm2 = _tile(t, 2048)
    proj = _matmul(
        "in_proj", grid=(t // tm2, N_CHIPS, 1), mode="nn",
        pairs=[[(h, win_g)]],
        pair_specs=[[(rows3(tm2, d), pl.BlockSpec((None, d, e4), lambda i, j, k: (j, 0, 0)))]],
        acc_shapes=[(tm2, e4)], out_shapes=[jax.ShapeDtypeStruct((t, e), jnp.float32)],
        out_specs=[pl.BlockSpec((tm2, e4), lambda i, j, k: (i, j))], epilogue=_identity_epilogue)[0]
    pool_g, wout_full = landed(["pool_w", "w_out"], proj)
    pool_g, wout_full = pool_g.reshape(N_CHIPS, n_groups, dg4, dg), wout_full.reshape(d, d)
    mixed = _mixers_fwd(proj, conv_full, pool_g, pool_scale)

    def post_mix_epilogue(accs, extra_refs):
        x_ref, g2_ref, g3_ref = extra_refs
        mo = accs[0]
        x1 = x_ref[...] + mo * _rstd(mo) * g2_ref[...]
        return mo, x1, x1 * _rstd(x1) * g3_ref[...]

    tm_mix = _tile(t, 512)
    mix_out, x1, hf = _matmul(
        "out_proj", grid=(t // tm_mix, 1, 1), mode="nn",
        pairs=[[(mixed, wout_full)]],
        pair_specs=[[(rows3(tm_mix, d), pl.BlockSpec((d, d), lambda i, j, k: (0, 0)))]],
        acc_shapes=[(tm_mix, d)], extras=[xs, ln_mix_post, ln_ffn_pre],
        extra_specs=[rows3(tm_mix, d), gain3, gain3], out_shapes=[f32_td, f32_td, mm_td],
        out_specs=[rows3(tm_mix, d)] * 3, epilogue=post_mix_epilogue, epi_rows=EPILOGUE_ROWS)
    wg_full, wu_full = landed(["w_gate", "w_up"], hf)

    def gate_up_epilogue(accs, extra_refs):
        g, up = accs
        sig = jax.nn.sigmoid(g)
        silu = g * sig
        return up * (sig + silu * (1.0 - sig)), silu, silu * up

    tf = _tile(f, 512)
    ff_tile = jax.ShapeDtypeStruct((t, f), MM_DTYPE)
    act_by_g, act_by_up, act = _matmul(
        "gate_up", grid=(f // tf, t // tm, 1), mode="nn",
        pairs=[[(hf, wg_full)], [(hf, wu_full)]],
        pair_specs=[[(pl.BlockSpec((tm, d), lambda j, i, k: (i, 0)),
                      pl.BlockSpec((d, tf), lambda j, i, k: (0, j)))]] * 2,
        acc_shapes=[(tm, tf)] * 2, out_shapes=[ff_tile] * 3,
        out_specs=[pl.BlockSpec((tm, tf), lambda j, i, k: (i, j))] * 3, epilogue=gate_up_epilogue)
    wdown_full = landed(["w_down"], act)[0].reshape(f, d)

    def loss_epilogue(accs, extra_refs):
        x1_ref, tg_ref, g_ref = extra_refs
        ff_v = accs[0]
        rstd = _rstd(ff_v)
        n = ff_v * rstd
        g = g_ref[...]
        err = x1_ref[...] + n * g - tg_ref[...]
        rows_loss = 0.5 * jnp.sum(jnp.mean(err * err, axis=-1, keepdims=True), axis=0, keepdims=True)
        dout = err / d
        return (dout, _norm_bwd(dout * g, n, rstd), jnp.broadcast_to(rows_loss, (8, 128)),
                jnp.sum(dout * n, axis=0, keepdims=True))

    tm_row = _tile(t, 1024)
    dout, dff, loss_tile, dg_ffn_post = _matmul(
        "down_proj", grid=(t // tm_row, 1, f // tf), mode="nn",
        pairs=[[(act, wdown_full)]],
        pair_specs=[[(pl.BlockSpec((tm_row, tf), lambda i, j, k: (i, k)),
                      pl.BlockSpec((tf, d), lambda i, j, k: (k, 0)))]],
        acc_shapes=[(tm_row, d)], extras=[x1, tgt, ln_ffn_post], extra_specs=[STREAM] * 2 + [gain3],
        out_shapes=[f32_td, mm_td, jax.ShapeDtypeStruct((8, 128), jnp.float32), f32_gain],
        out_specs=[STREAM] * 2 + [pl.BlockSpec((8, 128), lambda i, j, k: (0, 0)), gain3],
        epilogue=loss_epilogue, epi_rows=EPILOGUE_ROWS, sum_outs=(2, 3))

    def dact_epilogue(accs, extra_refs):
        dact = accs[0]
        return dact * extra_refs[0][...].astype(jnp.float32), dact * extra_refs[1][...].astype(jnp.float32)

    ff_spec_ij = pl.BlockSpec((tm2, tf), lambda i, j, k: (i, j))
    dg_act, dup_act = _matmul(
        "dact", grid=(t // tm2, f // tf, 1), mode="nt",
        pairs=[[(dff, wdown_full)]],
        pair_specs=[[(rows3(tm2, d), pl.BlockSpec((tf, d), lambda i, j, k: (j, 0)))]],
        acc_shapes=[(tm2, tf)], extras=[act_by_g, act_by_up], extra_specs=[ff_spec_ij, ff_spec_ij],
        out_shapes=[ff_tile] * 2, out_specs=[ff_spec_ij] * 2, epilogue=dact_epilogue)
    tk = _tile(t, 2048)
    dw_down = _matmul(
        "dw_down", grid=(N_CHIPS, 1, t // tk), mode="tn",
        pairs=[[(act, dff)]],
        pair_specs=[[(pl.BlockSpec((tk, f4), lambda i, j, k: (k, i)),
                      pl.BlockSpec((tk, d), lambda i, j, k: (k, 0)))]],
        acc_shapes=[(f4, d)], out_shapes=[jax.ShapeDtypeStruct((N_CHIPS, f4, d), COMM_DTYPE)],
        out_specs=[pl.BlockSpec((None, f4, d), lambda i, j, k: (i, 0, 0))], epilogue=_identity_epilogue)[0]
    tn = _tile(d, 1024)
    leaving = {"w_down": _scatter_start("scatter_start_w_down", dw_down)}
    tmo = _tile(d, 2048)
    grad_ff = jax.ShapeDtypeStruct((d, f), COMM_DTYPE)
    dw_gate, dw_up = _matmul(
        "dw_gate_up", grid=(d // tmo, f // tf, t // tm2), mode="tn",
        pairs=[[(hf, dg_act)], [(hf, dup_act)]],
        pair_specs=[[(pl.BlockSpec((tm2, tmo), lambda i, j, k: (k, i)),
                      pl.BlockSpec((tm2, tf), lambda i, j, k: (k, j)))]] * 2,
        acc_shapes=[(tmo, tf)] * 2, out_shapes=[grad_ff] * 2,
        out_specs=[pl.BlockSpec((tmo, tf), lambda i, j, k: (i, j))] * 2, epilogue=_identity_epilogue,
        deps=leaving["w_down"][4:])
    leaving["w_gate"] = _scatter_start("scatter_start_w_gate", dw_gate)
    leaving["w_up"] = _scatter_start("scatter_start_w_up", dw_up, leaving["w_gate"][4:])

    def ffn_pre_epilogue(accs, extra_refs):
        x1_ref, dout_ref, mo_ref, g3_ref, g2_ref = extra_refs
        dhf_v = accs[0]
        x1_v = x1_ref[...]
        rstd3 = _rstd(x1_v)
        n3 = x1_v * rstd3
        dx1_v = dout_ref[...] + _norm_bwd(dhf_v * g3_ref[...], n3, rstd3)
        mo = mo_ref[...]
        rstd2 = _rstd(mo)
        n2 = mo * rstd2
        return (dx1_v, _norm_bwd(dx1_v * g2_ref[...], n2, rstd2), jnp.sum(dhf_v * n3, axis=0, keepdims=True),
                jnp.sum(dx1_v * n2, axis=0, keepdims=True))

    dx1, dmo, dg_ffn_pre, dg_mix_post = _matmul(
        "dhf", grid=(t // tm_row, 1, f // tf), mode="nt",
        pairs=[[(dg_act, wg_full), (dup_act, wu_full)]],
        pair_specs=[[(pl.BlockSpec((tm_row, tf), lambda i, j, k: (i, k)),
                      pl.BlockSpec((d, tf), lambda i, j, k: (0, k)))] * 2],
        acc_shapes=[(tm_row, d)], extras=[x1, dout, mix_out, ln_ffn_pre, ln_mix_post],
        extra_specs=[STREAM] * 3 + [gain3] * 2, out_shapes=[f32_td, mm_td, f32_gain, f32_gain],
        out_specs=[STREAM] * 2 + [gain3] * 2, epilogue=ffn_pre_epilogue, epi_rows=EPILOGUE_ROWS,
        sum_outs=(2, 3), deps=leaving["w_up"][4:])

    dmixed = _matmul(
        "dmixed", grid=(t // tm, 1, 1), mode="nt",
        pairs=[[(dmo, wout_full)]],
        pair_specs=[[(rows3(tm, d), pl.BlockSpec((d, d), lambda i, j, k: (0, 0)))]],
        acc_shapes=[(tm, d)], out_shapes=[f32_td],
        out_specs=[rows3(tm, d)], epilogue=_identity_epilogue,
        deps=leaving["w_up"][4:])[0]
    tmo = _tile(d, 2048)
    dw_out = _matmul(
        "dw_out", grid=(d // tmo, d // tn, t // tk), mode="tn",
        pairs=[[(mixed, dmo)]],
        pair_specs=[[(pl.BlockSpec((tk, tmo), lambda i, j, k: (k, i)),
                      pl.BlockSpec((tk, tn), lambda i, j, k: (k, j)))]],
        acc_shapes=[(tmo, tn)], out_shapes=[jax.ShapeDtypeStruct((d, d), COMM_DTYPE)],
        out_specs=[pl.BlockSpec((tmo, tn), lambda i, j, k: (i, j))], epilogue=_identity_epilogue)[0]
    leaving["w_out"] = _scatter_start("scatter_start_w_out", dw_out.reshape(N_CHIPS, d // N_CHIPS, d))
    dproj, dconv_full, dpool_g, dpool_scale = _mixers_bwd(proj, dmixed, conv_full, pool_g, pool_scale,
                                                          leaving["w_out"][4:])
    dpool_slots = _cast_rows("cast_dpool", dpool_g.reshape(N_CHIPS * n_groups * dg4, dg), COMM_DTYPE)
    leaving["pool_w"] = _scatter_start("scatter_start_pool_w", dpool_slots.reshape(N_CHIPS, n_groups * dg4, dg))
    dw_in = _matmul(
        "dw_in", grid=(d // tmo, N_CHIPS, t // tk), mode="tn",
        pairs=[[(h, dproj)]],
        pair_specs=[[(pl.BlockSpec((tk, tmo), lambda i, j, k: (k, i)),
                      pl.BlockSpec((tk, e4), lambda i, j, k: (k, j)))]],
        acc_shapes=[(tmo, e4)], out_shapes=[jax.ShapeDtypeStruct((N_CHIPS, d, e4), COMM_DTYPE)],
        out_specs=[pl.BlockSpec((None, tmo, e4), lambda i, j, k: (j, i, 0))], epilogue=_identity_epilogue,
        deps=leaving["pool_w"][4:])[0]
    leaving["w_in"] = _scatter_start("scatter_start_w_in", dw_in)

    def mix_pre_epilogue(accs, extra_refs):
        x_ref, dx1_ref, g_ref = extra_refs
        dh_v = accs[0]
        xv = x_ref[...]
        rstd = _rstd(xv)
        n = xv * rstd
        return dx1_ref[...] + _norm_bwd(dh_v * g_ref[...], n, rstd), jnp.sum(dh_v * n, axis=0, keepdims=True)

    grad_x, dg_mix_pre = _matmul(
        "dh", grid=(t // tm_row, 1, N_CHIPS), mode="nt",
        pairs=[[(dproj, win_g)]],
        pair_specs=[[(pl.BlockSpec((tm_row, e4), lambda i, j, k: (i, k)),
                      pl.BlockSpec((None, d, e4), lambda i, j, k: (k, 0, 0)))]],
        acc_shapes=[(tm_row, d)], extras=[xs, dx1, ln_mix_pre], extra_specs=[STREAM] * 2 + [gain3],
        out_shapes=[f32_td, f32_gain], out_specs=[STREAM, gain3], epilogue=mix_pre_epilogue,
        epi_rows=EPILOGUE_ROWS, sum_outs=(1,), deps=leaving["w_in"][4:], prefetch_extras=True)

    names = ["w_down", "w_gate", "w_up", "w_out", "pool_w", "w_in"]
    partial, after = [], grad_x
    for k in names:
        send_sems, recv_sems, own, land, _ = leaving[k]
        own, land = _scatter_wait("scatter_wait_" + k, send_sems, recv_sems, own, land, after)
        partial.append(_sum_own_and_received("sum_" + k, own, land, chip_arr))
        after = land
    other = _swap_sibling("swap_grads", partial)
    moments = {"w_in": (m_w_in, v_w_in), "w_out": (m_w_out, v_w_out), "w_gate": (m_w_gate, v_w_gate),
               "w_up": (m_w_up, v_w_up), "w_down": (m_w_down, v_w_down), "pool_w": (m_pool_w, v_pool_w)}
    result = {}
    for k, mine, theirs in zip(names, partial, other):
        shape = moments[k][0].shape
        two_d = big[k].shape
        outs = _adamw("adamw_" + k, big[k], moments[k][0].reshape(two_d), moments[k][1].reshape(two_d),
                      [mine, theirs])
        result[k] = [o.reshape(shape) for o in outs]

    small_shapes = [(1, d)] * 4 + [pool_scale.shape, (CONV_K, N_CHIPS * cw4), (1, 128)]
    packed = _pack_rows([dg_mix_pre, dg_mix_post, dg_ffn_pre, dg_ffn_post, dpool_scale, dconv_full,
                         loss_tile[0:1, :]], 1024)
    summed = _sum_slots("sum_small", _gather_devices("gather_small", packed))
    (g_mix_pre, g_mix_post, g_ffn_pre, g_ffn_post, g_pool_scale, g_conv_full,
     loss_row) = _unpack_rows(summed, small_shapes)
    g_conv = lax.dynamic_slice(g_conv_full, (0, chip * cw4), (CONV_K, cw4))[None]
    small = [("ln_mix_pre", ln_mix_pre, m_ln_mix_pre, v_ln_mix_pre, g_mix_pre),
             ("conv_w", conv_w, m_conv_w, v_conv_w, g_conv),
             ("pool_scale", pool_scale, m_pool_scale, v_pool_scale, g_pool_scale),
             ("ln_mix_post", ln_mix_post, m_ln_mix_post, v_ln_mix_post, g_mix_post),
             ("ln_ffn_pre", ln_ffn_pre, m_ln_ffn_pre, v_ln_ffn_pre, g_ffn_pre),
             ("ln_ffn_post", ln_ffn_post, m_ln_ffn_post, v_ln_ffn_post, g_ffn_post)]
    shapes_small = [s[1].shape for s in small]
    packs = [_pack_rows([s[q] for s in small], 128) for q in (1, 2, 3, 4)]
    outs = _adamw("adamw_small", packs[0], packs[1], packs[2], [packs[3]])
    unpacked = [_unpack_rows(o, shapes_small) for o in outs]
    for idx, s in enumerate(small):
        result[s[0]] = [u[idx] for u in unpacked]

    loss = loss_row[0, 0]
    order = ["ln_mix_pre", "w_in", "conv_w", "pool_w", "pool_scale", "w_out", "ln_mix_post", "ln_ffn_pre",
             "w_gate", "w_up", "w_down", "ln_ffn_post"]
    return (loss, grad_x[None], *[result[k][0] for k in order], *[result[k][1] for k in order],
            *[result[k][2] for k in order], *[result[k][3] for k in order])
```

```python
import functools

import jax
import jax.numpy as jnp
from jax import lax
from jax.experimental import pallas as pl
from jax.experimental.pallas import tpu as pltpu

EPS = 1e-6
CONV_HEAD_DIM = 128
CONV_K = 3
POOL_WINDOWS = (2, 4, 8, 16)
HALO = 16
EPILOGUE_ROWS = 64
STREAM_ROWS = 256
N_CHIPS = 4
N_DEV = 8

ADAM_LR = 0.001
ADAM_B1 = 0.9
ADAM_B2 = 0.999
ADAM_EPS = 1e-08
ADAM_WD = 0.01
ADAM_STEP = 10

MM_DTYPE = jnp.bfloat16
COMM_DTYPE = jnp.bfloat16
VMEM_LIMIT = 62 * 1024 * 1024
MESH = pl.DeviceIdType.MESH
ANY = pl.BlockSpec(memory_space=pl.ANY)
STREAM = "stream"


def _tile(n, pref):
    t = min(pref, n)
    while n % t:
        t //= 2
    return t


def _params(sem):
    return pltpu.CompilerParams(dimension_semantics=sem, vmem_limit_bytes=VMEM_LIMIT)


def _rstd(x):
    return lax.rsqrt(jnp.mean(x * x, axis=-1, keepdims=True) + EPS)


def _norm_bwd(dn, n, rstd):
    return rstd * (dn - n * jnp.mean(dn * n, axis=-1, keepdims=True))


_DOT_DIMS = {
    "nn": (((1,), (0,)), ((), ())),
    "nt": (((1,), (1,)), ((), ())),
    "tn": (((0,), (0,)), ((), ())),
}


def _dot(a, b, mode):
    return lax.dot_general(a.astype(MM_DTYPE), b.astype(MM_DTYPE), _DOT_DIMS[mode],
                           preferred_element_type=jnp.float32)


def _matmul(name, *, grid, mode, pairs, pair_specs, acc_shapes, extras=(), extra_specs=(),
            out_shapes, out_specs, epilogue, deps=(), epi_rows=0, sum_outs=(), prefetch_extras=False):
    nk = grid[2]
    operands, operand_specs, where, counts = [], [], {}, []
    pair_index = []
    for ps, ss in zip(pairs, pair_specs):
        counts.append(len(ps))
        for arrays, specs in zip(ps, ss):
            for arr, spec in zip(arrays, specs):
                key = (id(arr), id(spec))
                if key not in where:
                    where[key] = len(operands)
                    operands.append(arr)
                    operand_specs.append(spec)
                pair_index.append(where[key])
    n_operands = len(operands)
    n_extra = len(extras)
    n_out = len(out_shapes)
    n_in = n_operands + n_extra + len(deps)
    in_streams = [q for q, s in enumerate(extra_specs) if s is STREAM]
    out_streams = [q for q, s in enumerate(out_specs) if s is STREAM]
    in_buf_rows = (acc_shapes[0][0],) if prefetch_extras else (2, STREAM_ROWS)
    stream_bufs = ([pltpu.VMEM(in_buf_rows + (extras[q].shape[1],), extras[q].dtype) for q in in_streams]
                   + [pltpu.VMEM((2, STREAM_ROWS, out_shapes[q].shape[1]), out_shapes[q].dtype) for q in out_streams])
    n_streams = len(stream_bufs)
    n_acc = 0 if nk == 1 else len(acc_shapes)

    def body(*refs):
        pair_refs = [refs[q] for q in pair_index]
        extra_refs = refs[n_operands:n_operands + n_extra]
        out_refs = refs[n_in:n_in + n_out]
        acc_refs = refs[n_in + n_out:n_in + n_out + n_acc]
        bufs = refs[n_in + n_out + n_acc:n_in + n_out + n_acc + n_streams]
        i, k = pl.program_id(0), pl.program_id(2)

        def stream_copies():
            sems = refs[-1]
            n_rows = acc_refs[0].shape[0]

            def hbm_rows(ref, c):
                return ref.at[pl.ds(pl.multiple_of(i * n_rows + c * STREAM_ROWS, STREAM_ROWS), STREAM_ROWS)]

            def fetch(s, c):
                if prefetch_extras:
                    return pltpu.make_async_copy(
                        extra_refs[in_streams[s]].at[pl.ds(pl.multiple_of(i * n_rows, STREAM_ROWS), n_rows)],
                        bufs[s], sems.at[s, 0])
                return pltpu.make_async_copy(hbm_rows(extra_refs[in_streams[s]], c), bufs[s].at[c % 2],
                                             sems.at[s, c % 2])

            def drain(s, c):
                return pltpu.make_async_copy(bufs[len(in_streams) + s].at[c % 2], hbm_rows(out_refs[out_streams[s]], c),
                                             sems.at[len(in_streams) + s, c % 2])

            return fetch, drain

        def streamed_finish(accs):
            fetch, drain = stream_copies()
            n_chunks = accs[0].shape[0] // STREAM_ROWS
            for c in range(n_chunks):
                for s in range(len(in_streams)):
                    if c + 1 < n_chunks and not prefetch_extras:
                        fetch(s, c + 1).start()
                    if c == 0 or not prefetch_extras:
                        fetch(s, c).wait()
                for s in range(len(out_streams)):
                    if c >= 2:
                        drain(s, c - 2).wait()
                for r0 in range(0, STREAM_ROWS, epi_rows):
                    sub = slice(r0, r0 + epi_rows)
                    rows = slice(c * STREAM_ROWS + r0, c * STREAM_ROWS + r0 + epi_rows)
                    views = [(bufs[in_streams.index(q)].at[rows] if prefetch_extras
                              else bufs[in_streams.index(q)].at[c % 2, sub]) if q in in_streams else e
                             for q, e in enumerate(extra_refs)]
                    outs = epilogue([a[rows, :] for a in accs], views)
                    for q, (o_ref, o) in enumerate(zip(out_refs, outs)):
                        if q in out_streams:
                            bufs[len(in_streams) + out_streams.index(q)][c % 2, sub, :] = o.astype(o_ref.dtype)
                        elif c == 0 and r0 == 0:
                            _accumulate(o_ref, o, i)
                        else:
                            o_ref[...] += o
                for s in range(len(out_streams)):
                    drain(s, c).start()
            for s in range(len(out_streams)):
                for c in range(max(n_chunks - 2, 0), n_chunks):
                    drain(s, c).wait()

        def partial_sums():
            res, p = [], 0
            for cnt in counts:
                tot = None
                for _ in range(cnt):
                    d = _dot(pair_refs[p][...], pair_refs[p + 1][...], mode)
                    tot = d if tot is None else tot + d
                    p += 2
                res.append(tot)
            return res

        def finish(accs):
            n_rows = accs[0].shape[0]
            step = epi_rows or n_rows
            for r0 in range(0, n_rows, step):
                rows = slice(r0, r0 + step)
                outs = epilogue([a[rows, :] for a in accs], [e.at[rows] if e.shape[0] == n_rows else e
                                                             for e in extra_refs])
                for q, (o_ref, o) in enumerate(zip(out_refs, outs)):
                    if q not in sum_outs:
                        o_ref[rows, :] = o.astype(o_ref.dtype)
                    elif r0 == 0:
                        _accumulate(o_ref, o, i)
                    else:
                        o_ref[...] += o

        if nk == 1:
            finish(partial_sums())
        else:
            if n_streams:
                @pl.when(k == (0 if prefetch_extras else nk - 1))
                def _():
                    for s in range(len(in_streams)):
                        stream_copies()[0](s, 0).start()

            @pl.when(k == 0)
            def _():
                for acc_ref, s in zip(acc_refs, partial_sums()):
                    acc_ref[...] = s

            @pl.when(k > 0)
            def _():
                for acc_ref, s in zip(acc_refs, partial_sums()):
                    acc_ref[...] += s

            @pl.when(k == nk - 1)
            def _():
                (streamed_finish if n_streams else finish)(acc_refs)

    scratch = [] if nk == 1 else [pltpu.VMEM(s, jnp.float32) for s in acc_shapes]
    if n_streams:
        assert nk > 1 and epi_rows and not any(q in sum_outs for q in out_streams)
        scratch = scratch + stream_bufs + [pltpu.SemaphoreType.DMA((n_streams, 2))]
    return pl.pallas_call(
        body, name=name, grid=grid,
        in_specs=operand_specs + [ANY if s is STREAM else s for s in extra_specs] + [ANY] * len(deps),
        out_specs=[ANY if s is STREAM else s for s in out_specs],
        out_shape=list(out_shapes), scratch_shapes=scratch,
        compiler_params=_params(("arbitrary", "arbitrary", "arbitrary")),
    )(*operands, *extras, *deps)


def _identity_epilogue(accs, extra_refs):
    return tuple(accs)


def _row_spec(tr, n):
    return pl.BlockSpec((tr, n), lambda i: (i, 0))


def _const_spec(shape):
    return pl.BlockSpec(shape, lambda i: tuple(0 for _ in shape))


def _accumulate(ref, val, i):
    @pl.when(i == 0)
    def _():
        ref[...] = val

    @pl.when(i > 0)
    def _():
        ref[...] += val


def _pre_norm(x, gain, deps=()):
    t, d = x.shape
    tr = _tile(t, 512)

    def body(x_ref, g_ref, *rest):
        h_ref = rest[-1]
        xv = x_ref[...]
        h_ref[...] = (xv * _rstd(xv) * g_ref[...]).astype(h_ref.dtype)

    return pl.pallas_call(
        body, name="pre_norm", grid=(t // tr,),
        in_specs=[_row_spec(tr, d), _const_spec((1, d))] + [ANY] * len(deps), out_specs=_row_spec(tr, d),
        out_shape=jax.ShapeDtypeStruct((t, d), MM_DTYPE), compiler_params=_params(("arbitrary",)),
    )(x, gain, *deps)


def _pool_matrix(pool_ref, g):
    return jnp.concatenate([pool_ref[c, g] for c in range(N_CHIPS)], axis=0)


def _inv_count(row0, n, w):
    pos = (row0 + lax.broadcasted_iota(jnp.int32, (n, 1), 0) + 1).astype(jnp.float32)
    return 1.0 / jnp.minimum(pos, float(w))


def _conv_piece(cu_buf, start, n, b_piece, convw_ref):
    conv = None
    for k in range(CONV_K):
        term = convw_ref[k:k + 1, :] * cu_buf[pl.ds(HALO + start + k - (CONV_K - 1), n), :]
        conv = term if conv is None else conv + term
    return conv, b_piece * conv


def _head_stats(a, width):
    return [_rstd(a[:, h * width:(h + 1) * width]) for h in range(a.shape[1] // width)]


def _pooled_piece(v_buf, start, n, v_piece, row0, dg):
    outs = []
    for gi, w in enumerate(POOL_WINDOWS):
        cols = slice(gi * dg, (gi + 1) * dg)
        win = None
        for k in range(w):
            term = v_buf[pl.ds(HALO + start - k, n), cols]
            win = term if win is None else win + term
        outs.append(win * _inv_count(row0 + start, n, w) - v_piece[:, cols])
    return outs


def _halo_specs(t, tr, width, col):
    per = tr // HALO
    last = t // HALO - 1
    prev = pl.BlockSpec((HALO, width), lambda i: (jnp.maximum(i * per - 1, 0), col))
    nxt = pl.BlockSpec((HALO, width), lambda i: (jnp.minimum((i + 1) * per, last), col))
    return prev, nxt


def _mixers_fwd(proj, conv_w, pool_g, pool_scale):
    t, e = proj.shape
    cw = e // 4
    dg = pool_g.shape[-1]
    tr = _tile(t, 512)

    def main(col):
        return pl.BlockSpec((tr, cw), lambda i: (i, col))

    def body(b_ref, c_ref, u_ref, v_ref, cp_ref, up_ref, vp_ref, convw_ref, pool_ref, scale_ref,
             out_ref, cu_buf, v_buf):
        i = pl.program_id(0)
        keep = (i > 0).astype(jnp.float32)
        cu_buf[pl.ds(0, HALO), :] = cp_ref[...] * up_ref[...] * keep
        cu_buf[pl.ds(HALO, tr), :] = c_ref[...] * u_ref[...]
        v_buf[pl.ds(0, HALO), :] = vp_ref[...] * keep
        v_buf[pl.ds(HALO, tr), :] = v_ref[...]
        _, a = _conv_piece(cu_buf, 0, tr, b_ref[...], convw_ref)
        for h, rstd in enumerate(_head_stats(a, CONV_HEAD_DIM)):
            cols = slice(h * CONV_HEAD_DIM, (h + 1) * CONV_HEAD_DIM)
            out_ref[:, cols] = (a[:, cols] * rstd).astype(out_ref.dtype)
        pooled = _pooled_piece(v_buf, 0, tr, v_ref[...], i * tr, dg)
        for gi, p in enumerate(pooled):
            z = _dot(p, _pool_matrix(pool_ref, gi), "nn")
            cols = slice(gi * dg, (gi + 1) * dg)
            out_ref[:, cw + gi * dg:cw + (gi + 1) * dg] = (z * _rstd(z) * scale_ref[:, cols]).astype(out_ref.dtype)

    prev_c, _ = _halo_specs(t, tr, cw, 1)
    prev_u, _ = _halo_specs(t, tr, cw, 2)
    prev_v, _ = _halo_specs(t, tr, cw, 3)
    return pl.pallas_call(
        body, name="mixers_fwd", grid=(t // tr,),
        in_specs=[main(0), main(1), main(2), main(3), prev_c, prev_u, prev_v,
                  _const_spec(conv_w.shape), _const_spec(pool_g.shape), _const_spec(pool_scale.shape)],
        out_specs=_row_spec(tr, 2 * cw),
        out_shape=jax.ShapeDtypeStruct((t, 2 * cw), MM_DTYPE),
        scratch_shapes=[pltpu.VMEM((tr + HALO, cw), jnp.float32), pltpu.VMEM((tr + HALO, cw), jnp.float32)],
        compiler_params=_params(("arbitrary",)),
    )(proj, proj, proj, proj, proj, proj, proj, conv_w, pool_g, pool_scale)


def _mixers_bwd(proj, dmixed, conv_w, pool_g, pool_scale, deps=()):
    t, e = proj.shape
    cw = e // 4
    dg = pool_g.shape[-1]
    n_groups = len(POOL_WINDOWS)
    tr = _tile(t, 512)
    n_tiles = t // tr
    ext = tr + 2 * HALO

    def main(col):
        return pl.BlockSpec((tr, cw), lambda i: (i, col))

    def body(b_ref, c_ref, u_ref, v_ref, dyc_ref, dyp_ref,
             cp_ref, up_ref, vp_ref,
             bn_ref, cn_ref, un_ref, vn_ref, dycn_ref, dypn_ref,
             convw_ref, pool_ref, scale_ref, *rest):
        (dproj_ref, dconvw_ref, dpool_ref, dscale_ref,
         cu_buf, v_buf, dconv_buf, dpn_buf, dpooled_buf) = rest[len(deps):]
        i = pl.program_id(0)
        keep_prev = (i > 0).astype(jnp.float32)
        keep_next = (i < n_tiles - 1).astype(jnp.float32)
        cu_buf[pl.ds(0, HALO), :] = cp_ref[...] * up_ref[...] * keep_prev
        cu_buf[pl.ds(HALO, tr), :] = c_ref[...] * u_ref[...]
        cu_buf[pl.ds(HALO + tr, HALO), :] = cn_ref[...] * un_ref[...]
        v_buf[pl.ds(0, HALO), :] = vp_ref[...] * keep_prev
        v_buf[pl.ds(HALO, tr), :] = v_ref[...]
        v_buf[pl.ds(HALO + tr, HALO), :] = vn_ref[...]

        def conv_piece(start, n, b_piece, dyc_piece, keep, is_main):
            conv, a = _conv_piece(cu_buf, start, n, b_piece, convw_ref)
            for h, rstd in enumerate(_head_stats(a, CONV_HEAD_DIM)):
                cols = slice(h * CONV_HEAD_DIM, (h + 1) * CONV_HEAD_DIM)
                da = _norm_bwd(dyc_piece[:, cols], a[:, cols] * rstd, rstd)
                dconv_buf[pl.ds(start, n), cols] = da * b_piece[:, cols] * keep
                if is_main:
                    dproj_ref[:, cols] = (da * conv[:, cols]).astype(dproj_ref.dtype)

        conv_piece(0, tr, b_ref[...], dyc_ref[...], 1.0, True)
        conv_piece(tr, HALO, bn_ref[...], dycn_ref[...], keep_next, False)

        dconv_main = dconv_buf[pl.ds(0, tr), :]
        dcu = None
        dw_rows = []
        for k in range(CONV_K):
            shift = CONV_K - 1 - k
            term = convw_ref[k:k + 1, :] * dconv_buf[pl.ds(shift, tr), :]
            dcu = term if dcu is None else dcu + term
            dw_rows.append(jnp.sum(dconv_main * cu_buf[pl.ds(HALO - shift, tr), :], axis=0, keepdims=True))
        dproj_ref[:, cw:2 * cw] = (dcu * u_ref[...]).astype(dproj_ref.dtype)
        dproj_ref[:, 2 * cw:3 * cw] = (dcu * c_ref[...]).astype(dproj_ref.dtype)
        _accumulate(dconvw_ref, jnp.concatenate(dw_rows, axis=0), i)

        def pool_piece(start, n, v_piece, dyp_piece, keep, is_main):
            pooled = _pooled_piece(v_buf, start, n, v_piece, i * tr, dg)
            dscale, dmats = [], []
            for gi, w in enumerate(POOL_WINDOWS):
                cols = slice(gi * dg, (gi + 1) * dg)
                mat = _pool_matrix(pool_ref, gi)
                z = _dot(pooled[gi], mat, "nn")
                rstd = _rstd(z)
                nz = z * rstd
                dyp_g = dyp_piece[:, cols]
                dz = _norm_bwd(dyp_g * scale_ref[:, cols], nz, rstd)
                dpooled = _dot(dz, mat, "nt") * keep
                dpn_buf[pl.ds(start, n), cols] = dpooled * _inv_count(i * tr + start, n, w)
                if is_main:
                    dpooled_buf[:, cols] = dpooled
                    dscale.append(jnp.sum(dyp_g * nz, axis=0, keepdims=True))
                    dmats.append(_dot(pooled[gi], dz, "tn"))
            return dscale, dmats

        dscale, dmats = pool_piece(0, tr, v_ref[...], dyp_ref[...], 1.0, True)
        pool_piece(tr, HALO, vn_ref[...], dypn_ref[...], keep_next, False)
        for gi, w in enumerate(POOL_WINDOWS):
            cols = slice(gi * dg, (gi + 1) * dg)
            back = None
            for k in range(w):
                term = dpn_buf[pl.ds(k, tr), cols]
                back = term if back is None else back + term
            dproj_ref[:, 3 * cw + gi * dg:3 * cw + (gi + 1) * dg] = (back - dpooled_buf[:, cols]).astype(dproj_ref.dtype)
        _accumulate(dscale_ref, jnp.concatenate(dscale, axis=1), i)
        rows = dg // N_CHIPS
        for gi in range(n_groups):
            for c in range(N_CHIPS):
                _accumulate(dpool_ref.at[c, gi], dmats[gi][c * rows:(c + 1) * rows, :], i)

    prev_c, next_c = _halo_specs(t, tr, cw, 1)
    prev_u, next_u = _halo_specs(t, tr, cw, 2)
    prev_v, next_v = _halo_specs(t, tr, cw, 3)
    _, next_b = _halo_specs(t, tr, cw, 0)
    _, next_dyc = _halo_specs(t, tr, cw, 0)
    _, next_dyp = _halo_specs(t, tr, cw, 1)
    return pl.pallas_call(
        body, name="mixers_bwd", grid=(n_tiles,),
        in_specs=[main(0), main(1), main(2), main(3), main(0), main(1),
                  prev_c, prev_u, prev_v,
                  next_b, next_c, next_u, next_v, next_dyc, next_dyp,
                  _const_spec(conv_w.shape), _const_spec(pool_g.shape), _const_spec(pool_scale.shape)]
        + [ANY] * len(deps),
        out_specs=[_row_spec(tr, e), _const_spec(conv_w.shape), _const_spec(pool_g.shape),
                   _const_spec(pool_scale.shape)],
        out_shape=[jax.ShapeDtypeStruct((t, e), MM_DTYPE), jax.ShapeDtypeStruct(conv_w.shape, jnp.float32),
                   jax.ShapeDtypeStruct(pool_g.shape, jnp.float32),
                   jax.ShapeDtypeStruct(pool_scale.shape, jnp.float32)],
        scratch_shapes=[pltpu.VMEM((ext, cw), jnp.float32), pltpu.VMEM((ext, cw), jnp.float32),
                        pltpu.VMEM((tr + HALO, cw), jnp.float32), pltpu.VMEM((tr + HALO, cw), jnp.float32),
                        pltpu.VMEM((tr, cw), jnp.float32)],
        compiler_params=_params(("arbitrary",)),
    )(proj, proj, proj, proj, dmixed, dmixed,
      proj, proj, proj,
      proj, proj, proj, proj, dmixed, dmixed,
      conv_w, pool_g, pool_scale, *deps)


def _cast_rows(name, w, dtype):
    r, c = w.shape
    tr = _tile(r, 512)

    def body(w_ref, o_ref):
        o_ref[...] = w_ref[...].astype(o_ref.dtype)

    return pl.pallas_call(
        body, name=name, grid=(r // tr,), in_specs=[_row_spec(tr, c)], out_specs=_row_spec(tr, c),
        out_shape=jax.ShapeDtypeStruct((r, c), dtype), compiler_params=_params(("arbitrary",)),
    )(w)


def _cast_to_slot(name, w, chip, by_columns=False):
    r, c = w.shape
    tr = _tile(r, 512)
    if by_columns:
        out_spec = pl.BlockSpec((tr, c), lambda i, chip_ref: (i, chip_ref[0]))
        out_shape = jax.ShapeDtypeStruct((r, N_CHIPS * c), MM_DTYPE)
    else:
        out_spec = pl.BlockSpec((None, tr, c), lambda i, chip_ref: (chip_ref[0], i, 0))
        out_shape = jax.ShapeDtypeStruct((N_CHIPS, r, c), MM_DTYPE)

    def body(chip_ref, w_ref, o_ref):
        o_ref[...] = w_ref[...].astype(o_ref.dtype)

    return pl.pallas_call(
        body, name=name,
        grid_spec=pltpu.PrefetchScalarGridSpec(
            num_scalar_prefetch=1, grid=(r // tr,),
            in_specs=[pl.BlockSpec((tr, c), lambda i, chip_ref: (i, 0))], out_specs=out_spec),
        out_shape=out_shape, compiler_params=_params(("arbitrary",)),
    )(chip, w)


def _sum_own_and_received(name, own, land, chip):
    _, r, c = land.shape
    tr = _tile(r, 256)

    def body(chip_ref, own_ref, a_ref, b_ref, c_ref, o_ref):
        tot = own_ref[...].astype(jnp.float32) + a_ref[...].astype(jnp.float32)
        tot = tot + b_ref[...].astype(jnp.float32)
        o_ref[...] = (tot + c_ref[...].astype(jnp.float32)).astype(o_ref.dtype)

    def slot(k):
        return pl.BlockSpec((None, tr, c), lambda i, chip_ref: ((chip_ref[0] + k) % N_CHIPS, i, 0))

    own_spec = slot(0) if len(own.shape) == 3 else pl.BlockSpec((tr, c), lambda i, chip_ref: (i, chip_ref[0]))
    return pl.pallas_call(
        body, name=name,
        grid_spec=pltpu.PrefetchScalarGridSpec(
            num_scalar_prefetch=1, grid=(r // tr,), in_specs=[own_spec, slot(1), slot(2), slot(3)],
            out_specs=pl.BlockSpec((tr, c), lambda i, chip_ref: (i, 0))),
        out_shape=jax.ShapeDtypeStruct((r, c), COMM_DTYPE), compiler_params=_params(("arbitrary",)),
    )(chip, own, land, land, land)


def _sum_slots(name, slots):
    n, r, c = slots.shape
    tr = _tile(r, 256)

    def body(s_ref, o_ref):
        tot = s_ref[0].astype(jnp.float32)
        for s in range(1, n):
            tot = tot + s_ref[s].astype(jnp.float32)
        o_ref[...] = tot

    return pl.pallas_call(
        body, name=name, grid=(r // tr,),
        in_specs=[pl.BlockSpec((n, tr, c), lambda i: (0, i, 0))], out_specs=_row_spec(tr, c),
        out_shape=jax.ShapeDtypeStruct((r, c), jnp.float32), compiler_params=_params(("arbitrary",)),
    )(slots)


def _adamw_math(w, g, m, v):
    m = ADAM_B1 * m + (1.0 - ADAM_B1) * g
    v = ADAM_B2 * v + (1.0 - ADAM_B2) * (g * g)
    m_hat = m / (1.0 - ADAM_B1 ** ADAM_STEP)
    v_hat = v / (1.0 - ADAM_B2 ** ADAM_STEP)
    delta = -ADAM_LR * (m_hat / (jnp.sqrt(v_hat) + ADAM_EPS) + ADAM_WD * w)
    return delta, m, v


def _adamw(name, w, m, v, grad_parts):
    r, c = w.shape
    tr = _tile(r, 256)
    n_parts = len(grad_parts)

    def body(*refs):
        w_ref, m_ref, v_ref = refs[:3]
        part_refs = refs[3:3 + n_parts]
        g_ref, d_ref, nm_ref, nv_ref = refs[3 + n_parts:]
        g = part_refs[0][...].astype(jnp.float32)
        for p in part_refs[1:]:
            g = g + p[...].astype(jnp.float32)
        delta, nm, nv = _adamw_math(w_ref[...], g, m_ref[...], v_ref[...])
        g_ref[...] = g
        d_ref[...] = delta
        nm_ref[...] = nm
        nv_ref[...] = nv

    spec = _row_spec(tr, c)
    out = jax.ShapeDtypeStruct((r, c), jnp.float32)
    return pl.pallas_call(
        body, name=name, grid=(r // tr,), in_specs=[spec] * (3 + n_parts), out_specs=[spec] * 4,
        out_shape=[out] * 4, compiler_params=_params(("arbitrary",)),
    )(w, m, v, *grad_parts)


def _chip_peers():
    x, y, c = lax.axis_index("x"), lax.axis_index("y"), lax.axis_index("c")
    return x, y, c, [(1 - x, y), (x, 1 - y), (1 - x, 1 - y)]


HBM = pl.BlockSpec(memory_space=pltpu.HBM)
SEM = pl.BlockSpec(memory_space=pltpu.SEMAPHORE)
TOKEN = jax.ShapeDtypeStruct((8, 128), jnp.float32)
N_PEER_CHIPS = N_CHIPS - 1


def _in_flight():
    return pltpu.CompilerParams(has_side_effects=pltpu.SideEffectType.DATAFLOW_SIDE_EFFECTING)


def _slot(ref, slot):
    if len(ref.shape) == 3:
        return ref.at[slot]
    width = ref.shape[1] // N_CHIPS
    return ref.at[:, pl.ds(pl.multiple_of(slot * width, 128), width)]


def _half_slot(ref, slot, half):
    rows = ref.shape[-2] // 2
    if len(ref.shape) == 3:
        return ref.at[slot, pl.ds(half * rows, rows)]
    width = ref.shape[1] // N_CHIPS
    return ref.at[pl.ds(half * rows, rows), pl.ds(pl.multiple_of(slot * width, 128), width)]


def _slot_shape(shape):
    return shape[1:] if len(shape) == 3 else (shape[0], shape[1] // N_CHIPS)


def _gather_start(name, full, deps=()):
    def body(full_ref, *rest):
        send_sems, recv_sems, _, token_ref = rest[len(deps):]
        x, y, c, peers = _chip_peers()
        mine = _half_slot(full_ref, 2 * x + y, c)
        for p, (px, py) in enumerate(peers):
            pltpu.make_async_remote_copy(
                src_ref=mine, dst_ref=mine, send_sem=send_sems.at[p], recv_sem=recv_sems.at[p],
                device_id=(px, py, c), device_id_type=MESH).start()
        token_ref[...] = jnp.zeros_like(token_ref)

    return pl.pallas_call(
        body, name=name,
        out_shape=(pltpu.SemaphoreType.DMA((N_PEER_CHIPS,)), pltpu.SemaphoreType.DMA((N_PEER_CHIPS,)),
                   pltpu.HBM(full.shape, full.dtype), TOKEN),
        in_specs=[HBM] + [ANY] * len(deps), out_specs=(SEM, SEM, HBM, pl.BlockSpec(memory_space=pltpu.VMEM)),
        input_output_aliases={0: 2}, compiler_params=_in_flight(),
    )(pltpu.with_memory_space_constraint(full, pltpu.HBM), *deps)


def _gather_wait(name, send_sems, recv_sems, full, after):
    def body(full_ref, send_sems, recv_sems, after_ref, out_ref):
        x, y, c, peers = _chip_peers()
        for p, (px, py) in enumerate(peers):
            cp = pltpu.make_async_remote_copy(
                src_ref=_half_slot(full_ref, 2 * x + y, c), dst_ref=_half_slot(full_ref, 2 * px + py, c),
                send_sem=send_sems.at[p], recv_sem=recv_sems.at[p], device_id=(px, py, c), device_id_type=MESH)
            cp.wait_send()
            cp.wait_recv()

    return pl.pallas_call(
        body, name=name, out_shape=pltpu.HBM(full.shape, full.dtype),
        in_specs=(HBM, SEM, SEM, ANY), out_specs=HBM, input_output_aliases={0: 0}, compiler_params=_in_flight(),
    )(full, send_sems, recv_sems, after)


def _forward_sibling(name, fulls):
    n = len(fulls)

    def body(*refs):
        in_refs, out_refs = refs[:n], refs[n:2 * n]
        send_sems, recv_sems = refs[2 * n:]
        x, y, c, peers = _chip_peers()
        sends, recvs = [], []
        for a in range(n):
            for p, (px, py) in enumerate(peers):
                k = N_PEER_CHIPS * a + p
                slot = 2 * px + py
                cp = pltpu.make_async_remote_copy(
                    src_ref=_half_slot(in_refs[a], slot, c), dst_ref=_half_slot(out_refs[a], slot, c),
                    send_sem=send_sems.at[k], recv_sem=recv_sems.at[k], device_id=(x, y, 1 - c), device_id_type=MESH)
                cp.start()
                sends.append(cp)
                recvs.append(pltpu.make_async_remote_copy(
                    src_ref=_half_slot(in_refs[a], slot, c), dst_ref=_half_slot(out_refs[a], slot, 1 - c),
                    send_sem=send_sems.at[k], recv_sem=recv_sems.at[k], device_id=(x, y, 1 - c), device_id_type=MESH))
        for cp in recvs:
            cp.wait_recv()
        for cp in sends:
            cp.wait_send()

    return pl.pallas_call(
        body, name=name, in_specs=[ANY] * n, out_specs=[ANY] * n,
        out_shape=[jax.ShapeDtypeStruct(f.shape, f.dtype) for f in fulls],
        input_output_aliases={a: a for a in range(n)},
        scratch_shapes=[pltpu.SemaphoreType.DMA((N_PEER_CHIPS * n,)), pltpu.SemaphoreType.DMA((N_PEER_CHIPS * n,))],
    )(*fulls)


def _scatter_start(name, grads, deps=()):
    def body(g_ref, land_ref, *rest):
        send_sems, recv_sems, _, _, token_ref = rest[len(deps):]
        x, y, c, peers = _chip_peers()
        me = 2 * x + y
        for p, (px, py) in enumerate(peers):
            pltpu.make_async_remote_copy(
                src_ref=_slot(g_ref, 2 * px + py), dst_ref=land_ref.at[me], send_sem=send_sems.at[p],
                recv_sem=recv_sems.at[p], device_id=(px, py, c), device_id_type=MESH).start()
        token_ref[...] = jnp.zeros_like(token_ref)

    land = lax.empty((N_CHIPS,) + _slot_shape(grads.shape), grads.dtype)
    return pl.pallas_call(
        body, name=name,
        out_shape=(pltpu.SemaphoreType.DMA((N_PEER_CHIPS,)), pltpu.SemaphoreType.DMA((N_PEER_CHIPS,)),
                   pltpu.HBM(grads.shape, grads.dtype), pltpu.HBM(land.shape, land.dtype), TOKEN),
        in_specs=[HBM, HBM] + [ANY] * len(deps),
        out_specs=(SEM, SEM, HBM, HBM, pl.BlockSpec(memory_space=pltpu.VMEM)),
        input_output_aliases={0: 2, 1: 3}, compiler_params=_in_flight(),
    )(pltpu.with_memory_space_constraint(grads, pltpu.HBM), pltpu.with_memory_space_constraint(land, pltpu.HBM), *deps)


def _scatter_wait(name, send_sems, recv_sems, grads, land, after):
    def body(g_ref, land_ref, send_sems, recv_sems, after_ref, g_out, land_out):
        x, y, c, peers = _chip_peers()
        for p, (px, py) in enumerate(peers):
            cp = pltpu.make_async_remote_copy(
                src_ref=_slot(g_ref, 2 * px + py), dst_ref=land_ref.at[2 * px + py], send_sem=send_sems.at[p],
                recv_sem=recv_sems.at[p], device_id=(px, py, c), device_id_type=MESH)
            cp.wait_send()
            cp.wait_recv()

    return pl.pallas_call(
        body, name=name,
        out_shape=(pltpu.HBM(grads.shape, grads.dtype), pltpu.HBM(land.shape, land.dtype)),
        in_specs=(HBM, HBM, SEM, SEM, ANY), out_specs=(HBM, HBM), input_output_aliases={0: 0, 1: 1},
        compiler_params=_in_flight(),
    )(grads, land, send_sems, recv_sems, after)


def _swap_start(name, parts):
    n = len(parts)

    def body(*refs):
        part_refs, land_refs = refs[:n], refs[n:2 * n]
        send_sems, recv_sems = refs[2 * n], refs[2 * n + 1]
        token_ref = refs[-1]
        x, y, c = lax.axis_index("x"), lax.axis_index("y"), lax.axis_index("c")
        for a in range(n):
            pltpu.make_async_remote_copy(
                src_ref=part_refs[a], dst_ref=land_refs[a], send_sem=send_sems.at[a], recv_sem=recv_sems.at[a],
                device_id=(x, y, 1 - c), device_id_type=MESH).start()
        token_ref[...] = jnp.zeros_like(token_ref)

    lands = [lax.empty(p.shape, p.dtype) for p in parts]
    hbm = [pltpu.HBM(p.shape, p.dtype) for p in parts]
    outs = pl.pallas_call(
        body, name=name,
        out_shape=(pltpu.SemaphoreType.DMA((n,)), pltpu.SemaphoreType.DMA((n,)), *hbm, *hbm, TOKEN),
        in_specs=[HBM] * (2 * n), out_specs=(SEM, SEM, *([HBM] * (2 * n)), pl.BlockSpec(memory_space=pltpu.VMEM)),
        input_output_aliases={a: 2 + a for a in range(2 * n)}, compiler_params=_in_flight(),
    )(*[pltpu.with_memory_space_constraint(p, pltpu.HBM) for p in list(parts) + lands])
    return outs[0], outs[1], list(outs[2:2 + n]), list(outs[2 + n:2 + 2 * n]), outs[-1]


def _swap_wait(name, send_sems, recv_sems, parts, lands, after):
    n = len(parts)

    def body(*refs):
        part_refs, land_refs = refs[:n], refs[n:2 * n]
        send_sems, recv_sems = refs[2 * n], refs[2 * n + 1]
        x, y, c = lax.axis_index("x"), lax.axis_index("y"), lax.axis_index("c")
        for a in range(n):
            cp = pltpu.make_async_remote_copy(
                src_ref=part_refs[a], dst_ref=land_refs[a], send_sem=send_sems.at[a], recv_sem=recv_sems.at[a],
                device_id=(x, y, 1 - c), device_id_type=MESH)
            cp.wait_send()
            cp.wait_recv()

    hbm = [pltpu.HBM(p.shape, p.dtype) for p in parts]
    outs = pl.pallas_call(
        body, name=name, out_shape=(*hbm, *hbm),
        in_specs=[HBM] * (2 * n) + [SEM, SEM, ANY], out_specs=tuple([HBM] * (2 * n)),
        input_output_aliases={a: a for a in range(2 * n)}, compiler_params=_in_flight(),
    )(*parts, *lands, send_sems, recv_sems, after)
    return list(outs[:n]), list(outs[n:])


def _gather_devices(name, block, deps=()):
    def body(in_ref, *rest):
        out_ref, send_sems, recv_sems, local_sem = rest[len(deps):]
        x, y, c = lax.axis_index("x"), lax.axis_index("y"), lax.axis_index("c")
        me = 4 * x + 2 * y + c
        local = pltpu.make_async_copy(in_ref, out_ref.at[me], local_sem)
        local.start()
        sends, recvs = [], []
        k = 0
        for fx in range(2):
            for fy in range(2):
                for fc in range(2):
                    if fx == fy == fc == 0:
                        continue
                    px = x if fx == 0 else 1 - x
                    py = y if fy == 0 else 1 - y
                    pc = c if fc == 0 else 1 - c
                    cp = pltpu.make_async_remote_copy(
                        src_ref=in_ref, dst_ref=out_ref.at[me], send_sem=send_sems.at[k], recv_sem=recv_sems.at[k],
                        device_id=(px, py, pc), device_id_type=MESH)
                    cp.start()
                    sends.append(cp)
                    recvs.append(pltpu.make_async_remote_copy(
                        src_ref=in_ref, dst_ref=out_ref.at[4 * px + 2 * py + pc], send_sem=send_sems.at[k],
                        recv_sem=recv_sems.at[k], device_id=(px, py, pc), device_id_type=MESH))
                    k += 1
        for cp in recvs:
            cp.wait_recv()
        for cp in sends:
            cp.wait_send()
        local.wait()

    return pl.pallas_call(
        body, name=name, in_specs=[ANY] * (1 + len(deps)), out_specs=ANY,
        out_shape=jax.ShapeDtypeStruct((N_DEV,) + block.shape, block.dtype),
        scratch_shapes=[pltpu.SemaphoreType.DMA((N_DEV - 1,)), pltpu.SemaphoreType.DMA((N_DEV - 1,)),
                        pltpu.SemaphoreType.DMA],
    )(block, *deps)


def _pack_rows(pieces, width):
    flat = jnp.concatenate([p.reshape(-1) for p in pieces])
    rows = -(-flat.shape[0] // width)
    rows = -(-rows // 8) * 8
    flat = jnp.pad(flat, (0, rows * width - flat.shape[0]))
    return flat.reshape(rows, width)


def _unpack_rows(packed, shapes):
    flat = packed.reshape(-1)
    out, off = [], 0
    for s in shapes:
        size = 1
        for d in s:
            size *= d
        out.append(flat[off:off + size].reshape(s))
        off += size
    return out


def kernel(x, ln_mix_pre, w_in, conv_w, pool_w, pool_scale, w_out, ln_mix_post, ln_ffn_pre, w_gate, w_up, w_down, ln_ffn_post, loss_target, m_ln_mix_pre, m_w_in, m_conv_w, m_pool_w, m_pool_scale, m_w_out, m_ln_mix_post, m_ln_ffn_pre, m_w_gate, m_w_up, m_w_down, m_ln_ffn_post, v_ln_mix_pre, v_w_in, v_conv_w, v_pool_w, v_pool_scale, v_w_out, v_ln_mix_post, v_ln_ffn_pre, v_w_gate, v_w_up, v_w_down, v_ln_ffn_post):
    t, d = x.shape[1], x.shape[2]
    e4 = w_in.shape[2]
    e = N_CHIPS * e4
    f4 = w_gate.shape[2]
    f = N_CHIPS * f4
    n_groups, dg4, dg = pool_w.shape[1], pool_w.shape[2], pool_w.shape[3]
    cw4 = conv_w.shape[2]
    chip = 2 * lax.axis_index("x") + lax.axis_index("y")
    xs, tgt = x[0], loss_target[0]

    big = {"w_in": w_in[0], "w_out": w_out[0], "w_gate": w_gate[0], "w_up": w_up[0], "w_down": w_down[0],
           "pool_w": pool_w[0].reshape(n_groups * dg4, dg)}
    names = ["w_in", "pool_w", "w_out", "w_gate", "w_up", "w_down"]
    chip_arr = chip.astype(jnp.int32).reshape(1)
    conv_all = _gather_devices("gather_conv_w", _pack_rows([conv_w[0]], 128))
    conv_full = jnp.concatenate(
        [conv_all[2 * j].reshape(-1)[:CONV_K * cw4].reshape(CONV_K, cw4) for j in range(N_CHIPS)], axis=1)
    in_flight, deps = {}, (conv_all,)
    for k in names:
        by_columns = k in ("w_gate", "w_up")
        in_flight[k] = _gather_start(
            "gather_start_" + k, _cast_to_slot("cast_" + k, big[k], chip_arr, by_columns), deps)
        deps = (in_flight[k][3],)

    def landed(ks, after):
        fulls = []
        for k in ks:
            send_sems, recv_sems, full, _ = in_flight[k]
            fulls.append(_gather_wait("gather_wait_" + k, send_sems, recv_sems, full, after))
            after = fulls[-1]
        return _forward_sibling("forward_" + ks[0], fulls)

    def rows3(tile, width):
        return pl.BlockSpec((tile, width), lambda i, j, k: (i, 0))

    gain3 = pl.BlockSpec((1, d), lambda i, j, k: (0, 0))
    f32_td = jax.ShapeDtypeStruct((t, d), jnp.float32)
    mm_td = jax.ShapeDtypeStruct((t, d), MM_DTYPE)
    f32_gain = jax.ShapeDtypeStruct((1, d), jnp.float32)

    h = _pre_norm(xs, ln_mix_pre, deps)
    (win_g,) = landed(["w_in"], h)
    tm = _tile(t, 1024)
    tm2 = _tile(t, 2048)
    proj = _matmul(
        "in_proj", grid=(t // tm2, N_CHIPS, 1), mode="nn",
        pairs=[[(h, win_g)]],
        pair_specs=[[(rows3(tm2, d), pl.BlockSpec((None, d, e4), lambda i, j, k: (j, 0, 0)))]],
        acc_shapes=[(tm2, e4)], out_shapes=[jax.ShapeDtypeStruct((t, e), jnp.float32)],
        out_specs=[pl.BlockSpec((tm2, e4), lambda i, j, k: (i, j))], epilogue=_identity_epilogue)[0]
    pool_g, wout_full = landed(["pool_w", "w_out"], proj)
    pool_g, wout_full = pool_g.reshape(N_CHIPS, n_groups, dg4, dg), wout_full.reshape(d, d)
    mixed = _mixers_fwd(proj, conv_full, pool_g, pool_scale)

    def post_mix_epilogue(accs, extra_refs):
        x_ref, g2_ref, g3_ref = extra_refs
        mo = accs[0]
        x1 = x_ref[...] + mo * _rstd(mo) * g2_ref[...]
        return mo, x1, x1 * _rstd(x1) * g3_ref[...]

    tm_mix = _tile(t, 512)
    mix_out, x1, hf = _matmul(
        "out_proj", grid=(t // tm_mix, 1, 1), mode="nn",
        pairs=[[(mixed, wout_full)]],
        pair_specs=[[(rows3(tm_mix, d), pl.BlockSpec((d, d), lambda i, j, k: (0, 0)))]],
        acc_shapes=[(tm_mix, d)], extras=[xs, ln_mix_post, ln_ffn_pre],
        extra_specs=[rows3(tm_mix, d), gain3, gain3], out_shapes=[f32_td, f32_td, mm_td],
        out_specs=[rows3(tm_mix, d)] * 3, epilogue=post_mix_epilogue, epi_rows=EPILOGUE_ROWS)
    wg_full, wu_full = landed(["w_gate", "w_up"], hf)

    def gate_up_epilogue(accs, extra_refs):
        g, up = accs
        sig = jax.nn.sigmoid(g)
        silu = g * sig
        return up * (sig + silu * (1.0 - sig)), silu, silu * up

    tf = _tile(f, 512)
    ff_tile = jax.ShapeDtypeStruct((t, f), MM_DTYPE)
    act_by_g, act_by_up, act = _matmul(
        "gate_up", grid=(f // tf, t // tm, 1), mode="nn",
        pairs=[[(hf, wg_full)], [(hf, wu_full)]],
        pair_specs=[[(pl.BlockSpec((tm, d), lambda j, i, k: (i, 0)),
                      pl.BlockSpec((d, tf), lambda j, i, k: (0, j)))]] * 2,
        acc_shapes=[(tm, tf)] * 2, out_shapes=[ff_tile] * 3,
        out_specs=[pl.BlockSpec((tm, tf), lambda j, i, k: (i, j))] * 3, epilogue=gate_up_epilogue)
    wdown_full = landed(["w_down"], act)[0].reshape(f, d)

    def loss_epilogue(accs, extra_refs):
        x1_ref, tg_ref, g_ref = extra_refs
        ff_v = accs[0]
        rstd = _rstd(ff_v)
        n = ff_v * rstd
        g = g_ref[...]
        err = x1_ref[...] + n * g - tg_ref[...]
        rows_loss = 0.5 * jnp.sum(jnp.mean(err * err, axis=-1, keepdims=True), axis=0, keepdims=True)
        dout = err / d
        return (dout, _norm_bwd(dout * g, n, rstd), jnp.broadcast_to(rows_loss, (8, 128)),
                jnp.sum(dout * n, axis=0, keepdims=True))

    tm_row = _tile(t, 1024)
    dout, dff, loss_tile, dg_ffn_post = _matmul(
        "down_proj", grid=(t // tm_row, 1, f // tf), mode="nn",
        pairs=[[(act, wdown_full)]],
        pair_specs=[[(pl.BlockSpec((tm_row, tf), lambda i, j, k: (i, k)),
                      pl.BlockSpec((tf, d), lambda i, j, k: (k, 0)))]],
        acc_shapes=[(tm_row, d)], extras=[x1, tgt, ln_ffn_post], extra_specs=[STREAM] * 2 + [gain3],
        out_shapes=[f32_td, mm_td, jax.ShapeDtypeStruct((8, 128), jnp.float32), f32_gain],
        out_specs=[STREAM] * 2 + [pl.BlockSpec((8, 128), lambda i, j, k: (0, 0)), gain3],
        epilogue=loss_epilogue, epi_rows=EPILOGUE_ROWS, sum_outs=(2, 3))

    def dact_epilogue(accs, extra_refs):
        dact = accs[0]
        return dact * extra_refs[0][...].astype(jnp.float32), dact * extra_refs[1][...].astype(jnp.float32)

    ff_spec_ij = pl.BlockSpec((tm2, tf), lambda i, j, k: (i, j))
    dg_act, dup_act = _matmul(
        "dact", grid=(t // tm2, f // tf, 1), mode="nt",
        pairs=[[(dff, wdown_full)]],
        pair_specs=[[(rows3(tm2, d), pl.BlockSpec((tf, d), lambda i, j, k: (j, 0)))]],
        acc_shapes=[(tm2, tf)], extras=[act_by_g, act_by_up], extra_specs=[ff_spec_ij, ff_spec_ij],
        out_shapes=[ff_tile] * 2, out_specs=[ff_spec_ij] * 2, epilogue=dact_epilogue)
    tk = _tile(t, 2048)
    dw_down = _matmul(
        "dw_down", grid=(N_CHIPS, 1, t // tk), mode="tn",
        pairs=[[(act, dff)]],
        pair_specs=[[(pl.BlockSpec((tk, f4), lambda i, j, k: (k, i)),
                      pl.BlockSpec((tk, d), lambda i, j, k: (k, 0)))]],
        acc_shapes=[(f4, d)], out_shapes=[jax.ShapeDtypeStruct((N_CHIPS, f4, d), COMM_DTYPE)],
        out_specs=[pl.BlockSpec((None, f4, d), lambda i, j, k: (i, 0, 0))], epilogue=_identity_epilogue)[0]
    tn = _tile(d, 1024)
    leaving = {"w_down": _scatter_start("scatter_start_w_down", dw_down)}
    tmo = _tile(d, 2048)
    grad_ff = jax.ShapeDtypeStruct((d, f), COMM_DTYPE)
    dw_gate, dw_up = _matmul(
        "dw_gate_up", grid=(d // tmo, f // tf, t // tm2), mode="tn",
        pairs=[[(hf, dg_act)], [(hf, dup_act)]],
        pair_specs=[[(pl.BlockSpec((tm2, tmo), lambda i, j, k: (k, i)),
                      pl.BlockSpec((tm2, tf), lambda i, j, k: (k, j)))]] * 2,
        acc_shapes=[(tmo, tf)] * 2, out_shapes=[grad_ff] * 2,
        out_specs=[pl.BlockSpec((tmo, tf), lambda i, j, k: (i, j))] * 2, epilogue=_identity_epilogue,
        deps=leaving["w_down"][4:])
    leaving["w_gate"] = _scatter_start("scatter_start_w_gate", dw_gate)
    leaving["w_up"] = _scatter_start("scatter_start_w_up", dw_up, leaving["w_gate"][4:])

    def ffn_pre_epilogue(accs, extra_refs):
        x1_ref, dout_ref, mo_ref, g3_ref, g2_ref = extra_refs
        dhf_v = accs[0]
        x1_v = x1_ref[...]
        rstd3 = _rstd(x1_v)
        n3 = x1_v * rstd3
        dx1_v = dout_ref[...] + _norm_bwd(dhf_v * g3_ref[...], n3, rstd3)
        mo = mo_ref[...]
        rstd2 = _rstd(mo)
        n2 = mo * rstd2
        return (dx1_v, _norm_bwd(dx1_v * g2_ref[...], n2, rstd2), jnp.sum(dhf_v * n3, axis=0, keepdims=True),
                jnp.sum(dx1_v * n2, axis=0, keepdims=True))

    dx1, dmo, dg_ffn_pre, dg_mix_post = _matmul(
        "dhf", grid=(t // tm_row, 1, f // tf), mode="nt",
        pairs=[[(dg_act, wg_full), (dup_act, wu_full)]],
        pair_specs=[[(pl.BlockSpec((tm_row, tf), lambda i, j, k: (i, k)),
                      pl.BlockSpec((d, tf), lambda i, j, k: (0, k)))] * 2],
        acc_shapes=[(tm_row, d)], extras=[x1, dout, mix_out, ln_ffn_pre, ln_mix_post],
        extra_specs=[STREAM] * 3 + [gain3] * 2, out_shapes=[f32_td, mm_td, f32_gain, f32_gain],
        out_specs=[STREAM] * 2 + [gain3] * 2, epilogue=ffn_pre_epilogue, epi_rows=EPILOGUE_ROWS,
        sum_outs=(2, 3), deps=leaving["w_up"][4:])

    dmixed = _matmul(
        "dmixed", grid=(t // tm, 1, 1), mode="nt",
        pairs=[[(dmo, wout_full)]],
        pair_specs=[[(rows3(tm, d), pl.BlockSpec((d, d), lambda i, j, k: (0, 0)))]],
        acc_shapes=[(tm, d)], out_shapes=[f32_td],
        out_specs=[rows3(tm, d)], epilogue=_identity_epilogue,
        deps=leaving["w_up"][4:])[0]
    tmo = _tile(d, 2048)
    dw_out = _matmul(
        "dw_out", grid=(d // tmo, d // tn, t // tk), mode="tn",
        pairs=[[(mixed, dmo)]],
        pair_specs=[[(pl.BlockSpec((tk, tmo), lambda i, j, k: (k, i)),
                      pl.BlockSpec((tk, tn), lambda i, j, k: (k, j)))]],
        acc_shapes=[(tmo, tn)], out_shapes=[jax.ShapeDtypeStruct((d, d), COMM_DTYPE)],
        out_specs=[pl.BlockSpec((tmo, tn), lambda i, j, k: (i, j))], epilogue=_identity_epilogue)[0]
    leaving["w_out"] = _scatter_start("scatter_start_w_out", dw_out.reshape(N_CHIPS, d // N_CHIPS, d))
    dproj, dconv_full, dpool_g, dpool_scale = _mixers_bwd(proj, dmixed, conv_full, pool_g, pool_scale,
                                                          leaving["w_out"][4:])
    dpool_slots = _cast_rows("cast_dpool", dpool_g.reshape(N_CHIPS * n_groups * dg4, dg), COMM_DTYPE)
    leaving["pool_w"] = _scatter_start("scatter_start_pool_w", dpool_slots.reshape(N_CHIPS, n_groups * dg4, dg))
    dw_in = _matmul(
        "dw_in", grid=(d // tmo, N_CHIPS, t // tk), mode="tn",
        pairs=[[(h, dproj)]],
        pair_specs=[[(pl.BlockSpec((tk, tmo), lambda i, j, k: (k, i)),
                      pl.BlockSpec((tk, e4), lambda i, j, k: (k, j)))]],
        acc_shapes=[(tmo, e4)], out_shapes=[jax.ShapeDtypeStruct((N_CHIPS, d, e4), COMM_DTYPE)],
        out_specs=[pl.BlockSpec((None, tmo, e4), lambda i, j, k: (j, i, 0))], epilogue=_identity_epilogue,
        deps=leaving["pool_w"][4:])[0]
    leaving["w_in"] = _scatter_start("scatter_start_w_in", dw_in)

    def mix_pre_epilogue(accs, extra_refs):
        x_ref, dx1_ref, g_ref = extra_refs
        dh_v = accs[0]
        xv = x_ref[...]
        rstd = _rstd(xv)
        n = xv * rstd
        return dx1_ref[...] + _norm_bwd(dh_v * g_ref[...], n, rstd), jnp.sum(dh_v * n, axis=0, keepdims=True)

    grad_x, dg_mix_pre = _matmul(
        "dh", grid=(t // tm_row, 1, N_CHIPS), mode="nt",
        pairs=[[(dproj, win_g)]],
        pair_specs=[[(pl.BlockSpec((tm_row, e4), lambda i, j, k: (i, k)),
                      pl.BlockSpec((None, d, e4), lambda i, j, k: (k, 0, 0)))]],
        acc_shapes=[(tm_row, d)], extras=[xs, dx1, ln_mix_pre], extra_specs=[STREAM] * 2 + [gain3],
        out_shapes=[f32_td, f32_gain], out_specs=[STREAM, gain3], epilogue=mix_pre_epilogue,
        epi_rows=EPILOGUE_ROWS, sum_outs=(1,), deps=leaving["w_in"][4:], prefetch_extras=True)

    names = ["w_down", "w_gate", "w_up", "w_out", "pool_w", "w_in"]
    partial, after = [], grad_x
    for k in names:
        send_sems, recv_sems, own, land, _ = leaving[k]
        own, land = _scatter_wait("scatter_wait_" + k, send_sems, recv_sems, own, land, after)
        partial.append(_sum_own_and_received("sum_" + k, own, land, chip_arr))
        after = land
    swap_sems = _swap_start("swap_start", partial)
    result = {}

    small_shapes = [(1, d)] * 4 + [pool_scale.shape, (CONV_K, N_CHIPS * cw4), (1, 128)]
    packed = _pack_rows([dg_mix_pre, dg_mix_post, dg_ffn_pre, dg_ffn_post, dpool_scale, dconv_full,
                         loss_tile[0:1, :]], 1024)
    summed = _sum_slots("sum_small", _gather_devices("gather_small", packed, swap_sems[4:]))
    (g_mix_pre, g_mix_post, g_ffn_pre, g_ffn_post, g_pool_scale, g_conv_full,
     loss_row) = _unpack_rows(summed, small_shapes)
    g_conv = lax.dynamic_slice(g_conv_full, (0, chip * cw4), (CONV_K, cw4))[None]
    small = [("ln_mix_pre", ln_mix_pre, m_ln_mix_pre, v_ln_mix_pre, g_mix_pre),
             ("conv_w", conv_w, m_conv_w, v_conv_w, g_conv),
             ("pool_scale", pool_scale, m_pool_scale, v_pool_scale, g_pool_scale),
             ("ln_mix_post", ln_mix_post, m_ln_mix_post, v_ln_mix_post, g_mix_post),
             ("ln_ffn_pre", ln_ffn_pre, m_ln_ffn_pre, v_ln_ffn_pre, g_ffn_pre),
             ("ln_ffn_post", ln_ffn_post, m_ln_ffn_post, v_ln_ffn_post, g_ffn_post)]
    shapes_small = [s[1].shape for s in small]
    packs = [_pack_rows([s[q] for s in small], 128) for q in (1, 2, 3, 4)]
    outs = _adamw("adamw_small", packs[0], packs[1], packs[2], [packs[3]])
    unpacked = [_unpack_rows(o, shapes_small) for o in outs]
    for idx, s in enumerate(small):
        result[s[0]] = [u[idx] for u in unpacked]

    partial, other = _swap_wait("swap_wait", *swap_sems[:4], outs[0])
    moments = {"w_in": (m_w_in, v_w_in), "w_out": (m_w_out, v_w_out), "w_gate": (m_w_gate, v_w_gate),
               "w_up": (m_w_up, v_w_up), "w_down": (m_w_down, v_w_down), "pool_w": (m_pool_w, v_pool_w)}
    for k, mine, theirs in zip(names, partial, other):
        shape = moments[k][0].shape
        two_d = big[k].shape
        outs = _adamw("adamw_" + k, big[k], moments[k][0].reshape(two_d), moments[k][1].reshape(two_d),
                      [mine, theirs])
        result[k] = [o.reshape(shape) for o in outs]

    loss = loss_row[0, 0]
    order = ["ln_mix_pre", "w_in", "conv_w", "pool_w", "pool_scale", "w_out", "ln_mix_post", "ln_ffn_pre",
             "w_gate", "w_up", "w_down", "ln_ffn_post"]
    return (loss, grad_x[None], *[result[k][0] for k in order], *[result[k][1] for k in order],
            *[result[k][2] for k in order], *[result[k][3] for k in order])
```
